```python
import math
import jax, jax.numpy as jnp
from jax import lax
import numpy as np

D_MODEL = 2048
BATCH = 4
SEQ = 4096
DEPTH = 4
DEC_BATCH = 16
DEC_SEQ = 16
PAST_LEN = 1024

CHUNK = 64
QBLOCK = 128
CONV_W = 3
ROPE_THETA = 10000.0
MIX_WIDTH = D_MODEL
A_WIDTH = D_MODEL // 4
A_HEAD_DIM = 64
A_HEADS = A_WIDTH // A_HEAD_DIM
IDX_HEADS = 16
IDX_DIM = 64
TOPK_MAX = 256
B_WIDTH = D_MODEL // 4
C_WIDTH = D_MODEL // 2
C_V_DIM = 128
C_HEADS = C_WIDTH // C_V_DIM
C_NOPE_DIM = 128
C_ROPE_DIM = 64
C_QK_DIM = C_NOPE_DIM + C_ROPE_DIM
KV_LORA = D_MODEL // 4
D_FF = ((8 * D_MODEL // 3 + 255) // 256) * 256
N_MOD = 6
ALPHA = (2 * DEPTH) ** 0.25
BETA = (8 * DEPTH) ** -0.25
LN_EPS = 1e-5
RMS_EPS = 1e-6
IN_WIDTHS = (A_WIDTH, A_WIDTH, A_WIDTH, IDX_HEADS * IDX_DIM, IDX_DIM, IDX_HEADS,
             B_WIDTH, B_WIDTH, B_WIDTH,
             C_HEADS * C_QK_DIM, KV_LORA, C_ROPE_DIM)
IN_TOTAL = sum(IN_WIDTHS)

kernel_name = 'hybrid_stream_dsa_conv_mla_step'


def layer_norm(z, g, b):
    zf = z.astype(jnp.float32)
    mu = zf.mean(-1, keepdims=True)
    var = jnp.square(zf - mu).mean(-1, keepdims=True)
    return ((zf - mu) * lax.rsqrt(var + LN_EPS) * g.astype(jnp.float32) + b.astype(jnp.float32)).astype(z.dtype)


def rms_norm(z, g):
    zf = z.astype(jnp.float32)
    return (zf * lax.rsqrt(jnp.mean(zf * zf, -1, keepdims=True) + RMS_EPS) * g.astype(jnp.float32)).astype(z.dtype)


def rope(x, pos):
    half = x.shape[-1] // 2
    inv = jnp.power(jnp.float32(ROPE_THETA), -jnp.arange(half, dtype=jnp.float32) / half)
    ang = pos.astype(jnp.float32)[:, None] * inv[None, :]
    cos = jnp.cos(ang)[None, :, None, :]
    sin = jnp.sin(ang)[None, :, None, :]
    x1 = x[..., :half].astype(jnp.float32)
    x2 = x[..., half:].astype(jnp.float32)
    return jnp.concatenate([x1 * cos - x2 * sin, x1 * sin + x2 * cos], axis=-1).astype(x.dtype)


def chunk_visible(q_pos, k_pos):
    return (k_pos[None, :] // CHUNK) <= (q_pos[:, None] // CHUNK)


def causal_dwconv(u, prev, w):
    t = u.shape[1]
    up = jnp.concatenate([prev.astype(u.dtype), u], axis=1)
    y = up[:, 0:t] * w[0]
    for i in range(1, CONV_W):
        y = y + up[:, i:i + t] * w[i]
    return y, up[:, t:]


def extend(past, new):
    return new if past is None else jnp.concatenate([past.astype(new.dtype), new], axis=1)


def sweep_query_blocks(fn, q_arrays, q_pos):
    t = q_pos.shape[0]
    if t <= QBLOCK:
        return fn(*q_arrays, q_pos)
    nb = t // QBLOCK
    blocks = tuple(a.reshape((a.shape[0], nb, QBLOCK) + a.shape[2:]).swapaxes(0, 1) for a in q_arrays)
    out = lax.map(lambda args: fn(*args[0], args[1]), (blocks, q_pos.reshape(nb, QBLOCK)))
    return out.swapaxes(0, 1).reshape((out.shape[1], t) + out.shape[3:])


def dsa_block(q, qi, wi, q_pos, k, v, ki, k_pos, topk):
    vis = chunk_visible(q_pos, k_pos)
    rel = jax.nn.relu(jnp.einsum('bthd,bsd->bths', qi, ki).astype(jnp.float32) * IDX_DIM ** -0.5)
    score = jnp.einsum('bths,bth->bts', rel, wi.astype(jnp.float32))
    score = jnp.where(vis[None], score, -jnp.inf)
    _, idx = lax.top_k(score, topk)
    valid = (k_pos[idx] // CHUNK) <= (q_pos[None, :, None] // CHUNK)
    gather = jax.vmap(lambda rows, ids: rows[ids])
    kg = gather(k, idx)
    vg = gather(v, idx)
    s = jnp.einsum('bthd,btkhd->bthk', q, kg).astype(jnp.float32) * A_HEAD_DIM ** -0.5
    s = jnp.where(valid[:, :, None, :], s, -jnp.inf)
    p = jax.nn.softmax(s, axis=-1).astype(v.dtype)
    return jnp.einsum('bthk,btkhd->bthd', p, vg)


def mla_block(q, q_pos, k, v, k_pos):
    vis = chunk_visible(q_pos, k_pos)
    s = jnp.einsum('bqhd,bkhd->bhqk', q, k).astype(jnp.float32) * C_QK_DIM ** -0.5
    s = jnp.where(vis[None, None], s, -jnp.inf)
    p = jax.nn.softmax(s, axis=-1).astype(v.dtype)
    return jnp.einsum('bhqk,bkhd->bqhd', p, v)


def trunk_layer(x, c, q_pos, past, w_in, w_out, conv_b_w, kv_norm, w_uk, w_uv, w_mod, b_mod,
                ln1_g, ln1_b, ln2_g, ln2_b, w_gu, conv_f_w, w_down):
    bsz, t = x.shape[0], x.shape[1]
    if past is None:
        p_ak = p_av = p_ik = p_lat = p_kr = None
        prev_b = jnp.zeros((bsz, CONV_W - 1, B_WIDTH), x.dtype)
        prev_f = jnp.zeros((bsz, CONV_W - 1, D_FF), x.dtype)
    else:
        p_ak, p_av, p_ik, p_lat, p_kr, prev_b, prev_f = past

    mod = jax.nn.silu(c) @ w_mod + b_mod
    sh1, sc1, g1, sh2, sc2, g2 = [m[:, None, :] for m in jnp.split(mod, N_MOD, axis=-1)]
    h = x * (1 + sc1) + sh1

    parts = jnp.split(h @ w_in, np.cumsum(IN_WIDTHS)[:-1].tolist(), axis=-1)
    qa, ka, va, qi, ki, wi, bg, cg, xb, qc, lat, kr = parts

    qa = rope(qa.reshape(bsz, t, A_HEADS, A_HEAD_DIM), q_pos)
    ka = rope(ka.reshape(bsz, t, A_HEADS, A_HEAD_DIM), q_pos)
    va = va.reshape(bsz, t, A_HEADS, A_HEAD_DIM)
    qi = rope(qi.reshape(bsz, t, IDX_HEADS, IDX_DIM), q_pos)
    ki = rope(ki[:, :, None, :], q_pos)[:, :, 0]
    wi = wi * IDX_HEADS ** -0.5
    k_all, v_all, ki_all = extend(p_ak, ka), extend(p_av, va), extend(p_ik, ki)
    n_keys = k_all.shape[1]
    k_pos = jnp.arange(n_keys, dtype=jnp.int32)
    topk = min(TOPK_MAX, n_keys // 4)
    oa = sweep_query_blocks(
        lambda q_, qi_, wi_, qp: dsa_block(q_, qi_, wi_, qp, k_all, v_all, ki_all, k_pos, topk),
        (qa, qi, wi), q_pos)

    conv_u, new_conv_b = causal_dwconv(cg * xb, prev_b, conv_b_w)
    yb = bg * conv_u

    qc = qc.reshape(bsz, t, C_HEADS, C_QK_DIM)
    qc = jnp.concatenate([qc[..., :C_NOPE_DIM], rope(qc[..., C_NOPE_DIM:], q_pos)], axis=-1)
    lat = rms_norm(lat, kv_norm)
    kr = rope(kr[:, :, None, :], q_pos)[:, :, 0]
    lat_all, kr_all = extend(p_lat, lat), extend(p_kr, kr)
    k_nope = jnp.einsum('bsr,rhd->bshd', lat_all, w_uk.reshape(KV_LORA, C_HEADS, C_NOPE_DIM))
    v_c = jnp.einsum('bsr,rhd->bshd', lat_all, w_uv.reshape(KV_LORA, C_HEADS, C_V_DIM))
    k_c = jnp.concatenate(
        [k_nope, jnp.broadcast_to(kr_all[:, :, None, :], k_nope.shape[:3] + (C_ROPE_DIM,))], axis=-1)
    oc = sweep_query_blocks(lambda q_, qp: mla_block(q_, qp, k_c, v_c, k_pos), (qc,), q_pos)

    mix = jnp.concatenate([oa.reshape(bsz, t, A_WIDTH), yb, oc.reshape(bsz, t, C_WIDTH)], axis=-1) @ w_out
    x = layer_norm(ALPHA * x + (1 + g1) * mix, ln1_g, ln1_b)

    h2 = x * (1 + sc2) + sh2
    gate, up = jnp.split(h2 @ w_gu, 2, axis=-1)
    gate, new_conv_f = causal_dwconv(gate, prev_f, conv_f_w)
    f = (jax.nn.silu(gate) * up) @ w_down
    x = layer_norm(ALPHA * x + (1 + g2) * f, ln2_g, ln2_b)
    return x, (ka, va, ki, lat, kr, new_conv_b, new_conv_f)


def setup_inputs(seed: int = 0) -> dict:
    key = jax.random.key(seed)
    ks = jax.random.split(key, 32)

    def nrm(k, shape, scale):
        return jax.random.normal(k, shape, jnp.float32) * scale

    return {
        'x_prompt': nrm(ks[0], (BATCH, SEQ, D_MODEL), 1.0),
        'x_sample': nrm(ks[1], (DEC_BATCH, DEC_SEQ, D_MODEL), 1.0),
        'c_prompt': nrm(ks[2], (BATCH, D_MODEL), 1.0),
        'c_sample': nrm(ks[3], (DEC_BATCH, D_MODEL), 1.0),
        'cache_a_k': nrm(ks[4], (DEPTH, DEC_BATCH, PAST_LEN, A_HEADS, A_HEAD_DIM), 1.0),
        'cache_a_v': nrm(ks[5], (DEPTH, DEC_BATCH, PAST_LEN, A_HEADS, A_HEAD_DIM), 1.0),
        'cache_idx_k': nrm(ks[6], (DEPTH, DEC_BATCH, PAST_LEN, IDX_DIM), 1.0),
        'cache_mla_latent': nrm(ks[7], (DEPTH, DEC_BATCH, PAST_LEN, KV_LORA), 1.0),
        'cache_mla_krope': nrm(ks[8], (DEPTH, DEC_BATCH, PAST_LEN, C_ROPE_DIM), 1.0),
        'state_conv_b': nrm(ks[9], (DEPTH, DEC_BATCH, CONV_W - 1, B_WIDTH), 1.0),
        'state_conv_ffn': nrm(ks[10], (DEPTH, DEC_BATCH, CONV_W - 1, D_FF), 1.0),
        'w_in': nrm(ks[11], (DEPTH, D_MODEL, IN_TOTAL), D_MODEL ** -0.5),
        'w_out': nrm(ks[12], (DEPTH, MIX_WIDTH, D_MODEL), BETA * MIX_WIDTH ** -0.5),
        'conv_b_w': nrm(ks[13], (DEPTH, CONV_W, B_WIDTH), CONV_W ** -0.5),
        'mla_kv_norm': 1.0 + nrm(ks[14], (DEPTH, KV_LORA), 0.02),
        'mla_w_uk': nrm(ks[15], (DEPTH, KV_LORA, C_HEADS * C_NOPE_DIM), KV_LORA ** -0.5),
        'mla_w_uv': nrm(ks[16], (DEPTH, KV_LORA, C_HEADS * C_V_DIM), KV_LORA ** -0.5),
        'w_mod': nrm(ks[17], (DEPTH, D_MODEL, N_MOD * D_MODEL), 0.1 * D_MODEL ** -0.5),
        'b_mod': nrm(ks[18], (DEPTH, N_MOD * D_MODEL), 0.01),
        'ln1_g': 1.0 + nrm(ks[19], (DEPTH, D_MODEL), 0.02),
        'ln1_b': nrm(ks[20], (DEPTH, D_MODEL), 0.02),
        'ln2_g': 1.0 + nrm(ks[21], (DEPTH, D_MODEL), 0.02),
        'ln2_b': nrm(ks[22], (DEPTH, D_MODEL), 0.02),
        'ffn_w_gu': nrm(ks[23], (DEPTH, D_MODEL, 2 * D_FF), D_MODEL ** -0.5),
        'ffn_conv_w': nrm(ks[24], (DEPTH, CONV_W, D_FF), CONV_W ** -0.5),
        'ffn_w_down': nrm(ks[25], (DEPTH, D_FF, D_MODEL), BETA * D_FF ** -0.5),
    }


def reference(x_prompt, x_sample, c_prompt, c_sample, cache_a_k, cache_a_v, cache_idx_k,
              cache_mla_latent, cache_mla_krope, state_conv_b, state_conv_ffn,
              w_in, w_out, conv_b_w, mla_kv_norm, mla_w_uk, mla_w_uv, w_mod, b_mod,
              ln1_g, ln1_b, ln2_g, ln2_b, ffn_w_gu, ffn_conv_w, ffn_w_down):
    pos_p = jnp.arange(x_prompt.shape[1], dtype=jnp.int32)
    pos_s = cache_a_k.shape[2] + jnp.arange(x_sample.shape[1], dtype=jnp.int32)
    xp, xs = x_prompt, x_sample
    rows_p, rows_s = [], []
    for l in range(DEPTH):
        wl = (w_in[l], w_out[l], conv_b_w[l], mla_kv_norm[l], mla_w_uk[l], mla_w_uv[l],
              w_mod[l], b_mod[l], ln1_g[l], ln1_b[l], ln2_g[l], ln2_b[l],
              ffn_w_gu[l], ffn_conv_w[l], ffn_w_down[l])
        xp, rp = trunk_layer(xp, c_prompt, pos_p, None, *wl)
        past_l = (cache_a_k[l], cache_a_v[l], cache_idx_k[l], cache_mla_latent[l],
                  cache_mla_krope[l], state_conv_b[l], state_conv_ffn[l])
        xs, rs = trunk_layer(xs, c_sample, pos_s, past_l, *wl)
        rows_p.append(rp)
        rows_s.append(rs)
    p_a_k, p_a_v, p_idx_k, p_mla_latent, p_mla_krope, p_conv_b, p_conv_ffn = [jnp.stack(r) for r in zip(*rows_p)]
    s_a_k, s_a_v, s_idx_k, s_mla_latent, s_mla_krope, s_conv_b, s_conv_ffn = [jnp.stack(r) for r in zip(*rows_s)]
    return (xp, xs,
            p_a_k, p_a_v, p_idx_k, p_mla_latent, p_mla_krope, p_conv_b, p_conv_ffn,
            s_a_k, s_a_v, s_idx_k, s_mla_latent, s_mla_krope, s_conv_b, s_conv_ffn)
```

```python
import functools
from typing import NamedTuple

import numpy as np
import jax
import jax.numpy as jnp
from jax import lax
from jax.experimental import pallas as pl
from jax.experimental.pallas import tpu as pltpu

F32, BF16, I32 = jnp.float32, jnp.bfloat16, jnp.int32

CHUNK = 64
CONV_W = 3
ROPE_THETA = 10000.0
HEAD = 64
IDX_HEADS = 16
TOPK_MAX = 256
C_NOPE = 128
C_V = 128
N_MOD = 6
LN_EPS = 1e-5
RMS_EPS = 1e-6

LANE = 128
SUBLANE = 8
VMEM_LIMIT = 50 * 1024 * 1024

MASKED = -1e30
INT_MIN = -2 ** 31
KEY_NEG_INF = int(np.array(-np.inf, np.float32).view(np.int32)) ^ 0x7FFFFFFF


class Dims(NamedTuple):
    d: int
    aw: int
    bw: int
    ch: int
    r: int
    dff: int
    depth: int


class Group(NamedTuple):
    b: int
    t: int
    past: int


def _row_tile(m, pref):
    if m <= pref:
        return m
    t = pref - pref % SUBLANE
    while m % t:
        t -= SUBLANE
    return t


def _key_tile(l):
    lp = -(-l // LANE) * LANE
    for tk in (512, 384, 256, 128):
        if lp % tk == 0:
            return lp, tk
    raise AssertionError(lp)


def _params(sem):
    return pltpu.CompilerParams(dimension_semantics=sem, vmem_limit_bytes=VMEM_LIMIT)


def _resident(shape, index_map):
    return pl.BlockSpec(shape, index_map, pipeline_mode=pl.Buffered(1))


def _mod_kernel(c_ref, w_ref, b_ref, o_ref):
    a = jax.nn.silu(c_ref[...]).astype(BF16)
    o_ref[...] = jnp.dot(a, w_ref[...].astype(BF16), preferred_element_type=F32) + b_ref[...]


def _modulation(c, w_mod, b_mod):
    depth, d, n = w_mod.shape
    rows = c.shape[0]
    tn = _row_tile(n, 1024)
    return pl.pallas_call(
        _mod_kernel,
        grid=(depth, n // tn),
        in_specs=[pl.BlockSpec((rows, d), lambda l, j: (0, 0)),
                  pl.BlockSpec((None, d, tn), lambda l, j: (l, 0, j)),
                  pl.BlockSpec((None, 1, tn), lambda l, j: (l, 0, j))],
        out_specs=pl.BlockSpec((None, rows, tn), lambda l, j: (l, 0, j)),
        out_shape=jax.ShapeDtypeStruct((depth, rows, n), F32),
        compiler_params=_params(("arbitrary", "arbitrary")),
        name="modulation",
    )(c, w_mod, b_mod.reshape(depth, 1, n))


def _pack_w_in(w, dm):
    d, aw, bw, ch, r = dm.d, dm.aw, dm.bw, dm.ch, dm.r
    o = np.cumsum([0, aw, aw, aw, IDX_HEADS * HEAD, HEAD, IDX_HEADS, bw, bw, bw, ch * (C_NOPE + HEAD), r, HEAD])
    qa, ka, va, qi, ki, wi, bg, cg, xb, qc, lat, kr = [w[:, o[i]:o[i + 1]] for i in range(12)]
    qc = qc.reshape(d, ch, C_NOPE + HEAD)
    qcn = qc[:, :, :C_NOPE].reshape(d, ch * C_NOPE)
    qcr = qc[:, :, C_NOPE:].reshape(d, ch * HEAD)
    pad = jnp.zeros((d, LANE - IDX_HEADS), w.dtype)
    return jnp.concatenate([qa, ka, qi, qcr, ki, ki, kr, kr, va, bg, cg, xb, qcn, lat, wi, pad],
                           axis=1).astype(BF16)


def _rope(acc, cos, s1, s2):
    outs = []
    for s in range(acc.shape[1] // LANE):
        xs = acc[:, s * LANE:(s + 1) * LANE]
        outs.append(xs * cos + pltpu.roll(xs, HEAD // 2, 1) * s1 + pltpu.roll(xs, LANE - HEAD // 2, 1) * s2)
    return outs[0] if len(outs) == 1 else jnp.concatenate(outs, axis=1)


def _inproj_kernel(x_ref, sc_ref, sh_ref, w_ref, cos_ref, s1_ref, s2_ref, nrm_ref,
                   qa_ref, ka_ref, kab_ref, qi_ref, qcr_ref, ki_ref, kib_ref, kr_ref, krb_ref,
                   va_ref, vab_ref, bg_ref, u_ref, qcn_ref, lat_ref, latb_ref, wi_ref, *, dm):
    aw, bw, ch, r = dm.aw, dm.bw, dm.ch, dm.r
    h = (x_ref[...] * (1.0 + sc_ref[...]) + sh_ref[...]).astype(BF16)
    cos, s1, s2 = cos_ref[...], s1_ref[...], s2_ref[...]
    col = [0]

    def proj(width):
        c0 = col[0]
        col[0] = c0 + width
        return jnp.dot(h, w_ref[:, c0:c0 + width], preferred_element_type=F32)

    def pieces(width, step=512):
        return [(o, min(step, width - o)) for o in range(0, width, step)]

    for o, wd in pieces(aw):
        qa_ref[:, o:o + wd] = _rope(proj(wd), cos, s1, s2).astype(BF16)
    for o, wd in pieces(aw):
        y = _rope(proj(wd), cos, s1, s2)
        ka_ref[:, o:o + wd] = y
        kab_ref[:, o:o + wd] = y.astype(BF16)
    for o, wd in pieces(IDX_HEADS * HEAD):
        qi_ref[:, o:o + wd] = _rope(proj(wd), cos, s1, s2).astype(BF16)
    for o, wd in pieces(ch * HEAD):
        qcr_ref[:, o:o + wd] = _rope(proj(wd), cos, s1, s2).astype(BF16)
    for f32_ref, b16_ref in ((ki_ref, kib_ref), (kr_ref, krb_ref)):
        y = _rope(proj(LANE), cos, s1, s2)
        f32_ref[...] = y[:, :HEAD]
        b16_ref[...] = y.astype(BF16)
    for o, wd in pieces(aw):
        y = proj(wd)
        va_ref[:, o:o + wd] = y
        vab_ref[:, o:o + wd] = y.astype(BF16)
    for o, wd in pieces(bw):
        bg_ref[:, o:o + wd] = proj(wd)
    c_cg = col[0]
    for o, wd in pieces(bw):
        cg = jnp.dot(h, w_ref[:, c_cg + o:c_cg + o + wd], preferred_element_type=F32)
        xb = jnp.dot(h, w_ref[:, c_cg + bw + o:c_cg + bw + o + wd], preferred_element_type=F32)
        u_ref[:, o:o + wd] = cg * xb
    col[0] = c_cg + 2 * bw
    for o, wd in pieces(ch * C_NOPE):
        qcn_ref[:, o:o + wd] = proj(wd).astype(BF16)
    lat = proj(r)
    lat = lat * lax.rsqrt(jnp.mean(lat * lat, axis=-1, keepdims=True) + RMS_EPS) * nrm_ref[...]
    lat_ref[...] = lat
    latb_ref[...] = lat.astype(BF16)
    wi_ref[...] = proj(LANE)[:, :IDX_HEADS] * (IDX_HEADS ** -0.5)


def _in_projection(x, sc, sh, w_packed, tables, kv_norm, dm, tm):
    m, d = x.shape
    aw, bw, ch, r = dm.aw, dm.bw, dm.ch, dm.r
    npk = w_packed.shape[1]
    n_i = m // tm
    tiles_per_mod = n_i // sc.shape[0]
    mod_rows = sc.shape[1]
    tab_tiles = tables[0].shape[0] // tm
    widths = [(aw, BF16), (aw, F32), (aw, BF16), (IDX_HEADS * HEAD, BF16), (ch * HEAD, BF16),
              (HEAD, F32), (LANE, BF16), (HEAD, F32), (LANE, BF16),
              (aw, F32), (aw, BF16), (bw, F32), (bw, F32), (ch * C_NOPE, BF16), (r, F32), (r, BF16),
              (IDX_HEADS, F32)]
    row = lambda i: (i, 0)
    mod_spec = pl.BlockSpec((None, mod_rows, d), lambda i: (i // tiles_per_mod, 0, 0))
    tab_spec = pl.BlockSpec((tm, LANE), lambda i: (i % tab_tiles, 0))
    return pl.pallas_call(
        functools.partial(_inproj_kernel, dm=dm),
        grid=(n_i,),
        in_specs=[pl.BlockSpec((tm, d), row), mod_spec, mod_spec,
                  _resident((d, npk), lambda i: (0, 0)),
                  tab_spec, tab_spec, tab_spec,
                  pl.BlockSpec((1, r), lambda i: (0, 0))],
        out_specs=[pl.BlockSpec((tm, w), row) for w, _ in widths],
        out_shape=[jax.ShapeDtypeStruct((m, w), dt) for w, dt in widths],
        compiler_params=_params(("arbitrary",)),
        name="in_projection",
    )(x, sc, sh, w_packed, *tables, kv_norm.reshape(1, r))


def _kvup_kernel(l_ref, wk_ref, wv_ref, k_ref, v_ref):
    lat = l_ref[...]
    k_ref[...] = jnp.dot(lat, wk_ref[...], preferred_element_type=F32).astype(BF16)
    v_ref[...] = jnp.dot(lat, wv_ref[...], preferred_element_type=F32).astype(BF16)


def _kv_up(lat, w_uk, w_uv):
    m, r = lat.shape
    n = w_uk.shape[1]
    tm = _row_tile(m, 512)
    return pl.pallas_call(
        _kvup_kernel,
        grid=(m // tm,),
        in_specs=[pl.BlockSpec((tm, r), lambda i: (i, 0)),
                  pl.BlockSpec((r, n), lambda i: (0, 0)),
                  pl.BlockSpec((r, n), lambda i: (0, 0))],
        out_specs=[pl.BlockSpec((tm, n), lambda i: (i, 0))] * 2,
        out_shape=[jax.ShapeDtypeStruct((m, n), BF16)] * 2,
        compiler_params=_params(("arbitrary",)),
        name="latent_up_projection",
    )(lat, w_uk, w_uv)


_NT = (((1,), (1,)), ((), ()))


def _visible_tiles(qpos0, tq, l, tk):
    nvis = jnp.minimum(l, ((qpos0 + tq - 1) // CHUNK + 1) * CHUNK)
    return (nvis + tk - 1) // tk


def _flash_step(carry, s, v):
    m, l, acc = carry
    m_new = jnp.maximum(m, jnp.max(s, axis=1, keepdims=True))
    alpha = jnp.exp(m - m_new)
    p = jnp.exp(s - m_new)
    l = alpha * l + jnp.sum(p, axis=1, keepdims=True)
    acc = alpha * acc + jnp.dot(p.astype(BF16), v, preferred_element_type=F32)
    return m_new, l, acc


def _flash_init(rows, width):
    return (jnp.full((rows, 1), MASKED, F32), jnp.zeros((rows, 1), F32), jnp.zeros((rows, width), F32))


def _ordered_key(x):
    b = pltpu.bitcast(x, I32)
    return jnp.where(b < 0, b ^ 0x7FFFFFFF, b)


def _dsa_kernel(qi_ref, wi_ref, ki_ref, qa_ref, k_ref, v_ref, o_ref, key_scr, bias_scr, thr2_scr,
                *, tq, tk, l, lp, past, topk, aw):
    qpos0 = past + pl.program_id(1) * tq
    ntiles = _visible_tiles(qpos0, tq, l, tk)
    qchunk = (qpos0 + lax.broadcasted_iota(I32, (tq, 1), 0)) // CHUNK
    lane = lax.broadcasted_iota(I32, (1, LANE), 1)
    lo_half = lane < HEAD
    kf = jnp.float32(topk)

    def split_heads(qs):
        zero = jnp.zeros_like(qs)
        return jnp.concatenate([jnp.where(lo_half, qs, zero), jnp.where(lo_half, zero, qs)], axis=0)

    wi = wi_ref[...] * (HEAD ** -0.5)
    q_idx = [split_heads(qi_ref[:, s * LANE:(s + 1) * LANE]) for s in range(IDX_HEADS // 2)]
    w_idx = [wi[:, h:h + 1] for h in range(IDX_HEADS)]

    def score_body(kt, _):
        k0 = pl.multiple_of(kt * tk, tk)
        kt_tile = ki_ref[pl.ds(k0, tk), :]
        score = jnp.zeros((tq, tk), F32)
        for s in range(IDX_HEADS // 2):
            s2 = lax.dot_general(q_idx[s], kt_tile, _NT, preferred_element_type=F32)
            score = score + jnp.maximum(s2[:tq], 0.0) * w_idx[2 * s]
            score = score + jnp.maximum(s2[tq:], 0.0) * w_idx[2 * s + 1]
        kpos = k0 + lax.broadcasted_iota(I32, (1, tk), 1)
        vis = (kpos // CHUNK <= qchunk) & (kpos < l)
        key_scr[kt] = _ordered_key(jnp.where(vis, score, -jnp.inf))
        return 0

    lax.fori_loop(0, ntiles, score_body, 0)

    def count(pred):
        def body(kt, acc):
            for j in range(tk // LANE):
                ks = key_scr[kt, :, j * LANE:(j + 1) * LANE]
                acc = acc + jnp.where(pred(ks, kt * tk + j * LANE), 1.0, 0.0)
            return acc
        acc = lax.fori_loop(0, ntiles, body, jnp.zeros((tq, LANE), F32))
        return jnp.sum(acc, axis=1, keepdims=True)

    def wide(col):
        return jnp.broadcast_to(col, (tq, LANE))

    def bit_body(b, lo):
        cand = lo + jnp.left_shift(jnp.int32(1), 31 - b)
        cand_w = wide(cand)
        c = count(lambda ks, _: ks >= cand_w)
        return jnp.where(c >= kf, cand, lo)

    thr = lax.fori_loop(0, 32, bit_body, jnp.full((tq, 1), INT_MIN, I32))
    thr_w = wide(thr)

    n_ge = count(lambda ks, _: ks >= thr_w)
    n_gt = count(lambda ks, _: ks > thr_w)
    excess = (n_ge > kf) & (thr > KEY_NEG_INF)
    thr2_scr[...] = jnp.zeros((tq, 1), I32)

    @pl.when(jnp.max(jnp.where(excess, 1.0, 0.0)) > 0.0)
    def _():
        need = kf - n_gt
        nbits = lp.bit_length()

        def bit2_body(b, lo):
            cand = lo + jnp.left_shift(jnp.int32(1), nbits - 1 - b)
            cand_w = wide(cand)
            c = count(lambda ks, base: jnp.where(ks == thr_w, lp - (base + lane), 0) >= cand_w)
            return jnp.where(c >= need, cand, lo)

        thr2_scr[...] = lax.fori_loop(0, nbits, bit2_body, jnp.zeros((tq, 1), I32))

    thr2_w = wide(thr2_scr[...])

    def bias_body(kt, _):
        for j in range(tk // LANE):
            ks = key_scr[kt, :, j * LANE:(j + 1) * LANE]
            tie = jnp.where(lp - (kt * tk + j * LANE + lane) >= thr2_w, 0.0, MASKED)
            bias = jnp.where(ks > thr_w, 0.0, jnp.where(ks == thr_w, tie, MASKED))
            bias_scr[kt, :, j * LANE:(j + 1) * LANE] = jnp.where(ks > KEY_NEG_INF, bias, MASKED)
        return 0

    lax.fori_loop(0, ntiles, bias_body, 0)

    for pr in range(aw // LANE):
        cols = slice(pr * LANE, (pr + 1) * LANE)
        q2 = split_heads(qa_ref[:, cols])

        def att_body(kt, carry, cols=cols, q2=q2):
            k0 = pl.multiple_of(kt * tk, tk)
            s = lax.dot_general(q2, k_ref[pl.ds(k0, tk), cols], _NT, preferred_element_type=F32)
            bias = bias_scr[kt]
            s = s * (HEAD ** -0.5) + jnp.concatenate([bias, bias], axis=0)
            return _flash_step(carry, s, v_ref[pl.ds(k0, tk), cols])

        _, den, acc = lax.fori_loop(0, ntiles, att_body, _flash_init(2 * tq, LANE))
        o2 = acc / den
        o_ref[:, cols] = jnp.where(lo_half, o2[:tq], o2[tq:]).astype(BF16)


def _dsa_attention(qi, wi, ki, qa, k, v, grp, l, tq):
    b, t, aw = qa.shape
    lp = k.shape[1]
    _, tk = _key_tile(l)
    topk = min(TOPK_MAX, l // 4)
    qblk = lambda w: pl.BlockSpec((None, tq, w), lambda i, j: (i, j, 0))
    kblk = lambda w: pl.BlockSpec((None, lp, w), lambda i, j: (i, 0, 0))
    return pl.pallas_call(
        functools.partial(_dsa_kernel, tq=tq, tk=tk, l=l, lp=lp, past=grp.past, topk=topk, aw=aw),
        grid=(b, t // tq),
        in_specs=[qblk(IDX_HEADS * HEAD), qblk(IDX_HEADS), kblk(LANE), qblk(aw), kblk(aw), kblk(aw)],
        out_specs=qblk(aw),
        out_shape=jax.ShapeDtypeStruct((b, t, aw), BF16),
        scratch_shapes=[pltpu.VMEM((lp // tk, tq, tk), I32),
                        pltpu.VMEM((lp // tk, tq, tk), F32),
                        pltpu.VMEM((tq, 1), I32)],
        compiler_params=_params(("arbitrary", "arbitrary")),
        name="dsa_attention",
    )(qi, wi, ki, qa, k, v)


def _mla_kernel(qn_ref, qr_ref, kn_ref, v_ref, kr_ref, o_ref, *, tq, tk, l, past, ch):
    qpos0 = past + pl.program_id(1) * tq
    ntiles = _visible_tiles(qpos0, tq, l, tk)
    qchunk = (qpos0 + lax.broadcasted_iota(I32, (tq, 1), 0)) // CHUNK
    lo_half = lax.broadcasted_iota(I32, (1, LANE), 1) < HEAD
    scale = (C_NOPE + HEAD) ** -0.5
    for h in range(ch):
        cols = slice(h * LANE, (h + 1) * LANE)
        qs = qr_ref[:, (h // 2) * LANE:(h // 2 + 1) * LANE]
        zero = jnp.zeros_like(qs)
        qr = jnp.where(lo_half, qs, zero) if h % 2 == 0 else jnp.where(lo_half, zero, qs)
        qcat = jnp.concatenate([qn_ref[:, cols], qr], axis=1)

        def body(kt, carry, cols=cols, qcat=qcat):
            k0 = pl.multiple_of(kt * tk, tk)
            kcat = jnp.concatenate([kn_ref[pl.ds(k0, tk), cols], kr_ref[pl.ds(k0, tk), :]], axis=1)
            s = lax.dot_general(qcat, kcat, _NT, preferred_element_type=F32) * scale
            kpos = k0 + lax.broadcasted_iota(I32, (1, tk), 1)
            vis = (kpos // CHUNK <= qchunk) & (kpos < l)
            return _flash_step(carry, jnp.where(vis, s, MASKED), v_ref[pl.ds(k0, tk), cols])

        _, den, acc = lax.fori_loop(0, ntiles, body, _flash_init(tq, LANE))
        o_ref[:, cols] = (acc / den).astype(BF16)


def _mla_attention(qn, qr, kn, v, kr, grp, l, tq):
    b, t, wn = qn.shape
    ch = wn // C_NOPE
    lp = kn.shape[1]
    _, tk = _key_tile(l)
    qblk = lambda w: pl.BlockSpec((None, tq, w), lambda i, j: (i, j, 0))
    kblk = lambda w: _resident((None, lp, w), lambda i, j: (i, 0, 0))
    return pl.pallas_call(
        functools.partial(_mla_kernel, tq=tq, tk=tk, l=l, past=grp.past, ch=ch),
        grid=(b, t // tq),
        in_specs=[qblk(wn), qblk(ch * HEAD), kblk(wn), kblk(wn), kblk(LANE)],
        out_specs=qblk(wn),
        out_shape=jax.ShapeDtypeStruct((b, t, wn), BF16),
        compiler_params=_params(("arbitrary", "arbitrary")),
        name="mla_attention",
    )(qn, qr, kn, v, kr)


def _causal_conv(u, e0, e1, w, seg):
    rmod = lax.broadcasted_iota(I32, (u.shape[0], 1), 0) % seg
    u1 = jnp.where(rmod == 0, e1, pltpu.roll(u, 1, 0))
    u2 = jnp.where(rmod == 0, e0, jnp.where(rmod == 1, e1, pltpu.roll(u, 2, 0)))
    return u2 * w[0:1] + u1 * w[1:2] + u * w[2:3]


def _layer_norm(z, g, b):
    mu = jnp.mean(z, axis=-1, keepdims=True)
    zc = z - mu
    var = jnp.mean(zc * zc, axis=-1, keepdims=True)
    return zc * lax.rsqrt(var + LN_EPS) * g + b


def _outproj_kernel(*refs, alpha, seq_tiles, seg, carried):
    if carried:
        (x_ref, oa_ref, bg_ref, u_ref, oc_ref, w_ref, cw_ref, g1_ref, lng_ref, lnb_ref, sc2_ref, sh2_ref,
         x1_ref, h2_ref, prev_scr) = refs

        @pl.when(pl.program_id(0) % seq_tiles == 0)
        def _():
            prev_scr[...] = jnp.zeros_like(prev_scr)

        e0, e1 = prev_scr[SUBLANE - 2:SUBLANE - 1, :], prev_scr[SUBLANE - 1:SUBLANE, :]
    else:
        (x_ref, oa_ref, bg_ref, u_ref, oc_ref, w_ref, cw_ref, g1_ref, lng_ref, lnb_ref, sc2_ref, sh2_ref,
         e0_ref, e1_ref, x1_ref, h2_ref) = refs
        e0, e1 = e0_ref[...], e1_ref[...]
    u = u_ref[...]
    yb = bg_ref[...] * _causal_conv(u, e0, e1, cw_ref[...], seg)
    if carried:
        prev_scr[...] = u[u.shape[0] - SUBLANE:, :]
    mixed = jnp.concatenate([oa_ref[...], yb.astype(BF16), oc_ref[...]], axis=1)
    mix = jnp.dot(mixed, w_ref[...], preferred_element_type=F32)
    x1 = _layer_norm(alpha * x_ref[...] + (1.0 + g1_ref[...]) * mix, lng_ref[...], lnb_ref[...])
    x1_ref[...] = x1
    h2_ref[...] = (x1 * (1.0 + sc2_ref[...]) + sh2_ref[...]).astype(BF16)


def _out_projection(x, oa, bg, u, oc, w_out, conv_w, g1, ln_g, ln_b, sc2, sh2, prev, grp, alpha, tm):
    m, d = x.shape
    aw, bw, cw = oa.shape[1], bg.shape[1], oc.shape[1]
    n_i = m // tm
    tiles_per_mod = n_i // g1.shape[0]
    mod_rows = g1.shape[1]
    row = lambda i: (i, 0)
    fix = lambda i: (0, 0)
    mod_spec = pl.BlockSpec((None, mod_rows, d), lambda i: (i // tiles_per_mod, 0, 0))
    in_specs = [pl.BlockSpec((tm, d), row), pl.BlockSpec((tm, aw), row), pl.BlockSpec((tm, bw), row),
                pl.BlockSpec((tm, bw), row), pl.BlockSpec((tm, cw), row),
                _resident((aw + bw + cw, d), fix), pl.BlockSpec((CONV_W, bw), fix),
                mod_spec, pl.BlockSpec((1, d), fix), pl.BlockSpec((1, d), fix), mod_spec, mod_spec]
    args = [x, oa, bg, u, oc, w_out, conv_w, g1, ln_g.reshape(1, d), ln_b.reshape(1, d), sc2, sh2]
    scratch = []
    if prev is None:
        assert grp.t % tm == 0
        scratch = [pltpu.VMEM((SUBLANE, bw), F32)]
    else:
        assert tm % grp.t == 0
        in_specs += [pl.BlockSpec((tm, bw), row), pl.BlockSpec((tm, bw), row)]
        args += [prev[0], prev[1]]
    return pl.pallas_call(
        functools.partial(_outproj_kernel, alpha=alpha, seq_tiles=max(grp.t // tm, 1), seg=min(grp.t, tm),
                          carried=prev is None),
        grid=(n_i,),
        in_specs=in_specs,
        out_specs=[pl.BlockSpec((tm, d), row), pl.BlockSpec((tm, d), row)],
        out_shape=[jax.ShapeDtypeStruct((m, d), F32), jax.ShapeDtypeStruct((m, d), BF16)],
        scratch_shapes=scratch,
        compiler_params=_params(("arbitrary",)),
        name="out_projection",
    )(*args)


def _ffn_kernel(*refs, alpha, seq_tiles, seg, carried):
    if carried:
        (h_ref, x_ref, wg_ref, wu_ref, wd_ref, cw_ref, g2_ref, lng_ref, lnb_ref,
         o_ref, gt_ref, acc_scr, prev_scr) = refs
    else:
        (h_ref, x_ref, wg_ref, wu_ref, wd_ref, cw_ref, g2_ref, lng_ref, lnb_ref, e0_ref, e1_ref,
         o_ref, gt_ref, acc_scr) = refs
    f = pl.program_id(1)
    h = h_ref[...]
    gate = jnp.dot(h, wg_ref[...], preferred_element_type=F32)
    up = jnp.dot(h, wu_ref[...], preferred_element_type=F32)
    tm = gate.shape[0]
    if carried:
        @pl.when(pl.program_id(0) % seq_tiles == 0)
        def _():
            prev_scr[f] = jnp.zeros(prev_scr.shape[1:], F32)

        e0, e1 = prev_scr[f, SUBLANE - 2:SUBLANE - 1, :], prev_scr[f, SUBLANE - 1:SUBLANE, :]
    else:
        e0, e1 = e0_ref[...], e1_ref[...]
    conv = _causal_conv(gate, e0, e1, cw_ref[...], seg)
    if carried:
        prev_scr[f] = gate[tm - SUBLANE:, :]
    gt_ref[...] = gate[tm - gt_ref.shape[0]:, :]
    act = (jax.nn.silu(conv) * up).astype(BF16)
    part = jnp.dot(act, wd_ref[...], preferred_element_type=F32)

    @pl.when(f == 0)
    def _():
        acc_scr[...] = part

    @pl.when(f > 0)
    def _():
        acc_scr[...] += part

    @pl.when(f == pl.num_programs(1) - 1)
    def _():
        z = alpha * x_ref[...] + (1.0 + g2_ref[...]) * acc_scr[...]
        o_ref[...] = _layer_norm(z, lng_ref[...], lnb_ref[...])


def _channel_mixer(h2, x1, w_gu, w_down, conv_w, g2, ln_g, ln_b, prev, grp, alpha, tm, tf):
    m, d = x1.shape
    dff = w_down.shape[0]
    n_i, n_f = m // tm, dff // tf
    tiles_per_mod = n_i // g2.shape[0]
    mod_rows = g2.shape[1]
    fix = lambda i, f: (0, 0)
    in_specs = [pl.BlockSpec((tm, d), lambda i, f: (i, 0)), pl.BlockSpec((tm, d), lambda i, f: (i, 0)),
                pl.BlockSpec((d, tf), lambda i, f: (0, f)), pl.BlockSpec((d, tf), lambda i, f: (0, n_f + f)),
                pl.BlockSpec((tf, d), lambda i, f: (f, 0)), pl.BlockSpec((CONV_W, tf), lambda i, f: (0, f)),
                pl.BlockSpec((None, mod_rows, d), lambda i, f: (i // tiles_per_mod, 0, 0)),
                pl.BlockSpec((1, d), fix), pl.BlockSpec((1, d), fix)]
    args = [h2, x1, w_gu, w_gu, w_down, conv_w, g2, ln_g.reshape(1, d), ln_b.reshape(1, d)]
    scratch = [pltpu.VMEM((tm, d), F32)]
    if prev is None:
        assert grp.t % tm == 0
        scratch.append(pltpu.VMEM((n_f, SUBLANE, tf), F32))
        gt_spec = pl.BlockSpec((None, SUBLANE, tf), lambda i, f: (i, 0, f))
        gt_shape = jax.ShapeDtypeStruct((n_i, SUBLANE, dff), F32)
    else:
        assert tm % grp.t == 0
        in_specs += [pl.BlockSpec((tm, tf), lambda i, f: (i, f))] * 2
        args += [prev[0], prev[1]]
        gt_spec = pl.BlockSpec((tm, tf), lambda i, f: (i, f))
        gt_shape = jax.ShapeDtypeStruct((m, dff), F32)
    return pl.pallas_call(
        functools.partial(_ffn_kernel, alpha=alpha, seq_tiles=max(grp.t // tm, 1), seg=min(grp.t, tm),
                          carried=prev is None),
        grid=(n_i, n_f),
        in_specs=in_specs,
        out_specs=[pl.BlockSpec((tm, d), lambda i, f: (i, 0)), gt_spec],
        out_shape=[jax.ShapeDtypeStruct((m, d), F32), gt_shape],
        scratch_shapes=scratch,
        compiler_params=_params(("arbitrary", "arbitrary")),
        name="channel_mixer",
    )(*args)


def _rope_tables(pos):
    half = HEAD // 2
    inv = jnp.power(jnp.float32(ROPE_THETA), -jnp.arange(half, dtype=F32) / half)
    ang = pos.astype(F32)[:, None] * inv[None, :]
    cos, sin = jnp.cos(ang), jnp.sin(ang)
    zero = jnp.zeros_like(sin)
    reps = LANE // HEAD
    return (jnp.tile(jnp.concatenate([cos, cos], axis=1), (1, reps)),
            jnp.tile(jnp.concatenate([zero, sin], axis=1), (1, reps)),
            jnp.tile(jnp.concatenate([-sin, zero], axis=1), (1, reps)))


def _with_past(past, new, lp, twice=False):
    b, t, w = new.shape
    parts = [new]
    if past is not None:
        p = past.reshape(b, past.shape[1], -1).astype(BF16)
        parts = [jnp.concatenate([p, p], axis=-1) if twice else p, new]
    n = sum(a.shape[1] for a in parts)
    if lp > n:
        parts.append(jnp.zeros((b, lp - n, w), BF16))
    return parts[0] if len(parts) == 1 else jnp.concatenate(parts, axis=1)


def _layer(x, mod, tables, grp, past, wts, dm, alpha):
    (w_in_p, w_out, conv_b_w, kv_norm, w_uk, w_uv, ln1_g, ln1_b, ln2_g, ln2_b, w_gu, conv_f_w, w_down) = wts
    b, t = grp.b, grp.t
    m, d = x.shape
    carried = past is None
    tm = _row_tile(t, 256) if carried else m
    if carried:
        mods = [a.reshape(b, 1, d) for a in jnp.split(mod, N_MOD, axis=-1)]
    else:
        mods = [jnp.repeat(a, t, axis=0).reshape(1, m, d) for a in jnp.split(mod, N_MOD, axis=-1)]
    sh1, sc1, g1, sh2, sc2, g2 = mods

    (qa, ka, kab, qi, qcr, ki, kib, kr, krb, va, vab, bg, u, qcn, lat, latb, wi) = _in_projection(
        x, sc1, sh1, w_in_p, tables, kv_norm, dm, tm)

    l = grp.past + t
    lp, _ = _key_tile(l)
    three = lambda a: a.reshape(b, t, a.shape[-1])
    p_ak, p_av, p_ik, p_lat, p_kr, prev_b, prev_f = past if past is not None else (None,) * 7
    k_all = _with_past(p_ak, three(kab), lp)
    v_all = _with_past(p_av, three(vab), lp)
    ki_all = _with_past(p_ik, three(kib), lp, twice=True)
    kr_all = _with_past(p_kr, three(krb), lp, twice=True)
    lat_all = _with_past(p_lat, three(latb), lp)

    kn, vc = _kv_up(lat_all.reshape(b * lp, dm.r), w_uk, w_uv)
    kn, vc = kn.reshape(b, lp, -1), vc.reshape(b, lp, -1)

    tq_a = _row_tile(t, 128)
    tq_c = _row_tile(t, 256)
    oa = _dsa_attention(three(qi), three(wi), ki_all, three(qa), k_all, v_all, grp, l, tq_a)
    oc = _mla_attention(three(qcn), three(qcr), kn, vc, kr_all, grp, l, tq_c)

    def expand(state):
        return jnp.repeat(state[:, 0], t, axis=0), jnp.repeat(state[:, 1], t, axis=0)

    x1, h2 = _out_projection(x, oa.reshape(m, -1), bg, u, oc.reshape(m, -1), w_out, conv_b_w, g1, ln1_g, ln1_b,
                             sc2, sh2, None if carried else expand(prev_b), grp, alpha, tm)
    tm_f = _row_tile(t, 512) if carried else m
    tf = _row_tile(dm.dff, 512)
    x2, gate_rows = _channel_mixer(h2, x1, w_gu, w_down, conv_f_w, g2, ln2_g, ln2_b,
                                   None if carried else expand(prev_f), grp, alpha, tm_f, tf)
    if carried:
        new_f = gate_rows.reshape(b, t // tm_f, SUBLANE, dm.dff)[:, -1, SUBLANE - (CONV_W - 1):, :]
    else:
        new_f = gate_rows.reshape(b, t, dm.dff)[:, t - (CONV_W - 1):, :]
    new_b = u.reshape(b, t, dm.bw)[:, t - (CONV_W - 1):, :]
    heads = dm.aw // HEAD
    rows = (ka.reshape(b, t, heads, HEAD), va.reshape(b, t, heads, HEAD), ki.reshape(b, t, HEAD),
            lat.reshape(b, t, dm.r), kr.reshape(b, t, HEAD), new_b, new_f)
    return x2, rows


def kernel(x_prompt, x_sample, c_prompt, c_sample, cache_a_k, cache_a_v, cache_idx_k, cache_mla_latent,
           cache_mla_krope, state_conv_b, state_conv_ffn, w_in, w_out, conv_b_w, mla_kv_norm, mla_w_uk,
           mla_w_uv, w_mod, b_mod, ln1_g, ln1_b, ln2_g, ln2_b, ffn_w_gu, ffn_conv_w, ffn_w_down):
    depth, d, _ = w_in.shape
    a_heads = cache_a_k.shape[3]
    dm = Dims(d=d, aw=a_heads * HEAD, bw=conv_b_w.shape[2], ch=mla_w_uk.shape[2] // C_NOPE,
              r=mla_w_uk.shape[1], dff=ffn_w_down.shape[1], depth=depth)
    alpha = (2 * depth) ** 0.25
    grp_p = Group(b=x_prompt.shape[0], t=x_prompt.shape[1], past=0)
    grp_s = Group(b=x_sample.shape[0], t=x_sample.shape[1], past=cache_a_k.shape[2])

    n_c = grp_p.b + grp_s.b
    c_all = jnp.concatenate([c_prompt, c_sample, jnp.zeros((-n_c % SUBLANE, d), F32)], axis=0)
    mod = _modulation(c_all, w_mod, b_mod)

    tab_p = _rope_tables(jnp.arange(grp_p.t, dtype=I32))
    tab_s = tuple(jnp.tile(a, (grp_s.b, 1)) for a in _rope_tables(grp_s.past + jnp.arange(grp_s.t, dtype=I32)))

    xp = x_prompt.reshape(grp_p.b * grp_p.t, d)
    xs = x_sample.reshape(grp_s.b * grp_s.t, d)
    rows_p, rows_s = [], []
    for l in range(depth):
        wts = (_pack_w_in(w_in[l], dm), w_out[l].astype(BF16), conv_b_w[l], mla_kv_norm[l],
               mla_w_uk[l].astype(BF16), mla_w_uv[l].astype(BF16), ln1_g[l], ln1_b[l], ln2_g[l], ln2_b[l],
               ffn_w_gu[l].astype(BF16), ffn_conv_w[l], ffn_w_down[l].astype(BF16))
        xp, rp = _layer(xp, mod[l, :grp_p.b], tab_p, grp_p, None, wts, dm, alpha)
        past_l = (cache_a_k[l], cache_a_v[l], cache_idx_k[l], cache_mla_latent[l], cache_mla_krope[l],
                  state_conv_b[l], state_conv_ffn[l])
        xs, rs = _layer(xs, mod[l, grp_p.b:n_c], tab_s, grp_s, past_l, wts, dm, alpha)
        rows_p.append(rp)
        rows_s.append(rs)
    outs_p = [jnp.stack(r) for r in zip(*rows_p)]
    outs_s = [jnp.stack(r) for r in zip(*rows_s)]
    return (xp.reshape(x_prompt.shape), xs.reshape(x_sample.shape), *outs_p, *outs_s)
```

```python
import functools
from typing import NamedTuple

import numpy as np
import jax
import jax.numpy as jnp
from jax import lax
from jax.experimental import pallas as pl
from jax.experimental.pallas import tpu as pltpu

F32, BF16, I32 = jnp.float32, jnp.bfloat16, jnp.int32

CHUNK = 64
CONV_W = 3
ROPE_THETA = 10000.0
HEAD = 64
IDX_HEADS = 16
TOPK_MAX = 256
C_NOPE = 128
C_V = 128
N_MOD = 6
LN_EPS = 1e-5
RMS_EPS = 1e-6

LANE = 128
SUBLANE = 8
VMEM_LIMIT = 50 * 1024 * 1024

MASKED = -1e30
LOG2E = 1.4426950408889634
INT_MIN = -2 ** 31
KEY_NEG_INF = int(np.array(-np.inf, np.float32).view(np.int32)) ^ 0x7FFFFFFF


class Dims(NamedTuple):
    d: int
    aw: int
    bw: int
    ch: int
    r: int
    dff: int
    depth: int


class Group(NamedTuple):
    b: int
    t: int
    past: int


def _row_tile(m, pref):
    if m <= pref:
        return m
    t = pref - pref % SUBLANE
    while m % t:
        t -= SUBLANE
    return t


def _key_tile(l):
    lp = -(-l // LANE) * LANE
    for tk in (512, 384, 256, 128):
        if lp % tk == 0:
            return lp, tk
    raise AssertionError(lp)


def _params(sem):
    return pltpu.CompilerParams(dimension_semantics=sem, vmem_limit_bytes=VMEM_LIMIT)


def _resident(shape, index_map):
    return pl.BlockSpec(shape, index_map, pipeline_mode=pl.Buffered(1))


def _mod_kernel(c_ref, w_ref, b_ref, o_ref):
    a = jax.nn.silu(c_ref[...]).astype(BF16)
    o_ref[...] = jnp.dot(a, w_ref[...].astype(BF16), preferred_element_type=F32) + b_ref[...]


def _modulation(c, w_mod, b_mod):
    depth, d, n = w_mod.shape
    rows = c.shape[0]
    tn = _row_tile(n, 1024)
    return pl.pallas_call(
        _mod_kernel,
        grid=(depth, n // tn),
        in_specs=[pl.BlockSpec((rows, d), lambda l, j: (0, 0)),
                  pl.BlockSpec((None, d, tn), lambda l, j: (l, 0, j)),
                  pl.BlockSpec((None, 1, tn), lambda l, j: (l, 0, j))],
        out_specs=pl.BlockSpec((None, rows, tn), lambda l, j: (l, 0, j)),
        out_shape=jax.ShapeDtypeStruct((depth, rows, n), F32),
        compiler_params=_params(("arbitrary", "arbitrary")),
        name="modulation",
    )(c, w_mod, b_mod.reshape(depth, 1, n))


def _pack_w_in(w, dm):
    d, aw, bw, ch, r = dm.d, dm.aw, dm.bw, dm.ch, dm.r
    o = np.cumsum([0, aw, aw, aw, IDX_HEADS * HEAD, HEAD, IDX_HEADS, bw, bw, bw, ch * (C_NOPE + HEAD), r, HEAD])
    qa, ka, va, qi, ki, wi, bg, cg, xb, qc, lat, kr = [w[:, o[i]:o[i + 1]] for i in range(12)]
    qc = qc.reshape(d, ch, C_NOPE + HEAD)
    qcn = qc[:, :, :C_NOPE].reshape(d, ch * C_NOPE)
    qcr = qc[:, :, C_NOPE:].reshape(d, ch * HEAD)
    pad = jnp.zeros((d, LANE - IDX_HEADS), w.dtype)
    return jnp.concatenate([qa, ka, qi, qcr, ki, ki, kr, kr, va, bg, cg, xb, qcn, lat, wi, pad],
                           axis=1).astype(BF16)


def _rope(acc, cos, s1, s2):
    outs = []
    for s in range(acc.shape[1] // LANE):
        xs = acc[:, s * LANE:(s + 1) * LANE]
        outs.append(xs * cos + pltpu.roll(xs, HEAD // 2, 1) * s1 + pltpu.roll(xs, LANE - HEAD // 2, 1) * s2)
    return outs[0] if len(outs) == 1 else jnp.concatenate(outs, axis=1)


def _inproj_kernel(x_ref, sc_ref, sh_ref, w_ref, cos_ref, s1_ref, s2_ref, nrm_ref,
                   qa_ref, ka_ref, kab_ref, qi_ref, qcr_ref, ki_ref, kib_ref, kr_ref, krb_ref,
                   va_ref, vab_ref, bg_ref, u_ref, qcn_ref, lat_ref, latb_ref, wi_ref, vat_ref, *, dm):
    aw, bw, ch, r = dm.aw, dm.bw, dm.ch, dm.r
    h = (x_ref[...] * (1.0 + sc_ref[...]) + sh_ref[...]).astype(BF16)
    cos, s1, s2 = cos_ref[...], s1_ref[...], s2_ref[...]
    col = [0]

    def proj(width):
        c0 = col[0]
        col[0] = c0 + width
        return jnp.dot(h, w_ref[:, c0:c0 + width], preferred_element_type=F32)

    def pieces(width, step=512):
        return [(o, min(step, width - o)) for o in range(0, width, step)]

    for o, wd in pieces(aw):
        qa_ref[:, o:o + wd] = _rope(proj(wd), cos, s1, s2).astype(BF16)
    for o, wd in pieces(aw):
        y = _rope(proj(wd), cos, s1, s2)
        ka_ref[:, o:o + wd] = y
        kab_ref[:, o:o + wd] = y.astype(BF16)
    for o, wd in pieces(IDX_HEADS * HEAD):
        qi_ref[:, o:o + wd] = _rope(proj(wd), cos, s1, s2).astype(BF16)
    for o, wd in pieces(ch * HEAD):
        qcr_ref[:, o:o + wd] = _rope(proj(wd), cos, s1, s2).astype(BF16)
    for f32_ref, b16_ref in ((ki_ref, kib_ref), (kr_ref, krb_ref)):
        y = _rope(proj(LANE), cos, s1, s2)
        f32_ref[...] = y[:, :HEAD]
        b16_ref[...] = y.astype(BF16)
    for o, wd in pieces(aw):
        y = proj(wd)
        va_ref[:, o:o + wd] = y
        vab_ref[:, o:o + wd] = y.astype(BF16)
        vat_ref[o:o + wd, :] = y.T.astype(BF16)
    for o, wd in pieces(bw):
        bg_ref[:, o:o + wd] = proj(wd)
    c_cg = col[0]
    for o, wd in pieces(bw):
        cg = jnp.dot(h, w_ref[:, c_cg + o:c_cg + o + wd], preferred_element_type=F32)
        xb = jnp.dot(h, w_ref[:, c_cg + bw + o:c_cg + bw + o + wd], preferred_element_type=F32)
        u_ref[:, o:o + wd] = cg * xb
    col[0] = c_cg + 2 * bw
    for o, wd in pieces(ch * C_NOPE):
        qcn_ref[:, o:o + wd] = proj(wd).astype(BF16)
    lat = proj(r)
    lat = lat * lax.rsqrt(jnp.mean(lat * lat, axis=-1, keepdims=True) + RMS_EPS) * nrm_ref[...]
    lat_ref[...] = lat
    latb_ref[...] = lat.astype(BF16)
    wi_ref[...] = proj(LANE) * (IDX_HEADS ** -0.5)


def _in_projection(x, sc, sh, w_packed, tables, kv_norm, dm, tm):
    m, d = x.shape
    aw, bw, ch, r = dm.aw, dm.bw, dm.ch, dm.r
    npk = w_packed.shape[1]
    n_i = m // tm
    tiles_per_mod = n_i // sc.shape[0]
    mod_rows = sc.shape[1]
    tab_tiles = tables[0].shape[0] // tm
    widths = [(aw, BF16), (aw, F32), (aw, BF16), (IDX_HEADS * HEAD, BF16), (ch * HEAD, BF16),
              (HEAD, F32), (LANE, BF16), (HEAD, F32), (LANE, BF16),
              (aw, F32), (aw, BF16), (bw, F32), (bw, F32), (ch * C_NOPE, BF16), (r, F32), (r, BF16),
              (LANE, F32)]
    row = lambda i: (i, 0)
    mod_spec = pl.BlockSpec((None, mod_rows, d), lambda i: (i // tiles_per_mod, 0, 0))
    tab_spec = pl.BlockSpec((tm, LANE), lambda i: (i % tab_tiles, 0))
    return pl.pallas_call(
        functools.partial(_inproj_kernel, dm=dm),
        grid=(n_i,),
        in_specs=[pl.BlockSpec((tm, d), row), mod_spec, mod_spec,
                  _resident((d, npk), lambda i: (0, 0)),
                  tab_spec, tab_spec, tab_spec,
                  pl.BlockSpec((1, r), lambda i: (0, 0))],
        out_specs=[pl.BlockSpec((tm, w), row) for w, _ in widths] + [pl.BlockSpec((aw, tm), lambda i: (0, i))],
        out_shape=[jax.ShapeDtypeStruct((m, w), dt) for w, dt in widths] + [jax.ShapeDtypeStruct((aw, m), BF16)],
        compiler_params=_params(("arbitrary",)),
        name="in_projection",
    )(x, sc, sh, w_packed, *tables, kv_norm.reshape(1, r))


def _kvup_kernel(l_ref, wk_ref, wv_ref, k_ref, v_ref, *, v_transposed):
    lat = l_ref[...]
    k_ref[...] = jnp.dot(lat, wk_ref[...], preferred_element_type=F32).astype(BF16)
    if v_transposed:
        v_ref[...] = lax.dot_general(wv_ref[...], lat, _NT, preferred_element_type=F32).astype(BF16)
    else:
        v_ref[...] = jnp.dot(lat, wv_ref[...], preferred_element_type=F32).astype(BF16)


def _kv_up(lat, w_uk, w_uv, v_transposed):
    m, r = lat.shape
    n = w_uk.shape[1]
    tm = _row_tile(m, 512)
    if v_transposed:
        w_v, v_spec, v_shape = w_uv.T, pl.BlockSpec((n, tm), lambda i: (0, i)), (n, m)
    else:
        w_v, v_spec, v_shape = w_uv, pl.BlockSpec((tm, n), lambda i: (i, 0)), (m, n)
    return pl.pallas_call(
        functools.partial(_kvup_kernel, v_transposed=v_transposed),
        grid=(m // tm,),
        in_specs=[pl.BlockSpec((tm, r), lambda i: (i, 0)),
                  pl.BlockSpec((r, n), lambda i: (0, 0)),
                  pl.BlockSpec(w_v.shape, lambda i: (0, 0))],
        out_specs=[pl.BlockSpec((tm, n), lambda i: (i, 0)), v_spec],
        out_shape=[jax.ShapeDtypeStruct((m, n), BF16), jax.ShapeDtypeStruct(v_shape, BF16)],
        compiler_params=_params(("arbitrary",)),
        name="latent_up_projection",
    )(lat, w_uk, w_v)


_NT = (((1,), (1,)), ((), ()))


def _visible_tiles(qpos0, tq, l, tk):
    nvis = jnp.minimum(l, ((qpos0 + tq - 1) // CHUNK + 1) * CHUNK)
    return (nvis + tk - 1) // tk


def _flash_step(carry, s, v):
    m, l, acc = carry
    m_new = jnp.maximum(m, jnp.max(s, axis=1, keepdims=True))
    alpha = jnp.exp(m - m_new)
    p = jnp.exp(s - m_new)
    l = alpha * l + jnp.sum(p, axis=1, keepdims=True)
    acc = alpha * acc + jnp.dot(p.astype(BF16), v, preferred_element_type=F32)
    return m_new, l, acc


def _flash_init(rows, width):
    return (jnp.full((rows, 1), MASKED, F32), jnp.zeros((rows, 1), F32), jnp.zeros((rows, width), F32))


def _ordered_key(x):
    b = pltpu.bitcast(x, I32)
    return jnp.where(b < 0, b ^ 0x7FFFFFFF, b)


def _dsa_kernel(qi_ref, wi_ref, ki_ref, qa_ref, k_ref, v_ref, o_ref, key_scr, bias_scr, thr2_scr,
                *, tq, tk, l, lp, past, topk, aw):
    qpos0 = past + pl.program_id(1) * tq
    ntiles = _visible_tiles(qpos0, tq, l, tk)
    qchunk = (qpos0 + lax.broadcasted_iota(I32, (tq, 1), 0)) // CHUNK
    lane = lax.broadcasted_iota(I32, (1, LANE), 1)
    lo_half = lane < HEAD
    kf = jnp.float32(topk)

    def split_heads(qs):
        zero = jnp.zeros_like(qs)
        return jnp.concatenate([jnp.where(lo_half, qs, zero), jnp.where(lo_half, zero, qs)], axis=0)

    wi = wi_ref[...] * (HEAD ** -0.5)
    q_idx = [split_heads(qi_ref[:, s * LANE:(s + 1) * LANE]) for s in range(IDX_HEADS // 2)]
    w_idx = [wi[:, h:h + 1] for h in range(IDX_HEADS)]

    def score_body(kt, _):
        k0 = pl.multiple_of(kt * tk, tk)
        kt_tile = ki_ref[pl.ds(k0, tk), :]
        score = jnp.zeros((tq, tk), F32)
        for s in range(IDX_HEADS // 2):
            s2 = lax.dot_general(q_idx[s], kt_tile, _NT, preferred_element_type=F32)
            score = score + jnp.maximum(s2[:tq], 0.0) * w_idx[2 * s]
            score = score + jnp.maximum(s2[tq:], 0.0) * w_idx[2 * s + 1]
        kpos = k0 + lax.broadcasted_iota(I32, (1, tk), 1)
        vis = (kpos // CHUNK <= qchunk) & (kpos < l)
        key_scr[kt] = _ordered_key(jnp.where(vis, score, -jnp.inf))
        return 0

    lax.fori_loop(0, ntiles, score_body, 0)

    def count(pred):
        def body(kt, acc):
            for j in range(tk // LANE):
                ks = key_scr[kt, :, j * LANE:(j + 1) * LANE]
                acc = acc + jnp.where(pred(ks, kt * tk + j * LANE), 1.0, 0.0)
            return acc
        acc = lax.fori_loop(0, ntiles, body, jnp.zeros((tq, LANE), F32))
        return jnp.sum(acc, axis=1, keepdims=True)

    def wide(col):
        return jnp.broadcast_to(col, (tq, LANE))

    def bit_body(b, lo):
        cand = lo + jnp.left_shift(jnp.int32(1), 31 - b)
        cand_w = wide(cand)
        c = count(lambda ks, _: ks >= cand_w)
        return jnp.where(c >= kf, cand, lo)

    thr = lax.fori_loop(0, 32, bit_body, jnp.full((tq, 1), INT_MIN, I32))
    thr_w = wide(thr)

    n_ge = count(lambda ks, _: ks >= thr_w)
    n_gt = count(lambda ks, _: ks > thr_w)
    excess = (n_ge > kf) & (thr > KEY_NEG_INF)
    thr2_scr[...] = jnp.zeros((tq, 1), I32)

    @pl.when(jnp.max(jnp.where(excess, 1.0, 0.0)) > 0.0)
    def _():
        need = kf - n_gt
        nbits = lp.bit_length()

        def bit2_body(b, lo):
            cand = lo + jnp.left_shift(jnp.int32(1), nbits - 1 - b)
            cand_w = wide(cand)
            c = count(lambda ks, base: jnp.where(ks == thr_w, lp - (base + lane), 0) >= cand_w)
            return jnp.where(c >= need, cand, lo)

        thr2_scr[...] = lax.fori_loop(0, nbits, bit2_body, jnp.zeros((tq, 1), I32))

    thr2_w = wide(thr2_scr[...])

    def bias_body(kt, _):
        for j in range(tk // LANE):
            ks = key_scr[kt, :, j * LANE:(j + 1) * LANE]
            tie = jnp.where(lp - (kt * tk + j * LANE + lane) >= thr2_w, 0.0, MASKED)
            bias = jnp.where(ks > thr_w, 0.0, jnp.where(ks == thr_w, tie, MASKED))
            bias_scr[kt, :, j * LANE:(j + 1) * LANE] = jnp.where(ks > KEY_NEG_INF, bias, MASKED)
        return 0

    lax.fori_loop(0, ntiles, bias_body, 0)

    for pr in range(aw // LANE):
        cols = slice(pr * LANE, (pr + 1) * LANE)
        q2 = split_heads(qa_ref[:, cols])

        def att_body(kt, carry, cols=cols, q2=q2):
            k0 = pl.multiple_of(kt * tk, tk)
            s = lax.dot_general(q2, k_ref[pl.ds(k0, tk), cols], _NT, preferred_element_type=F32)
            bias = bias_scr[kt]
            s = s * (HEAD ** -0.5) + jnp.concatenate([bias, bias], axis=0)
            return _flash_step(carry, s, v_ref[pl.ds(k0, tk), cols])

        _, den, acc = lax.fori_loop(0, ntiles, att_body, _flash_init(2 * tq, LANE))
        o2 = acc / den
        o_ref[:, cols] = jnp.where(lo_half, o2[:tq], o2[tq:]).astype(BF16)


def _dsa_attention(qi, wi, ki, qa, k, v, grp, l, tq):
    b, t, aw = qa.shape
    lp = k.shape[1]
    _, tk = _key_tile(l)
    topk = min(TOPK_MAX, l // 4)
    qblk = lambda w: pl.BlockSpec((None, tq, w), lambda i, j: (i, j, 0))
    kblk = lambda w: pl.BlockSpec((None, lp, w), lambda i, j: (i, 0, 0))
    return pl.pallas_call(
        functools.partial(_dsa_kernel, tq=tq, tk=tk, l=l, lp=lp, past=grp.past, topk=topk, aw=aw),
        grid=(b, t // tq),
        in_specs=[qblk(IDX_HEADS * HEAD), qblk(LANE), kblk(LANE), qblk(aw), kblk(aw), kblk(aw)],
        out_specs=qblk(aw),
        out_shape=jax.ShapeDtypeStruct((b, t, aw), BF16),
        scratch_shapes=[pltpu.VMEM((lp // tk, tq, tk), I32),
                        pltpu.VMEM((lp // tk, tq, tk), F32),
                        pltpu.VMEM((tq, 1), I32)],
        compiler_params=_params(("arbitrary", "arbitrary")),
        name="dsa_attention",
    )(qi, wi, ki, qa, k, v)


def _mla_kernel(qn_ref, qr_ref, kn_ref, v_ref, kr_ref, o_ref, *, tq, tk, l, past, ch):
    qpos0 = past + pl.program_id(1) * tq
    ntiles = _visible_tiles(qpos0, tq, l, tk)
    qchunk = (qpos0 + lax.broadcasted_iota(I32, (tq, 1), 0)) // CHUNK
    lo_half = lax.broadcasted_iota(I32, (1, LANE), 1) < HEAD
    scale = (C_NOPE + HEAD) ** -0.5
    for h in range(ch):
        cols = slice(h * LANE, (h + 1) * LANE)
        qs = qr_ref[:, (h // 2) * LANE:(h // 2 + 1) * LANE]
        zero = jnp.zeros_like(qs)
        qr = jnp.where(lo_half, qs, zero) if h % 2 == 0 else jnp.where(lo_half, zero, qs)
        qcat = jnp.concatenate([qn_ref[:, cols], qr], axis=1)

        def body(kt, carry, cols=cols, qcat=qcat):
            k0 = pl.multiple_of(kt * tk, tk)
            kcat = jnp.concatenate([kn_ref[pl.ds(k0, tk), cols], kr_ref[pl.ds(k0, tk), :]], axis=1)
            s = lax.dot_general(qcat, kcat, _NT, preferred_element_type=F32) * scale
            kpos = k0 + lax.broadcasted_iota(I32, (1, tk), 1)
            vis = (kpos // CHUNK <= qchunk) & (kpos < l)
            return _flash_step(carry, jnp.where(vis, s, MASKED), v_ref[pl.ds(k0, tk), cols])

        _, den, acc = lax.fori_loop(0, ntiles, body, _flash_init(tq, LANE))
        o_ref[:, cols] = (acc / den).astype(BF16)


def _mla_attention(qn, qr, kn, v, kr, grp, l, tq):
    b, t, wn = qn.shape
    ch = wn // C_NOPE
    lp = kn.shape[1]
    _, tk = _key_tile(l)
    qblk = lambda w: pl.BlockSpec((None, tq, w), lambda i, j: (i, j, 0))
    kblk = lambda w: _resident((None, lp, w), lambda i, j: (i, 0, 0))
    return pl.pallas_call(
        functools.partial(_mla_kernel, tq=tq, tk=tk, l=l, past=grp.past, ch=ch),
        grid=(b, t // tq),
        in_specs=[qblk(wn), qblk(ch * HEAD), kblk(wn), kblk(wn), kblk(LANE)],
        out_specs=qblk(wn),
        out_shape=jax.ShapeDtypeStruct((b, t, wn), BF16),
        compiler_params=_params(("arbitrary", "arbitrary")),
        name="mla_attention",
    )(qn, qr, kn, v, kr)


ONES_ROWS = 16


def _flash_t_init(m_scr, acc_scr):
    m_scr[...] = jnp.full(m_scr.shape, MASKED, F32)
    acc_scr[...] = jnp.zeros(acc_scr.shape, F32)


def _flash_t_step(h, s, vt, c, m_scr, acc_scr):
    m_old = m_scr[h]
    m_new = jnp.maximum(m_old, jnp.max(s, axis=0, keepdims=True))
    alpha = jnp.exp2((m_old - m_new) * c)
    p = jnp.exp2((s - m_new) * c).astype(BF16)
    vt_ones = jnp.concatenate([vt, jnp.ones((ONES_ROWS, vt.shape[1]), BF16)], axis=0)
    acc_scr[h] = alpha * acc_scr[h] + jnp.dot(vt_ones, p, preferred_element_type=F32)
    m_scr[h] = m_new


def _flash_t_out(o_ref, acc_scr):
    heads, width = acc_scr.shape[0], acc_scr.shape[1] - ONES_ROWS
    ot = jnp.concatenate([acc_scr[h, :width, :] / acc_scr[h, width:width + 1, :] for h in range(heads)], axis=0)
    o_ref[...] = ot.T.astype(BF16)


def _split_heads_t(slab_t):
    row_lo = lax.broadcasted_iota(I32, (LANE, 1), 0) < HEAD
    zero = jnp.zeros_like(slab_t)
    return jnp.where(row_lo, slab_t, zero), jnp.where(row_lo, zero, slab_t)


def _dsa_t_kernel(qi_ref, wi_ref, ki_ref, qa_ref, k_ref, vt_ref, o_ref,
                  key_scr, bias_scr, thr2_scr, qit_scr, qat_scr, s_scr, m_scr, acc_scr,
                  *, tq, tk, l, past, topk):
    heads = acc_scr.shape[0]
    qpos0 = past + pl.program_id(1) * tq
    ntiles = _visible_tiles(qpos0, tq, l, tk)
    qchunk = (qpos0 + lax.broadcasted_iota(I32, (1, tq), 1)) // CHUNK
    krow = lax.broadcasted_iota(I32, (tk, 1), 0)
    kf = jnp.float32(topk)

    for s in range(IDX_HEADS // 2):
        qit_scr[2 * s], qit_scr[2 * s + 1] = _split_heads_t(qi_ref[:, s * LANE:(s + 1) * LANE].T)
    for s in range(heads // 2):
        slab_t = (qa_ref[:, s * LANE:(s + 1) * LANE].astype(F32) * (HEAD ** -0.5)).astype(BF16).T
        qat_scr[2 * s], qat_scr[2 * s + 1] = _split_heads_t(slab_t)
    w_t = wi_ref[...].T * (HEAD ** -0.5)
    w_rows = [w_t[h:h + 1, :] for h in range(IDX_HEADS)]

    def score_body(kt, _):
        k0 = pl.multiple_of(kt * tk, tk)
        ki_tile = ki_ref[pl.ds(k0, tk), :]
        score = jnp.zeros((tk, tq), F32)
        for h in range(IDX_HEADS):
            s = jnp.dot(ki_tile, qit_scr[h], preferred_element_type=F32)
            score = score + jnp.maximum(s, 0.0) * w_rows[h]
        kpos = k0 + krow
        vis = (kpos // CHUNK <= qchunk) & (kpos < l)
        key_scr[kt] = _ordered_key(jnp.where(vis, score, -jnp.inf))
        return 0

    lax.fori_loop(0, ntiles, score_body, 0)

    def count(pred):
        def body(kt, acc):
            hit = jnp.where(pred(key_scr[kt], kt * tk), 1.0, 0.0)
            return acc + hit.reshape(tk // SUBLANE, SUBLANE, tq).sum(axis=0)
        acc = lax.fori_loop(0, ntiles, body, jnp.zeros((SUBLANE, tq), F32))
        return jnp.sum(acc, axis=0, keepdims=True)

    def bit_body(b, lo):
        cand = lo + jnp.left_shift(jnp.int32(1), 31 - b)
        c = count(lambda ks, _: ks >= cand)
        return jnp.where(c >= kf, cand, lo)

    thr = lax.fori_loop(0, 32, bit_body, jnp.full((1, tq), INT_MIN, I32))

    lp = key_scr.shape[0] * tk
    n_ge = count(lambda ks, _: ks >= thr)
    n_gt = count(lambda ks, _: ks > thr)
    excess = (n_ge > kf) & (thr > KEY_NEG_INF)
    thr2_scr[...] = jnp.zeros((1, tq), I32)

    @pl.when(jnp.max(jnp.where(excess, 1.0, 0.0)) > 0.0)
    def _():
        need = kf - n_gt
        nbits = lp.bit_length()

        def bit2_body(b, lo):
            cand = lo + jnp.left_shift(jnp.int32(1), nbits - 1 - b)
            c = count(lambda ks, base: jnp.where(ks == thr, lp - (base + krow), 0) >= cand)
            return jnp.where(c >= need, cand, lo)

        thr2_scr[...] = lax.fori_loop(0, nbits, bit2_body, jnp.zeros((1, tq), I32))

    thr2 = thr2_scr[...]

    def bias_body(kt, _):
        ks = key_scr[kt]
        tie = jnp.where(lp - (kt * tk + krow) >= thr2, 0.0, MASKED)
        bias = jnp.where(ks > thr, 0.0, jnp.where(ks == thr, tie, MASKED))
        bias_scr[kt] = jnp.where(ks > KEY_NEG_INF, bias, MASKED)
        return 0

    lax.fori_loop(0, ntiles, bias_body, 0)

    _flash_t_init(m_scr, acc_scr)

    def att_body(kt, _):
        k0 = pl.multiple_of(kt * tk, tk)
        for h in range(heads):
            cols = slice((h // 2) * LANE, (h // 2 + 1) * LANE)
            s_scr[h] = jnp.dot(k_ref[pl.ds(k0, tk), cols], qat_scr[h], preferred_element_type=F32)
        for h in range(heads):
            s = s_scr[h] + bias_scr[kt]
            _flash_t_step(h, s, vt_ref[h * HEAD:(h + 1) * HEAD, pl.ds(k0, tk)], LOG2E, m_scr, acc_scr)
        return 0

    lax.fori_loop(0, ntiles, att_body, 0)
    _flash_t_out(o_ref, acc_scr)


def _dsa_attention_t(qi, wi, ki, qa, k, vt, grp, tq):
    b, t, aw = qa.shape
    lp, tk = _key_tile(t)
    assert lp == t
    heads = aw // HEAD
    topk = min(TOPK_MAX, t // 4)
    qblk = lambda w: pl.BlockSpec((None, tq, w), lambda i, j: (i, j, 0))
    kblk = lambda w: pl.BlockSpec((None, t, w), lambda i, j: (i, 0, 0))
    return pl.pallas_call(
        functools.partial(_dsa_t_kernel, tq=tq, tk=tk, l=t, past=grp.past, topk=topk),
        grid=(b, t // tq),
        in_specs=[qblk(IDX_HEADS * HEAD), qblk(LANE), kblk(LANE), qblk(aw), kblk(aw),
                  pl.BlockSpec((aw, t), lambda i, j: (0, i))],
        out_specs=qblk(aw),
        out_shape=jax.ShapeDtypeStruct((b, t, aw), BF16),
        scratch_shapes=[pltpu.VMEM((t // tk, tk, tq), I32),
                        pltpu.VMEM((t // tk, tk, tq), F32),
                        pltpu.VMEM((1, tq), I32),
                        pltpu.VMEM((IDX_HEADS, LANE, tq), BF16),
                        pltpu.VMEM((heads, LANE, tq), BF16),
                        pltpu.VMEM((heads, tk, tq), F32),
                        pltpu.VMEM((heads, 1, tq), F32),
                        pltpu.VMEM((heads, HEAD + ONES_ROWS, tq), F32)],
        compiler_params=_params(("arbitrary", "arbitrary")),
        name="dsa_attention_t",
    )(qi, wi, ki, qa, k, vt)


def _mla_t_kernel(qn_ref, qr_ref, kn_ref, vt_ref, kr_ref, o_ref, qt_scr, s_scr, m_scr, acc_scr,
                  *, tq, tk, l, past):
    ch = acc_scr.shape[0]
    qpos0 = past + pl.program_id(1) * tq
    ntiles = _visible_tiles(qpos0, tq, l, tk)
    nfull = jnp.minimum(l, (qpos0 // CHUNK + 1) * CHUNK) // tk
    qchunk = (qpos0 + lax.broadcasted_iota(I32, (1, tq), 1)) // CHUNK
    krow = lax.broadcasted_iota(I32, (tk, 1), 0)
    c = (C_NOPE + HEAD) ** -0.5 * LOG2E
    for s in range(ch // 2):
        pair = _split_heads_t(qr_ref[:, s * LANE:(s + 1) * LANE].T)
        for half in range(2):
            h = 2 * s + half
            qt_scr[h] = jnp.concatenate([qn_ref[:, h * LANE:(h + 1) * LANE].T, pair[half]], axis=0)
    _flash_t_init(m_scr, acc_scr)

    def tile(kt, masked):
        k0 = pl.multiple_of(kt * tk, tk)
        k_rope = kr_ref[pl.ds(k0, tk), :]
        for h in range(ch):
            kcat = jnp.concatenate([kn_ref[pl.ds(k0, tk), h * LANE:(h + 1) * LANE], k_rope], axis=1)
            s_scr[h] = jnp.dot(kcat, qt_scr[h], preferred_element_type=F32)
        if masked:
            kpos = k0 + krow
            bias = jnp.where((kpos // CHUNK <= qchunk) & (kpos < l), 0.0, MASKED)
        for h in range(ch):
            s = s_scr[h] + bias if masked else s_scr[h]
            _flash_t_step(h, s, vt_ref[h * C_V:(h + 1) * C_V, pl.ds(k0, tk)], c, m_scr, acc_scr)
        return 0

    lax.fori_loop(0, nfull, lambda kt, _: tile(kt, False), 0)
    lax.fori_loop(nfull, ntiles, lambda kt, _: tile(kt, True), 0)
    _flash_t_out(o_ref, acc_scr)


def _mla_attention_t(qn, qr, kn, vt, kr, grp, tq):
    b, t, wn = qn.shape
    ch = wn // C_NOPE
    lp, tk = _key_tile(t)
    assert lp == t and ch % 2 == 0
    qblk = lambda w: pl.BlockSpec((None, tq, w), lambda i, j: (i, j, 0))
    kblk = lambda w: _resident((None, t, w), lambda i, j: (i, 0, 0))
    return pl.pallas_call(
        functools.partial(_mla_t_kernel, tq=tq, tk=tk, l=t, past=grp.past),
        grid=(b, t // tq),
        in_specs=[qblk(wn), qblk(ch * HEAD), kblk(wn), _resident((wn, t), lambda i, j: (0, i)), kblk(LANE)],
        out_specs=qblk(wn),
        out_shape=jax.ShapeDtypeStruct((b, t, wn), BF16),
        scratch_shapes=[pltpu.VMEM((ch, 2 * LANE, tq), BF16),
                        pltpu.VMEM((ch, tk, tq), F32),
                        pltpu.VMEM((ch, 1, tq), F32),
                        pltpu.VMEM((ch, C_V + ONES_ROWS, tq), F32)],
        compiler_params=_params(("arbitrary", "arbitrary")),
        name="mla_attention_t",
    )(qn, qr, kn, vt, kr)


def _causal_conv(u, e0, e1, w, seg):
    rmod = lax.broadcasted_iota(I32, (u.shape[0], 1), 0) % seg
    u1 = jnp.where(rmod == 0, e1, pltpu.roll(u, 1, 0))
    u2 = jnp.where(rmod == 0, e0, jnp.where(rmod == 1, e1, pltpu.roll(u, 2, 0)))
    return u2 * w[0:1] + u1 * w[1:2] + u * w[2:3]


def _layer_norm(z, g, b):
    mu = jnp.mean(z, axis=-1, keepdims=True)
    zc = z - mu
    var = jnp.mean(zc * zc, axis=-1, keepdims=True)
    return zc * lax.rsqrt(var + LN_EPS) * g + b


def _outproj_kernel(*refs, alpha, seq_tiles, seg, carried):
    if carried:
        (x_ref, oa_ref, bg_ref, u_ref, oc_ref, w_ref, cw_ref, g1_ref, lng_ref, lnb_ref, sc2_ref, sh2_ref,
         x1_ref, h2_ref, prev_scr) = refs

        @pl.when(pl.program_id(0) % seq_tiles == 0)
        def _():
            prev_scr[...] = jnp.zeros_like(prev_scr)

        e0, e1 = prev_scr[SUBLANE - 2:SUBLANE - 1, :], prev_scr[SUBLANE - 1:SUBLANE, :]
    else:
        (x_ref, oa_ref, bg_ref, u_ref, oc_ref, w_ref, cw_ref, g1_ref, lng_ref, lnb_ref, sc2_ref, sh2_ref,
         e0_ref, e1_ref, x1_ref, h2_ref) = refs
        e0, e1 = e0_ref[...], e1_ref[...]
    u = u_ref[...]
    yb = bg_ref[...] * _causal_conv(u, e0, e1, cw_ref[...], seg)
    if carried:
        prev_scr[...] = u[u.shape[0] - SUBLANE:, :]
    mixed = jnp.concatenate([oa_ref[...], yb.astype(BF16), oc_ref[...]], axis=1)
    mix = jnp.dot(mixed, w_ref[...], preferred_element_type=F32)
    x1 = _layer_norm(alpha * x_ref[...] + (1.0 + g1_ref[...]) * mix, lng_ref[...], lnb_ref[...])
    x1_ref[...] = x1
    h2_ref[...] = (x1 * (1.0 + sc2_ref[...]) + sh2_ref[...]).astype(BF16)


def _out_projection(x, oa, bg, u, oc, w_out, conv_w, g1, ln_g, ln_b, sc2, sh2, prev, grp, alpha, tm):
    m, d = x.shape
    aw, bw, cw = oa.shape[1], bg.shape[1], oc.shape[1]
    n_i = m // tm
    tiles_per_mod = n_i // g1.shape[0]
    mod_rows = g1.shape[1]
    row = lambda i: (i, 0)
    fix = lambda i: (0, 0)
    mod_spec = pl.BlockSpec((None, mod_rows, d), lambda i: (i // tiles_per_mod, 0, 0))
    in_specs = [pl.BlockSpec((tm, d), row), pl.BlockSpec((tm, aw), row), pl.BlockSpec((tm, bw), row),
                pl.BlockSpec((tm, bw), row), pl.BlockSpec((tm, cw), row),
                _resident((aw + bw + cw, d), fix), pl.BlockSpec((CONV_W, bw), fix),
                mod_spec, pl.BlockSpec((1, d), fix), pl.BlockSpec((1, d), fix), mod_spec, mod_spec]
    args = [x, oa, bg, u, oc, w_out, conv_w, g1, ln_g.reshape(1, d), ln_b.reshape(1, d), sc2, sh2]
    scratch = []
    if prev is None:
        assert grp.t % tm == 0
        scratch = [pltpu.VMEM((SUBLANE, bw), F32)]
    else:
        assert tm % grp.t == 0
        in_specs += [pl.BlockSpec((tm, bw), row), pl.BlockSpec((tm, bw), row)]
        args += [prev[0], prev[1]]
    return pl.pallas_call(
        functools.partial(_outproj_kernel, alpha=alpha, seq_tiles=max(grp.t // tm, 1), seg=min(grp.t, tm),
                          carried=prev is None),
        grid=(n_i,),
        in_specs=in_specs,
        out_specs=[pl.BlockSpec((tm, d), row), pl.BlockSpec((tm, d), row)],
        out_shape=[jax.ShapeDtypeStruct((m, d), F32), jax.ShapeDtypeStruct((m, d), BF16)],
        scratch_shapes=scratch,
        compiler_params=_params(("arbitrary",)),
        name="out_projection",
    )(*args)


def _ffn_kernel(*refs, alpha, seq_tiles, seg, carried):
    if carried:
        (h_ref, x_ref, wg_ref, wu_ref, wd_ref, cw_ref, g2_ref, lng_ref, lnb_ref,
         o_ref, gt_ref, acc_scr, prev_scr) = refs
    else:
        (h_ref, x_ref, wg_ref, wu_ref, wd_ref, cw_ref, g2_ref, lng_ref, lnb_ref, e0_ref, e1_ref,
         o_ref, gt_ref, acc_scr) = refs
    f = pl.program_id(1)
    h = h_ref[...]
    gate = jnp.dot(h, wg_ref[...], preferred_element_type=F32)
    up = jnp.dot(h, wu_ref[...], preferred_element_type=F32)
    tm = gate.shape[0]
    if carried:
        @pl.when(pl.program_id(0) % seq_tiles == 0)
        def _():
            prev_scr[f] = jnp.zeros(prev_scr.shape[1:], F32)

        e0, e1 = prev_scr[f, SUBLANE - 2:SUBLANE - 1, :], prev_scr[f, SUBLANE - 1:SUBLANE, :]
    else:
        e0, e1 = e0_ref[...], e1_ref[...]
    conv = _causal_conv(gate, e0, e1, cw_ref[...], seg)
    if carried:
        prev_scr[f] = gate[tm - SUBLANE:, :]
    gt_ref[...] = gate[tm - gt_ref.shape[0]:, :]
    act = (jax.nn.silu(conv) * up).astype(BF16)
    part = jnp.dot(act, wd_ref[...], preferred_element_type=F32)

    @pl.when(f == 0)
    def _():
        acc_scr[...] = part

    @pl.when(f > 0)
    def _():
        acc_scr[...] += part

    @pl.when(f == pl.num_programs(1) - 1)
    def _():
        z = alpha * x_ref[...] + (1.0 + g2_ref[...]) * acc_scr[...]
        o_ref[...] = _layer_norm(z, lng_ref[...], lnb_ref[...])


def _channel_mixer(h2, x1, w_gu, w_down, conv_w, g2, ln_g, ln_b, prev, grp, alpha, tm, tf):
    m, d = x1.shape
    dff = w_down.shape[0]
    n_i, n_f = m // tm, dff // tf
    tiles_per_mod = n_i // g2.shape[0]
    mod_rows = g2.shape[1]
    fix = lambda i, f: (0, 0)
    in_specs = [pl.BlockSpec((tm, d), lambda i, f: (i, 0)), pl.BlockSpec((tm, d), lambda i, f: (i, 0)),
                pl.BlockSpec((d, tf), lambda i, f: (0, f)), pl.BlockSpec((d, tf), lambda i, f: (0, n_f + f)),
                pl.BlockSpec((tf, d), lambda i, f: (f, 0)), pl.BlockSpec((CONV_W, tf), lambda i, f: (0, f)),
                pl.BlockSpec((None, mod_rows, d), lambda i, f: (i // tiles_per_mod, 0, 0)),
                pl.BlockSpec((1, d), fix), pl.BlockSpec((1, d), fix)]
    args = [h2, x1, w_gu, w_gu, w_down, conv_w, g2, ln_g.reshape(1, d), ln_b.reshape(1, d)]
    scratch = [pltpu.VMEM((tm, d), F32)]
    if prev is None:
        assert grp.t % tm == 0
        scratch.append(pltpu.VMEM((n_f, SUBLANE, tf), F32))
        gt_spec = pl.BlockSpec((None, SUBLANE, tf), lambda i, f: (i, 0, f))
        gt_shape = jax.ShapeDtypeStruct((n_i, SUBLANE, dff), F32)
    else:
        assert tm % grp.t == 0
        in_specs += [pl.BlockSpec((tm, tf), lambda i, f: (i, f))] * 2
        args += [prev[0], prev[1]]
        gt_spec = pl.BlockSpec((tm, tf), lambda i, f: (i, f))
        gt_shape = jax.ShapeDtypeStruct((m, dff), F32)
    return pl.pallas_call(
        functools.partial(_ffn_kernel, alpha=alpha, seq_tiles=max(grp.t // tm, 1), seg=min(grp.t, tm),
                          carried=prev is None),
        grid=(n_i, n_f),
        in_specs=in_specs,
        out_specs=[pl.BlockSpec((tm, d), lambda i, f: (i, 0)), gt_spec],
        out_shape=[jax.ShapeDtypeStruct((m, d), F32), gt_shape],
        scratch_shapes=scratch,
        compiler_params=_params(("arbitrary", "arbitrary")),
        name="channel_mixer",
    )(*args)


def _rope_tables(pos):
    half = HEAD // 2
    inv = jnp.power(jnp.float32(ROPE_THETA), -jnp.arange(half, dtype=F32) / half)
    ang = pos.astype(F32)[:, None] * inv[None, :]
    cos, sin = jnp.cos(ang), jnp.sin(ang)
    zero = jnp.zeros_like(sin)
    reps = LANE // HEAD
    return (jnp.tile(jnp.concatenate([cos, cos], axis=1), (1, reps)),
            jnp.tile(jnp.concatenate([zero, sin], axis=1), (1, reps)),
            jnp.tile(jnp.concatenate([-sin, zero], axis=1), (1, reps)))


def _with_past(past, new, lp, twice=False):
    b, t, w = new.shape
    parts = [new]
    if past is not None:
        p = past.reshape(b, past.shape[1], -1).astype(BF16)
        parts = [jnp.concatenate([p, p], axis=-1) if twice else p, new]
    n = sum(a.shape[1] for a in parts)
    if lp > n:
        parts.append(jnp.zeros((b, lp - n, w), BF16))
    return parts[0] if len(parts) == 1 else jnp.concatenate(parts, axis=1)


def _layer(x, mod, tables, grp, past, wts, dm, alpha):
    (w_in_p, w_out, conv_b_w, kv_norm, w_uk, w_uv, ln1_g, ln1_b, ln2_g, ln2_b, w_gu, conv_f_w, w_down) = wts
    b, t = grp.b, grp.t
    m, d = x.shape
    carried = past is None
    tm = _row_tile(t, 256) if carried else m
    if carried:
        mods = [a.reshape(b, 1, d) for a in jnp.split(mod, N_MOD, axis=-1)]
    else:
        mods = [jnp.repeat(a, t, axis=0).reshape(1, m, d) for a in jnp.split(mod, N_MOD, axis=-1)]
    sh1, sc1, g1, sh2, sc2, g2 = mods

    (qa, ka, kab, qi, qcr, ki, kib, kr, krb, va, vab, bg, u, qcn, lat, latb, wi, vat) = _in_projection(
        x, sc1, sh1, w_in_p, tables, kv_norm, dm, tm)

    three = lambda a: a.reshape(b, t, a.shape[-1])
    if carried:
        tq = _row_tile(t, 256)
        kn, vct = _kv_up(latb, w_uk, w_uv, v_transposed=True)
        oa = _dsa_attention_t(three(qi), three(wi), three(kib), three(qa), three(kab), vat, grp, tq)
        oc = _mla_attention_t(three(qcn), three(qcr), three(kn), vct, three(krb), grp, tq)
        prev_b = prev_f = None
    else:
        l = grp.past + t
        lp, _ = _key_tile(l)
        p_ak, p_av, p_ik, p_lat, p_kr, prev_b, prev_f = past
        k_all = _with_past(p_ak, three(kab), lp)
        v_all = _with_past(p_av, three(vab), lp)
        ki_all = _with_past(p_ik, three(kib), lp, twice=True)
        kr_all = _with_past(p_kr, three(krb), lp, twice=True)
        lat_all = _with_past(p_lat, three(latb), lp)
        kn, vc = _kv_up(lat_all.reshape(b * lp, dm.r), w_uk, w_uv, v_transposed=False)
        oa = _dsa_attention(three(qi), three(wi), ki_all, three(qa), k_all, v_all, grp, l, t)
        oc = _mla_attention(three(qcn), three(qcr), kn.reshape(b, lp, -1), vc.reshape(b, lp, -1), kr_all,
                            grp, l, t)

    def expand(state):
        return jnp.repeat(state[:, 0], t, axis=0), jnp.repeat(state[:, 1], t, axis=0)

    x1, h2 = _out_projection(x, oa.reshape(m, -1), bg, u, oc.reshape(m, -1), w_out, conv_b_w, g1, ln1_g, ln1_b,
                             sc2, sh2, None if carried else expand(prev_b), grp, alpha, tm)
    tm_f = _row_tile(t, 512) if carried else m
    tf = _row_tile(dm.dff, 512)
    x2, gate_rows = _channel_mixer(h2, x1, w_gu, w_down, conv_f_w, g2, ln2_g, ln2_b,
                                   None if carried else expand(prev_f), grp, alpha, tm_f, tf)
    if carried:
        new_f = gate_rows.reshape(b, t // tm_f, SUBLANE, dm.dff)[:, -1, SUBLANE - (CONV_W - 1):, :]
    else:
        new_f = gate_rows.reshape(b, t, dm.dff)[:, t - (CONV_W - 1):, :]
    new_b = u.reshape(b, t, dm.bw)[:, t - (CONV_W - 1):, :]
    heads = dm.aw // HEAD
    rows = (ka.reshape(b, t, heads, HEAD), va.reshape(b, t, heads, HEAD), ki.reshape(b, t, HEAD),
            lat.reshape(b, t, dm.r), kr.reshape(b, t, HEAD), new_b, new_f)
    return x2, rows


def kernel(x_prompt, x_sample, c_prompt, c_sample, cache_a_k, cache_a_v, cache_idx_k, cache_mla_latent,
           cache_mla_krope, state_conv_b, state_conv_ffn, w_in, w_out, conv_b_w, mla_kv_norm, mla_w_uk,
           mla_w_uv, w_mod, b_mod, ln1_g, ln1_b, ln2_g, ln2_b, ffn_w_gu, ffn_conv_w, ffn_w_down):
    depth, d, _ = w_in.shape
    a_heads = cache_a_k.shape[3]
    dm = Dims(d=d, aw=a_heads * HEAD, bw=conv_b_w.shape[2], ch=mla_w_uk.shape[2] // C_NOPE,
              r=mla_w_uk.shape[1], dff=ffn_w_down.shape[1], depth=depth)
    alpha = (2 * depth) ** 0.25
    grp_p = Group(b=x_prompt.shape[0], t=x_prompt.shape[1], past=0)
    grp_s = Group(b=x_sample.shape[0], t=x_sample.shape[1], past=cache_a_k.shape[2])

    n_c = grp_p.b + grp_s.b
    c_all = jnp.concatenate([c_prompt, c_sample, jnp.zeros((-n_c % SUBLANE, d), F32)], axis=0)
    mod = _modulation(c_all, w_mod, b_mod)

    tab_p = _rope_tables(jnp.arange(grp_p.t, dtype=I32))
    tab_s = tuple(jnp.tile(a, (grp_s.b, 1)) for a in _rope_tables(grp_s.past + jnp.arange(grp_s.t, dtype=I32)))

    xp = x_prompt.reshape(grp_p.b * grp_p.t, d)
    xs = x_sample.reshape(grp_s.b * grp_s.t, d)
    rows_p, rows_s = [], []
    for l in range(depth):
        wts = (_pack_w_in(w_in[l], dm), w_out[l].astype(BF16), conv_b_w[l], mla_kv_norm[l],
               mla_w_uk[l].astype(BF16), mla_w_uv[l].astype(BF16), ln1_g[l], ln1_b[l], ln2_g[l], ln2_b[l],
               ffn_w_gu[l].astype(BF16), ffn_conv_w[l], ffn_w_down[l].astype(BF16))
        xp, rp = _layer(xp, mod[l, :grp_p.b], tab_p, grp_p, None, wts, dm, alpha)
        past_l = (cache_a_k[l], cache_a_v[l], cache_idx_k[l], cache_mla_latent[l], cache_mla_krope[l],
                  state_conv_b[l], state_conv_ffn[l])
        xs, rs = _layer(xs, mod[l, grp_p.b:n_c], tab_s, grp_s, past_l, wts, dm, alpha)
        rows_p.append(rp)
        rows_s.append(rs)
    outs_p = [jnp.stack(r) for r in zip(*rows_p)]
    outs_s = [jnp.stack(r) for r in zip(*rows_s)]
    return (xp.reshape(x_prompt.shape), xs.reshape(x_sample.shape), *outs_p, *outs_s)
```

```python
import functools
from typing import NamedTuple

import numpy as np
import jax
import jax.numpy as jnp
from jax import lax
from jax.experimental import pallas as pl
from jax.experimental.pallas import tpu as pltpu

F32, BF16, I32, I16 = jnp.float32, jnp.bfloat16, jnp.int32, jnp.int16

CHUNK = 64
CONV_W = 3
ROPE_THETA = 10000.0
HEAD = 64
IDX_HEADS = 16
TOPK_MAX = 256
C_NOPE = 128
C_V = 128
N_MOD = 6
LN_EPS = 1e-5
RMS_EPS = 1e-6

LANE = 128
SUBLANE = 8
VMEM_LIMIT = 50 * 1024 * 1024

MASKED = -1e30
LOG2E = 1.4426950408889634
INT_MIN = -2 ** 31
I16_MIN = -2 ** 15
KEY_NEG_INF = int(np.array(-np.inf, np.float32).view(np.int32)) ^ 0x7FFFFFFF


class Dims(NamedTuple):
    d: int
    aw: int
    bw: int
    ch: int
    r: int
    dff: int
    depth: int


class Group(NamedTuple):
    b: int
    t: int
    past: int


def _row_tile(m, pref):
    if m <= pref:
        return m
    t = pref - pref % SUBLANE
    while m % t:
        t -= SUBLANE
    return t


def _key_tile(l):
    lp = -(-l // LANE) * LANE
    for tk in (512, 384, 256, 128):
        if lp % tk == 0:
            return lp, tk
    raise AssertionError(lp)


def _params(sem):
    return pltpu.CompilerParams(dimension_semantics=sem, vmem_limit_bytes=VMEM_LIMIT)


def _resident(shape, index_map):
    return pl.BlockSpec(shape, index_map, pipeline_mode=pl.Buffered(1))


def _mod_kernel(c_ref, w_ref, b_ref, o_ref):
    a = jax.nn.silu(c_ref[...]).astype(BF16)
    o_ref[...] = jnp.dot(a, w_ref[...].astype(BF16), preferred_element_type=F32) + b_ref[...]


def _modulation(c, w_mod, b_mod):
    depth, d, n = w_mod.shape
    rows = c.shape[0]
    tn = _row_tile(n, 1024)
    return pl.pallas_call(
        _mod_kernel,
        grid=(depth, n // tn),
        in_specs=[pl.BlockSpec((rows, d), lambda l, j: (0, 0)),
                  pl.BlockSpec((None, d, tn), lambda l, j: (l, 0, j)),
                  pl.BlockSpec((None, 1, tn), lambda l, j: (l, 0, j))],
        out_specs=pl.BlockSpec((None, rows, tn), lambda l, j: (l, 0, j)),
        out_shape=jax.ShapeDtypeStruct((depth, rows, n), F32),
        compiler_params=_params(("arbitrary", "arbitrary")),
        name="modulation",
    )(c, w_mod, b_mod.reshape(depth, 1, n))


def _pack_w_in(w, dm):
    d, aw, bw, ch, r = dm.d, dm.aw, dm.bw, dm.ch, dm.r
    o = np.cumsum([0, aw, aw, aw, IDX_HEADS * HEAD, HEAD, IDX_HEADS, bw, bw, bw, ch * (C_NOPE + HEAD), r, HEAD])
    qa, ka, va, qi, ki, wi, bg, cg, xb, qc, lat, kr = [w[:, o[i]:o[i + 1]] for i in range(12)]
    qc = qc.reshape(d, ch, C_NOPE + HEAD)
    qcn = qc[:, :, :C_NOPE].reshape(d, ch * C_NOPE)
    qcr = qc[:, :, C_NOPE:].reshape(d, ch * HEAD)
    pad = jnp.zeros((d, LANE - IDX_HEADS), w.dtype)
    return jnp.concatenate([qa, ka, qi, qcr, ki, ki, kr, kr, va, bg, cg, xb, qcn, lat, wi, pad],
                           axis=1).astype(BF16)


def _rope(acc, cos, s1, s2):
    outs = []
    for s in range(acc.shape[1] // LANE):
        xs = acc[:, s * LANE:(s + 1) * LANE]
        outs.append(xs * cos + pltpu.roll(xs, HEAD // 2, 1) * s1 + pltpu.roll(xs, LANE - HEAD // 2, 1) * s2)
    return outs[0] if len(outs) == 1 else jnp.concatenate(outs, axis=1)


def _inproj_kernel(x_ref, sc_ref, sh_ref, w_ref, cos_ref, s1_ref, s2_ref, nrm_ref,
                   qa_ref, ka_ref, kab_ref, qi_ref, qcr_ref, ki_ref, kib_ref, kr_ref, krb_ref,
                   va_ref, vab_ref, bg_ref, u_ref, qcn_ref, lat_ref, latb_ref, wi_ref, vat_ref, *, dm):
    aw, bw, ch, r = dm.aw, dm.bw, dm.ch, dm.r
    h = (x_ref[...] * (1.0 + sc_ref[...]) + sh_ref[...]).astype(BF16)
    cos, s1, s2 = cos_ref[...], s1_ref[...], s2_ref[...]
    col = [0]

    def proj(width):
        c0 = col[0]
        col[0] = c0 + width
        return jnp.dot(h, w_ref[:, c0:c0 + width], preferred_element_type=F32)

    def pieces(width, step=512):
        return [(o, min(step, width - o)) for o in range(0, width, step)]

    for o, wd in pieces(aw):
        qa_ref[:, o:o + wd] = _rope(proj(wd), cos, s1, s2).astype(BF16)
    for o, wd in pieces(aw):
        y = _rope(proj(wd), cos, s1, s2)
        ka_ref[:, o:o + wd] = y
        kab_ref[:, o:o + wd] = y.astype(BF16)
    for o, wd in pieces(IDX_HEADS * HEAD):
        qi_ref[:, o:o + wd] = _rope(proj(wd), cos, s1, s2).astype(BF16)
    for o, wd in pieces(ch * HEAD):
        qcr_ref[:, o:o + wd] = _rope(proj(wd), cos, s1, s2).astype(BF16)
    for f32_ref, b16_ref in ((ki_ref, kib_ref), (kr_ref, krb_ref)):
        y = _rope(proj(LANE), cos, s1, s2)
        f32_ref[...] = y[:, :HEAD]
        b16_ref[...] = y.astype(BF16)
    for o, wd in pieces(aw):
        y = proj(wd)
        va_ref[:, o:o + wd] = y
        vab_ref[:, o:o + wd] = y.astype(BF16)
        vat_ref[o:o + wd, :] = y.T.astype(BF16)
    for o, wd in pieces(bw):
        bg_ref[:, o:o + wd] = proj(wd)
    c_cg = col[0]
    for o, wd in pieces(bw):
        cg = jnp.dot(h, w_ref[:, c_cg + o:c_cg + o + wd], preferred_element_type=F32)
        xb = jnp.dot(h, w_ref[:, c_cg + bw + o:c_cg + bw + o + wd], preferred_element_type=F32)
        u_ref[:, o:o + wd] = cg * xb
    col[0] = c_cg + 2 * bw
    for o, wd in pieces(ch * C_NOPE):
        qcn_ref[:, o:o + wd] = proj(wd).astype(BF16)
    lat = proj(r)
    lat = lat * lax.rsqrt(jnp.mean(lat * lat, axis=-1, keepdims=True) + RMS_EPS) * nrm_ref[...]
    lat_ref[...] = lat
    latb_ref[...] = lat.astype(BF16)
    wi_ref[...] = proj(LANE) * (IDX_HEADS ** -0.5)


def _in_projection(x, sc, sh, w_packed, tables, kv_norm, dm, tm):
    m, d = x.shape
    aw, bw, ch, r = dm.aw, dm.bw, dm.ch, dm.r
    npk = w_packed.shape[1]
    n_i = m // tm
    tiles_per_mod = n_i // sc.shape[0]
    mod_rows = sc.shape[1]
    tab_tiles = tables[0].shape[0] // tm
    widths = [(aw, BF16), (aw, F32), (aw, BF16), (IDX_HEADS * HEAD, BF16), (ch * HEAD, BF16),
              (HEAD, F32), (LANE, BF16), (HEAD, F32), (LANE, BF16),
              (aw, F32), (aw, BF16), (bw, F32), (bw, F32), (ch * C_NOPE, BF16), (r, F32), (r, BF16),
              (LANE, F32)]
    row = lambda i: (i, 0)
    mod_spec = pl.BlockSpec((None, mod_rows, d), lambda i: (i // tiles_per_mod, 0, 0))
    tab_spec = pl.BlockSpec((tm, LANE), lambda i: (i % tab_tiles, 0))
    return pl.pallas_call(
        functools.partial(_inproj_kernel, dm=dm),
        grid=(n_i,),
        in_specs=[pl.BlockSpec((tm, d), row), mod_spec, mod_spec,
                  _resident((d, npk), lambda i: (0, 0)),
                  tab_spec, tab_spec, tab_spec,
                  pl.BlockSpec((1, r), lambda i: (0, 0))],
        out_specs=[pl.BlockSpec((tm, w), row) for w, _ in widths] + [pl.BlockSpec((aw, tm), lambda i: (0, i))],
        out_shape=[jax.ShapeDtypeStruct((m, w), dt) for w, dt in widths] + [jax.ShapeDtypeStruct((aw, m), BF16)],
        compiler_params=_params(("arbitrary",)),
        name="in_projection",
    )(x, sc, sh, w_packed, *tables, kv_norm.reshape(1, r))


def _kvup_kernel(l_ref, wk_ref, wv_ref, k_ref, v_ref, *, v_transposed):
    lat = l_ref[...]
    k_ref[...] = jnp.dot(lat, wk_ref[...], preferred_element_type=F32).astype(BF16)
    if v_transposed:
        v_ref[...] = lax.dot_general(wv_ref[...], lat, _NT, preferred_element_type=F32).astype(BF16)
    else:
        v_ref[...] = jnp.dot(lat, wv_ref[...], preferred_element_type=F32).astype(BF16)


def _kv_up(lat, w_uk, w_uv, v_transposed):
    m, r = lat.shape
    n = w_uk.shape[1]
    tm = _row_tile(m, 512)
    if v_transposed:
        w_v, v_spec, v_shape = w_uv.T, pl.BlockSpec((n, tm), lambda i: (0, i)), (n, m)
    else:
        w_v, v_spec, v_shape = w_uv, pl.BlockSpec((tm, n), lambda i: (i, 0)), (m, n)
    return pl.pallas_call(
        functools.partial(_kvup_kernel, v_transposed=v_transposed),
        grid=(m // tm,),
        in_specs=[pl.BlockSpec((tm, r), lambda i: (i, 0)),
                  pl.BlockSpec((r, n), lambda i: (0, 0)),
                  pl.BlockSpec(w_v.shape, lambda i: (0, 0))],
        out_specs=[pl.BlockSpec((tm, n), lambda i: (i, 0)), v_spec],
        out_shape=[jax.ShapeDtypeStruct((m, n), BF16), jax.ShapeDtypeStruct(v_shape, BF16)],
        compiler_params=_params(("arbitrary",)),
        name="latent_up_projection",
    )(lat, w_uk, w_v)


_NT = (((1,), (1,)), ((), ()))


def _visible_tiles(qpos0, tq, l, tk):
    nvis = jnp.minimum(l, ((qpos0 + tq - 1) // CHUNK + 1) * CHUNK)
    return (nvis + tk - 1) // tk


def _flash_step(carry, s, v):
    m, l, acc = carry
    m_new = jnp.maximum(m, jnp.max(s, axis=1, keepdims=True))
    alpha = jnp.exp(m - m_new)
    p = jnp.exp(s - m_new)
    l = alpha * l + jnp.sum(p, axis=1, keepdims=True)
    acc = alpha * acc + jnp.dot(p.astype(BF16), v, preferred_element_type=F32)
    return m_new, l, acc


def _flash_init(rows, width):
    return (jnp.full((rows, 1), MASKED, F32), jnp.zeros((rows, 1), F32), jnp.zeros((rows, width), F32))


def _ordered_key(x):
    b = pltpu.bitcast(x, I32)
    return jnp.where(b < 0, b ^ 0x7FFFFFFF, b)


def _dsa_kernel(qi_ref, wi_ref, ki_ref, qa_ref, k_ref, v_ref, o_ref, key_scr, bias_scr, thr2_scr,
                *, tq, tk, l, lp, past, topk, aw):
    qpos0 = past + pl.program_id(1) * tq
    ntiles = _visible_tiles(qpos0, tq, l, tk)
    qchunk = (qpos0 + lax.broadcasted_iota(I32, (tq, 1), 0)) // CHUNK
    lane = lax.broadcasted_iota(I32, (1, LANE), 1)
    lo_half = lane < HEAD
    kf = jnp.float32(topk)

    def split_heads(qs):
        zero = jnp.zeros_like(qs)
        return jnp.concatenate([jnp.where(lo_half, qs, zero), jnp.where(lo_half, zero, qs)], axis=0)

    wi = wi_ref[...] * (HEAD ** -0.5)
    q_idx = [split_heads(qi_ref[:, s * LANE:(s + 1) * LANE]) for s in range(IDX_HEADS // 2)]
    w_idx = [wi[:, h:h + 1] for h in range(IDX_HEADS)]

    def score_body(kt, _):
        k0 = pl.multiple_of(kt * tk, tk)
        kt_tile = ki_ref[pl.ds(k0, tk), :]
        score = jnp.zeros((tq, tk), F32)
        for s in range(IDX_HEADS // 2):
            s2 = lax.dot_general(q_idx[s], kt_tile, _NT, preferred_element_type=F32)
            score = score + jnp.maximum(s2[:tq], 0.0) * w_idx[2 * s]
            score = score + jnp.maximum(s2[tq:], 0.0) * w_idx[2 * s + 1]
        kpos = k0 + lax.broadcasted_iota(I32, (1, tk), 1)
        vis = (kpos // CHUNK <= qchunk) & (kpos < l)
        key_scr[kt] = _ordered_key(jnp.where(vis, score, -jnp.inf))
        return 0

    lax.fori_loop(0, ntiles, score_body, 0)

    def count(pred):
        def body(kt, acc):
            for j in range(tk // LANE):
                ks = key_scr[kt, :, j * LANE:(j + 1) * LANE]
                acc = acc + jnp.where(pred(ks, kt * tk + j * LANE), 1.0, 0.0)
            return acc
        acc = lax.fori_loop(0, ntiles, body, jnp.zeros((tq, LANE), F32))
        return jnp.sum(acc, axis=1, keepdims=True)

    def wide(col):
        return jnp.broadcast_to(col, (tq, LANE))

    def bit_body(b, lo):
        cand = lo + jnp.left_shift(jnp.int32(1), 31 - b)
        cand_w = wide(cand)
        c = count(lambda ks, _: ks >= cand_w)
        return jnp.where(c >= kf, cand, lo)

    thr = lax.fori_loop(0, 32, bit_body, jnp.full((tq, 1), INT_MIN, I32))
    thr_w = wide(thr)

    n_ge = count(lambda ks, _: ks >= thr_w)
    n_gt = count(lambda ks, _: ks > thr_w)
    excess = (n_ge > kf) & (thr > KEY_NEG_INF)
    thr2_scr[...] = jnp.zeros((tq, 1), I32)

    @pl.when(jnp.max(jnp.where(excess, 1.0, 0.0)) > 0.0)
    def _():
        need = kf - n_gt
        nbits = lp.bit_length()

        def bit2_body(b, lo):
            cand = lo + jnp.left_shift(jnp.int32(1), nbits - 1 - b)
            cand_w = wide(cand)
            c = count(lambda ks, base: jnp.where(ks == thr_w, lp - (base + lane), 0) >= cand_w)
            return jnp.where(c >= need, cand, lo)

        thr2_scr[...] = lax.fori_loop(0, nbits, bit2_body, jnp.zeros((tq, 1), I32))

    thr2_w = wide(thr2_scr[...])

    def bias_body(kt, _):
        for j in range(tk // LANE):
            ks = key_scr[kt, :, j * LANE:(j + 1) * LANE]
            tie = jnp.where(lp - (kt * tk + j * LANE + lane) >= thr2_w, 0.0, MASKED)
            bias = jnp.where(ks > thr_w, 0.0, jnp.where(ks == thr_w, tie, MASKED))
            bias_scr[kt, :, j * LANE:(j + 1) * LANE] = jnp.where(ks > KEY_NEG_INF, bias, MASKED)
        return 0

    lax.fori_loop(0, ntiles, bias_body, 0)

    for pr in range(aw // LANE):
        cols = slice(pr * LANE, (pr + 1) * LANE)
        q2 = split_heads(qa_ref[:, cols])

        def att_body(kt, carry, cols=cols, q2=q2):
            k0 = pl.multiple_of(kt * tk, tk)
            s = lax.dot_general(q2, k_ref[pl.ds(k0, tk), cols], _NT, preferred_element_type=F32)
            bias = bias_scr[kt]
            s = s * (HEAD ** -0.5) + jnp.concatenate([bias, bias], axis=0)
            return _flash_step(carry, s, v_ref[pl.ds(k0, tk), cols])

        _, den, acc = lax.fori_loop(0, ntiles, att_body, _flash_init(2 * tq, LANE))
        o2 = acc / den
        o_ref[:, cols] = jnp.where(lo_half, o2[:tq], o2[tq:]).astype(BF16)


def _dsa_attention(qi, wi, ki, qa, k, v, grp, l, tq):
    b, t, aw = qa.shape
    lp = k.shape[1]
    _, tk = _key_tile(l)
    topk = min(TOPK_MAX, l // 4)
    qblk = lambda w: pl.BlockSpec((None, tq, w), lambda i, j: (i, j, 0))
    kblk = lambda w: pl.BlockSpec((None, lp, w), lambda i, j: (i, 0, 0))
    return pl.pallas_call(
        functools.partial(_dsa_kernel, tq=tq, tk=tk, l=l, lp=lp, past=grp.past, topk=topk, aw=aw),
        grid=(b, t // tq),
        in_specs=[qblk(IDX_HEADS * HEAD), qblk(LANE), kblk(LANE), qblk(aw), kblk(aw), kblk(aw)],
        out_specs=qblk(aw),
        out_shape=jax.ShapeDtypeStruct((b, t, aw), BF16),
        scratch_shapes=[pltpu.VMEM((lp // tk, tq, tk), I32),
                        pltpu.VMEM((lp // tk, tq, tk), F32),
                        pltpu.VMEM((tq, 1), I32)],
        compiler_params=_params(("arbitrary", "arbitrary")),
        name="dsa_attention",
    )(qi, wi, ki, qa, k, v)


def _mla_kernel(qn_ref, qr_ref, kn_ref, v_ref, kr_ref, o_ref, *, tq, tk, l, past, ch):
    qpos0 = past + pl.program_id(1) * tq
    ntiles = _visible_tiles(qpos0, tq, l, tk)
    qchunk = (qpos0 + lax.broadcasted_iota(I32, (tq, 1), 0)) // CHUNK
    lo_half = lax.broadcasted_iota(I32, (1, LANE), 1) < HEAD
    scale = (C_NOPE + HEAD) ** -0.5
    for h in range(ch):
        cols = slice(h * LANE, (h + 1) * LANE)
        qs = qr_ref[:, (h // 2) * LANE:(h // 2 + 1) * LANE]
        zero = jnp.zeros_like(qs)
        qr = jnp.where(lo_half, qs, zero) if h % 2 == 0 else jnp.where(lo_half, zero, qs)
        qcat = jnp.concatenate([qn_ref[:, cols], qr], axis=1)

        def body(kt, carry, cols=cols, qcat=qcat):
            k0 = pl.multiple_of(kt * tk, tk)
            kcat = jnp.concatenate([kn_ref[pl.ds(k0, tk), cols], kr_ref[pl.ds(k0, tk), :]], axis=1)
            s = lax.dot_general(qcat, kcat, _NT, preferred_element_type=F32) * scale
            kpos = k0 + lax.broadcasted_iota(I32, (1, tk), 1)
            vis = (kpos // CHUNK <= qchunk) & (kpos < l)
            return _flash_step(carry, jnp.where(vis, s, MASKED), v_ref[pl.ds(k0, tk), cols])

        _, den, acc = lax.fori_loop(0, ntiles, body, _flash_init(tq, LANE))
        o_ref[:, cols] = (acc / den).astype(BF16)


def _mla_attention(qn, qr, kn, v, kr, grp, l, tq):
    b, t, wn = qn.shape
    ch = wn // C_NOPE
    lp = kn.shape[1]
    _, tk = _key_tile(l)
    qblk = lambda w: pl.BlockSpec((None, tq, w), lambda i, j: (i, j, 0))
    kblk = lambda w: _resident((None, lp, w), lambda i, j: (i, 0, 0))
    return pl.pallas_call(
        functools.partial(_mla_kernel, tq=tq, tk=tk, l=l, past=grp.past, ch=ch),
        grid=(b, t // tq),
        in_specs=[qblk(wn), qblk(ch * HEAD), kblk(wn), kblk(wn), kblk(LANE)],
        out_specs=qblk(wn),
        out_shape=jax.ShapeDtypeStruct((b, t, wn), BF16),
        compiler_params=_params(("arbitrary", "arbitrary")),
        name="mla_attention",
    )(qn, qr, kn, v, kr)


ONES_ROWS = 16


def _flash_t_init(m_scr, acc_scr):
    m_scr[...] = jnp.full(m_scr.shape, MASKED, F32)
    acc_scr[...] = jnp.zeros(acc_scr.shape, F32)


def _flash_t_step(h, s, vt, c, m_scr, acc_scr):
    m_old = m_scr[h]
    m_new = jnp.maximum(m_old, jnp.max(s, axis=0, keepdims=True))
    alpha = jnp.exp2((m_old - m_new) * c)
    p = jnp.exp2((s - m_new) * c).astype(BF16)
    vt_ones = jnp.concatenate([vt, jnp.ones((ONES_ROWS, vt.shape[1]), BF16)], axis=0)
    acc_scr[h] = alpha * acc_scr[h] + jnp.dot(vt_ones, p, preferred_element_type=F32)
    m_scr[h] = m_new


def _flash_t_out(o_ref, acc_scr):
    heads, width = acc_scr.shape[0], acc_scr.shape[1] - ONES_ROWS
    ot = jnp.concatenate([acc_scr[h, :width, :] / acc_scr[h, width:width + 1, :] for h in range(heads)], axis=0)
    o_ref[...] = ot.T.astype(BF16)


def _split_heads_t(slab_t):
    row_lo = lax.broadcasted_iota(I32, (LANE, 1), 0) < HEAD
    zero = jnp.zeros_like(slab_t)
    return jnp.where(row_lo, slab_t, zero), jnp.where(row_lo, zero, slab_t)


def _dsa_t_kernel(qi_ref, wi_ref, ki_ref, qa_ref, k_ref, vt_ref, o_ref,
                  key_scr, hi_scr, lo_scr, bias_scr, thr2_scr, qit_scr, qat_scr, s_scr, m_scr, acc_scr,
                  *, tq, tk, l, past, topk):
    heads = acc_scr.shape[0]
    qpos0 = past + pl.program_id(1) * tq
    ntiles = _visible_tiles(qpos0, tq, l, tk)
    qchunk = (qpos0 + lax.broadcasted_iota(I32, (1, tq), 1)) // CHUNK
    krow = lax.broadcasted_iota(I32, (tk, 1), 0)
    kf = jnp.float32(topk)

    for s in range(IDX_HEADS // 2):
        qit_scr[2 * s], qit_scr[2 * s + 1] = _split_heads_t(qi_ref[:, s * LANE:(s + 1) * LANE].T)
    for s in range(heads // 2):
        slab_t = (qa_ref[:, s * LANE:(s + 1) * LANE].astype(F32) * (HEAD ** -0.5)).astype(BF16).T
        qat_scr[2 * s], qat_scr[2 * s + 1] = _split_heads_t(slab_t)
    w_t = wi_ref[...].T * (HEAD ** -0.5)
    w_rows = [w_t[h:h + 1, :] for h in range(IDX_HEADS)]

    def score_body(kt, _):
        k0 = pl.multiple_of(kt * tk, tk)
        ki_tile = ki_ref[pl.ds(k0, tk), :]
        score = jnp.zeros((tk, tq), F32)
        for h in range(IDX_HEADS):
            s = jnp.dot(ki_tile, qit_scr[h], preferred_element_type=F32)
            score = score + jnp.maximum(s, 0.0) * w_rows[h]
        kpos = k0 + krow
        vis = (kpos // CHUNK <= qchunk) & (kpos < l)
        key = _ordered_key(jnp.where(vis, score, -jnp.inf))
        key_scr[kt] = key
        hi_scr[kt] = (key >> 16).astype(I16)
        lo_scr[kt] = ((key & 0xFFFF) + I16_MIN).astype(I16)
        return 0

    lax.fori_loop(0, ntiles, score_body, 0)

    def count(pred):
        def body(kt, acc):
            hit = jnp.where(pred(key_scr[kt], kt * tk), 1.0, 0.0)
            return acc + hit.reshape(tk // SUBLANE, SUBLANE, tq).sum(axis=0)
        acc = lax.fori_loop(0, ntiles, body, jnp.zeros((SUBLANE, tq), F32))
        return jnp.sum(acc, axis=0, keepdims=True)

    def count16(half_scr, cand, strict):
        rows = 2 * SUBLANE
        cand16 = cand.astype(I16)

        def body(kt, acc):
            half = half_scr[kt]
            hit = jnp.where(half > cand16 if strict else half >= cand16, jnp.int16(1), jnp.int16(0))
            parts = hit.reshape(tk // (4 * rows), 4, rows, tq)
            for g in range(parts.shape[0]):
                acc = acc + parts[g]
            return acc

        acc = lax.fori_loop(0, ntiles, body, jnp.zeros((4, rows, tq), I16))
        return acc.astype(I32).sum(axis=0).sum(axis=0, keepdims=True)

    def kth_largest16(half_scr, k_need):
        def bit_body(b, lo):
            cand = lo + jnp.left_shift(jnp.int32(1), 15 - b)
            return jnp.where(count16(half_scr, cand, False) >= k_need, cand, lo)
        return lax.fori_loop(0, 16, bit_body, jnp.full((1, tq), I16_MIN, I32))

    thr_hi = kth_largest16(hi_scr, jnp.full((1, tq), topk, I32))
    need_lo = topk - count16(hi_scr, thr_hi, True)
    thr_hi16 = thr_hi.astype(I16)

    def mark_body(kt, _):
        lo_scr[kt] = jnp.where(hi_scr[kt] == thr_hi16, lo_scr[kt], jnp.int16(I16_MIN))
        return 0

    lax.fori_loop(0, ntiles, mark_body, 0)
    thr = thr_hi * 65536 + (kth_largest16(lo_scr, need_lo) - I16_MIN)

    lp = key_scr.shape[0] * tk
    n_ge = count(lambda ks, _: ks >= thr)
    n_gt = count(lambda ks, _: ks > thr)
    excess = (n_ge > kf) & (thr > KEY_NEG_INF)
    thr2_scr[...] = jnp.zeros((1, tq), I32)

    @pl.when(jnp.max(jnp.where(excess, 1.0, 0.0)) > 0.0)
    def _():
        need = kf - n_gt
        nbits = lp.bit_length()

        def bit2_body(b, lo):
            cand = lo + jnp.left_shift(jnp.int32(1), nbits - 1 - b)
            c = count(lambda ks, base: jnp.where(ks == thr, lp - (base + krow), 0) >= cand)
            return jnp.where(c >= need, cand, lo)

        thr2_scr[...] = lax.fori_loop(0, nbits, bit2_body, jnp.zeros((1, tq), I32))

    thr2 = thr2_scr[...]

    def bias_body(kt, _):
        ks = key_scr[kt]
        tie = jnp.where(lp - (kt * tk + krow) >= thr2, 0.0, MASKED)
        bias = jnp.where(ks > thr, 0.0, jnp.where(ks == thr, tie, MASKED))
        bias_scr[kt] = jnp.where(ks > KEY_NEG_INF, bias, MASKED)
        return 0

    lax.fori_loop(0, ntiles, bias_body, 0)

    _flash_t_init(m_scr, acc_scr)

    def att_body(kt, _):
        k0 = pl.multiple_of(kt * tk, tk)
        for h in range(heads):
            cols = slice((h // 2) * LANE, (h // 2 + 1) * LANE)
            s_scr[h] = jnp.dot(k_ref[pl.ds(k0, tk), cols], qat_scr[h], preferred_element_type=F32)
        for h in range(heads):
            s = s_scr[h] + bias_scr[kt]
            _flash_t_step(h, s, vt_ref[h * HEAD:(h + 1) * HEAD, pl.ds(k0, tk)], LOG2E, m_scr, acc_scr)
        return 0

    lax.fori_loop(0, ntiles, att_body, 0)
    _flash_t_out(o_ref, acc_scr)


def _dsa_attention_t(qi, wi, ki, qa, k, vt, grp, tq):
    b, t, aw = qa.shape
    lp, tk = _key_tile(t)
    assert lp == t
    heads = aw // HEAD
    topk = min(TOPK_MAX, t // 4)
    qblk = lambda w: pl.BlockSpec((None, tq, w), lambda i, j: (i, j, 0))
    kblk = lambda w: pl.BlockSpec((None, t, w), lambda i, j: (i, 0, 0))
    return pl.pallas_call(
        functools.partial(_dsa_t_kernel, tq=tq, tk=tk, l=t, past=grp.past, topk=topk),
        grid=(b, t // tq),
        in_specs=[qblk(IDX_HEADS * HEAD), qblk(LANE), kblk(LANE), qblk(aw), kblk(aw),
                  pl.BlockSpec((aw, t), lambda i, j: (0, i))],
        out_specs=qblk(aw),
        out_shape=jax.ShapeDtypeStruct((b, t, aw), BF16),
        scratch_shapes=[pltpu.VMEM((t // tk, tk, tq), I32),
                        pltpu.VMEM((t // tk, tk, tq), I16),
                        pltpu.VMEM((t // tk, tk, tq), I16),
                        pltpu.VMEM((t // tk, tk, tq), F32),
                        pltpu.VMEM((1, tq), I32),
                        pltpu.VMEM((IDX_HEADS, LANE, tq), BF16),
                        pltpu.VMEM((heads, LANE, tq), BF16),
                        pltpu.VMEM((heads, tk, tq), F32),
                        pltpu.VMEM((heads, 1, tq), F32),
                        pltpu.VMEM((heads, HEAD + ONES_ROWS, tq), F32)],
        compiler_params=_params(("arbitrary", "arbitrary")),
        name="dsa_attention_t",
    )(qi, wi, ki, qa, k, vt)


def _mla_t_kernel(qn_ref, qr_ref, kn_ref, vt_ref, kr_ref, o_ref, qt_scr, s_scr, m_scr, acc_scr,
                  *, tq, tk, l, past):
    ch = acc_scr.shape[0]
    qpos0 = past + pl.program_id(1) * tq
    ntiles = _visible_tiles(qpos0, tq, l, tk)
    nfull = jnp.minimum(l, (qpos0 // CHUNK + 1) * CHUNK) // tk
    qchunk = (qpos0 + lax.broadcasted_iota(I32, (1, tq), 1)) // CHUNK
    krow = lax.broadcasted_iota(I32, (tk, 1), 0)
    c = (C_NOPE + HEAD) ** -0.5 * LOG2E
    for s in range(ch // 2):
        pair = _split_heads_t(qr_ref[:, s * LANE:(s + 1) * LANE].T)
        for half in range(2):
            h = 2 * s + half
            qt_scr[h] = jnp.concatenate([qn_ref[:, h * LANE:(h + 1) * LANE].T, pair[half]], axis=0)
    _flash_t_init(m_scr, acc_scr)

    def tile(kt, masked):
        k0 = pl.multiple_of(kt * tk, tk)
        k_rope = kr_ref[pl.ds(k0, tk), :]
        for h in range(ch):
            kcat = jnp.concatenate([kn_ref[pl.ds(k0, tk), h * LANE:(h + 1) * LANE], k_rope], axis=1)
            s_scr[h] = jnp.dot(kcat, qt_scr[h], preferred_element_type=F32)
        if masked:
            kpos = k0 + krow
            bias = jnp.where((kpos // CHUNK <= qchunk) & (kpos < l), 0.0, MASKED)
        for h in range(ch):
            s = s_scr[h] + bias if masked else s_scr[h]
            _flash_t_step(h, s, vt_ref[h * C_V:(h + 1) * C_V, pl.ds(k0, tk)], c, m_scr, acc_scr)
        return 0

    lax.fori_loop(0, nfull, lambda kt, _: tile(kt, False), 0)
    lax.fori_loop(nfull, ntiles, lambda kt, _: tile(kt, True), 0)
    _flash_t_out(o_ref, acc_scr)


def _mla_attention_t(qn, qr, kn, vt, kr, grp, tq):
    b, t, wn = qn.shape
    ch = wn // C_NOPE
    lp, tk = _key_tile(t)
    assert lp == t and ch % 2 == 0
    qblk = lambda w: pl.BlockSpec((None, tq, w), lambda i, j: (i, j, 0))
    kblk = lambda w: _resident((None, t, w), lambda i, j: (i, 0, 0))
    return pl.pallas_call(
        functools.partial(_mla_t_kernel, tq=tq, tk=tk, l=t, past=grp.past),
        grid=(b, t // tq),
        in_specs=[qblk(wn), qblk(ch * HEAD), kblk(wn), _resident((wn, t), lambda i, j: (0, i)), kblk(LANE)],
        out_specs=qblk(wn),
        out_shape=jax.ShapeDtypeStruct((b, t, wn), BF16),
        scratch_shapes=[pltpu.VMEM((ch, 2 * LANE, tq), BF16),
                        pltpu.VMEM((ch, tk, tq), F32),
                        pltpu.VMEM((ch, 1, tq), F32),
                        pltpu.VMEM((ch, C_V + ONES_ROWS, tq), F32)],
        compiler_params=_params(("arbitrary", "arbitrary")),
        name="mla_attention_t",
    )(qn, qr, kn, vt, kr)


def _causal_conv(u, e0, e1, w, seg):
    rmod = lax.broadcasted_iota(I32, (u.shape[0], 1), 0) % seg
    u1 = jnp.where(rmod == 0, e1, pltpu.roll(u, 1, 0))
    u2 = jnp.where(rmod == 0, e0, jnp.where(rmod == 1, e1, pltpu.roll(u, 2, 0)))
    return u2 * w[0:1] + u1 * w[1:2] + u * w[2:3]


def _layer_norm(z, g, b):
    mu = jnp.mean(z, axis=-1, keepdims=True)
    zc = z - mu
    var = jnp.mean(zc * zc, axis=-1, keepdims=True)
    return zc * lax.rsqrt(var + LN_EPS) * g + b


def _outproj_kernel(*refs, alpha, seq_tiles, seg, carried):
    if carried:
        (x_ref, oa_ref, bg_ref, u_ref, oc_ref, w_ref, cw_ref, g1_ref, lng_ref, lnb_ref, sc2_ref, sh2_ref,
         x1_ref, h2_ref, prev_scr) = refs

        @pl.when(pl.program_id(0) % seq_tiles == 0)
        def _():
            prev_scr[...] = jnp.zeros_like(prev_scr)

        e0, e1 = prev_scr[SUBLANE - 2:SUBLANE - 1, :], prev_scr[SUBLANE - 1:SUBLANE, :]
    else:
        (x_ref, oa_ref, bg_ref, u_ref, oc_ref, w_ref, cw_ref, g1_ref, lng_ref, lnb_ref, sc2_ref, sh2_ref,
         e0_ref, e1_ref, x1_ref, h2_ref) = refs
        e0, e1 = e0_ref[...], e1_ref[...]
    u = u_ref[...]
    yb = bg_ref[...] * _causal_conv(u, e0, e1, cw_ref[...], seg)
    if carried:
        prev_scr[...] = u[u.shape[0] - SUBLANE:, :]
    mixed = jnp.concatenate([oa_ref[...], yb.astype(BF16), oc_ref[...]], axis=1)
    mix = jnp.dot(mixed, w_ref[...], preferred_element_type=F32)
    x1 = _layer_norm(alpha * x_ref[...] + (1.0 + g1_ref[...]) * mix, lng_ref[...], lnb_ref[...])
    x1_ref[...] = x1
    h2_ref[...] = (x1 * (1.0 + sc2_ref[...]) + sh2_ref[...]).astype(BF16)


def _out_projection(x, oa, bg, u, oc, w_out, conv_w, g1, ln_g, ln_b, sc2, sh2, prev, grp, alpha, tm):
    m, d = x.shape
    aw, bw, cw = oa.shape[1], bg.shape[1], oc.shape[1]
    n_i = m // tm
    tiles_per_mod = n_i // g1.shape[0]
    mod_rows = g1.shape[1]
    row = lambda i: (i, 0)
    fix = lambda i: (0, 0)
    mod_spec = pl.BlockSpec((None, mod_rows, d), lambda i: (i // tiles_per_mod, 0, 0))
    in_specs = [pl.BlockSpec((tm, d), row), pl.BlockSpec((tm, aw), row), pl.BlockSpec((tm, bw), row),
                pl.BlockSpec((tm, bw), row), pl.BlockSpec((tm, cw), row),
                _resident((aw + bw + cw, d), fix), pl.BlockSpec((CONV_W, bw), fix),
                mod_spec, pl.BlockSpec((1, d), fix), pl.BlockSpec((1, d), fix), mod_spec, mod_spec]
    args = [x, oa, bg, u, oc, w_out, conv_w, g1, ln_g.reshape(1, d), ln_b.reshape(1, d), sc2, sh2]
    scratch = []
    if prev is None:
        assert grp.t % tm == 0
        scratch = [pltpu.VMEM((SUBLANE, bw), F32)]
    else:
        assert tm % grp.t == 0
        in_specs += [pl.BlockSpec((tm, bw), row), pl.BlockSpec((tm, bw), row)]
        args += [prev[0], prev[1]]
    return pl.pallas_call(
        functools.partial(_outproj_kernel, alpha=alpha, seq_tiles=max(grp.t // tm, 1), seg=min(grp.t, tm),
                          carried=prev is None),
        grid=(n_i,),
        in_specs=in_specs,
        out_specs=[pl.BlockSpec((tm, d), row), pl.BlockSpec((tm, d), row)],
        out_shape=[jax.ShapeDtypeStruct((m, d), F32), jax.ShapeDtypeStruct((m, d), BF16)],
        scratch_shapes=scratch,
        compiler_params=_params(("arbitrary",)),
        name="out_projection",
    )(*args)


def _ffn_kernel(*refs, alpha, seq_tiles, seg, carried):
    if carried:
        (h_ref, x_ref, wg_ref, wu_ref, wd_ref, cw_ref, g2_ref, lng_ref, lnb_ref,
         o_ref, gt_ref, acc_scr, prev_scr) = refs
    else:
        (h_ref, x_ref, wg_ref, wu_ref, wd_ref, cw_ref, g2_ref, lng_ref, lnb_ref, e0_ref, e1_ref,
         o_ref, gt_ref, acc_scr) = refs
    f = pl.program_id(1)
    h = h_ref[...]
    gate = jnp.dot(h, wg_ref[...], preferred_element_type=F32)
    up = jnp.dot(h, wu_ref[...], preferred_element_type=F32)
    tm = gate.shape[0]
    if carried:
        @pl.when(pl.program_id(0) % seq_tiles == 0)
        def _():
            prev_scr[f] = jnp.zeros(prev_scr.shape[1:], F32)

        e0, e1 = prev_scr[f, SUBLANE - 2:SUBLANE - 1, :], prev_scr[f, SUBLANE - 1:SUBLANE, :]
    else:
        e0, e1 = e0_ref[...], e1_ref[...]
    conv = _causal_conv(gate, e0, e1, cw_ref[...], seg)
    if carried:
        prev_scr[f] = gate[tm - SUBLANE:, :]
    gt_ref[...] = gate[tm - gt_ref.shape[0]:, :]
    act = (jax.nn.silu(conv) * up).astype(BF16)
    part = jnp.dot(act, wd_ref[...], preferred_element_type=F32)

    @pl.when(f == 0)
    def _():
        acc_scr[...] = part

    @pl.when(f > 0)
    def _():
        acc_scr[...] += part

    @pl.when(f == pl.num_programs(1) - 1)
    def _():
        z = alpha * x_ref[...] + (1.0 + g2_ref[...]) * acc_scr[...]
        o_ref[...] = _layer_norm(z, lng_ref[...], lnb_ref[...])


def _channel_mixer(h2, x1, w_gu, w_down, conv_w, g2, ln_g, ln_b, prev, grp, alpha, tm, tf):
    m, d = x1.shape
    dff = w_down.shape[0]
    n_i, n_f = m // tm, dff // tf
    tiles_per_mod = n_i // g2.shape[0]
    mod_rows = g2.shape[1]
    fix = lambda i, f: (0, 0)
    in_specs = [pl.BlockSpec((tm, d), lambda i, f: (i, 0)), pl.BlockSpec((tm, d), lambda i, f: (i, 0)),
                pl.BlockSpec((d, tf), lambda i, f: (0, f)), pl.BlockSpec((d, tf), lambda i, f: (0, n_f + f)),
                pl.BlockSpec((tf, d), lambda i, f: (f, 0)), pl.BlockSpec((CONV_W, tf), lambda i, f: (0, f)),
                pl.BlockSpec((None, mod_rows, d), lambda i, f: (i // tiles_per_mod, 0, 0)),
                pl.BlockSpec((1, d), fix), pl.BlockSpec((1, d), fix)]
    args = [h2, x1, w_gu, w_gu, w_down, conv_w, g2, ln_g.reshape(1, d), ln_b.reshape(1, d)]
    scratch = [pltpu.VMEM((tm, d), F32)]
    if prev is None:
        assert grp.t % tm == 0
        scratch.append(pltpu.VMEM((n_f, SUBLANE, tf), F32))
        gt_spec = pl.BlockSpec((None, SUBLANE, tf), lambda i, f: (i, 0, f))
        gt_shape = jax.ShapeDtypeStruct((n_i, SUBLANE, dff), F32)
    else:
        assert tm % grp.t == 0
        in_specs += [pl.BlockSpec((tm, tf), lambda i, f: (i, f))] * 2
        args += [prev[0], prev[1]]
        gt_spec = pl.BlockSpec((tm, tf), lambda i, f: (i, f))
        gt_shape = jax.ShapeDtypeStruct((m, dff), F32)
    return pl.pallas_call(
        functools.partial(_ffn_kernel, alpha=alpha, seq_tiles=max(grp.t // tm, 1), seg=min(grp.t, tm),
                          carried=prev is None),
        grid=(n_i, n_f),
        in_specs=in_specs,
        out_specs=[pl.BlockSpec((tm, d), lambda i, f: (i, 0)), gt_spec],
        out_shape=[jax.ShapeDtypeStruct((m, d), F32), gt_shape],
        scratch_shapes=scratch,
        compiler_params=_params(("arbitrary", "arbitrary")),
        name="channel_mixer",
    )(*args)


def _rope_tables(pos):
    half = HEAD // 2
    inv = jnp.power(jnp.float32(ROPE_THETA), -jnp.arange(half, dtype=F32) / half)
    ang = pos.astype(F32)[:, None] * inv[None, :]
    cos, sin = jnp.cos(ang), jnp.sin(ang)
    zero = jnp.zeros_like(sin)
    reps = LANE // HEAD
    return (jnp.tile(jnp.concatenate([cos, cos], axis=1), (1, reps)),
            jnp.tile(jnp.concatenate([zero, sin], axis=1), (1, reps)),
            jnp.tile(jnp.concatenate([-sin, zero], axis=1), (1, reps)))


def _with_past(past, new, lp, twice=False):
    b, t, w = new.shape
    parts = [new]
    if past is not None:
        p = past.reshape(b, past.shape[1], -1).astype(BF16)
        parts = [jnp.concatenate([p, p], axis=-1) if twice else p, new]
    n = sum(a.shape[1] for a in parts)
    if lp > n:
        parts.append(jnp.zeros((b, lp - n, w), BF16))
    return parts[0] if len(parts) == 1 else jnp.concatenate(parts, axis=1)


def _layer(x, mod, tables, grp, past, wts, dm, alpha):
    (w_in_p, w_out, conv_b_w, kv_norm, w_uk, w_uv, ln1_g, ln1_b, ln2_g, ln2_b, w_gu, conv_f_w, w_down) = wts
    b, t = grp.b, grp.t
    m, d = x.shape
    carried = past is None
    tm = _row_tile(t, 256) if carried else m
    if carried:
        mods = [a.reshape(b, 1, d) for a in jnp.split(mod, N_MOD, axis=-1)]
    else:
        mods = [jnp.repeat(a, t, axis=0).reshape(1, m, d) for a in jnp.split(mod, N_MOD, axis=-1)]
    sh1, sc1, g1, sh2, sc2, g2 = mods

    (qa, ka, kab, qi, qcr, ki, kib, kr, krb, va, vab, bg, u, qcn, lat, latb, wi, vat) = _in_projection(
        x, sc1, sh1, w_in_p, tables, kv_norm, dm, tm)

    three = lambda a: a.reshape(b, t, a.shape[-1])
    if carried:
        tq = _row_tile(t, 256)
        kn, vct = _kv_up(latb, w_uk, w_uv, v_transposed=True)
        oa = _dsa_attention_t(three(qi), three(wi), three(kib), three(qa), three(kab), vat, grp, tq)
        oc = _mla_attention_t(three(qcn), three(qcr), three(kn), vct, three(krb), grp, tq)
        prev_b = prev_f = None
    else:
        l = grp.past + t
        lp, _ = _key_tile(l)
        p_ak, p_av, p_ik, p_lat, p_kr, prev_b, prev_f = past
        k_all = _with_past(p_ak, three(kab), lp)
        v_all = _with_past(p_av, three(vab), lp)
        ki_all = _with_past(p_ik, three(kib), lp, twice=True)
        kr_all = _with_past(p_kr, three(krb), lp, twice=True)
        lat_all = _with_past(p_lat, three(latb), lp)
        kn, vc = _kv_up(lat_all.reshape(b * lp, dm.r), w_uk, w_uv, v_transposed=False)
        oa = _dsa_attention(three(qi), three(wi), ki_all, three(qa), k_all, v_all, grp, l, t)
        oc = _mla_attention(three(qcn), three(qcr), kn.reshape(b, lp, -1), vc.reshape(b, lp, -1), kr_all,
                            grp, l, t)

    def expand(state):
        return jnp.repeat(state[:, 0], t, axis=0), jnp.repeat(state[:, 1], t, axis=0)

    x1, h2 = _out_projection(x, oa.reshape(m, -1), bg, u, oc.reshape(m, -1), w_out, conv_b_w, g1, ln1_g, ln1_b,
                             sc2, sh2, None if carried else expand(prev_b), grp, alpha, tm)
    tm_f = _row_tile(t, 512) if carried else m
    tf = _row_tile(dm.dff, 512)
    x2, gate_rows = _channel_mixer(h2, x1, w_gu, w_down, conv_f_w, g2, ln2_g, ln2_b,
                                   None if carried else expand(prev_f), grp, alpha, tm_f, tf)
    if carried:
        new_f = gate_rows.reshape(b, t // tm_f, SUBLANE, dm.dff)[:, -1, SUBLANE - (CONV_W - 1):, :]
    else:
        new_f = gate_rows.reshape(b, t, dm.dff)[:, t - (CONV_W - 1):, :]
    new_b = u.reshape(b, t, dm.bw)[:, t - (CONV_W - 1):, :]
    heads = dm.aw // HEAD
    rows = (ka.reshape(b, t, heads, HEAD), va.reshape(b, t, heads, HEAD), ki.reshape(b, t, HEAD),
            lat.reshape(b, t, dm.r), kr.reshape(b, t, HEAD), new_b, new_f)
    return x2, rows


def kernel(x_prompt, x_sample, c_prompt, c_sample, cache_a_k, cache_a_v, cache_idx_k, cache_mla_latent,
           cache_mla_krope, state_conv_b, state_conv_ffn, w_in, w_out, conv_b_w, mla_kv_norm, mla_w_uk,
           mla_w_uv, w_mod, b_mod, ln1_g, ln1_b, ln2_g, ln2_b, ffn_w_gu, ffn_conv_w, ffn_w_down):
    depth, d, _ = w_in.shape
    a_heads = cache_a_k.shape[3]
    dm = Dims(d=d, aw=a_heads * HEAD, bw=conv_b_w.shape[2], ch=mla_w_uk.shape[2] // C_NOPE,
              r=mla_w_uk.shape[1], dff=ffn_w_down.shape[1], depth=depth)
    alpha = (2 * depth) ** 0.25
    grp_p = Group(b=x_prompt.shape[0], t=x_prompt.shape[1], past=0)
    grp_s = Group(b=x_sample.shape[0], t=x_sample.shape[1], past=cache_a_k.shape[2])

    n_c = grp_p.b + grp_s.b
    c_all = jnp.concatenate([c_prompt, c_sample, jnp.zeros((-n_c % SUBLANE, d), F32)], axis=0)
    mod = _modulation(c_all, w_mod, b_mod)

    tab_p = _rope_tables(jnp.arange(grp_p.t, dtype=I32))
    tab_s = tuple(jnp.tile(a, (grp_s.b, 1)) for a in _rope_tables(grp_s.past + jnp.arange(grp_s.t, dtype=I32)))

    xp = x_prompt.reshape(grp_p.b * grp_p.t, d)
    xs = x_sample.reshape(grp_s.b * grp_s.t, d)
    rows_p, rows_s = [], []
    for l in range(depth):
        wts = (_pack_w_in(w_in[l], dm), w_out[l].astype(BF16), conv_b_w[l], mla_kv_norm[l],
               mla_w_uk[l].astype(BF16), mla_w_uv[l].astype(BF16), ln1_g[l], ln1_b[l], ln2_g[l], ln2_b[l],
               ffn_w_gu[l].astype(BF16), ffn_conv_w[l], ffn_w_down[l].astype(BF16))
        xp, rp = _layer(xp, mod[l, :grp_p.b], tab_p, grp_p, None, wts, dm, alpha)
        past_l = (cache_a_k[l], cache_a_v[l], cache_idx_k[l], cache_mla_latent[l], cache_mla_krope[l],
                  state_conv_b[l], state_conv_ffn[l])
        xs, rs = _layer(xs, mod[l, grp_p.b:n_c], tab_s, grp_s, past_l, wts, dm, alpha)
        rows_p.append(rp)
        rows_s.append(rs)
    outs_p = [jnp.stack(r) for r in zip(*rows_p)]
    outs_s = [jnp.stack(r) for r in zip(*rows_s)]
    return (xp.reshape(x_prompt.shape), xs.reshape(x_sample.shape), *outs_p, *outs_s)
```

```python
import functools
from typing import NamedTuple

import numpy as np
import jax
import jax.numpy as jnp
from jax import lax
from jax.experimental import pallas as pl
from jax.experimental.pallas import tpu as pltpu

F32, BF16, I32, I16 = jnp.float32, jnp.bfloat16, jnp.int32, jnp.int16

CHUNK = 64
CONV_W = 3
ROPE_THETA = 10000.0
HEAD = 64
IDX_HEADS = 16
TOPK_MAX = 256
C_NOPE = 128
C_V = 128
N_MOD = 6
LN_EPS = 1e-5
RMS_EPS = 1e-6

LANE = 128
SUBLANE = 8
VMEM_LIMIT = 50 * 1024 * 1024

MASKED = -1e30
LOG2E = 1.4426950408889634
INT_MIN = -2 ** 31
I16_MIN = -2 ** 15
KEY_NEG_INF = int(np.array(-np.inf, np.float32).view(np.int32)) ^ 0x7FFFFFFF


class Dims(NamedTuple):
    d: int
    aw: int
    bw: int
    ch: int
    r: int
    dff: int
    depth: int


class Group(NamedTuple):
    b: int
    t: int
    past: int


def _row_tile(m, pref):
    if m <= pref:
        return m
    t = pref - pref % SUBLANE
    while m % t:
        t -= SUBLANE
    return t


def _key_tile(l):
    lp = -(-l // LANE) * LANE
    for tk in (512, 384, 256, 128):
        if lp % tk == 0:
            return lp, tk
    raise AssertionError(lp)


def _params(sem):
    return pltpu.CompilerParams(dimension_semantics=sem, vmem_limit_bytes=VMEM_LIMIT)


def _resident(shape, index_map):
    return pl.BlockSpec(shape, index_map, pipeline_mode=pl.Buffered(1))


def _mod_kernel(c_ref, w_ref, b_ref, o_ref):
    a = jax.nn.silu(c_ref[...]).astype(BF16)
    o_ref[...] = jnp.dot(a, w_ref[...].astype(BF16), preferred_element_type=F32) + b_ref[...]


def _modulation(c, w_mod, b_mod):
    depth, d, n = w_mod.shape
    rows = c.shape[0]
    tn = _row_tile(n, 1024)
    return pl.pallas_call(
        _mod_kernel,
        grid=(depth, n // tn),
        in_specs=[pl.BlockSpec((rows, d), lambda l, j: (0, 0)),
                  pl.BlockSpec((None, d, tn), lambda l, j: (l, 0, j)),
                  pl.BlockSpec((None, 1, tn), lambda l, j: (l, 0, j))],
        out_specs=pl.BlockSpec((None, rows, tn), lambda l, j: (l, 0, j)),
        out_shape=jax.ShapeDtypeStruct((depth, rows, n), F32),
        compiler_params=_params(("arbitrary", "arbitrary")),
        name="modulation",
    )(c, w_mod, b_mod.reshape(depth, 1, n))


def _pack_w_in(w, dm):
    d, aw, bw, ch, r = dm.d, dm.aw, dm.bw, dm.ch, dm.r
    o = np.cumsum([0, aw, aw, aw, IDX_HEADS * HEAD, HEAD, IDX_HEADS, bw, bw, bw, ch * (C_NOPE + HEAD), r, HEAD])
    qa, ka, va, qi, ki, wi, bg, cg, xb, qc, lat, kr = [w[:, o[i]:o[i + 1]] for i in range(12)]
    qc = qc.reshape(d, ch, C_NOPE + HEAD)
    qcn = qc[:, :, :C_NOPE].reshape(d, ch * C_NOPE)
    qcr = qc[:, :, C_NOPE:].reshape(d, ch * HEAD)
    pad = jnp.zeros((d, LANE - IDX_HEADS), w.dtype)
    return jnp.concatenate([qa, ka, qi, qcr, ki, ki, kr, kr, va, bg, cg, xb, qcn, lat, wi, pad],
                           axis=1).astype(BF16)


def _rope(acc, cos, s1, s2):
    outs = []
    for s in range(acc.shape[1] // LANE):
        xs = acc[:, s * LANE:(s + 1) * LANE]
        outs.append(xs * cos + pltpu.roll(xs, HEAD // 2, 1) * s1 + pltpu.roll(xs, LANE - HEAD // 2, 1) * s2)
    return outs[0] if len(outs) == 1 else jnp.concatenate(outs, axis=1)


def _inproj_kernel(x_ref, sc_ref, sh_ref, w_ref, cos_ref, s1_ref, s2_ref, nrm_ref,
                   qa_ref, ka_ref, kab_ref, qi_ref, qcr_ref, ki_ref, kib_ref, kr_ref, krb_ref,
                   va_ref, vab_ref, bg_ref, u_ref, qcn_ref, lat_ref, latb_ref, wi_ref, vat_ref, *, dm):
    aw, bw, ch, r = dm.aw, dm.bw, dm.ch, dm.r
    h = (x_ref[...] * (1.0 + sc_ref[...]) + sh_ref[...]).astype(BF16)
    cos, s1, s2 = cos_ref[...], s1_ref[...], s2_ref[...]
    col = [0]

    def proj(width):
        c0 = col[0]
        col[0] = c0 + width
        return jnp.dot(h, w_ref[:, c0:c0 + width], preferred_element_type=F32)

    def pieces(width, step=512):
        return [(o, min(step, width - o)) for o in range(0, width, step)]

    for o, wd in pieces(aw):
        qa_ref[:, o:o + wd] = _rope(proj(wd), cos, s1, s2).astype(BF16)
    for o, wd in pieces(aw):
        y = _rope(proj(wd), cos, s1, s2)
        ka_ref[:, o:o + wd] = y
        kab_ref[:, o:o + wd] = y.astype(BF16)
    for o, wd in pieces(IDX_HEADS * HEAD):
        qi_ref[:, o:o + wd] = _rope(proj(wd), cos, s1, s2).astype(BF16)
    for o, wd in pieces(ch * HEAD):
        qcr_ref[:, o:o + wd] = _rope(proj(wd), cos, s1, s2).astype(BF16)
    for f32_ref, b16_ref in ((ki_ref, kib_ref), (kr_ref, krb_ref)):
        y = _rope(proj(LANE), cos, s1, s2)
        f32_ref[...] = y[:, :HEAD]
        b16_ref[...] = y.astype(BF16)
    for o, wd in pieces(aw):
        y = proj(wd)
        va_ref[:, o:o + wd] = y
        vab_ref[:, o:o + wd] = y.astype(BF16)
        vat_ref[o:o + wd, :] = y.T.astype(BF16)
    for o, wd in pieces(bw):
        bg_ref[:, o:o + wd] = proj(wd)
    c_cg = col[0]
    for o, wd in pieces(bw):
        cg = jnp.dot(h, w_ref[:, c_cg + o:c_cg + o + wd], preferred_element_type=F32)
        xb = jnp.dot(h, w_ref[:, c_cg + bw + o:c_cg + bw + o + wd], preferred_element_type=F32)
        u_ref[:, o:o + wd] = cg * xb
    col[0] = c_cg + 2 * bw
    for o, wd in pieces(ch * C_NOPE):
        qcn_ref[:, o:o + wd] = proj(wd).astype(BF16)
    lat = proj(r)
    lat = lat * lax.rsqrt(jnp.mean(lat * lat, axis=-1, keepdims=True) + RMS_EPS) * nrm_ref[...]
    lat_ref[...] = lat
    latb_ref[...] = lat.astype(BF16)
    wi_ref[...] = proj(LANE) * (IDX_HEADS ** -0.5)


def _in_projection(x, sc, sh, w_packed, tables, kv_norm, dm, tm):
    m, d = x.shape
    aw, bw, ch, r = dm.aw, dm.bw, dm.ch, dm.r
    npk = w_packed.shape[1]
    n_i = m // tm
    tiles_per_mod = n_i // sc.shape[0]
    mod_rows = sc.shape[1]
    tab_tiles = tables[0].shape[0] // tm
    widths = [(aw, BF16), (aw, F32), (aw, BF16), (IDX_HEADS * HEAD, BF16), (ch * HEAD, BF16),
              (HEAD, F32), (LANE, BF16), (HEAD, F32), (LANE, BF16),
              (aw, F32), (aw, BF16), (bw, F32), (bw, F32), (ch * C_NOPE, BF16), (r, F32), (r, BF16),
              (LANE, F32)]
    row = lambda i: (i, 0)
    mod_spec = pl.BlockSpec((None, mod_rows, d), lambda i: (i // tiles_per_mod, 0, 0))
    tab_spec = pl.BlockSpec((tm, LANE), lambda i: (i % tab_tiles, 0))
    return pl.pallas_call(
        functools.partial(_inproj_kernel, dm=dm),
        grid=(n_i,),
        in_specs=[pl.BlockSpec((tm, d), row), mod_spec, mod_spec,
                  _resident((d, npk), lambda i: (0, 0)),
                  tab_spec, tab_spec, tab_spec,
                  pl.BlockSpec((1, r), lambda i: (0, 0))],
        out_specs=[pl.BlockSpec((tm, w), row) for w, _ in widths] + [pl.BlockSpec((aw, tm), lambda i: (0, i))],
        out_shape=[jax.ShapeDtypeStruct((m, w), dt) for w, dt in widths] + [jax.ShapeDtypeStruct((aw, m), BF16)],
        compiler_params=_params(("arbitrary",)),
        name="in_projection",
    )(x, sc, sh, w_packed, *tables, kv_norm.reshape(1, r))


def _kvup_kernel(l_ref, wk_ref, wv_ref, k_ref, v_ref, *, v_transposed):
    lat = l_ref[...]
    k_ref[...] = jnp.dot(lat, wk_ref[...], preferred_element_type=F32).astype(BF16)
    if v_transposed:
        v_ref[...] = lax.dot_general(wv_ref[...], lat, _NT, preferred_element_type=F32).astype(BF16)
    else:
        v_ref[...] = jnp.dot(lat, wv_ref[...], preferred_element_type=F32).astype(BF16)


def _kv_up(lat, w_uk, w_uv, v_transposed):
    m, r = lat.shape
    n = w_uk.shape[1]
    tm = _row_tile(m, 512)
    if v_transposed:
        w_v, v_spec, v_shape = w_uv.T, pl.BlockSpec((n, tm), lambda i: (0, i)), (n, m)
    else:
        w_v, v_spec, v_shape = w_uv, pl.BlockSpec((tm, n), lambda i: (i, 0)), (m, n)
    return pl.pallas_call(
        functools.partial(_kvup_kernel, v_transposed=v_transposed),
        grid=(m // tm,),
        in_specs=[pl.BlockSpec((tm, r), lambda i: (i, 0)),
                  pl.BlockSpec((r, n), lambda i: (0, 0)),
                  pl.BlockSpec(w_v.shape, lambda i: (0, 0))],
        out_specs=[pl.BlockSpec((tm, n), lambda i: (i, 0)), v_spec],
        out_shape=[jax.ShapeDtypeStruct((m, n), BF16), jax.ShapeDtypeStruct(v_shape, BF16)],
        compiler_params=_params(("arbitrary",)),
        name="latent_up_projection",
    )(lat, w_uk, w_v)


_NT = (((1,), (1,)), ((), ()))


def _visible_tiles(qpos0, tq, l, tk):
    nvis = jnp.minimum(l, ((qpos0 + tq - 1) // CHUNK + 1) * CHUNK)
    return (nvis + tk - 1) // tk


def _flash_step(carry, s, v):
    m, l, acc = carry
    m_new = jnp.maximum(m, jnp.max(s, axis=1, keepdims=True))
    alpha = jnp.exp(m - m_new)
    p = jnp.exp(s - m_new)
    l = alpha * l + jnp.sum(p, axis=1, keepdims=True)
    acc = alpha * acc + jnp.dot(p.astype(BF16), v, preferred_element_type=F32)
    return m_new, l, acc


def _flash_init(rows, width):
    return (jnp.full((rows, 1), MASKED, F32), jnp.zeros((rows, 1), F32), jnp.zeros((rows, width), F32))


def _ordered_key(x):
    b = pltpu.bitcast(x, I32)
    return jnp.where(b < 0, b ^ 0x7FFFFFFF, b)


def _dsa_kernel(qi_ref, wi_ref, ki_ref, qa_ref, k_ref, v_ref, o_ref, key_scr, bias_scr, thr2_scr,
                *, tq, tk, l, lp, past, topk, aw):
    qpos0 = past + pl.program_id(1) * tq
    ntiles = _visible_tiles(qpos0, tq, l, tk)
    qchunk = (qpos0 + lax.broadcasted_iota(I32, (tq, 1), 0)) // CHUNK
    lane = lax.broadcasted_iota(I32, (1, LANE), 1)
    lo_half = lane < HEAD
    kf = jnp.float32(topk)

    def split_heads(qs):
        zero = jnp.zeros_like(qs)
        return jnp.concatenate([jnp.where(lo_half, qs, zero), jnp.where(lo_half, zero, qs)], axis=0)

    wi = wi_ref[...] * (HEAD ** -0.5)
    q_idx = [split_heads(qi_ref[:, s * LANE:(s + 1) * LANE]) for s in range(IDX_HEADS // 2)]
    w_idx = [wi[:, h:h + 1] for h in range(IDX_HEADS)]

    def score_body(kt, _):
        k0 = pl.multiple_of(kt * tk, tk)
        kt_tile = ki_ref[pl.ds(k0, tk), :]
        score = jnp.zeros((tq, tk), F32)
        for s in range(IDX_HEADS // 2):
            s2 = lax.dot_general(q_idx[s], kt_tile, _NT, preferred_element_type=F32)
            score = score + jnp.maximum(s2[:tq], 0.0) * w_idx[2 * s]
            score = score + jnp.maximum(s2[tq:], 0.0) * w_idx[2 * s + 1]
        kpos = k0 + lax.broadcasted_iota(I32, (1, tk), 1)
        vis = (kpos // CHUNK <= qchunk) & (kpos < l)
        key_scr[kt] = _ordered_key(jnp.where(vis, score, -jnp.inf))
        return 0

    lax.fori_loop(0, ntiles, score_body, 0)

    def count(pred):
        def body(kt, acc):
            for j in range(tk // LANE):
                ks = key_scr[kt, :, j * LANE:(j + 1) * LANE]
                acc = acc + jnp.where(pred(ks, kt * tk + j * LANE), 1.0, 0.0)
            return acc
        acc = lax.fori_loop(0, ntiles, body, jnp.zeros((tq, LANE), F32))
        return jnp.sum(acc, axis=1, keepdims=True)

    def wide(col):
        return jnp.broadcast_to(col, (tq, LANE))

    def bit_body(b, lo):
        cand = lo + jnp.left_shift(jnp.int32(1), 31 - b)
        cand_w = wide(cand)
        c = count(lambda ks, _: ks >= cand_w)
        return jnp.where(c >= kf, cand, lo)

    thr = lax.fori_loop(0, 32, bit_body, jnp.full((tq, 1), INT_MIN, I32))
    thr_w = wide(thr)

    n_ge = count(lambda ks, _: ks >= thr_w)
    n_gt = count(lambda ks, _: ks > thr_w)
    excess = (n_ge > kf) & (thr > KEY_NEG_INF)
    thr2_scr[...] = jnp.zeros((tq, 1), I32)

    @pl.when(jnp.max(jnp.where(excess, 1.0, 0.0)) > 0.0)
    def _():
        need = kf - n_gt
        nbits = lp.bit_length()

        def bit2_body(b, lo):
            cand = lo + jnp.left_shift(jnp.int32(1), nbits - 1 - b)
            cand_w = wide(cand)
            c = count(lambda ks, base: jnp.where(ks == thr_w, lp - (base + lane), 0) >= cand_w)
            return jnp.where(c >= need, cand, lo)

        thr2_scr[...] = lax.fori_loop(0, nbits, bit2_body, jnp.zeros((tq, 1), I32))

    thr2_w = wide(thr2_scr[...])

    def bias_body(kt, _):
        for j in range(tk // LANE):
            ks = key_scr[kt, :, j * LANE:(j + 1) * LANE]
            tie = jnp.where(lp - (kt * tk + j * LANE + lane) >= thr2_w, 0.0, MASKED)
            bias = jnp.where(ks > thr_w, 0.0, jnp.where(ks == thr_w, tie, MASKED))
            bias_scr[kt, :, j * LANE:(j + 1) * LANE] = jnp.where(ks > KEY_NEG_INF, bias, MASKED)
        return 0

    lax.fori_loop(0, ntiles, bias_body, 0)

    for pr in range(aw // LANE):
        cols = slice(pr * LANE, (pr + 1) * LANE)
        q2 = split_heads(qa_ref[:, cols])

        def att_body(kt, carry, cols=cols, q2=q2):
            k0 = pl.multiple_of(kt * tk, tk)
            s = lax.dot_general(q2, k_ref[pl.ds(k0, tk), cols], _NT, preferred_element_type=F32)
            bias = bias_scr[kt]
            s = s * (HEAD ** -0.5) + jnp.concatenate([bias, bias], axis=0)
            return _flash_step(carry, s, v_ref[pl.ds(k0, tk), cols])

        _, den, acc = lax.fori_loop(0, ntiles, att_body, _flash_init(2 * tq, LANE))
        o2 = acc / den
        o_ref[:, cols] = jnp.where(lo_half, o2[:tq], o2[tq:]).astype(BF16)


def _dsa_attention(qi, wi, ki, qa, k, v, grp, l, tq):
    b, t, aw = qa.shape
    lp = k.shape[1]
    _, tk = _key_tile(l)
    topk = min(TOPK_MAX, l // 4)
    qblk = lambda w: pl.BlockSpec((None, tq, w), lambda i, j: (i, j, 0))
    kblk = lambda w: pl.BlockSpec((None, lp, w), lambda i, j: (i, 0, 0))
    return pl.pallas_call(
        functools.partial(_dsa_kernel, tq=tq, tk=tk, l=l, lp=lp, past=grp.past, topk=topk, aw=aw),
        grid=(b, t // tq),
        in_specs=[qblk(IDX_HEADS * HEAD), qblk(LANE), kblk(LANE), qblk(aw), kblk(aw), kblk(aw)],
        out_specs=qblk(aw),
        out_shape=jax.ShapeDtypeStruct((b, t, aw), BF16),
        scratch_shapes=[pltpu.VMEM((lp // tk, tq, tk), I32),
                        pltpu.VMEM((lp // tk, tq, tk), F32),
                        pltpu.VMEM((tq, 1), I32)],
        compiler_params=_params(("arbitrary", "arbitrary")),
        name="dsa_attention",
    )(qi, wi, ki, qa, k, v)


def _mla_kernel(qn_ref, qr_ref, kn_ref, v_ref, kr_ref, o_ref, *, tq, tk, l, past, ch):
    qpos0 = past + pl.program_id(1) * tq
    ntiles = _visible_tiles(qpos0, tq, l, tk)
    qchunk = (qpos0 + lax.broadcasted_iota(I32, (tq, 1), 0)) // CHUNK
    lo_half = lax.broadcasted_iota(I32, (1, LANE), 1) < HEAD
    scale = (C_NOPE + HEAD) ** -0.5
    for h in range(ch):
        cols = slice(h * LANE, (h + 1) * LANE)
        qs = qr_ref[:, (h // 2) * LANE:(h // 2 + 1) * LANE]
        zero = jnp.zeros_like(qs)
        qr = jnp.where(lo_half, qs, zero) if h % 2 == 0 else jnp.where(lo_half, zero, qs)
        qcat = jnp.concatenate([qn_ref[:, cols], qr], axis=1)

        def body(kt, carry, cols=cols, qcat=qcat):
            k0 = pl.multiple_of(kt * tk, tk)
            kcat = jnp.concatenate([kn_ref[pl.ds(k0, tk), cols], kr_ref[pl.ds(k0, tk), :]], axis=1)
            s = lax.dot_general(qcat, kcat, _NT, preferred_element_type=F32) * scale
            kpos = k0 + lax.broadcasted_iota(I32, (1, tk), 1)
            vis = (kpos // CHUNK <= qchunk) & (kpos < l)
            return _flash_step(carry, jnp.where(vis, s, MASKED), v_ref[pl.ds(k0, tk), cols])

        _, den, acc = lax.fori_loop(0, ntiles, body, _flash_init(tq, LANE))
        o_ref[:, cols] = (acc / den).astype(BF16)


def _mla_attention(qn, qr, kn, v, kr, grp, l, tq):
    b, t, wn = qn.shape
    ch = wn // C_NOPE
    lp = kn.shape[1]
    _, tk = _key_tile(l)
    qblk = lambda w: pl.BlockSpec((None, tq, w), lambda i, j: (i, j, 0))
    kblk = lambda w: _resident((None, lp, w), lambda i, j: (i, 0, 0))
    return pl.pallas_call(
        functools.partial(_mla_kernel, tq=tq, tk=tk, l=l, past=grp.past, ch=ch),
        grid=(b, t // tq),
        in_specs=[qblk(wn), qblk(ch * HEAD), kblk(wn), kblk(wn), kblk(LANE)],
        out_specs=qblk(wn),
        out_shape=jax.ShapeDtypeStruct((b, t, wn), BF16),
        compiler_params=_params(("arbitrary", "arbitrary")),
        name="mla_attention",
    )(qn, qr, kn, v, kr)


ONES_ROWS = 16


def _flash_t_init(m_scr, acc_scr):
    m_scr[...] = jnp.full(m_scr.shape, MASKED, F32)
    acc_scr[...] = jnp.zeros(acc_scr.shape, F32)


def _flash_t_stage(slot, h, s, s_scr):
    s_scr[slot, h] = s
    return jnp.max(s, axis=0, keepdims=True)


def _flash_t_step(slot, h, s_max, vt, c, s_scr, m_scr, acc_scr):
    m_old = m_scr[h]
    m_new = jnp.maximum(m_old, s_max)
    alpha = jnp.exp2((m_old - m_new) * c)
    p = jnp.exp2((s_scr[slot, h] - m_new) * c).astype(BF16)
    vt_ones = jnp.concatenate([vt, jnp.ones((ONES_ROWS, vt.shape[1]), BF16)], axis=0)
    acc_scr[h] = alpha * acc_scr[h] + jnp.dot(vt_ones, p, preferred_element_type=F32)
    m_scr[h] = m_new


def _flash_t_out(o_ref, acc_scr):
    heads, width = acc_scr.shape[0], acc_scr.shape[1] - ONES_ROWS
    ot = jnp.concatenate([acc_scr[h, :width, :] / acc_scr[h, width:width + 1, :] for h in range(heads)], axis=0)
    o_ref[...] = ot.T.astype(BF16)


def _split_heads_t(slab_t):
    row_lo = lax.broadcasted_iota(I32, (LANE, 1), 0) < HEAD
    zero = jnp.zeros_like(slab_t)
    return jnp.where(row_lo, slab_t, zero), jnp.where(row_lo, zero, slab_t)


def _dsa_t_kernel(qi_ref, wi_ref, ki_ref, qa_ref, k_ref, vt_ref, o_ref,
                  key_scr, hi_scr, lo_scr, bias_scr, thr2_scr, qit_scr, qat_scr, s_scr, m_scr, acc_scr,
                  *, tq, tk, l, past, topk):
    heads = acc_scr.shape[0]
    qpos0 = past + pl.program_id(1) * tq
    ntiles = _visible_tiles(qpos0, tq, l, tk)
    qchunk = (qpos0 + lax.broadcasted_iota(I32, (1, tq), 1)) // CHUNK
    krow = lax.broadcasted_iota(I32, (tk, 1), 0)
    kf = jnp.float32(topk)

    for s in range(IDX_HEADS // 2):
        qit_scr[2 * s], qit_scr[2 * s + 1] = _split_heads_t(qi_ref[:, s * LANE:(s + 1) * LANE].T)
    for s in range(heads // 2):
        slab_t = (qa_ref[:, s * LANE:(s + 1) * LANE].astype(F32) * (HEAD ** -0.5)).astype(BF16).T
        qat_scr[2 * s], qat_scr[2 * s + 1] = _split_heads_t(slab_t)
    w_t = wi_ref[...].T * (HEAD ** -0.5)
    w_rows = [w_t[h:h + 1, :] for h in range(IDX_HEADS)]

    def score_body(kt, _):
        k0 = pl.multiple_of(kt * tk, tk)
        ki_tile = ki_ref[pl.ds(k0, tk), :]
        score = jnp.zeros((tk, tq), F32)
        for h in range(IDX_HEADS):
            s = jnp.dot(ki_tile, qit_scr[h], preferred_element_type=F32)
            score = score + jnp.maximum(s, 0.0) * w_rows[h]
        kpos = k0 + krow
        vis = (kpos // CHUNK <= qchunk) & (kpos < l)
        key = _ordered_key(jnp.where(vis, score, -jnp.inf))
        key_scr[kt] = key
        hi_scr[kt] = (key >> 16).astype(I16)
        lo_scr[kt] = ((key & 0xFFFF) + I16_MIN).astype(I16)
        return 0

    lax.fori_loop(0, ntiles, score_body, 0)

    def count(pred):
        def body(kt, acc):
            hit = jnp.where(pred(key_scr[kt], kt * tk), 1.0, 0.0)
            return acc + hit.reshape(tk // SUBLANE, SUBLANE, tq).sum(axis=0)
        acc = lax.fori_loop(0, ntiles, body, jnp.zeros((SUBLANE, tq), F32))
        return jnp.sum(acc, axis=0, keepdims=True)

    def count16(half_scr, cand, strict):
        rows = 2 * SUBLANE
        cand16 = cand.astype(I16)

        def body(kt, acc):
            half = half_scr[kt]
            hit = jnp.where(half > cand16 if strict else half >= cand16, jnp.int16(1), jnp.int16(0))
            parts = hit.reshape(tk // (4 * rows), 4, rows, tq)
            for g in range(parts.shape[0]):
                acc = acc + parts[g]
            return acc

        acc = lax.fori_loop(0, ntiles, body, jnp.zeros((4, rows, tq), I16))
        return acc.astype(I32).sum(axis=0).sum(axis=0, keepdims=True)

    def kth_largest16(half_scr, k_need):
        def bit_body(b, lo):
            cand = lo + jnp.left_shift(jnp.int32(1), 15 - b)
            return jnp.where(count16(half_scr, cand, False) >= k_need, cand, lo)
        return lax.fori_loop(0, 16, bit_body, jnp.full((1, tq), I16_MIN, I32))

    thr_hi = kth_largest16(hi_scr, jnp.full((1, tq), topk, I32))
    need_lo = topk - count16(hi_scr, thr_hi, True)
    thr_hi16 = thr_hi.astype(I16)

    def mark_body(kt, _):
        lo_scr[kt] = jnp.where(hi_scr[kt] == thr_hi16, lo_scr[kt], jnp.int16(I16_MIN))
        return 0

    lax.fori_loop(0, ntiles, mark_body, 0)
    thr = thr_hi * 65536 + (kth_largest16(lo_scr, need_lo) - I16_MIN)

    lp = key_scr.shape[0] * tk
    n_ge = count(lambda ks, _: ks >= thr)
    n_gt = count(lambda ks, _: ks > thr)
    excess = (n_ge > kf) & (thr > KEY_NEG_INF)
    thr2_scr[...] = jnp.zeros((1, tq), I32)

    @pl.when(jnp.max(jnp.where(excess, 1.0, 0.0)) > 0.0)
    def _():
        need = kf - n_gt
        nbits = lp.bit_length()

        def bit2_body(b, lo):
            cand = lo + jnp.left_shift(jnp.int32(1), nbits - 1 - b)
            c = count(lambda ks, base: jnp.where(ks == thr, lp - (base + krow), 0) >= cand)
            return jnp.where(c >= need, cand, lo)

        thr2_scr[...] = lax.fori_loop(0, nbits, bit2_body, jnp.zeros((1, tq), I32))

    thr2 = thr2_scr[...]

    def bias_body(kt, _):
        ks = key_scr[kt]
        tie = jnp.where(lp - (kt * tk + krow) >= thr2, 0.0, MASKED)
        bias = jnp.where(ks > thr, 0.0, jnp.where(ks == thr, tie, MASKED))
        bias_scr[kt] = jnp.where(ks > KEY_NEG_INF, bias, MASKED)
        return 0

    lax.fori_loop(0, ntiles, bias_body, 0)

    _flash_t_init(m_scr, acc_scr)

    def att_body(kt, _):
        k0 = pl.multiple_of(kt * tk, tk)
        slot = kt % 2
        s_max = []
        for h in range(heads):
            cols = slice((h // 2) * LANE, (h // 2 + 1) * LANE)
            s = jnp.dot(k_ref[pl.ds(k0, tk), cols], qat_scr[h], preferred_element_type=F32) + bias_scr[kt]
            s_max.append(_flash_t_stage(slot, h, s, s_scr))
        for h in range(heads):
            _flash_t_step(slot, h, s_max[h], vt_ref[h * HEAD:(h + 1) * HEAD, pl.ds(k0, tk)], LOG2E,
                          s_scr, m_scr, acc_scr)
        return 0

    lax.fori_loop(0, ntiles, att_body, 0)
    _flash_t_out(o_ref, acc_scr)


def _dsa_attention_t(qi, wi, ki, qa, k, vt, grp, tq):
    b, t, aw = qa.shape
    lp, tk = _key_tile(t)
    assert lp == t
    heads = aw // HEAD
    topk = min(TOPK_MAX, t // 4)
    qblk = lambda w: pl.BlockSpec((None, tq, w), lambda i, j: (i, j, 0))
    kblk = lambda w: pl.BlockSpec((None, t, w), lambda i, j: (i, 0, 0))
    return pl.pallas_call(
        functools.partial(_dsa_t_kernel, tq=tq, tk=tk, l=t, past=grp.past, topk=topk),
        grid=(b, t // tq),
        in_specs=[qblk(IDX_HEADS * HEAD), qblk(LANE), kblk(LANE), qblk(aw), kblk(aw),
                  pl.BlockSpec((aw, t), lambda i, j: (0, i))],
        out_specs=qblk(aw),
        out_shape=jax.ShapeDtypeStruct((b, t, aw), BF16),
        scratch_shapes=[pltpu.VMEM((t // tk, tk, tq), I32),
                        pltpu.VMEM((t // tk, tk, tq), I16),
                        pltpu.VMEM((t // tk, tk, tq), I16),
                        pltpu.VMEM((t // tk, tk, tq), F32),
                        pltpu.VMEM((1, tq), I32),
                        pltpu.VMEM((IDX_HEADS, LANE, tq), BF16),
                        pltpu.VMEM((heads, LANE, tq), BF16),
                        pltpu.VMEM((2, heads, tk, tq), F32),
                        pltpu.VMEM((heads, 1, tq), F32),
                        pltpu.VMEM((heads, HEAD + ONES_ROWS, tq), F32)],
        compiler_params=_params(("arbitrary", "arbitrary")),
        name="dsa_attention_t",
    )(qi, wi, ki, qa, k, vt)


def _mla_t_kernel(qn_ref, qr_ref, kn_ref, vt_ref, kr_ref, o_ref, qt_scr, s_scr, m_scr, acc_scr,
                  *, tq, tk, l, past):
    ch = acc_scr.shape[0]
    qpos0 = past + pl.program_id(1) * tq
    ntiles = _visible_tiles(qpos0, tq, l, tk)
    nfull = jnp.minimum(l, (qpos0 // CHUNK + 1) * CHUNK) // tk
    qchunk = (qpos0 + lax.broadcasted_iota(I32, (1, tq), 1)) // CHUNK
    krow = lax.broadcasted_iota(I32, (tk, 1), 0)
    c = (C_NOPE + HEAD) ** -0.5 * LOG2E
    for s in range(ch // 2):
        pair = _split_heads_t(qr_ref[:, s * LANE:(s + 1) * LANE].T)
        for half in range(2):
            h = 2 * s + half
            qt_scr[h] = jnp.concatenate([qn_ref[:, h * LANE:(h + 1) * LANE].T, pair[half]], axis=0)
    _flash_t_init(m_scr, acc_scr)

    def tile(kt, masked):
        k0 = pl.multiple_of(kt * tk, tk)
        k_rope = kr_ref[pl.ds(k0, tk), :]
        slot = kt % 2
        if masked:
            kpos = k0 + krow
            bias = jnp.where((kpos // CHUNK <= qchunk) & (kpos < l), 0.0, MASKED)
        s_max = []
        for h in range(ch):
            kcat = jnp.concatenate([kn_ref[pl.ds(k0, tk), h * LANE:(h + 1) * LANE], k_rope], axis=1)
            s = jnp.dot(kcat, qt_scr[h], preferred_element_type=F32)
            s_max.append(_flash_t_stage(slot, h, s + bias if masked else s, s_scr))
        for h in range(ch):
            _flash_t_step(slot, h, s_max[h], vt_ref[h * C_V:(h + 1) * C_V, pl.ds(k0, tk)], c,
                          s_scr, m_scr, acc_scr)
        return 0

    lax.fori_loop(0, nfull, lambda kt, _: tile(kt, False), 0)
    lax.fori_loop(nfull, ntiles, lambda kt, _: tile(kt, True), 0)
    _flash_t_out(o_ref, acc_scr)


def _mla_attention_t(qn, qr, kn, vt, kr, grp, tq):
    b, t, wn = qn.shape
    ch = wn // C_NOPE
    lp, tk = _key_tile(t)
    assert lp == t and ch % 2 == 0
    qblk = lambda w: pl.BlockSpec((None, tq, w), lambda i, j: (i, j, 0))
    kblk = lambda w: _resident((None, t, w), lambda i, j: (i, 0, 0))
    return pl.pallas_call(
        functools.partial(_mla_t_kernel, tq=tq, tk=tk, l=t, past=grp.past),
        grid=(b, t // tq),
        in_specs=[qblk(wn), qblk(ch * HEAD), kblk(wn), _resident((wn, t), lambda i, j: (0, i)), kblk(LANE)],
        out_specs=qblk(wn),
        out_shape=jax.ShapeDtypeStruct((b, t, wn), BF16),
        scratch_shapes=[pltpu.VMEM((ch, 2 * LANE, tq), BF16),
                        pltpu.VMEM((2, ch, tk, tq), F32),
                        pltpu.VMEM((ch, 1, tq), F32),
                        pltpu.VMEM((ch, C_V + ONES_ROWS, tq), F32)],
        compiler_params=_params(("arbitrary", "arbitrary")),
        name="mla_attention_t",
    )(qn, qr, kn, vt, kr)


def _causal_conv(u, e0, e1, w, seg):
    rmod = lax.broadcasted_iota(I32, (u.shape[0], 1), 0) % seg
    u1 = jnp.where(rmod == 0, e1, pltpu.roll(u, 1, 0))
    u2 = jnp.where(rmod == 0, e0, jnp.where(rmod == 1, e1, pltpu.roll(u, 2, 0)))
    return u2 * w[0:1] + u1 * w[1:2] + u * w[2:3]


def _layer_norm(z, g, b):
    mu = jnp.mean(z, axis=-1, keepdims=True)
    zc = z - mu
    var = jnp.mean(zc * zc, axis=-1, keepdims=True)
    return zc * lax.rsqrt(var + LN_EPS) * g + b


def _outproj_kernel(*refs, alpha, seq_tiles, seg, carried):
    if carried:
        (x_ref, oa_ref, bg_ref, u_ref, oc_ref, w_ref, cw_ref, g1_ref, lng_ref, lnb_ref, sc2_ref, sh2_ref,
         x1_ref, h2_ref, prev_scr) = refs

        @pl.when(pl.program_id(0) % seq_tiles == 0)
        def _():
            prev_scr[...] = jnp.zeros_like(prev_scr)

        e0, e1 = prev_scr[SUBLANE - 2:SUBLANE - 1, :], prev_scr[SUBLANE - 1:SUBLANE, :]
    else:
        (x_ref, oa_ref, bg_ref, u_ref, oc_ref, w_ref, cw_ref, g1_ref, lng_ref, lnb_ref, sc2_ref, sh2_ref,
         e0_ref, e1_ref, x1_ref, h2_ref) = refs
        e0, e1 = e0_ref[...], e1_ref[...]
    u = u_ref[...]
    yb = bg_ref[...] * _causal_conv(u, e0, e1, cw_ref[...], seg)
    if carried:
        prev_scr[...] = u[u.shape[0] - SUBLANE:, :]
    mixed = jnp.concatenate([oa_ref[...], yb.astype(BF16), oc_ref[...]], axis=1)
    mix = jnp.dot(mixed, w_ref[...], preferred_element_type=F32)
    x1 = _layer_norm(alpha * x_ref[...] + (1.0 + g1_ref[...]) * mix, lng_ref[...], lnb_ref[...])
    x1_ref[...] = x1
    h2_ref[...] = (x1 * (1.0 + sc2_ref[...]) + sh2_ref[...]).astype(BF16)


def _out_projection(x, oa, bg, u, oc, w_out, conv_w, g1, ln_g, ln_b, sc2, sh2, prev, grp, alpha, tm):
    m, d = x.shape
    aw, bw, cw = oa.shape[1], bg.shape[1], oc.shape[1]
    n_i = m // tm
    tiles_per_mod = n_i // g1.shape[0]
    mod_rows = g1.shape[1]
    row = lambda i: (i, 0)
    fix = lambda i: (0, 0)
    mod_spec = pl.BlockSpec((None, mod_rows, d), lambda i: (i // tiles_per_mod, 0, 0))
    in_specs = [pl.BlockSpec((tm, d), row), pl.BlockSpec((tm, aw), row), pl.BlockSpec((tm, bw), row),
                pl.BlockSpec((tm, bw), row), pl.BlockSpec((tm, cw), row),
                _resident((aw + bw + cw, d), fix), pl.BlockSpec((CONV_W, bw), fix),
                mod_spec, pl.BlockSpec((1, d), fix), pl.BlockSpec((1, d), fix), mod_spec, mod_spec]
    args = [x, oa, bg, u, oc, w_out, conv_w, g1, ln_g.reshape(1, d), ln_b.reshape(1, d), sc2, sh2]
    scratch = []
    if prev is None:
        assert grp.t % tm == 0
        scratch = [pltpu.VMEM((SUBLANE, bw), F32)]
    else:
        assert tm % grp.t == 0
        in_specs += [pl.BlockSpec((tm, bw), row), pl.BlockSpec((tm, bw), row)]
        args += [prev[0], prev[1]]
    return pl.pallas_call(
        functools.partial(_outproj_kernel, alpha=alpha, seq_tiles=max(grp.t // tm, 1), seg=min(grp.t, tm),
                          carried=prev is None),
        grid=(n_i,),
        in_specs=in_specs,
        out_specs=[pl.BlockSpec((tm, d), row), pl.BlockSpec((tm, d), row)],
        out_shape=[jax.ShapeDtypeStruct((m, d), F32), jax.ShapeDtypeStruct((m, d), BF16)],
        scratch_shapes=scratch,
        compiler_params=_params(("arbitrary",)),
        name="out_projection",
    )(*args)


def _ffn_kernel(*refs, alpha, seq_tiles, seg, carried):
    if carried:
        (h_ref, x_ref, wg_ref, wu_ref, wd_ref, cw_ref, g2_ref, lng_ref, lnb_ref,
         o_ref, gt_ref, acc_scr, prev_scr) = refs
    else:
        (h_ref, x_ref, wg_ref, wu_ref, wd_ref, cw_ref, g2_ref, lng_ref, lnb_ref, e0_ref, e1_ref,
         o_ref, gt_ref, acc_scr) = refs
    f = pl.program_id(1)
    if carried:
        @pl.when(pl.program_id(0) % seq_tiles == 0)
        def _():
            prev_scr[f] = jnp.zeros(prev_scr.shape[1:], F32)

    @pl.when(f == 0)
    def _():
        acc_scr[...] = jnp.zeros_like(acc_scr)

    h = h_ref[...]
    tm, tf = h.shape[0], wg_ref.shape[1]
    n_split = 2 if tf % (2 * LANE) == 0 else 1
    halves = [slice(c * tf // n_split, (c + 1) * tf // n_split) for c in range(n_split)]
    gates = [jnp.dot(h, wg_ref[:, cols], preferred_element_type=F32) for cols in halves]
    ups = [jnp.dot(h, wu_ref[:, cols], preferred_element_type=F32) for cols in halves]
    for cols, gate, up in zip(halves, gates, ups):
        if carried:
            e0, e1 = prev_scr[f, SUBLANE - 2:SUBLANE - 1, cols], prev_scr[f, SUBLANE - 1:SUBLANE, cols]
        else:
            e0, e1 = e0_ref[:, cols], e1_ref[:, cols]
        conv = _causal_conv(gate, e0, e1, cw_ref[:, cols], seg)
        if carried:
            prev_scr[f, :, cols] = gate[tm - SUBLANE:, :]
        gt_ref[:, cols] = gate[tm - gt_ref.shape[0]:, :]
        act = (jax.nn.silu(conv) * up).astype(BF16)
        acc_scr[...] = jnp.dot(act, wd_ref[cols, :], preferred_element_type=F32) + acc_scr[...]

    @pl.when(f == pl.num_programs(1) - 1)
    def _():
        z = alpha * x_ref[...] + (1.0 + g2_ref[...]) * acc_scr[...]
        o_ref[...] = _layer_norm(z, lng_ref[...], lnb_ref[...])


def _channel_mixer(h2, x1, w_gu, w_down, conv_w, g2, ln_g, ln_b, prev, grp, alpha, tm, tf):
    m, d = x1.shape
    dff = w_down.shape[0]
    n_i, n_f = m // tm, dff // tf
    tiles_per_mod = n_i // g2.shape[0]
    mod_rows = g2.shape[1]
    fix = lambda i, f: (0, 0)
    in_specs = [pl.BlockSpec((tm, d), lambda i, f: (i, 0)), pl.BlockSpec((tm, d), lambda i, f: (i, 0)),
                pl.BlockSpec((d, tf), lambda i, f: (0, f)), pl.BlockSpec((d, tf), lambda i, f: (0, n_f + f)),
                pl.BlockSpec((tf, d), lambda i, f: (f, 0)), pl.BlockSpec((CONV_W, tf), lambda i, f: (0, f)),
                pl.BlockSpec((None, mod_rows, d), lambda i, f: (i // tiles_per_mod, 0, 0)),
                pl.BlockSpec((1, d), fix), pl.BlockSpec((1, d), fix)]
    args = [h2, x1, w_gu, w_gu, w_down, conv_w, g2, ln_g.reshape(1, d), ln_b.reshape(1, d)]
    scratch = [pltpu.VMEM((tm, d), F32)]
    if prev is None:
        assert grp.t % tm == 0
        scratch.append(pltpu.VMEM((n_f, SUBLANE, tf), F32))
        gt_spec = pl.BlockSpec((None, SUBLANE, tf), lambda i, f: (i, 0, f))
        gt_shape = jax.ShapeDtypeStruct((n_i, SUBLANE, dff), F32)
    else:
        assert tm % grp.t == 0
        in_specs += [pl.BlockSpec((tm, tf), lambda i, f: (i, f))] * 2
        args += [prev[0], prev[1]]
        gt_spec = pl.BlockSpec((tm, tf), lambda i, f: (i, f))
        gt_shape = jax.ShapeDtypeStruct((m, dff), F32)
    return pl.pallas_call(
        functools.partial(_ffn_kernel, alpha=alpha, seq_tiles=max(grp.t // tm, 1), seg=min(grp.t, tm),
                          carried=prev is None),
        grid=(n_i, n_f),
        in_specs=in_specs,
        out_specs=[pl.BlockSpec((tm, d), lambda i, f: (i, 0)), gt_spec],
        out_shape=[jax.ShapeDtypeStruct((m, d), F32), gt_shape],
        scratch_shapes=scratch,
        compiler_params=_params(("arbitrary", "arbitrary")),
        name="channel_mixer",
    )(*args)


def _rope_tables(pos):
    half = HEAD // 2
    inv = jnp.power(jnp.float32(ROPE_THETA), -jnp.arange(half, dtype=F32) / half)
    ang = pos.astype(F32)[:, None] * inv[None, :]
    cos, sin = jnp.cos(ang), jnp.sin(ang)
    zero = jnp.zeros_like(sin)
    reps = LANE // HEAD
    return (jnp.tile(jnp.concatenate([cos, cos], axis=1), (1, reps)),
            jnp.tile(jnp.concatenate([zero, sin], axis=1), (1, reps)),
            jnp.tile(jnp.concatenate([-sin, zero], axis=1), (1, reps)))


def _with_past(past, new, lp, twice=False):
    b, t, w = new.shape
    parts = [new]
    if past is not None:
        p = past.reshape(b, past.shape[1], -1).astype(BF16)
        parts = [jnp.concatenate([p, p], axis=-1) if twice else p, new]
    n = sum(a.shape[1] for a in parts)
    if lp > n:
        parts.append(jnp.zeros((b, lp - n, w), BF16))
    return parts[0] if len(parts) == 1 else jnp.concatenate(parts, axis=1)


def _layer(x, mod, tables, grp, past, wts, dm, alpha):
    (w_in_p, w_out, conv_b_w, kv_norm, w_uk, w_uv, ln1_g, ln1_b, ln2_g, ln2_b, w_gu, conv_f_w, w_down) = wts
    b, t = grp.b, grp.t
    m, d = x.shape
    carried = past is None
    tm = _row_tile(t, 256) if carried else m
    if carried:
        mods = [a.reshape(b, 1, d) for a in jnp.split(mod, N_MOD, axis=-1)]
    else:
        mods = [jnp.repeat(a, t, axis=0).reshape(1, m, d) for a in jnp.split(mod, N_MOD, axis=-1)]
    sh1, sc1, g1, sh2, sc2, g2 = mods

    (qa, ka, kab, qi, qcr, ki, kib, kr, krb, va, vab, bg, u, qcn, lat, latb, wi, vat) = _in_projection(
        x, sc1, sh1, w_in_p, tables, kv_norm, dm, tm)

    three = lambda a: a.reshape(b, t, a.shape[-1])
    if carried:
        tq = _row_tile(t, 256)
        kn, vct = _kv_up(latb, w_uk, w_uv, v_transposed=True)
        oa = _dsa_attention_t(three(qi), three(wi), three(kib), three(qa), three(kab), vat, grp, tq)
        oc = _mla_attention_t(three(qcn), three(qcr), three(kn), vct, three(krb), grp, tq)
        prev_b = prev_f = None
    else:
        l = grp.past + t
        lp, _ = _key_tile(l)
        p_ak, p_av, p_ik, p_lat, p_kr, prev_b, prev_f = past
        k_all = _with_past(p_ak, three(kab), lp)
        v_all = _with_past(p_av, three(vab), lp)
        ki_all = _with_past(p_ik, three(kib), lp, twice=True)
        kr_all = _with_past(p_kr, three(krb), lp, twice=True)
        lat_all = _with_past(p_lat, three(latb), lp)
        kn, vc = _kv_up(lat_all.reshape(b * lp, dm.r), w_uk, w_uv, v_transposed=False)
        oa = _dsa_attention(three(qi), three(wi), ki_all, three(qa), k_all, v_all, grp, l, t)
        oc = _mla_attention(three(qcn), three(qcr), kn.reshape(b, lp, -1), vc.reshape(b, lp, -1), kr_all,
                            grp, l, t)

    def expand(state):
        return jnp.repeat(state[:, 0], t, axis=0), jnp.repeat(state[:, 1], t, axis=0)

    x1, h2 = _out_projection(x, oa.reshape(m, -1), bg, u, oc.reshape(m, -1), w_out, conv_b_w, g1, ln1_g, ln1_b,
                             sc2, sh2, None if carried else expand(prev_b), grp, alpha, tm)
    tm_f = _row_tile(t, 512) if carried else m
    tf = _row_tile(dm.dff, 512)
    x2, gate_rows = _channel_mixer(h2, x1, w_gu, w_down, conv_f_w, g2, ln2_g, ln2_b,
                                   None if carried else expand(prev_f), grp, alpha, tm_f, tf)
    if carried:
        new_f = gate_rows.reshape(b, t // tm_f, SUBLANE, dm.dff)[:, -1, SUBLANE - (CONV_W - 1):, :]
    else:
        new_f = gate_rows.reshape(b, t, dm.dff)[:, t - (CONV_W - 1):, :]
    new_b = u.reshape(b, t, dm.bw)[:, t - (CONV_W - 1):, :]
    heads = dm.aw // HEAD
    rows = (ka.reshape(b, t, heads, HEAD), va.reshape(b, t, heads, HEAD), ki.reshape(b, t, HEAD),
            lat.reshape(b, t, dm.r), kr.reshape(b, t, HEAD), new_b, new_f)
    return x2, rows


def kernel(x_prompt, x_sample, c_prompt, c_sample, cache_a_k, cache_a_v, cache_idx_k, cache_mla_latent,
           cache_mla_krope, state_conv_b, state_conv_ffn, w_in, w_out, conv_b_w, mla_kv_norm, mla_w_uk,
           mla_w_uv, w_mod, b_mod, ln1_g, ln1_b, ln2_g, ln2_b, ffn_w_gu, ffn_conv_w, ffn_w_down):
    depth, d, _ = w_in.shape
    a_heads = cache_a_k.shape[3]
    dm = Dims(d=d, aw=a_heads * HEAD, bw=conv_b_w.shape[2], ch=mla_w_uk.shape[2] // C_NOPE,
              r=mla_w_uk.shape[1], dff=ffn_w_down.shape[1], depth=depth)
    alpha = (2 * depth) ** 0.25
    grp_p = Group(b=x_prompt.shape[0], t=x_prompt.shape[1], past=0)
    grp_s = Group(b=x_sample.shape[0], t=x_sample.shape[1], past=cache_a_k.shape[2])

    n_c = grp_p.b + grp_s.b
    c_all = jnp.concatenate([c_prompt, c_sample, jnp.zeros((-n_c % SUBLANE, d), F32)], axis=0)
    mod = _modulation(c_all, w_mod, b_mod)

    tab_p = _rope_tables(jnp.arange(grp_p.t, dtype=I32))
    tab_s = tuple(jnp.tile(a, (grp_s.b, 1)) for a in _rope_tables(grp_s.past + jnp.arange(grp_s.t, dtype=I32)))

    xp = x_prompt.reshape(grp_p.b * grp_p.t, d)
    xs = x_sample.reshape(grp_s.b * grp_s.t, d)
    rows_p, rows_s = [], []
    for l in range(depth):
        wts = (_pack_w_in(w_in[l], dm), w_out[l].astype(BF16), conv_b_w[l], mla_kv_norm[l],
               mla_w_uk[l].astype(BF16), mla_w_uv[l].astype(BF16), ln1_g[l], ln1_b[l], ln2_g[l], ln2_b[l],
               ffn_w_gu[l].astype(BF16), ffn_conv_w[l], ffn_w_down[l].astype(BF16))
        xp, rp = _layer(xp, mod[l, :grp_p.b], tab_p, grp_p, None, wts, dm, alpha)
        past_l = (cache_a_k[l], cache_a_v[l], cache_idx_k[l], cache_mla_latent[l], cache_mla_krope[l],
                  state_conv_b[l], state_conv_ffn[l])
        xs, rs = _layer(xs, mod[l, grp_p.b:n_c], tab_s, grp_s, past_l, wts, dm, alpha)
        rows_p.append(rp)
        rows_s.append(rs)
    outs_p = [jnp.stack(r) for r in zip(*rows_p)]
    outs_s = [jnp.stack(r) for r in zip(*rows_s)]
    return (xp.reshape(x_prompt.shape), xs.reshape(x_sample.shape), *outs_p, *outs_s)
```

```python
import functools
from typing import NamedTuple

import numpy as np
import jax
import jax.numpy as jnp
from jax import lax
from jax.experimental import pallas as pl
from jax.experimental.pallas import tpu as pltpu

F32, BF16, I32, I16 = jnp.float32, jnp.bfloat16, jnp.int32, jnp.int16

CHUNK = 64
CONV_W = 3
ROPE_THETA = 10000.0
HEAD = 64
IDX_HEADS = 16
TOPK_MAX = 256
C_NOPE = 128
C_V = 128
N_MOD = 6
LN_EPS = 1e-5
RMS_EPS = 1e-6

LANE = 128
SUBLANE = 8
VMEM_LIMIT = 50 * 1024 * 1024

MASKED = -1e30
LOG2E = 1.4426950408889634
INT_MIN = -2 ** 31
I16_MIN = -2 ** 15
KEY_NEG_INF = int(np.array(-np.inf, np.float32).view(np.int32)) ^ 0x7FFFFFFF


class Dims(NamedTuple):
    d: int
    aw: int
    bw: int
    ch: int
    r: int
    dff: int
    depth: int


class Group(NamedTuple):
    b: int
    t: int
    past: int


def _row_tile(m, pref):
    if m <= pref:
        return m
    t = pref - pref % SUBLANE
    while m % t:
        t -= SUBLANE
    return t


def _key_tile(l):
    lp = -(-l // LANE) * LANE
    for tk in (512, 384, 256, 128):
        if lp % tk == 0:
            return lp, tk
    raise AssertionError(lp)


def _params(sem):
    return pltpu.CompilerParams(dimension_semantics=sem, vmem_limit_bytes=VMEM_LIMIT)


def _resident(shape, index_map):
    return pl.BlockSpec(shape, index_map, pipeline_mode=pl.Buffered(1))


def _mod_kernel(c_ref, w_ref, b_ref, o_ref):
    a = jax.nn.silu(c_ref[...]).astype(BF16)
    o_ref[...] = jnp.dot(a, w_ref[...].astype(BF16), preferred_element_type=F32) + b_ref[...]


def _modulation(c, w_mod, b_mod):
    depth, d, n = w_mod.shape
    rows = c.shape[0]
    tn = _row_tile(n, 1024)
    return pl.pallas_call(
        _mod_kernel,
        grid=(depth, n // tn),
        in_specs=[pl.BlockSpec((rows, d), lambda l, j: (0, 0)),
                  pl.BlockSpec((None, d, tn), lambda l, j: (l, 0, j)),
                  pl.BlockSpec((None, 1, tn), lambda l, j: (l, 0, j))],
        out_specs=pl.BlockSpec((None, rows, tn), lambda l, j: (l, 0, j)),
        out_shape=jax.ShapeDtypeStruct((depth, rows, n), F32),
        compiler_params=_params(("arbitrary", "arbitrary")),
        name="modulation",
    )(c, w_mod, b_mod.reshape(depth, 1, n))


def _pack_w_in(w, dm):
    d, aw, bw, ch, r = dm.d, dm.aw, dm.bw, dm.ch, dm.r
    lead = w.shape[:-1]
    o = np.cumsum([0, aw, aw, aw, IDX_HEADS * HEAD, HEAD, IDX_HEADS, bw, bw, bw, ch * (C_NOPE + HEAD), r, HEAD])
    qa, ka, va, qi, ki, wi, bg, cg, xb, qc, lat, kr = [w[..., o[i]:o[i + 1]] for i in range(12)]
    qc = qc.reshape(lead + (ch, C_NOPE + HEAD))
    qcn = qc[..., :C_NOPE].reshape(lead + (ch * C_NOPE,))
    qcr = qc[..., C_NOPE:].reshape(lead + (ch * HEAD,))
    pad = jnp.zeros(lead + (LANE - IDX_HEADS,), w.dtype)
    return jnp.concatenate([qa, ka, qi, qcr, ki, ki, kr, kr, va, bg, cg, xb, qcn, lat, wi, pad],
                           axis=-1).astype(BF16)


def _rope(acc, cos, s1, s2):
    outs = []
    for s in range(acc.shape[1] // LANE):
        xs = acc[:, s * LANE:(s + 1) * LANE]
        outs.append(xs * cos + pltpu.roll(xs, HEAD // 2, 1) * s1 + pltpu.roll(xs, LANE - HEAD // 2, 1) * s2)
    return outs[0] if len(outs) == 1 else jnp.concatenate(outs, axis=1)


def _inproj_kernel(x_ref, sc_ref, sh_ref, w_ref, cos_ref, s1_ref, s2_ref, nrm_ref,
                   qa_ref, ka_ref, kab_ref, qi_ref, qcr_ref, ki_ref, kib_ref, kr_ref, krb_ref,
                   va_ref, vab_ref, bg_ref, u_ref, qcn_ref, lat_ref, latb_ref, wi_ref, vat_ref, *, dm):
    aw, bw, ch, r = dm.aw, dm.bw, dm.ch, dm.r
    h = (x_ref[...] * (1.0 + sc_ref[...]) + sh_ref[...]).astype(BF16)
    cos, s1, s2 = cos_ref[...], s1_ref[...], s2_ref[...]
    col = [0]

    def proj(width):
        c0 = col[0]
        col[0] = c0 + width
        return jnp.dot(h, w_ref[:, c0:c0 + width], preferred_element_type=F32)

    def pieces(width, step=512):
        return [(o, min(step, width - o)) for o in range(0, width, step)]

    for o, wd in pieces(aw):
        qa_ref[:, o:o + wd] = _rope(proj(wd), cos, s1, s2).astype(BF16)
    for o, wd in pieces(aw):
        y = _rope(proj(wd), cos, s1, s2)
        ka_ref[:, o:o + wd] = y
        kab_ref[:, o:o + wd] = y.astype(BF16)
    for o, wd in pieces(IDX_HEADS * HEAD):
        qi_ref[:, o:o + wd] = _rope(proj(wd), cos, s1, s2).astype(BF16)
    for o, wd in pieces(ch * HEAD):
        qcr_ref[:, o:o + wd] = _rope(proj(wd), cos, s1, s2).astype(BF16)
    for f32_ref, b16_ref in ((ki_ref, kib_ref), (kr_ref, krb_ref)):
        y = _rope(proj(LANE), cos, s1, s2)
        f32_ref[...] = y[:, :HEAD]
        b16_ref[...] = y.astype(BF16)
    for o, wd in pieces(aw):
        y = proj(wd)
        va_ref[:, o:o + wd] = y
        vab_ref[:, o:o + wd] = y.astype(BF16)
        vat_ref[o:o + wd, :] = y.T.astype(BF16)
    for o, wd in pieces(bw):
        bg_ref[:, o:o + wd] = proj(wd)
    c_cg = col[0]
    for o, wd in pieces(bw):
        cg = jnp.dot(h, w_ref[:, c_cg + o:c_cg + o + wd], preferred_element_type=F32)
        xb = jnp.dot(h, w_ref[:, c_cg + bw + o:c_cg + bw + o + wd], preferred_element_type=F32)
        u_ref[:, o:o + wd] = cg * xb
    col[0] = c_cg + 2 * bw
    for o, wd in pieces(ch * C_NOPE):
        qcn_ref[:, o:o + wd] = proj(wd).astype(BF16)
    lat = proj(r)
    lat = lat * lax.rsqrt(jnp.mean(lat * lat, axis=-1, keepdims=True) + RMS_EPS) * nrm_ref[...]
    lat_ref[...] = lat
    latb_ref[...] = lat.astype(BF16)
    wi_ref[...] = proj(LANE) * (IDX_HEADS ** -0.5)


def _in_projection(x, sc, sh, w_packed, layer, tables, kv_norm, dm, tm):
    m, d = x.shape
    aw, bw, ch, r = dm.aw, dm.bw, dm.ch, dm.r
    npk = w_packed.shape[2]
    n_i = m // tm
    tiles_per_mod = n_i // sc.shape[0]
    mod_rows = sc.shape[1]
    tab_tiles = tables[0].shape[0] // tm
    widths = [(aw, BF16), (aw, F32), (aw, BF16), (IDX_HEADS * HEAD, BF16), (ch * HEAD, BF16),
              (HEAD, F32), (LANE, BF16), (HEAD, F32), (LANE, BF16),
              (aw, F32), (aw, BF16), (bw, F32), (bw, F32), (ch * C_NOPE, BF16), (r, F32), (r, BF16),
              (LANE, F32)]
    row = lambda i: (i, 0)
    mod_spec = pl.BlockSpec((None, mod_rows, d), lambda i: (i // tiles_per_mod, 0, 0))
    tab_spec = pl.BlockSpec((tm, LANE), lambda i: (i % tab_tiles, 0))
    return pl.pallas_call(
        functools.partial(_inproj_kernel, dm=dm),
        grid=(n_i,),
        in_specs=[pl.BlockSpec((tm, d), row), mod_spec, mod_spec,
                  _resident((None, d, npk), lambda i: (layer, 0, 0)),
                  tab_spec, tab_spec, tab_spec,
                  pl.BlockSpec((1, r), lambda i: (0, 0))],
        out_specs=[pl.BlockSpec((tm, w), row) for w, _ in widths] + [pl.BlockSpec((aw, tm), lambda i: (0, i))],
        out_shape=[jax.ShapeDtypeStruct((m, w), dt) for w, dt in widths] + [jax.ShapeDtypeStruct((aw, m), BF16)],
        compiler_params=_params(("arbitrary",)),
        name="in_projection",
    )(x, sc, sh, w_packed, *tables, kv_norm.reshape(1, r))


def _kvup_kernel(l_ref, wk_ref, wv_ref, k_ref, v_ref, *, v_transposed):
    lat = l_ref[...]
    k_ref[...] = jnp.dot(lat, wk_ref[...], preferred_element_type=F32).astype(BF16)
    if v_transposed:
        v_ref[...] = lax.dot_general(wv_ref[...], lat, _NT, preferred_element_type=F32).astype(BF16)
    else:
        v_ref[...] = jnp.dot(lat, wv_ref[...], preferred_element_type=F32).astype(BF16)


def _kv_up(lat, w_uk, w_uv, layer, v_transposed):
    m, r = lat.shape
    n = w_uk.shape[2]
    tm = _row_tile(m, 512)
    if v_transposed:
        v_spec, v_shape = pl.BlockSpec((n, tm), lambda i: (0, i)), (n, m)
    else:
        v_spec, v_shape = pl.BlockSpec((tm, n), lambda i: (i, 0)), (m, n)
    return pl.pallas_call(
        functools.partial(_kvup_kernel, v_transposed=v_transposed),
        grid=(m // tm,),
        in_specs=[pl.BlockSpec((tm, r), lambda i: (i, 0)),
                  pl.BlockSpec((None, r, n), lambda i: (layer, 0, 0)),
                  pl.BlockSpec((None,) + w_uv.shape[1:], lambda i: (layer, 0, 0))],
        out_specs=[pl.BlockSpec((tm, n), lambda i: (i, 0)), v_spec],
        out_shape=[jax.ShapeDtypeStruct((m, n), BF16), jax.ShapeDtypeStruct(v_shape, BF16)],
        compiler_params=_params(("arbitrary",)),
        name="latent_up_projection",
    )(lat, w_uk, w_uv)


_NT = (((1,), (1,)), ((), ()))


def _visible_tiles(qpos0, tq, l, tk):
    last_visible = ((qpos0 + tq - 1) // CHUNK + 1) * CHUNK
    nvis = min(l, last_visible) if isinstance(qpos0, int) else jnp.minimum(l, last_visible)
    return (nvis + tk - 1) // tk


def _for(n, body, init):
    if isinstance(n, int):
        for i in range(n):
            init = body(i, init)
        return init
    return lax.fori_loop(0, n, body, init)


def _tile_start(kt, tk):
    return kt * tk if isinstance(kt, int) else pl.multiple_of(kt * tk, tk)


def _flash_step(carry, s, v):
    m, l, acc = carry
    m_new = jnp.maximum(m, jnp.max(s, axis=1, keepdims=True))
    alpha = jnp.exp(m - m_new)
    p = jnp.exp(s - m_new)
    l = alpha * l + jnp.sum(p, axis=1, keepdims=True)
    acc = alpha * acc + jnp.dot(p.astype(BF16), v, preferred_element_type=F32)
    return m_new, l, acc


def _flash_init(rows, width):
    return (jnp.full((rows, 1), MASKED, F32), jnp.zeros((rows, 1), F32), jnp.zeros((rows, width), F32))


def _ordered_key(x):
    b = pltpu.bitcast(x, I32)
    return jnp.where(b < 0, b ^ 0x7FFFFFFF, b)


def _dsa_kernel(qi_ref, wi_ref, ki_ref, qa_ref, k_ref, v_ref, o_ref, key_scr, bias_scr, thr2_scr,
                *, tq, tk, l, lp, past, topk, aw, one_block):
    qpos0 = past if one_block else past + pl.program_id(1) * tq
    ntiles = _visible_tiles(qpos0, tq, l, tk)
    qchunk = (qpos0 + lax.broadcasted_iota(I32, (tq, 1), 0)) // CHUNK
    lane = lax.broadcasted_iota(I32, (1, LANE), 1)
    lo_half = lane < HEAD
    kf = jnp.float32(topk)

    def split_heads(qs):
        zero = jnp.zeros_like(qs)
        return jnp.concatenate([jnp.where(lo_half, qs, zero), jnp.where(lo_half, zero, qs)], axis=0)

    wi = wi_ref[...] * (HEAD ** -0.5)
    q_idx = [split_heads(qi_ref[:, s * LANE:(s + 1) * LANE]) for s in range(IDX_HEADS // 2)]
    w_idx = [wi[:, h:h + 1] for h in range(IDX_HEADS)]

    def score_body(kt, _):
        k0 = _tile_start(kt, tk)
        kt_tile = ki_ref[pl.ds(k0, tk), :]
        score = jnp.zeros((tq, tk), F32)
        for s in range(IDX_HEADS // 2):
            s2 = lax.dot_general(q_idx[s], kt_tile, _NT, preferred_element_type=F32)
            score = score + jnp.maximum(s2[:tq], 0.0) * w_idx[2 * s]
            score = score + jnp.maximum(s2[tq:], 0.0) * w_idx[2 * s + 1]
        kpos = k0 + lax.broadcasted_iota(I32, (1, tk), 1)
        vis = (kpos // CHUNK <= qchunk) & (kpos < l)
        key_scr[kt] = _ordered_key(jnp.where(vis, score, -jnp.inf))
        return 0

    _for(ntiles, score_body, 0)

    def count(pred):
        def body(kt, acc):
            for j in range(tk // LANE):
                ks = key_scr[kt, :, j * LANE:(j + 1) * LANE]
                acc = acc + jnp.where(pred(ks, kt * tk + j * LANE), 1.0, 0.0)
            return acc
        acc = _for(ntiles, body, jnp.zeros((tq, LANE), F32))
        return jnp.sum(acc, axis=1, keepdims=True)

    def wide(col):
        return jnp.broadcast_to(col, (tq, LANE))

    def bit_body(b, lo):
        cand = lo + jnp.left_shift(jnp.int32(1), 31 - b)
        cand_w = wide(cand)
        c = count(lambda ks, _: ks >= cand_w)
        return jnp.where(c >= kf, cand, lo)

    thr = lax.fori_loop(0, 32, bit_body, jnp.full((tq, 1), INT_MIN, I32))
    thr_w = wide(thr)

    n_ge = count(lambda ks, _: ks >= thr_w)
    n_gt = count(lambda ks, _: ks > thr_w)
    excess = (n_ge > kf) & (thr > KEY_NEG_INF)
    thr2_scr[...] = jnp.zeros((tq, 1), I32)

    @pl.when(jnp.max(jnp.where(excess, 1.0, 0.0)) > 0.0)
    def _():
        need = kf - n_gt
        nbits = lp.bit_length()

        def bit2_body(b, lo):
            cand = lo + jnp.left_shift(jnp.int32(1), nbits - 1 - b)
            cand_w = wide(cand)
            c = count(lambda ks, base: jnp.where(ks == thr_w, lp - (base + lane), 0) >= cand_w)
            return jnp.where(c >= need, cand, lo)

        thr2_scr[...] = lax.fori_loop(0, nbits, bit2_body, jnp.zeros((tq, 1), I32))

    thr2_w = wide(thr2_scr[...])

    def bias_body(kt, _):
        for j in range(tk // LANE):
            ks = key_scr[kt, :, j * LANE:(j + 1) * LANE]
            tie = jnp.where(lp - (kt * tk + j * LANE + lane) >= thr2_w, 0.0, MASKED)
            bias = jnp.where(ks > thr_w, 0.0, jnp.where(ks == thr_w, tie, MASKED))
            bias_scr[kt, :, j * LANE:(j + 1) * LANE] = jnp.where(ks > KEY_NEG_INF, bias, MASKED)
        return 0

    _for(ntiles, bias_body, 0)

    for pr in range(aw // LANE):
        cols = slice(pr * LANE, (pr + 1) * LANE)
        q2 = split_heads(qa_ref[:, cols])

        def att_body(kt, carry, cols=cols, q2=q2):
            k0 = _tile_start(kt, tk)
            s = lax.dot_general(q2, k_ref[pl.ds(k0, tk), cols], _NT, preferred_element_type=F32)
            bias = bias_scr[kt]
            s = s * (HEAD ** -0.5) + jnp.concatenate([bias, bias], axis=0)
            return _flash_step(carry, s, v_ref[pl.ds(k0, tk), cols])

        _, den, acc = _for(ntiles, att_body, _flash_init(2 * tq, LANE))
        o2 = acc / den
        o_ref[:, cols] = jnp.where(lo_half, o2[:tq], o2[tq:]).astype(BF16)


def _dsa_attention(qi, wi, ki, qa, k, v, grp, l, tq):
    b, t, aw = qa.shape
    lp = k.shape[1]
    _, tk = _key_tile(l)
    topk = min(TOPK_MAX, l // 4)
    qblk = lambda w: pl.BlockSpec((None, tq, w), lambda i, j: (i, j, 0))
    kblk = lambda w: pl.BlockSpec((None, lp, w), lambda i, j: (i, 0, 0))
    return pl.pallas_call(
        functools.partial(_dsa_kernel, tq=tq, tk=tk, l=l, lp=lp, past=grp.past, topk=topk, aw=aw,
                          one_block=t == tq),
        grid=(b, t // tq),
        in_specs=[qblk(IDX_HEADS * HEAD), qblk(LANE), kblk(LANE), qblk(aw), kblk(aw), kblk(aw)],
        out_specs=qblk(aw),
        out_shape=jax.ShapeDtypeStruct((b, t, aw), BF16),
        scratch_shapes=[pltpu.VMEM((lp // tk, tq, tk), I32),
                        pltpu.VMEM((lp // tk, tq, tk), F32),
                        pltpu.VMEM((tq, 1), I32)],
        compiler_params=_params(("arbitrary", "arbitrary")),
        name="dsa_attention",
    )(qi, wi, ki, qa, k, v)


def _mla_kernel(qn_ref, qr_ref, kn_ref, v_ref, kr_ref, o_ref, *, tq, tk, l, past, ch, one_block):
    qpos0 = past if one_block else past + pl.program_id(1) * tq
    ntiles = _visible_tiles(qpos0, tq, l, tk)
    qchunk = (qpos0 + lax.broadcasted_iota(I32, (tq, 1), 0)) // CHUNK
    lo_half = lax.broadcasted_iota(I32, (1, LANE), 1) < HEAD
    scale = (C_NOPE + HEAD) ** -0.5
    for h in range(ch):
        cols = slice(h * LANE, (h + 1) * LANE)
        qs = qr_ref[:, (h // 2) * LANE:(h // 2 + 1) * LANE]
        zero = jnp.zeros_like(qs)
        qr = jnp.where(lo_half, qs, zero) if h % 2 == 0 else jnp.where(lo_half, zero, qs)
        qcat = jnp.concatenate([qn_ref[:, cols], qr], axis=1)

        def body(kt, carry, cols=cols, qcat=qcat):
            k0 = _tile_start(kt, tk)
            kcat = jnp.concatenate([kn_ref[pl.ds(k0, tk), cols], kr_ref[pl.ds(k0, tk), :]], axis=1)
            s = lax.dot_general(qcat, kcat, _NT, preferred_element_type=F32) * scale
            kpos = k0 + lax.broadcasted_iota(I32, (1, tk), 1)
            vis = (kpos // CHUNK <= qchunk) & (kpos < l)
            return _flash_step(carry, jnp.where(vis, s, MASKED), v_ref[pl.ds(k0, tk), cols])

        _, den, acc = _for(ntiles, body, _flash_init(tq, LANE))
        o_ref[:, cols] = (acc / den).astype(BF16)


def _mla_attention(qn, qr, kn, v, kr, grp, l, tq):
    b, t, wn = qn.shape
    ch = wn // C_NOPE
    lp = kn.shape[1]
    _, tk = _key_tile(l)
    qblk = lambda w: pl.BlockSpec((None, tq, w), lambda i, j: (i, j, 0))
    kblk = lambda w: _resident((None, lp, w), lambda i, j: (i, 0, 0))
    return pl.pallas_call(
        functools.partial(_mla_kernel, tq=tq, tk=tk, l=l, past=grp.past, ch=ch, one_block=t == tq),
        grid=(b, t // tq),
        in_specs=[qblk(wn), qblk(ch * HEAD), kblk(wn), kblk(wn), kblk(LANE)],
        out_specs=qblk(wn),
        out_shape=jax.ShapeDtypeStruct((b, t, wn), BF16),
        compiler_params=_params(("arbitrary", "arbitrary")),
        name="mla_attention",
    )(qn, qr, kn, v, kr)


ONES_ROWS = 16


def _flash_t_init(m_scr, acc_scr):
    m_scr[...] = jnp.full(m_scr.shape, MASKED, F32)
    acc_scr[...] = jnp.zeros(acc_scr.shape, F32)


def _flash_t_stage(slot, h, s, s_scr):
    s_scr[slot, h] = s
    return jnp.max(s, axis=0, keepdims=True)


def _flash_t_step(slot, h, s_max, vt, c, s_scr, m_scr, acc_scr):
    m_old = m_scr[h]
    m_new = jnp.maximum(m_old, s_max)
    alpha = jnp.exp2((m_old - m_new) * c)
    p = jnp.exp2((s_scr[slot, h] - m_new) * c).astype(BF16)
    vt_ones = jnp.concatenate([vt, jnp.ones((ONES_ROWS, vt.shape[1]), BF16)], axis=0)
    acc_scr[h] = alpha * acc_scr[h] + jnp.dot(vt_ones, p, preferred_element_type=F32)
    m_scr[h] = m_new


def _flash_t_out(o_ref, acc_scr):
    heads, width = acc_scr.shape[0], acc_scr.shape[1] - ONES_ROWS
    ot = jnp.concatenate([acc_scr[h, :width, :] / acc_scr[h, width:width + 1, :] for h in range(heads)], axis=0)
    o_ref[...] = ot.T.astype(BF16)


def _split_heads_t(slab_t):
    row_lo = lax.broadcasted_iota(I32, (LANE, 1), 0) < HEAD
    zero = jnp.zeros_like(slab_t)
    return jnp.where(row_lo, slab_t, zero), jnp.where(row_lo, zero, slab_t)


def _dsa_t_kernel(qi_ref, wi_ref, ki_ref, qa_ref, k_ref, vt_ref, o_ref,
                  key_scr, hi_scr, lo_scr, bias_scr, thr2_scr, qit_scr, qat_scr, s_scr, m_scr, acc_scr,
                  *, tq, tk, l, past, topk):
    heads = acc_scr.shape[0]
    qpos0 = past + pl.program_id(1) * tq
    ntiles = _visible_tiles(qpos0, tq, l, tk)
    qchunk = (qpos0 + lax.broadcasted_iota(I32, (1, tq), 1)) // CHUNK
    krow = lax.broadcasted_iota(I32, (tk, 1), 0)
    kf = jnp.float32(topk)

    for s in range(IDX_HEADS // 2):
        qit_scr[2 * s], qit_scr[2 * s + 1] = _split_heads_t(qi_ref[:, s * LANE:(s + 1) * LANE].T)
    for s in range(heads // 2):
        slab_t = (qa_ref[:, s * LANE:(s + 1) * LANE].astype(F32) * (HEAD ** -0.5)).astype(BF16).T
        qat_scr[2 * s], qat_scr[2 * s + 1] = _split_heads_t(slab_t)
    w_t = wi_ref[...].T * (HEAD ** -0.5)
    w_rows = [w_t[h:h + 1, :] for h in range(IDX_HEADS)]

    def score_body(kt, _):
        k0 = pl.multiple_of(kt * tk, tk)
        ki_tile = ki_ref[pl.ds(k0, tk), :]
        score = jnp.zeros((tk, tq), F32)
        for h in range(IDX_HEADS):
            s = jnp.dot(ki_tile, qit_scr[h], preferred_element_type=F32)
            score = score + jnp.maximum(s, 0.0) * w_rows[h]
        kpos = k0 + krow
        vis = (kpos // CHUNK <= qchunk) & (kpos < l)
        key = _ordered_key(jnp.where(vis, score, -jnp.inf))
        key_scr[kt] = key
        hi_scr[kt] = (key >> 16).astype(I16)
        lo_scr[kt] = ((key & 0xFFFF) + I16_MIN).astype(I16)
        return 0

    lax.fori_loop(0, ntiles, score_body, 0)

    def count(pred):
        def body(kt, acc):
            hit = jnp.where(pred(key_scr[kt], kt * tk), 1.0, 0.0)
            return acc + hit.reshape(tk // SUBLANE, SUBLANE, tq).sum(axis=0)
        acc = lax.fori_loop(0, ntiles, body, jnp.zeros((SUBLANE, tq), F32))
        return jnp.sum(acc, axis=0, keepdims=True)

    def count16(half_scr, cand, strict):
        rows = 2 * SUBLANE
        cand16 = cand.astype(I16)

        def body(kt, acc):
            half = half_scr[kt]
            hit = jnp.where(half > cand16 if strict else half >= cand16, jnp.int16(1), jnp.int16(0))
            parts = hit.reshape(tk // (4 * rows), 4, rows, tq)
            for g in range(parts.shape[0]):
                acc = acc + parts[g]
            return acc

        acc = lax.fori_loop(0, ntiles, body, jnp.zeros((4, rows, tq), I16))
        return acc.astype(I32).sum(axis=0).sum(axis=0, keepdims=True)

    def kth_largest16(half_scr, k_need):
        def bit_body(b, lo):
            cand = lo + jnp.left_shift(jnp.int32(1), 15 - b)
            return jnp.where(count16(half_scr, cand, False) >= k_need, cand, lo)
        return lax.fori_loop(0, 16, bit_body, jnp.full((1, tq), I16_MIN, I32))

    thr_hi = kth_largest16(hi_scr, jnp.full((1, tq), topk, I32))
    need_lo = topk - count16(hi_scr, thr_hi, True)
    thr_hi16 = thr_hi.astype(I16)

    def mark_body(kt, _):
        lo_scr[kt] = jnp.where(hi_scr[kt] == thr_hi16, lo_scr[kt], jnp.int16(I16_MIN))
        return 0

    lax.fori_loop(0, ntiles, mark_body, 0)
    thr = thr_hi * 65536 + (kth_largest16(lo_scr, need_lo) - I16_MIN)

    lp = key_scr.shape[0] * tk
    n_ge = count(lambda ks, _: ks >= thr)
    n_gt = count(lambda ks, _: ks > thr)
    excess = (n_ge > kf) & (thr > KEY_NEG_INF)
    thr2_scr[...] = jnp.zeros((1, tq), I32)

    @pl.when(jnp.max(jnp.where(excess, 1.0, 0.0)) > 0.0)
    def _():
        need = kf - n_gt
        nbits = lp.bit_length()

        def bit2_body(b, lo):
            cand = lo + jnp.left_shift(jnp.int32(1), nbits - 1 - b)
            c = count(lambda ks, base: jnp.where(ks == thr, lp - (base + krow), 0) >= cand)
            return jnp.where(c >= need, cand, lo)

        thr2_scr[...] = lax.fori_loop(0, nbits, bit2_body, jnp.zeros((1, tq), I32))

    thr2 = thr2_scr[...]

    def bias_body(kt, _):
        ks = key_scr[kt]
        tie = jnp.where(lp - (kt * tk + krow) >= thr2, 0.0, MASKED)
        bias = jnp.where(ks > thr, 0.0, jnp.where(ks == thr, tie, MASKED))
        bias_scr[kt] = jnp.where(ks > KEY_NEG_INF, bias, MASKED)
        return 0

    lax.fori_loop(0, ntiles, bias_body, 0)

    _flash_t_init(m_scr, acc_scr)

    def att_body(kt, _):
        k0 = pl.multiple_of(kt * tk, tk)
        slot = kt % 2
        s_max = []
        for h in range(heads):
            cols = slice((h // 2) * LANE, (h // 2 + 1) * LANE)
            s = jnp.dot(k_ref[pl.ds(k0, tk), cols], qat_scr[h], preferred_element_type=F32) + bias_scr[kt]
            s_max.append(_flash_t_stage(slot, h, s, s_scr))
        for h in range(heads):
            _flash_t_step(slot, h, s_max[h], vt_ref[h * HEAD:(h + 1) * HEAD, pl.ds(k0, tk)], LOG2E,
                          s_scr, m_scr, acc_scr)
        return 0

    lax.fori_loop(0, ntiles, att_body, 0)
    _flash_t_out(o_ref, acc_scr)


def _dsa_attention_t(qi, wi, ki, qa, k, vt, grp, tq):
    b, t, aw = qa.shape
    lp, tk = _key_tile(t)
    assert lp == t
    heads = aw // HEAD
    topk = min(TOPK_MAX, t // 4)
    qblk = lambda w: pl.BlockSpec((None, tq, w), lambda i, j: (i, j, 0))
    kblk = lambda w: pl.BlockSpec((None, t, w), lambda i, j: (i, 0, 0))
    return pl.pallas_call(
        functools.partial(_dsa_t_kernel, tq=tq, tk=tk, l=t, past=grp.past, topk=topk),
        grid=(b, t // tq),
        in_specs=[qblk(IDX_HEADS * HEAD), qblk(LANE), kblk(LANE), qblk(aw), kblk(aw),
                  pl.BlockSpec((aw, t), lambda i, j: (0, i))],
        out_specs=qblk(aw),
        out_shape=jax.ShapeDtypeStruct((b, t, aw), BF16),
        scratch_shapes=[pltpu.VMEM((t // tk, tk, tq), I32),
                        pltpu.VMEM((t // tk, tk, tq), I16),
                        pltpu.VMEM((t // tk, tk, tq), I16),
                        pltpu.VMEM((t // tk, tk, tq), F32),
                        pltpu.VMEM((1, tq), I32),
                        pltpu.VMEM((IDX_HEADS, LANE, tq), BF16),
                        pltpu.VMEM((heads, LANE, tq), BF16),
                        pltpu.VMEM((2, heads, tk, tq), F32),
                        pltpu.VMEM((heads, 1, tq), F32),
                        pltpu.VMEM((heads, HEAD + ONES_ROWS, tq), F32)],
        compiler_params=_params(("arbitrary", "arbitrary")),
        name="dsa_attention_t",
    )(qi, wi, ki, qa, k, vt)


def _mla_t_kernel(qn_ref, qr_ref, kn_ref, vt_ref, kr_ref, o_ref, qt_scr, s_scr, m_scr, acc_scr,
                  *, tq, tk, l, past):
    ch = acc_scr.shape[0]
    qpos0 = past + pl.program_id(1) * tq
    ntiles = _visible_tiles(qpos0, tq, l, tk)
    nfull = jnp.minimum(l, (qpos0 // CHUNK + 1) * CHUNK) // tk
    qchunk = (qpos0 + lax.broadcasted_iota(I32, (1, tq), 1)) // CHUNK
    krow = lax.broadcasted_iota(I32, (tk, 1), 0)
    c = (C_NOPE + HEAD) ** -0.5 * LOG2E
    for s in range(ch // 2):
        pair = _split_heads_t(qr_ref[:, s * LANE:(s + 1) * LANE].T)
        for half in range(2):
            h = 2 * s + half
            qt_scr[h] = jnp.concatenate([qn_ref[:, h * LANE:(h + 1) * LANE].T, pair[half]], axis=0)
    _flash_t_init(m_scr, acc_scr)

    def tile(kt, masked):
        k0 = pl.multiple_of(kt * tk, tk)
        k_rope = kr_ref[pl.ds(k0, tk), :]
        slot = kt % 2
        if masked:
            kpos = k0 + krow
            bias = jnp.where((kpos // CHUNK <= qchunk) & (kpos < l), 0.0, MASKED)
        s_max = []
        for h in range(ch):
            kcat = jnp.concatenate([kn_ref[pl.ds(k0, tk), h * LANE:(h + 1) * LANE], k_rope], axis=1)
            s = jnp.dot(kcat, qt_scr[h], preferred_element_type=F32)
            s_max.append(_flash_t_stage(slot, h, s + bias if masked else s, s_scr))
        for h in range(ch):
            _flash_t_step(slot, h, s_max[h], vt_ref[h * C_V:(h + 1) * C_V, pl.ds(k0, tk)], c,
                          s_scr, m_scr, acc_scr)
        return 0

    lax.fori_loop(0, nfull, lambda kt, _: tile(kt, False), 0)
    lax.fori_loop(nfull, ntiles, lambda kt, _: tile(kt, True), 0)
    _flash_t_out(o_ref, acc_scr)


def _mla_attention_t(qn, qr, kn, vt, kr, grp, tq):
    b, t, wn = qn.shape
    ch = wn // C_NOPE
    lp, tk = _key_tile(t)
    assert lp == t and ch % 2 == 0
    qblk = lambda w: pl.BlockSpec((None, tq, w), lambda i, j: (i, j, 0))
    kblk = lambda w: _resident((None, t, w), lambda i, j: (i, 0, 0))
    return pl.pallas_call(
        functools.partial(_mla_t_kernel, tq=tq, tk=tk, l=t, past=grp.past),
        grid=(b, t // tq),
        in_specs=[qblk(wn), qblk(ch * HEAD), kblk(wn), _resident((wn, t), lambda i, j: (0, i)), kblk(LANE)],
        out_specs=qblk(wn),
        out_shape=jax.ShapeDtypeStruct((b, t, wn), BF16),
        scratch_shapes=[pltpu.VMEM((ch, 2 * LANE, tq), BF16),
                        pltpu.VMEM((2, ch, tk, tq), F32),
                        pltpu.VMEM((ch, 1, tq), F32),
                        pltpu.VMEM((ch, C_V + ONES_ROWS, tq), F32)],
        compiler_params=_params(("arbitrary", "arbitrary")),
        name="mla_attention_t",
    )(qn, qr, kn, vt, kr)


def _causal_conv(u, e0, e1, w, seg):
    rmod = lax.broadcasted_iota(I32, (u.shape[0], 1), 0) % seg
    u1 = jnp.where(rmod == 0, e1, pltpu.roll(u, 1, 0))
    u2 = jnp.where(rmod == 0, e0, jnp.where(rmod == 1, e1, pltpu.roll(u, 2, 0)))
    return u2 * w[0:1] + u1 * w[1:2] + u * w[2:3]


def _layer_norm(z, g, b):
    mu = jnp.mean(z, axis=-1, keepdims=True)
    zc = z - mu
    var = jnp.mean(zc * zc, axis=-1, keepdims=True)
    return zc * lax.rsqrt(var + LN_EPS) * g + b


def _outproj_kernel(*refs, alpha, seq_tiles, seg, carried):
    if carried:
        (x_ref, oa_ref, bg_ref, u_ref, oc_ref, w_ref, cw_ref, g1_ref, lng_ref, lnb_ref, sc2_ref, sh2_ref,
         x1_ref, h2_ref, prev_scr) = refs

        @pl.when(pl.program_id(0) % seq_tiles == 0)
        def _():
            prev_scr[...] = jnp.zeros_like(prev_scr)

        e0, e1 = prev_scr[SUBLANE - 2:SUBLANE - 1, :], prev_scr[SUBLANE - 1:SUBLANE, :]
    else:
        (x_ref, oa_ref, bg_ref, u_ref, oc_ref, w_ref, cw_ref, g1_ref, lng_ref, lnb_ref, sc2_ref, sh2_ref,
         e0_ref, e1_ref, x1_ref, h2_ref) = refs
        e0, e1 = e0_ref[...], e1_ref[...]
    u = u_ref[...]
    yb = bg_ref[...] * _causal_conv(u, e0, e1, cw_ref[...], seg)
    if carried:
        prev_scr[...] = u[u.shape[0] - SUBLANE:, :]
    mixed = jnp.concatenate([oa_ref[...], yb.astype(BF16), oc_ref[...]], axis=1)
    mix = jnp.dot(mixed, w_ref[...], preferred_element_type=F32)
    x1 = _layer_norm(alpha * x_ref[...] + (1.0 + g1_ref[...]) * mix, lng_ref[...], lnb_ref[...])
    x1_ref[...] = x1
    h2_ref[...] = (x1 * (1.0 + sc2_ref[...]) + sh2_ref[...]).astype(BF16)


def _out_projection(x, oa, bg, u, oc, w_out, layer, conv_w, g1, ln_g, ln_b, sc2, sh2, prev, grp, alpha, tm):
    m, d = x.shape
    aw, bw, cw = oa.shape[1], bg.shape[1], oc.shape[1]
    n_i = m // tm
    tiles_per_mod = n_i // g1.shape[0]
    mod_rows = g1.shape[1]
    row = lambda i: (i, 0)
    fix = lambda i: (0, 0)
    mod_spec = pl.BlockSpec((None, mod_rows, d), lambda i: (i // tiles_per_mod, 0, 0))
    in_specs = [pl.BlockSpec((tm, d), row), pl.BlockSpec((tm, aw), row), pl.BlockSpec((tm, bw), row),
                pl.BlockSpec((tm, bw), row), pl.BlockSpec((tm, cw), row),
                _resident((None, aw + bw + cw, d), lambda i: (layer, 0, 0)), pl.BlockSpec((CONV_W, bw), fix),
                mod_spec, pl.BlockSpec((1, d), fix), pl.BlockSpec((1, d), fix), mod_spec, mod_spec]
    args = [x, oa, bg, u, oc, w_out, conv_w, g1, ln_g.reshape(1, d), ln_b.reshape(1, d), sc2, sh2]
    scratch = []
    if prev is None:
        assert grp.t % tm == 0
        scratch = [pltpu.VMEM((SUBLANE, bw), F32)]
    else:
        assert tm % grp.t == 0
        in_specs += [pl.BlockSpec((tm, bw), row), pl.BlockSpec((tm, bw), row)]
        args += [prev[0], prev[1]]
    return pl.pallas_call(
        functools.partial(_outproj_kernel, alpha=alpha, seq_tiles=max(grp.t // tm, 1), seg=min(grp.t, tm),
                          carried=prev is None),
        grid=(n_i,),
        in_specs=in_specs,
        out_specs=[pl.BlockSpec((tm, d), row), pl.BlockSpec((tm, d), row)],
        out_shape=[jax.ShapeDtypeStruct((m, d), F32), jax.ShapeDtypeStruct((m, d), BF16)],
        scratch_shapes=scratch,
        compiler_params=_params(("arbitrary",)),
        name="out_projection",
    )(*args)


def _ffn_kernel(*refs, alpha, seq_tiles, seg, carried):
    if carried:
        (h_ref, x_ref, wg_ref, wu_ref, wd_ref, cw_ref, g2_ref, lng_ref, lnb_ref,
         o_ref, gt_ref, acc_scr, prev_scr) = refs
    else:
        (h_ref, x_ref, wg_ref, wu_ref, wd_ref, cw_ref, g2_ref, lng_ref, lnb_ref, e0_ref, e1_ref,
         o_ref, gt_ref, acc_scr) = refs
    f = pl.program_id(1)
    if carried:
        @pl.when(pl.program_id(0) % seq_tiles == 0)
        def _():
            prev_scr[f] = jnp.zeros(prev_scr.shape[1:], F32)

    @pl.when(f == 0)
    def _():
        acc_scr[...] = jnp.zeros_like(acc_scr)

    h = h_ref[...]
    tm, tf = h.shape[0], wg_ref.shape[1]
    n_split = 2 if tf % (2 * LANE) == 0 else 1
    halves = [slice(c * tf // n_split, (c + 1) * tf // n_split) for c in range(n_split)]
    gates = [jnp.dot(h, wg_ref[:, cols], preferred_element_type=F32) for cols in halves]
    ups = [jnp.dot(h, wu_ref[:, cols], preferred_element_type=F32) for cols in halves]
    for cols, gate, up in zip(halves, gates, ups):
        if carried:
            e0, e1 = prev_scr[f, SUBLANE - 2:SUBLANE - 1, cols], prev_scr[f, SUBLANE - 1:SUBLANE, cols]
        else:
            e0, e1 = e0_ref[:, cols], e1_ref[:, cols]
        conv = _causal_conv(gate, e0, e1, cw_ref[:, cols], seg)
        if carried:
            prev_scr[f, :, cols] = gate[tm - SUBLANE:, :]
        gt_ref[:, cols] = gate[tm - gt_ref.shape[0]:, :]
        act = (jax.nn.silu(conv) * up).astype(BF16)
        acc_scr[...] = jnp.dot(act, wd_ref[cols, :], preferred_element_type=F32) + acc_scr[...]

    @pl.when(f == pl.num_programs(1) - 1)
    def _():
        z = alpha * x_ref[...] + (1.0 + g2_ref[...]) * acc_scr[...]
        o_ref[...] = _layer_norm(z, lng_ref[...], lnb_ref[...])


def _channel_mixer(h2, x1, w_gu, w_down, layer, conv_w, g2, ln_g, ln_b, prev, grp, alpha, tm, tf):
    m, d = x1.shape
    dff = w_down.shape[1]
    n_i, n_f = m // tm, dff // tf
    tiles_per_mod = n_i // g2.shape[0]
    mod_rows = g2.shape[1]
    fix = lambda i, f: (0, 0)
    in_specs = [pl.BlockSpec((tm, d), lambda i, f: (i, 0)), pl.BlockSpec((tm, d), lambda i, f: (i, 0)),
                pl.BlockSpec((None, d, tf), lambda i, f: (layer, 0, f)),
                pl.BlockSpec((None, d, tf), lambda i, f: (layer, 0, n_f + f)),
                pl.BlockSpec((None, tf, d), lambda i, f: (layer, f, 0)),
                pl.BlockSpec((CONV_W, tf), lambda i, f: (0, f)),
                pl.BlockSpec((None, mod_rows, d), lambda i, f: (i // tiles_per_mod, 0, 0)),
                pl.BlockSpec((1, d), fix), pl.BlockSpec((1, d), fix)]
    args = [h2, x1, w_gu, w_gu, w_down, conv_w, g2, ln_g.reshape(1, d), ln_b.reshape(1, d)]
    scratch = [pltpu.VMEM((tm, d), F32)]
    if prev is None:
        assert grp.t % tm == 0
        scratch.append(pltpu.VMEM((n_f, SUBLANE, tf), F32))
        gt_spec = pl.BlockSpec((None, SUBLANE, tf), lambda i, f: (i, 0, f))
        gt_shape = jax.ShapeDtypeStruct((n_i, SUBLANE, dff), F32)
    else:
        assert tm % grp.t == 0
        in_specs += [pl.BlockSpec((tm, tf), lambda i, f: (i, f))] * 2
        args += [prev[0], prev[1]]
        gt_spec = pl.BlockSpec((tm, tf), lambda i, f: (i, f))
        gt_shape = jax.ShapeDtypeStruct((m, dff), F32)
    return pl.pallas_call(
        functools.partial(_ffn_kernel, alpha=alpha, seq_tiles=max(grp.t // tm, 1), seg=min(grp.t, tm),
                          carried=prev is None),
        grid=(n_i, n_f),
        in_specs=in_specs,
        out_specs=[pl.BlockSpec((tm, d), lambda i, f: (i, 0)), gt_spec],
        out_shape=[jax.ShapeDtypeStruct((m, d), F32), gt_shape],
        scratch_shapes=scratch,
        compiler_params=_params(("arbitrary", "arbitrary")),
        name="channel_mixer",
    )(*args)


def _rope_tables(pos):
    half = HEAD // 2
    inv = jnp.power(jnp.float32(ROPE_THETA), -jnp.arange(half, dtype=F32) / half)
    ang = pos.astype(F32)[:, None] * inv[None, :]
    cos, sin = jnp.cos(ang), jnp.sin(ang)
    zero = jnp.zeros_like(sin)
    reps = LANE // HEAD
    return (jnp.tile(jnp.concatenate([cos, cos], axis=1), (1, reps)),
            jnp.tile(jnp.concatenate([zero, sin], axis=1), (1, reps)),
            jnp.tile(jnp.concatenate([-sin, zero], axis=1), (1, reps)))


def _with_past(past, new, lp, twice=False):
    b, t, w = new.shape
    parts = [new]
    if past is not None:
        p = past.reshape(b, past.shape[1], -1).astype(BF16)
        parts = [jnp.concatenate([p, p], axis=-1) if twice else p, new]
    n = sum(a.shape[1] for a in parts)
    if lp > n:
        parts.append(jnp.zeros((b, lp - n, w), BF16))
    return parts[0] if len(parts) == 1 else jnp.concatenate(parts, axis=1)


def _layer(x, mod, tables, grp, past, layer, big, small, dm, alpha):
    w_in_p, w_out, w_uk, w_uv, w_uv_t, w_gu, w_down = big
    conv_b_w, kv_norm, ln1_g, ln1_b, ln2_g, ln2_b, conv_f_w = small
    b, t = grp.b, grp.t
    m, d = x.shape
    carried = past is None
    tm = _row_tile(t, 256) if carried else m
    if carried:
        mods = [a.reshape(b, 1, d) for a in jnp.split(mod, N_MOD, axis=-1)]
    else:
        mods = [jnp.repeat(a, t, axis=0).reshape(1, m, d) for a in jnp.split(mod, N_MOD, axis=-1)]
    sh1, sc1, g1, sh2, sc2, g2 = mods

    (qa, ka, kab, qi, qcr, ki, kib, kr, krb, va, vab, bg, u, qcn, lat, latb, wi, vat) = _in_projection(
        x, sc1, sh1, w_in_p, layer, tables, kv_norm, dm, tm)

    three = lambda a: a.reshape(b, t, a.shape[-1])
    if carried:
        tq = _row_tile(t, 256)
        kn, vct = _kv_up(latb, w_uk, w_uv_t, layer, v_transposed=True)
        oa = _dsa_attention_t(three(qi), three(wi), three(kib), three(qa), three(kab), vat, grp, tq)
        oc = _mla_attention_t(three(qcn), three(qcr), three(kn), vct, three(krb), grp, tq)
        prev_b = prev_f = None
    else:
        l = grp.past + t
        lp, _ = _key_tile(l)
        p_ak, p_av, p_ik, p_lat, p_kr, prev_b, prev_f = past
        k_all = _with_past(p_ak, three(kab), lp)
        v_all = _with_past(p_av, three(vab), lp)
        ki_all = _with_past(p_ik, three(kib), lp, twice=True)
        kr_all = _with_past(p_kr, three(krb), lp, twice=True)
        lat_all = _with_past(p_lat, three(latb), lp)
        kn, vc = _kv_up(lat_all.reshape(b * lp, dm.r), w_uk, w_uv, layer, v_transposed=False)
        oa = _dsa_attention(three(qi), three(wi), ki_all, three(qa), k_all, v_all, grp, l, t)
        oc = _mla_attention(three(qcn), three(qcr), kn.reshape(b, lp, -1), vc.reshape(b, lp, -1), kr_all,
                            grp, l, t)

    def expand(state):
        return jnp.repeat(state[:, 0], t, axis=0), jnp.repeat(state[:, 1], t, axis=0)

    x1, h2 = _out_projection(x, oa.reshape(m, -1), bg, u, oc.reshape(m, -1), w_out, layer, conv_b_w, g1,
                             ln1_g, ln1_b, sc2, sh2, None if carried else expand(prev_b), grp, alpha, tm)
    tm_f = _row_tile(t, 512) if carried else m
    tf = _row_tile(dm.dff, 512)
    x2, gate_rows = _channel_mixer(h2, x1, w_gu, w_down, layer, conv_f_w, g2, ln2_g, ln2_b,
                                   None if carried else expand(prev_f), grp, alpha, tm_f, tf)
    if carried:
        new_f = gate_rows.reshape(b, t // tm_f, SUBLANE, dm.dff)[:, -1, SUBLANE - (CONV_W - 1):, :]
    else:
        new_f = gate_rows.reshape(b, t, dm.dff)[:, t - (CONV_W - 1):, :]
    new_b = u.reshape(b, t, dm.bw)[:, t - (CONV_W - 1):, :]
    heads = dm.aw // HEAD
    rows = (ka.reshape(b, t, heads, HEAD), va.reshape(b, t, heads, HEAD), ki.reshape(b, t, HEAD),
            lat.reshape(b, t, dm.r), kr.reshape(b, t, HEAD), new_b, new_f)
    return x2, rows


def kernel(x_prompt, x_sample, c_prompt, c_sample, cache_a_k, cache_a_v, cache_idx_k, cache_mla_latent,
           cache_mla_krope, state_conv_b, state_conv_ffn, w_in, w_out, conv_b_w, mla_kv_norm, mla_w_uk,
           mla_w_uv, w_mod, b_mod, ln1_g, ln1_b, ln2_g, ln2_b, ffn_w_gu, ffn_conv_w, ffn_w_down):
    depth, d, _ = w_in.shape
    a_heads = cache_a_k.shape[3]
    dm = Dims(d=d, aw=a_heads * HEAD, bw=conv_b_w.shape[2], ch=mla_w_uk.shape[2] // C_NOPE,
              r=mla_w_uk.shape[1], dff=ffn_w_down.shape[1], depth=depth)
    alpha = (2 * depth) ** 0.25
    grp_p = Group(b=x_prompt.shape[0], t=x_prompt.shape[1], past=0)
    grp_s = Group(b=x_sample.shape[0], t=x_sample.shape[1], past=cache_a_k.shape[2])

    n_c = grp_p.b + grp_s.b
    c_all = jnp.concatenate([c_prompt, c_sample, jnp.zeros((-n_c % SUBLANE, d), F32)], axis=0)
    mod = _modulation(c_all, w_mod, b_mod)

    tab_p = _rope_tables(jnp.arange(grp_p.t, dtype=I32))
    tab_s = tuple(jnp.tile(a, (grp_s.b, 1)) for a in _rope_tables(grp_s.past + jnp.arange(grp_s.t, dtype=I32)))

    xp = x_prompt.reshape(grp_p.b * grp_p.t, d)
    xs = x_sample.reshape(grp_s.b * grp_s.t, d)
    rows_p, rows_s = [], []
    w_uv_b = mla_w_uv.astype(BF16)
    big = (_pack_w_in(w_in, dm), w_out.astype(BF16), mla_w_uk.astype(BF16), w_uv_b, jnp.swapaxes(w_uv_b, 1, 2),
           ffn_w_gu.astype(BF16), ffn_w_down.astype(BF16))
    for l in range(depth):
        small = (conv_b_w[l], mla_kv_norm[l], ln1_g[l], ln1_b[l], ln2_g[l], ln2_b[l], ffn_conv_w[l])
        xp, rp = _layer(xp, mod[l, :grp_p.b], tab_p, grp_p, None, l, big, small, dm, alpha)
        past_l = (cache_a_k[l], cache_a_v[l], cache_idx_k[l], cache_mla_latent[l], cache_mla_krope[l],
                  state_conv_b[l], state_conv_ffn[l])
        xs, rs = _layer(xs, mod[l, grp_p.b:n_c], tab_s, grp_s, past_l, l, big, small, dm, alpha)
        rows_p.append(rp)
        rows_s.append(rs)
    outs_p = [jnp.stack(r) for r in zip(*rows_p)]
    outs_s = [jnp.stack(r) for r in zip(*rows_s)]
    return (xp.reshape(x_prompt.shape), xs.reshape(x_sample.shape), *outs_p, *outs_s)
```

```python
import functools
from typing import NamedTuple

import numpy as np
import jax
import jax.numpy as jnp
from jax import lax
from jax.experimental import pallas as pl
from jax.experimental.pallas import tpu as pltpu

F32, BF16, I32, I16 = jnp.float32, jnp.bfloat16, jnp.int32, jnp.int16

CHUNK = 64
CONV_W = 3
ROPE_THETA = 10000.0
HEAD = 64
IDX_HEADS = 16
TOPK_MAX = 256
C_NOPE = 128
C_V = 128
N_MOD = 6
LN_EPS = 1e-5
RMS_EPS = 1e-6

LANE = 128
SUBLANE = 8
VMEM_LIMIT = 50 * 1024 * 1024

MASKED = -1e30
LOG2E = 1.4426950408889634
INT_MIN = -2 ** 31
I16_MIN = -2 ** 15
KEY_NEG_INF = int(np.array(-np.inf, np.float32).view(np.int32)) ^ 0x7FFFFFFF


class Dims(NamedTuple):
    d: int
    aw: int
    bw: int
    ch: int
    r: int
    dff: int
    depth: int


class Group(NamedTuple):
    b: int
    t: int
    past: int


def _row_tile(m, pref):
    if m <= pref:
        return m
    t = pref - pref % SUBLANE
    while m % t:
        t -= SUBLANE
    return t


def _key_tile(l):
    lp = -(-l // LANE) * LANE
    for tk in (512, 384, 256, 128):
        if lp % tk == 0:
            return lp, tk
    raise AssertionError(lp)


def _params(sem):
    return pltpu.CompilerParams(dimension_semantics=sem, vmem_limit_bytes=VMEM_LIMIT)


def _resident(shape, index_map):
    return pl.BlockSpec(shape, index_map, pipeline_mode=pl.Buffered(1))


def _mod_kernel(c_ref, w_ref, b_ref, o_ref):
    a = jax.nn.silu(c_ref[...]).astype(BF16)
    o_ref[...] = jnp.dot(a, w_ref[...].astype(BF16), preferred_element_type=F32) + b_ref[...]


def _modulation(c, w_mod, b_mod):
    depth, d, n = w_mod.shape
    rows = c.shape[0]
    tn = _row_tile(n, 1024)
    return pl.pallas_call(
        _mod_kernel,
        grid=(depth, n // tn),
        in_specs=[pl.BlockSpec((rows, d), lambda l, j: (0, 0)),
                  pl.BlockSpec((None, d, tn), lambda l, j: (l, 0, j)),
                  pl.BlockSpec((None, 1, tn), lambda l, j: (l, 0, j))],
        out_specs=pl.BlockSpec((None, rows, tn), lambda l, j: (l, 0, j)),
        out_shape=jax.ShapeDtypeStruct((depth, rows, n), F32),
        compiler_params=_params(("arbitrary", "arbitrary")),
        name="modulation",
    )(c, w_mod, b_mod.reshape(depth, 1, n))


def _pack_w_in(w, dm):
    d, aw, bw, ch, r = dm.d, dm.aw, dm.bw, dm.ch, dm.r
    lead = w.shape[:-1]
    o = np.cumsum([0, aw, aw, aw, IDX_HEADS * HEAD, HEAD, IDX_HEADS, bw, bw, bw, ch * (C_NOPE + HEAD), r, HEAD])
    qa, ka, va, qi, ki, wi, bg, cg, xb, qc, lat, kr = [w[..., o[i]:o[i + 1]] for i in range(12)]
    qc = qc.reshape(lead + (ch, C_NOPE + HEAD))
    qcn = qc[..., :C_NOPE].reshape(lead + (ch * C_NOPE,))
    qcr = qc[..., C_NOPE:].reshape(lead + (ch * HEAD,))
    pad = jnp.zeros(lead + (LANE - IDX_HEADS,), w.dtype)
    return jnp.concatenate([qa, ka, qi, qcr, ki, ki, kr, kr, va, bg, cg, xb, qcn, lat, wi, pad],
                           axis=-1).astype(BF16)


def _rope(acc, cos, s1, s2):
    outs = []
    for s in range(acc.shape[1] // LANE):
        xs = acc[:, s * LANE:(s + 1) * LANE]
        outs.append(xs * cos + pltpu.roll(xs, HEAD // 2, 1) * s1 + pltpu.roll(xs, LANE - HEAD // 2, 1) * s2)
    return outs[0] if len(outs) == 1 else jnp.concatenate(outs, axis=1)


def _inproj_kernel(x_ref, sc_ref, sh_ref, w_ref, cos_ref, s1_ref, s2_ref, nrm_ref,
                   qa_ref, ka_ref, kab_ref, qi_ref, qcr_ref, ki_ref, kib_ref, kr_ref, krb_ref,
                   va_ref, vab_ref, bg_ref, u_ref, qcn_ref, lat_ref, latb_ref, wi_ref, vat_ref, *, dm):
    aw, bw, ch, r = dm.aw, dm.bw, dm.ch, dm.r
    h = (x_ref[...] * (1.0 + sc_ref[...]) + sh_ref[...]).astype(BF16)
    cos, s1, s2 = cos_ref[...], s1_ref[...], s2_ref[...]
    col = [0]

    def proj(width):
        c0 = col[0]
        col[0] = c0 + width
        return jnp.dot(h, w_ref[:, c0:c0 + width], preferred_element_type=F32)

    def pieces(width, step=512):
        return [(o, min(step, width - o)) for o in range(0, width, step)]

    for o, wd in pieces(aw):
        qa_ref[:, o:o + wd] = _rope(proj(wd), cos, s1, s2).astype(BF16)
    for o, wd in pieces(aw):
        y = _rope(proj(wd), cos, s1, s2)
        ka_ref[:, o:o + wd] = y
        kab_ref[:, o:o + wd] = y.astype(BF16)
    for o, wd in pieces(IDX_HEADS * HEAD):
        qi_ref[:, o:o + wd] = _rope(proj(wd), cos, s1, s2).astype(BF16)
    for o, wd in pieces(ch * HEAD):
        qcr_ref[:, o:o + wd] = _rope(proj(wd), cos, s1, s2).astype(BF16)
    for f32_ref, b16_ref in ((ki_ref, kib_ref), (kr_ref, krb_ref)):
        y = _rope(proj(LANE), cos, s1, s2)
        f32_ref[...] = y[:, :HEAD]
        b16_ref[...] = y.astype(BF16)
    for o, wd in pieces(aw):
        y = proj(wd)
        va_ref[:, o:o + wd] = y
        vab_ref[:, o:o + wd] = y.astype(BF16)
        vat_ref[o:o + wd, :] = y.T.astype(BF16)
    for o, wd in pieces(bw):
        bg_ref[:, o:o + wd] = proj(wd)
    c_cg = col[0]
    for o, wd in pieces(bw):
        cg = jnp.dot(h, w_ref[:, c_cg + o:c_cg + o + wd], preferred_element_type=F32)
        xb = jnp.dot(h, w_ref[:, c_cg + bw + o:c_cg + bw + o + wd], preferred_element_type=F32)
        u_ref[:, o:o + wd] = cg * xb
    col[0] = c_cg + 2 * bw
    for o, wd in pieces(ch * C_NOPE):
        qcn_ref[:, o:o + wd] = proj(wd).astype(BF16)
    lat = proj(r)
    lat = lat * lax.rsqrt(jnp.mean(lat * lat, axis=-1, keepdims=True) + RMS_EPS) * nrm_ref[...]
    lat_ref[...] = lat
    latb_ref[...] = lat.astype(BF16)
    wi_ref[...] = proj(LANE) * (IDX_HEADS ** -0.5)


def _in_projection(x, sc, sh, w_packed, layer, tables, kv_norm, dm, tm):
    m, d = x.shape
    aw, bw, ch, r = dm.aw, dm.bw, dm.ch, dm.r
    npk = w_packed.shape[2]
    n_i = m // tm
    tiles_per_mod = n_i // sc.shape[0]
    mod_rows = sc.shape[1]
    tab_tiles = tables[0].shape[0] // tm
    widths = [(aw, BF16), (aw, F32), (aw, BF16), (IDX_HEADS * HEAD, BF16), (ch * HEAD, BF16),
              (HEAD, F32), (LANE, BF16), (HEAD, F32), (LANE, BF16),
              (aw, F32), (aw, BF16), (bw, F32), (bw, F32), (ch * C_NOPE, BF16), (r, F32), (r, BF16),
              (LANE, F32)]
    row = lambda i: (i, 0)
    mod_spec = pl.BlockSpec((None, mod_rows, d), lambda i: (i // tiles_per_mod, 0, 0))
    tab_spec = pl.BlockSpec((tm, LANE), lambda i: (i % tab_tiles, 0))
    return pl.pallas_call(
        functools.partial(_inproj_kernel, dm=dm),
        grid=(n_i,),
        in_specs=[pl.BlockSpec((tm, d), row), mod_spec, mod_spec,
                  _resident((None, d, npk), lambda i: (layer, 0, 0)),
                  tab_spec, tab_spec, tab_spec,
                  pl.BlockSpec((1, r), lambda i: (0, 0))],
        out_specs=[pl.BlockSpec((tm, w), row) for w, _ in widths] + [pl.BlockSpec((aw, tm), lambda i: (0, i))],
        out_shape=[jax.ShapeDtypeStruct((m, w), dt) for w, dt in widths] + [jax.ShapeDtypeStruct((aw, m), BF16)],
        compiler_params=_params(("arbitrary",)),
        name="in_projection",
    )(x, sc, sh, w_packed, *tables, kv_norm.reshape(1, r))


def _kvup_kernel(l_ref, wk_ref, wv_ref, k_ref, v_ref, *, v_transposed):
    lat = l_ref[...]
    k_ref[...] = jnp.dot(lat, wk_ref[...], preferred_element_type=F32).astype(BF16)
    if v_transposed:
        v_ref[...] = lax.dot_general(wv_ref[...], lat, _NT, preferred_element_type=F32).astype(BF16)
    else:
        v_ref[...] = jnp.dot(lat, wv_ref[...], preferred_element_type=F32).astype(BF16)


def _kv_up(lat, w_uk, w_uv, layer, v_transposed):
    m, r = lat.shape
    n = w_uk.shape[2]
    tm = _row_tile(m, 512)
    if v_transposed:
        v_spec, v_shape = pl.BlockSpec((n, tm), lambda i: (0, i)), (n, m)
    else:
        v_spec, v_shape = pl.BlockSpec((tm, n), lambda i: (i, 0)), (m, n)
    return pl.pallas_call(
        functools.partial(_kvup_kernel, v_transposed=v_transposed),
        grid=(m // tm,),
        in_specs=[pl.BlockSpec((tm, r), lambda i: (i, 0)),
                  pl.BlockSpec((None, r, n), lambda i: (layer, 0, 0)),
                  pl.BlockSpec((None,) + w_uv.shape[1:], lambda i: (layer, 0, 0))],
        out_specs=[pl.BlockSpec((tm, n), lambda i: (i, 0)), v_spec],
        out_shape=[jax.ShapeDtypeStruct((m, n), BF16), jax.ShapeDtypeStruct(v_shape, BF16)],
        compiler_params=_params(("arbitrary",)),
        name="latent_up_projection",
    )(lat, w_uk, w_uv)


_NT = (((1,), (1,)), ((), ()))


def _visible_tiles(qpos0, tq, l, tk):
    last_visible = ((qpos0 + tq - 1) // CHUNK + 1) * CHUNK
    nvis = min(l, last_visible) if isinstance(qpos0, int) else jnp.minimum(l, last_visible)
    return (nvis + tk - 1) // tk


def _for(n, body, init):
    if isinstance(n, int):
        for i in range(n):
            init = body(i, init)
        return init
    return lax.fori_loop(0, n, body, init)


def _tile_start(kt, tk):
    return kt * tk if isinstance(kt, int) else pl.multiple_of(kt * tk, tk)


def _flash_step(carry, s, v):
    m, l, acc = carry
    m_new = jnp.maximum(m, jnp.max(s, axis=1, keepdims=True))
    alpha = jnp.exp(m - m_new)
    p = jnp.exp(s - m_new)
    l = alpha * l + jnp.sum(p, axis=1, keepdims=True)
    acc = alpha * acc + jnp.dot(p.astype(BF16), v, preferred_element_type=F32)
    return m_new, l, acc


def _flash_init(rows, width):
    return (jnp.full((rows, 1), MASKED, F32), jnp.zeros((rows, 1), F32), jnp.zeros((rows, width), F32))


def _ordered_key(x):
    b = pltpu.bitcast(x, I32)
    return jnp.where(b < 0, b ^ 0x7FFFFFFF, b)


def _dsa_kernel(qi_ref, wi_ref, ki_ref, qa_ref, k_ref, v_ref, o_ref, key_scr, bias_scr, thr2_scr,
                *, tq, tk, l, lp, past, topk, aw, one_block):
    qpos0 = past if one_block else past + pl.program_id(1) * tq
    ntiles = _visible_tiles(qpos0, tq, l, tk)
    qchunk = (qpos0 + lax.broadcasted_iota(I32, (tq, 1), 0)) // CHUNK
    lane = lax.broadcasted_iota(I32, (1, LANE), 1)
    lo_half = lane < HEAD
    kf = jnp.float32(topk)

    def split_heads(qs):
        zero = jnp.zeros_like(qs)
        return jnp.concatenate([jnp.where(lo_half, qs, zero), jnp.where(lo_half, zero, qs)], axis=0)

    wi = wi_ref[...] * (HEAD ** -0.5)
    q_idx = jnp.concatenate([split_heads(qi_ref[:, s * LANE:(s + 1) * LANE]) for s in range(IDX_HEADS // 2)],
                            axis=0)
    w_idx = [wi[:, h:h + 1] for h in range(IDX_HEADS)]

    def score_body(kt, _):
        k0 = _tile_start(kt, tk)
        s_all = lax.dot_general(q_idx, ki_ref[pl.ds(k0, tk), :], _NT, preferred_element_type=F32)
        score = jnp.zeros((tq, tk), F32)
        for h in range(IDX_HEADS):
            score = score + jnp.maximum(s_all[h * tq:(h + 1) * tq], 0.0) * w_idx[h]
        kpos = k0 + lax.broadcasted_iota(I32, (1, tk), 1)
        vis = (kpos // CHUNK <= qchunk) & (kpos < l)
        key_scr[kt] = _ordered_key(jnp.where(vis, score, -jnp.inf))
        return 0

    _for(ntiles, score_body, 0)

    def count(pred):
        def body(kt, acc):
            for j in range(tk // LANE):
                ks = key_scr[kt, :, j * LANE:(j + 1) * LANE]
                acc = acc + jnp.where(pred(ks, kt * tk + j * LANE), 1.0, 0.0)
            return acc
        acc = _for(ntiles, body, jnp.zeros((tq, LANE), F32))
        return jnp.sum(acc, axis=1, keepdims=True)

    def wide(col):
        return jnp.broadcast_to(col, (tq, LANE))

    def bit_body(b, lo):
        cand = lo + jnp.left_shift(jnp.int32(1), 31 - b)
        cand_w = wide(cand)
        c = count(lambda ks, _: ks >= cand_w)
        return jnp.where(c >= kf, cand, lo)

    thr = lax.fori_loop(0, 32, bit_body, jnp.full((tq, 1), INT_MIN, I32))
    thr_w = wide(thr)

    n_ge = count(lambda ks, _: ks >= thr_w)
    n_gt = count(lambda ks, _: ks > thr_w)
    excess = (n_ge > kf) & (thr > KEY_NEG_INF)
    thr2_scr[...] = jnp.zeros((tq, 1), I32)

    @pl.when(jnp.max(jnp.where(excess, 1.0, 0.0)) > 0.0)
    def _():
        need = kf - n_gt
        nbits = lp.bit_length()

        def bit2_body(b, lo):
            cand = lo + jnp.left_shift(jnp.int32(1), nbits - 1 - b)
            cand_w = wide(cand)
            c = count(lambda ks, base: jnp.where(ks == thr_w, lp - (base + lane), 0) >= cand_w)
            return jnp.where(c >= need, cand, lo)

        thr2_scr[...] = lax.fori_loop(0, nbits, bit2_body, jnp.zeros((tq, 1), I32))

    thr2_w = wide(thr2_scr[...])

    def bias_body(kt, _):
        for j in range(tk // LANE):
            ks = key_scr[kt, :, j * LANE:(j + 1) * LANE]
            tie = jnp.where(lp - (kt * tk + j * LANE + lane) >= thr2_w, 0.0, MASKED)
            bias = jnp.where(ks > thr_w, 0.0, jnp.where(ks == thr_w, tie, MASKED))
            bias_scr[kt, :, j * LANE:(j + 1) * LANE] = jnp.where(ks > KEY_NEG_INF, bias, MASKED)
        return 0

    _for(ntiles, bias_body, 0)

    for pr in range(aw // LANE):
        cols = slice(pr * LANE, (pr + 1) * LANE)
        q2 = split_heads(qa_ref[:, cols])

        def att_body(kt, carry, cols=cols, q2=q2):
            k0 = _tile_start(kt, tk)
            s = lax.dot_general(q2, k_ref[pl.ds(k0, tk), cols], _NT, preferred_element_type=F32)
            bias = bias_scr[kt]
            s = s * (HEAD ** -0.5) + jnp.concatenate([bias, bias], axis=0)
            return _flash_step(carry, s, v_ref[pl.ds(k0, tk), cols])

        _, den, acc = _for(ntiles, att_body, _flash_init(2 * tq, LANE))
        o2 = acc / den
        o_ref[:, cols] = jnp.where(lo_half, o2[:tq], o2[tq:]).astype(BF16)


def _dsa_attention(qi, wi, ki, qa, k, v, grp, l, tq):
    b, t, aw = qa.shape
    lp = k.shape[1]
    _, tk = _key_tile(l)
    topk = min(TOPK_MAX, l // 4)
    qblk = lambda w: pl.BlockSpec((None, tq, w), lambda i, j: (i, j, 0))
    kblk = lambda w: pl.BlockSpec((None, lp, w), lambda i, j: (i, 0, 0))
    return pl.pallas_call(
        functools.partial(_dsa_kernel, tq=tq, tk=tk, l=l, lp=lp, past=grp.past, topk=topk, aw=aw,
                          one_block=t == tq),
        grid=(b, t // tq),
        in_specs=[qblk(IDX_HEADS * HEAD), qblk(LANE), kblk(LANE), qblk(aw), kblk(aw), kblk(aw)],
        out_specs=qblk(aw),
        out_shape=jax.ShapeDtypeStruct((b, t, aw), BF16),
        scratch_shapes=[pltpu.VMEM((lp // tk, tq, tk), I32),
                        pltpu.VMEM((lp // tk, tq, tk), F32),
                        pltpu.VMEM((tq, 1), I32)],
        compiler_params=_params(("arbitrary", "arbitrary")),
        name="dsa_attention",
    )(qi, wi, ki, qa, k, v)


def _mla_latent_kernel(qn_ref, qr_ref, lat_ref, kr_ref, wuk_ref, wuv_ref, o_ref, *, t, l, past, ch):
    lo_half = lax.broadcasted_iota(I32, (1, LANE), 1) < HEAD
    scale = (C_NOPE + HEAD) ** -0.5
    q_lat, q_rope = [], []
    for h in range(ch):
        cols = slice(h * C_NOPE, (h + 1) * C_NOPE)
        q_lat.append(lax.dot_general(qn_ref[:, cols], wuk_ref[:, cols], _NT, preferred_element_type=F32))
        qs = qr_ref[:, (h // 2) * LANE:(h // 2 + 1) * LANE]
        zero = jnp.zeros_like(qs)
        q_rope.append(jnp.where(lo_half, qs, zero) if h % 2 == 0 else jnp.where(lo_half, zero, qs))
    q_lat = jnp.concatenate(q_lat, axis=0).astype(BF16)
    q_rope = jnp.concatenate(q_rope, axis=0)
    lat = lat_ref[...]
    s = (lax.dot_general(q_lat, lat, _NT, preferred_element_type=F32)
         + lax.dot_general(q_rope, kr_ref[...], _NT, preferred_element_type=F32)) * scale
    qchunk = (past + lax.broadcasted_iota(I32, (t, 1), 0)) // CHUNK
    kpos = lax.broadcasted_iota(I32, (1, lat.shape[0]), 1)
    vis = (kpos // CHUNK <= jnp.concatenate([qchunk] * ch, axis=0)) & (kpos < l)
    s = jnp.where(vis, s, MASKED)
    p = jnp.exp(s - jnp.max(s, axis=1, keepdims=True))
    den = jnp.sum(p, axis=1, keepdims=True)
    o_lat = (jnp.dot(p.astype(BF16), lat, preferred_element_type=F32) / den).astype(BF16)
    for h in range(ch):
        cols = slice(h * C_V, (h + 1) * C_V)
        o_ref[:, cols] = jnp.dot(o_lat[h * t:(h + 1) * t, :], wuv_ref[:, cols],
                                 preferred_element_type=F32).astype(BF16)


def _mla_attention_latent(qn, qr, lat, kr, w_uk, w_uv, layer, grp, l):
    b, t, wn = qn.shape
    ch = wn // C_NOPE
    lp, r = lat.shape[1:]
    qblk = lambda w: pl.BlockSpec((None, t, w), lambda i: (i, 0, 0))
    kblk = lambda w: pl.BlockSpec((None, lp, w), lambda i: (i, 0, 0))
    wblk = pl.BlockSpec((None, r, wn), lambda i: (layer, 0, 0))
    return pl.pallas_call(
        functools.partial(_mla_latent_kernel, t=t, l=l, past=grp.past, ch=ch),
        grid=(b,),
        in_specs=[qblk(wn), qblk(ch * HEAD), kblk(r), kblk(LANE), wblk, wblk],
        out_specs=qblk(wn),
        out_shape=jax.ShapeDtypeStruct((b, t, wn), BF16),
        compiler_params=_params(("arbitrary",)),
        name="mla_attention_latent",
    )(qn, qr, lat, kr, w_uk, w_uv)


ONES_ROWS = 16


def _flash_t_init(m_scr, acc_scr):
    m_scr[...] = jnp.full(m_scr.shape, MASKED, F32)
    acc_scr[...] = jnp.zeros(acc_scr.shape, F32)


def _flash_t_stage(slot, h, s, s_scr):
    s_scr[slot, h] = s
    return jnp.max(s, axis=0, keepdims=True)


def _flash_t_step(slot, h, s_max, vt, c, s_scr, m_scr, acc_scr):
    m_old = m_scr[h]
    m_new = jnp.maximum(m_old, s_max)
    alpha = jnp.exp2((m_old - m_new) * c)
    p = jnp.exp2((s_scr[slot, h] - m_new) * c).astype(BF16)
    vt_ones = jnp.concatenate([vt, jnp.ones((ONES_ROWS, vt.shape[1]), BF16)], axis=0)
    acc_scr[h] = alpha * acc_scr[h] + jnp.dot(vt_ones, p, preferred_element_type=F32)
    m_scr[h] = m_new


def _flash_t_out(o_ref, acc_scr):
    heads, width = acc_scr.shape[0], acc_scr.shape[1] - ONES_ROWS
    ot = jnp.concatenate([acc_scr[h, :width, :] / acc_scr[h, width:width + 1, :] for h in range(heads)], axis=0)
    o_ref[...] = ot.T.astype(BF16)


def _split_heads_t(slab_t):
    row_lo = lax.broadcasted_iota(I32, (LANE, 1), 0) < HEAD
    zero = jnp.zeros_like(slab_t)
    return jnp.where(row_lo, slab_t, zero), jnp.where(row_lo, zero, slab_t)


def _dsa_t_kernel(qi_ref, wi_ref, ki_ref, qa_ref, k_ref, vt_ref, o_ref,
                  key_scr, hi_scr, lo_scr, bias_scr, thr2_scr, qit_scr, qat_scr, s_scr, m_scr, acc_scr,
                  *, tq, tk, l, past, topk):
    heads = acc_scr.shape[0]
    qpos0 = past + pl.program_id(1) * tq
    ntiles = _visible_tiles(qpos0, tq, l, tk)
    qchunk = (qpos0 + lax.broadcasted_iota(I32, (1, tq), 1)) // CHUNK
    krow = lax.broadcasted_iota(I32, (tk, 1), 0)
    kf = jnp.float32(topk)

    for s in range(IDX_HEADS // 2):
        qit_scr[2 * s], qit_scr[2 * s + 1] = _split_heads_t(qi_ref[:, s * LANE:(s + 1) * LANE].T)
    for s in range(heads // 2):
        slab_t = (qa_ref[:, s * LANE:(s + 1) * LANE].astype(F32) * (HEAD ** -0.5)).astype(BF16).T
        qat_scr[2 * s], qat_scr[2 * s + 1] = _split_heads_t(slab_t)
    w_t = wi_ref[...].T * (HEAD ** -0.5)
    w_rows = [w_t[h:h + 1, :] for h in range(IDX_HEADS)]

    def score_body(kt, _):
        k0 = pl.multiple_of(kt * tk, tk)
        ki_tile = ki_ref[pl.ds(k0, tk), :]
        score = jnp.zeros((tk, tq), F32)
        for h in range(IDX_HEADS):
            s = jnp.dot(ki_tile, qit_scr[h], preferred_element_type=F32)
            score = score + jnp.maximum(s, 0.0) * w_rows[h]
        kpos = k0 + krow
        vis = (kpos // CHUNK <= qchunk) & (kpos < l)
        key = _ordered_key(jnp.where(vis, score, -jnp.inf))
        key_scr[kt] = key
        hi_scr[kt] = (key >> 16).astype(I16)
        lo_scr[kt] = ((key & 0xFFFF) + I16_MIN).astype(I16)
        return 0

    lax.fori_loop(0, ntiles, score_body, 0)

    def count(pred):
        def body(kt, acc):
            hit = jnp.where(pred(key_scr[kt], kt * tk), 1.0, 0.0)
            return acc + hit.reshape(tk // SUBLANE, SUBLANE, tq).sum(axis=0)
        acc = lax.fori_loop(0, ntiles, body, jnp.zeros((SUBLANE, tq), F32))
        return jnp.sum(acc, axis=0, keepdims=True)

    def count16(half_scr, cand, strict):
        rows = 2 * SUBLANE
        cand16 = cand.astype(I16)

        def body(kt, acc):
            half = half_scr[kt]
            hit = jnp.where(half > cand16 if strict else half >= cand16, jnp.int16(1), jnp.int16(0))
            parts = hit.reshape(tk // (4 * rows), 4, rows, tq)
            for g in range(parts.shape[0]):
                acc = acc + parts[g]
            return acc

        acc = lax.fori_loop(0, ntiles, body, jnp.zeros((4, rows, tq), I16))
        return acc.astype(I32).sum(axis=0).sum(axis=0, keepdims=True)

    def kth_largest16(half_scr, k_need):
        def bit_body(b, lo):
            cand = lo + jnp.left_shift(jnp.int32(1), 15 - b)
            return jnp.where(count16(half_scr, cand, False) >= k_need, cand, lo)
        return lax.fori_loop(0, 16, bit_body, jnp.full((1, tq), I16_MIN, I32))

    thr_hi = kth_largest16(hi_scr, jnp.full((1, tq), topk, I32))
    n_gt_hi = count16(hi_scr, thr_hi, True)
    thr_hi16 = thr_hi.astype(I16)

    def mark_body(kt, _):
        lo_scr[kt] = jnp.where(hi_scr[kt] == thr_hi16, lo_scr[kt], jnp.int16(I16_MIN))
        return 0

    lax.fori_loop(0, ntiles, mark_body, 0)
    thr_lo = kth_largest16(lo_scr, topk - n_gt_hi)
    thr = thr_hi * 65536 + (thr_lo - I16_MIN)

    lp = key_scr.shape[0] * tk
    n_gt = n_gt_hi + count16(lo_scr, thr_lo, True)
    n_ge = jnp.where(thr_lo > I16_MIN, n_gt_hi + count16(lo_scr, thr_lo, False), count16(hi_scr, thr_hi, False))
    excess = (n_ge > topk) & (thr > KEY_NEG_INF)
    thr2_scr[...] = jnp.zeros((1, tq), I32)

    @pl.when(jnp.max(jnp.where(excess, 1.0, 0.0)) > 0.0)
    def _():
        need = (topk - n_gt).astype(F32)
        nbits = lp.bit_length()

        def bit2_body(b, lo):
            cand = lo + jnp.left_shift(jnp.int32(1), nbits - 1 - b)
            c = count(lambda ks, base: jnp.where(ks == thr, lp - (base + krow), 0) >= cand)
            return jnp.where(c >= need, cand, lo)

        thr2_scr[...] = lax.fori_loop(0, nbits, bit2_body, jnp.zeros((1, tq), I32))

    thr2 = thr2_scr[...]

    def bias_body(kt, _):
        ks = key_scr[kt]
        tie = jnp.where(lp - (kt * tk + krow) >= thr2, 0.0, MASKED)
        bias = jnp.where(ks > thr, 0.0, jnp.where(ks == thr, tie, MASKED))
        bias_scr[kt] = jnp.where(ks > KEY_NEG_INF, bias, MASKED)
        return 0

    lax.fori_loop(0, ntiles, bias_body, 0)

    _flash_t_init(m_scr, acc_scr)

    def att_body(kt, _):
        k0 = pl.multiple_of(kt * tk, tk)
        slot = kt % 2
        s_max = []
        for h in range(heads):
            cols = slice((h // 2) * LANE, (h // 2 + 1) * LANE)
            s = jnp.dot(k_ref[pl.ds(k0, tk), cols], qat_scr[h], preferred_element_type=F32) + bias_scr[kt]
            s_max.append(_flash_t_stage(slot, h, s, s_scr))
        for h in range(heads):
            _flash_t_step(slot, h, s_max[h], vt_ref[h * HEAD:(h + 1) * HEAD, pl.ds(k0, tk)], LOG2E,
                          s_scr, m_scr, acc_scr)
        return 0

    lax.fori_loop(0, ntiles, att_body, 0)
    _flash_t_out(o_ref, acc_scr)


def _dsa_attention_t(qi, wi, ki, qa, k, vt, grp, tq):
    b, t, aw = qa.shape
    lp, tk = _key_tile(t)
    assert lp == t
    heads = aw // HEAD
    topk = min(TOPK_MAX, t // 4)
    qblk = lambda w: pl.BlockSpec((None, tq, w), lambda i, j: (i, j, 0))
    kblk = lambda w: pl.BlockSpec((None, t, w), lambda i, j: (i, 0, 0))
    return pl.pallas_call(
        functools.partial(_dsa_t_kernel, tq=tq, tk=tk, l=t, past=grp.past, topk=topk),
        grid=(b, t // tq),
        in_specs=[qblk(IDX_HEADS * HEAD), qblk(LANE), kblk(LANE), qblk(aw), kblk(aw),
                  pl.BlockSpec((aw, t), lambda i, j: (0, i))],
        out_specs=qblk(aw),
        out_shape=jax.ShapeDtypeStruct((b, t, aw), BF16),
        scratch_shapes=[pltpu.VMEM((t // tk, tk, tq), I32),
                        pltpu.VMEM((t // tk, tk, tq), I16),
                        pltpu.VMEM((t // tk, tk, tq), I16),
                        pltpu.VMEM((t // tk, tk, tq), F32),
                        pltpu.VMEM((1, tq), I32),
                        pltpu.VMEM((IDX_HEADS, LANE, tq), BF16),
                        pltpu.VMEM((heads, LANE, tq), BF16),
                        pltpu.VMEM((2, heads, tk, tq), F32),
                        pltpu.VMEM((heads, 1, tq), F32),
                        pltpu.VMEM((heads, HEAD + ONES_ROWS, tq), F32)],
        compiler_params=_params(("arbitrary", "arbitrary")),
        name="dsa_attention_t",
    )(qi, wi, ki, qa, k, vt)


def _mla_t_kernel(qn_ref, qr_ref, kn_ref, vt_ref, kr_ref, o_ref, qt_scr, s_scr, m_scr, acc_scr,
                  *, tq, tk, l, past):
    ch = acc_scr.shape[0]
    qpos0 = past + pl.program_id(1) * tq
    ntiles = _visible_tiles(qpos0, tq, l, tk)
    nfull = jnp.minimum(l, (qpos0 // CHUNK + 1) * CHUNK) // tk
    qchunk = (qpos0 + lax.broadcasted_iota(I32, (1, tq), 1)) // CHUNK
    krow = lax.broadcasted_iota(I32, (tk, 1), 0)
    c = (C_NOPE + HEAD) ** -0.5 * LOG2E
    for s in range(ch // 2):
        pair = _split_heads_t(qr_ref[:, s * LANE:(s + 1) * LANE].T)
        for half in range(2):
            h = 2 * s + half
            qt_scr[h] = jnp.concatenate([qn_ref[:, h * LANE:(h + 1) * LANE].T, pair[half]], axis=0)
    _flash_t_init(m_scr, acc_scr)

    def tile(kt, masked):
        k0 = pl.multiple_of(kt * tk, tk)
        k_rope = kr_ref[pl.ds(k0, tk), :]
        slot = kt % 2
        if masked:
            kpos = k0 + krow
            bias = jnp.where((kpos // CHUNK <= qchunk) & (kpos < l), 0.0, MASKED)
        s_max = []
        for h in range(ch):
            kcat = jnp.concatenate([kn_ref[pl.ds(k0, tk), h * LANE:(h + 1) * LANE], k_rope], axis=1)
            s = jnp.dot(kcat, qt_scr[h], preferred_element_type=F32)
            s_max.append(_flash_t_stage(slot, h, s + bias if masked else s, s_scr))
        for h in range(ch):
            _flash_t_step(slot, h, s_max[h], vt_ref[h * C_V:(h + 1) * C_V, pl.ds(k0, tk)], c,
                          s_scr, m_scr, acc_scr)
        return 0

    lax.fori_loop(0, nfull, lambda kt, _: tile(kt, False), 0)
    lax.fori_loop(nfull, ntiles, lambda kt, _: tile(kt, True), 0)
    _flash_t_out(o_ref, acc_scr)


def _mla_attention_t(qn, qr, kn, vt, kr, grp, tq):
    b, t, wn = qn.shape
    ch = wn // C_NOPE
    lp, tk = _key_tile(t)
    assert lp == t and ch % 2 == 0
    qblk = lambda w: pl.BlockSpec((None, tq, w), lambda i, j: (i, j, 0))
    kblk = lambda w: _resident((None, t, w), lambda i, j: (i, 0, 0))
    return pl.pallas_call(
        functools.partial(_mla_t_kernel, tq=tq, tk=tk, l=t, past=grp.past),
        grid=(b, t // tq),
        in_specs=[qblk(wn), qblk(ch * HEAD), kblk(wn), _resident((wn, t), lambda i, j: (0, i)), kblk(LANE)],
        out_specs=qblk(wn),
        out_shape=jax.ShapeDtypeStruct((b, t, wn), BF16),
        scratch_shapes=[pltpu.VMEM((ch, 2 * LANE, tq), BF16),
                        pltpu.VMEM((2, ch, tk, tq), F32),
                        pltpu.VMEM((ch, 1, tq), F32),
                        pltpu.VMEM((ch, C_V + ONES_ROWS, tq), F32)],
        compiler_params=_params(("arbitrary", "arbitrary")),
        name="mla_attention_t",
    )(qn, qr, kn, vt, kr)


def _causal_conv(u, e0, e1, w, seg):
    rmod = lax.broadcasted_iota(I32, (u.shape[0], 1), 0) % seg
    u1 = jnp.where(rmod == 0, e1, pltpu.roll(u, 1, 0))
    u2 = jnp.where(rmod == 0, e0, jnp.where(rmod == 1, e1, pltpu.roll(u, 2, 0)))
    return u2 * w[0:1] + u1 * w[1:2] + u * w[2:3]


def _layer_norm(z, g, b):
    mu = jnp.mean(z, axis=-1, keepdims=True)
    zc = z - mu
    var = jnp.mean(zc * zc, axis=-1, keepdims=True)
    return zc * lax.rsqrt(var + LN_EPS) * g + b


def _outproj_kernel(*refs, alpha, seq_tiles, seg, carried):
    if carried:
        (x_ref, oa_ref, bg_ref, u_ref, oc_ref, w_ref, cw_ref, g1_ref, lng_ref, lnb_ref, sc2_ref, sh2_ref,
         x1_ref, h2_ref, prev_scr) = refs

        @pl.when(pl.program_id(0) % seq_tiles == 0)
        def _():
            prev_scr[...] = jnp.zeros_like(prev_scr)

        e0, e1 = prev_scr[SUBLANE - 2:SUBLANE - 1, :], prev_scr[SUBLANE - 1:SUBLANE, :]
    else:
        (x_ref, oa_ref, bg_ref, u_ref, oc_ref, w_ref, cw_ref, g1_ref, lng_ref, lnb_ref, sc2_ref, sh2_ref,
         e0_ref, e1_ref, x1_ref, h2_ref) = refs
        e0, e1 = e0_ref[...], e1_ref[...]
    u = u_ref[...]
    yb = bg_ref[...] * _causal_conv(u, e0, e1, cw_ref[...], seg)
    if carried:
        prev_scr[...] = u[u.shape[0] - SUBLANE:, :]
    mixed = jnp.concatenate([oa_ref[...], yb.astype(BF16), oc_ref[...]], axis=1)
    tm = mixed.shape[0]
    n_split = 2 if tm % (2 * 2 * SUBLANE) == 0 else 1

    def rows_of(ref, rows):
        return ref[...] if ref.shape[0] == 1 else ref[rows, :]

    halves = [slice(c * tm // n_split, (c + 1) * tm // n_split) for c in range(n_split)]
    mixes = [jnp.dot(mixed[rows, :], w_ref[...], preferred_element_type=F32) for rows in halves]
    for rows, mix in zip(halves, mixes):
        x1 = _layer_norm(alpha * x_ref[rows, :] + (1.0 + rows_of(g1_ref, rows)) * mix, lng_ref[...], lnb_ref[...])
        x1_ref[rows, :] = x1
        h2_ref[rows, :] = (x1 * (1.0 + rows_of(sc2_ref, rows)) + rows_of(sh2_ref, rows)).astype(BF16)


def _out_projection(x, oa, bg, u, oc, w_out, layer, conv_w, g1, ln_g, ln_b, sc2, sh2, prev, grp, alpha, tm):
    m, d = x.shape
    aw, bw, cw = oa.shape[1], bg.shape[1], oc.shape[1]
    n_i = m // tm
    tiles_per_mod = n_i // g1.shape[0]
    mod_rows = g1.shape[1]
    row = lambda i: (i, 0)
    fix = lambda i: (0, 0)
    mod_spec = pl.BlockSpec((None, mod_rows, d), lambda i: (i // tiles_per_mod, 0, 0))
    in_specs = [pl.BlockSpec((tm, d), row), pl.BlockSpec((tm, aw), row), pl.BlockSpec((tm, bw), row),
                pl.BlockSpec((tm, bw), row), pl.BlockSpec((tm, cw), row),
                _resident((None, aw + bw + cw, d), lambda i: (layer, 0, 0)), pl.BlockSpec((CONV_W, bw), fix),
                mod_spec, pl.BlockSpec((1, d), fix), pl.BlockSpec((1, d), fix), mod_spec, mod_spec]
    args = [x, oa, bg, u, oc, w_out, conv_w, g1, ln_g.reshape(1, d), ln_b.reshape(1, d), sc2, sh2]
    scratch = []
    if prev is None:
        assert grp.t % tm == 0
        scratch = [pltpu.VMEM((SUBLANE, bw), F32)]
    else:
        assert tm % grp.t == 0
        in_specs += [pl.BlockSpec((tm, bw), row), pl.BlockSpec((tm, bw), row)]
        args += [prev[0], prev[1]]
    return pl.pallas_call(
        functools.partial(_outproj_kernel, alpha=alpha, seq_tiles=max(grp.t // tm, 1), seg=min(grp.t, tm),
                          carried=prev is None),
        grid=(n_i,),
        in_specs=in_specs,
        out_specs=[pl.BlockSpec((tm, d), row), pl.BlockSpec((tm, d), row)],
        out_shape=[jax.ShapeDtypeStruct((m, d), F32), jax.ShapeDtypeStruct((m, d), BF16)],
        scratch_shapes=scratch,
        compiler_params=_params(("arbitrary",)),
        name="out_projection",
    )(*args)


def _ffn_kernel(*refs, alpha, seq_tiles, seg, carried):
    if carried:
        (h_ref, x_ref, wg_ref, wu_ref, wd_ref, cw_ref, g2_ref, lng_ref, lnb_ref,
         o_ref, gt_ref, acc_scr, prev_scr) = refs
    else:
        (h_ref, x_ref, wg_ref, wu_ref, wd_ref, cw_ref, g2_ref, lng_ref, lnb_ref, e0_ref, e1_ref,
         o_ref, gt_ref, acc_scr) = refs
    f = pl.program_id(1)
    if carried:
        @pl.when(pl.program_id(0) % seq_tiles == 0)
        def _():
            prev_scr[f] = jnp.zeros(prev_scr.shape[1:], F32)

    @pl.when(f == 0)
    def _():
        acc_scr[...] = jnp.zeros_like(acc_scr)

    h = h_ref[...]
    tm, tf = h.shape[0], wg_ref.shape[1]
    n_split = 2 if tf % (2 * LANE) == 0 else 1
    halves = [slice(c * tf // n_split, (c + 1) * tf // n_split) for c in range(n_split)]
    gates = [jnp.dot(h, wg_ref[:, cols], preferred_element_type=F32) for cols in halves]
    ups = [jnp.dot(h, wu_ref[:, cols], preferred_element_type=F32) for cols in halves]
    for cols, gate, up in zip(halves, gates, ups):
        if carried:
            e0, e1 = prev_scr[f, SUBLANE - 2:SUBLANE - 1, cols], prev_scr[f, SUBLANE - 1:SUBLANE, cols]
        else:
            e0, e1 = e0_ref[:, cols], e1_ref[:, cols]
        conv = _causal_conv(gate, e0, e1, cw_ref[:, cols], seg)
        if carried:
            prev_scr[f, :, cols] = gate[tm - SUBLANE:, :]
        gt_ref[:, cols] = gate[tm - gt_ref.shape[0]:, :]
        act = (jax.nn.silu(conv) * up).astype(BF16)
        acc_scr[...] = jnp.dot(act, wd_ref[cols, :], preferred_element_type=F32) + acc_scr[...]

    @pl.when(f == pl.num_programs(1) - 1)
    def _():
        z = alpha * x_ref[...] + (1.0 + g2_ref[...]) * acc_scr[...]
        o_ref[...] = _layer_norm(z, lng_ref[...], lnb_ref[...])


def _channel_mixer(h2, x1, w_gu, w_down, layer, conv_w, g2, ln_g, ln_b, prev, grp, alpha, tm, tf):
    m, d = x1.shape
    dff = w_down.shape[1]
    n_i, n_f = m // tm, dff // tf
    tiles_per_mod = n_i // g2.shape[0]
    mod_rows = g2.shape[1]
    fix = lambda i, f: (0, 0)
    in_specs = [pl.BlockSpec((tm, d), lambda i, f: (i, 0)), pl.BlockSpec((tm, d), lambda i, f: (i, 0)),
                pl.BlockSpec((None, d, tf), lambda i, f: (layer, 0, f)),
                pl.BlockSpec((None, d, tf), lambda i, f: (layer, 0, n_f + f)),
                pl.BlockSpec((None, tf, d), lambda i, f: (layer, f, 0)),
                pl.BlockSpec((CONV_W, tf), lambda i, f: (0, f)),
                pl.BlockSpec((None, mod_rows, d), lambda i, f: (i // tiles_per_mod, 0, 0)),
                pl.BlockSpec((1, d), fix), pl.BlockSpec((1, d), fix)]
    args = [h2, x1, w_gu, w_gu, w_down, conv_w, g2, ln_g.reshape(1, d), ln_b.reshape(1, d)]
    scratch = [pltpu.VMEM((tm, d), F32)]
    if prev is None:
        assert grp.t % tm == 0
        scratch.append(pltpu.VMEM((n_f, SUBLANE, tf), F32))
        gt_spec = pl.BlockSpec((None, SUBLANE, tf), lambda i, f: (i, 0, f))
        gt_shape = jax.ShapeDtypeStruct((n_i, SUBLANE, dff), F32)
    else:
        assert tm % grp.t == 0
        in_specs += [pl.BlockSpec((tm, tf), lambda i, f: (i, f))] * 2
        args += [prev[0], prev[1]]
        gt_spec = pl.BlockSpec((tm, tf), lambda i, f: (i, f))
        gt_shape = jax.ShapeDtypeStruct((m, dff), F32)
    return pl.pallas_call(
        functools.partial(_ffn_kernel, alpha=alpha, seq_tiles=max(grp.t // tm, 1), seg=min(grp.t, tm),
                          carried=prev is None),
        grid=(n_i, n_f),
        in_specs=in_specs,
        out_specs=[pl.BlockSpec((tm, d), lambda i, f: (i, 0)), gt_spec],
        out_shape=[jax.ShapeDtypeStruct((m, d), F32), gt_shape],
        scratch_shapes=scratch,
        compiler_params=_params(("arbitrary", "arbitrary")),
        name="channel_mixer",
    )(*args)


def _rope_tables(pos):
    half = HEAD // 2
    inv = jnp.power(jnp.float32(ROPE_THETA), -jnp.arange(half, dtype=F32) / half)
    ang = pos.astype(F32)[:, None] * inv[None, :]
    cos, sin = jnp.cos(ang), jnp.sin(ang)
    zero = jnp.zeros_like(sin)
    reps = LANE // HEAD
    return (jnp.tile(jnp.concatenate([cos, cos], axis=1), (1, reps)),
            jnp.tile(jnp.concatenate([zero, sin], axis=1), (1, reps)),
            jnp.tile(jnp.concatenate([-sin, zero], axis=1), (1, reps)))


def _with_past(past, new, lp, twice=False):
    b, t, w = new.shape
    parts = [new]
    if past is not None:
        p = past.reshape(b, past.shape[1], -1).astype(BF16)
        parts = [jnp.concatenate([p, p], axis=-1) if twice else p, new]
    n = sum(a.shape[1] for a in parts)
    if lp > n:
        parts.append(jnp.zeros((b, lp - n, w), BF16))
    return parts[0] if len(parts) == 1 else jnp.concatenate(parts, axis=1)


def _layer(x, mod, tables, grp, past, layer, big, small, dm, alpha):
    w_in_p, w_out, w_uk, w_uv, w_uv_t, w_gu, w_down = big
    conv_b_w, kv_norm, ln1_g, ln1_b, ln2_g, ln2_b, conv_f_w = small
    b, t = grp.b, grp.t
    m, d = x.shape
    carried = past is None
    tm = _row_tile(t, 256) if carried else m
    if carried:
        mods = [a.reshape(b, 1, d) for a in jnp.split(mod, N_MOD, axis=-1)]
    else:
        mods = [jnp.repeat(a, t, axis=0).reshape(1, m, d) for a in jnp.split(mod, N_MOD, axis=-1)]
    sh1, sc1, g1, sh2, sc2, g2 = mods

    (qa, ka, kab, qi, qcr, ki, kib, kr, krb, va, vab, bg, u, qcn, lat, latb, wi, vat) = _in_projection(
        x, sc1, sh1, w_in_p, layer, tables, kv_norm, dm, tm)

    three = lambda a: a.reshape(b, t, a.shape[-1])
    if carried:
        tq = _row_tile(t, 256)
        kn, vct = _kv_up(latb, w_uk, w_uv_t, layer, v_transposed=True)
        oa = _dsa_attention_t(three(qi), three(wi), three(kib), three(qa), three(kab), vat, grp, tq)
        oc = _mla_attention_t(three(qcn), three(qcr), three(kn), vct, three(krb), grp, tq)
        prev_b = prev_f = None
    else:
        l = grp.past + t
        lp, _ = _key_tile(l)
        p_ak, p_av, p_ik, p_lat, p_kr, prev_b, prev_f = past
        k_all = _with_past(p_ak, three(kab), lp)
        v_all = _with_past(p_av, three(vab), lp)
        ki_all = _with_past(p_ik, three(kib), lp, twice=True)
        kr_all = _with_past(p_kr, three(krb), lp, twice=True)
        lat_all = _with_past(p_lat, three(latb), lp)
        oa = _dsa_attention(three(qi), three(wi), ki_all, three(qa), k_all, v_all, grp, l, t)
        oc = _mla_attention_latent(three(qcn), three(qcr), lat_all, kr_all, w_uk, w_uv, layer, grp, l)

    def expand(state):
        return jnp.repeat(state[:, 0], t, axis=0), jnp.repeat(state[:, 1], t, axis=0)

    x1, h2 = _out_projection(x, oa.reshape(m, -1), bg, u, oc.reshape(m, -1), w_out, layer, conv_b_w, g1,
                             ln1_g, ln1_b, sc2, sh2, None if carried else expand(prev_b), grp, alpha, tm)
    tm_f = _row_tile(t, 512) if carried else m
    tf = _row_tile(dm.dff, 512)
    x2, gate_rows = _channel_mixer(h2, x1, w_gu, w_down, layer, conv_f_w, g2, ln2_g, ln2_b,
                                   None if carried else expand(prev_f), grp, alpha, tm_f, tf)
    if carried:
        new_f = gate_rows.reshape(b, t // tm_f, SUBLANE, dm.dff)[:, -1, SUBLANE - (CONV_W - 1):, :]
    else:
        new_f = gate_rows.reshape(b, t, dm.dff)[:, t - (CONV_W - 1):, :]
    new_b = u.reshape(b, t, dm.bw)[:, t - (CONV_W - 1):, :]
    heads = dm.aw // HEAD
    rows = (ka.reshape(b, t, heads, HEAD), va.reshape(b, t, heads, HEAD), ki.reshape(b, t, HEAD),
            lat.reshape(b, t, dm.r), kr.reshape(b, t, HEAD), new_b, new_f)
    return x2, rows


def kernel(x_prompt, x_sample, c_prompt, c_sample, cache_a_k, cache_a_v, cache_idx_k, cache_mla_latent,
           cache_mla_krope, state_conv_b, state_conv_ffn, w_in, w_out, conv_b_w, mla_kv_norm, mla_w_uk,
           mla_w_uv, w_mod, b_mod, ln1_g, ln1_b, ln2_g, ln2_b, ffn_w_gu, ffn_conv_w, ffn_w_down):
    depth, d, _ = w_in.shape
    a_heads = cache_a_k.shape[3]
    dm = Dims(d=d, aw=a_heads * HEAD, bw=conv_b_w.shape[2], ch=mla_w_uk.shape[2] // C_NOPE,
              r=mla_w_uk.shape[1], dff=ffn_w_down.shape[1], depth=depth)
    alpha = (2 * depth) ** 0.25
    grp_p = Group(b=x_prompt.shape[0], t=x_prompt.shape[1], past=0)
    grp_s = Group(b=x_sample.shape[0], t=x_sample.shape[1], past=cache_a_k.shape[2])

    n_c = grp_p.b + grp_s.b
    c_all = jnp.concatenate([c_prompt, c_sample, jnp.zeros((-n_c % SUBLANE, d), F32)], axis=0)
    mod = _modulation(c_all, w_mod, b_mod)

    tab_p = _rope_tables(jnp.arange(grp_p.t, dtype=I32))
    tab_s = tuple(jnp.tile(a, (grp_s.b, 1)) for a in _rope_tables(grp_s.past + jnp.arange(grp_s.t, dtype=I32)))

    xp = x_prompt.reshape(grp_p.b * grp_p.t, d)
    xs = x_sample.reshape(grp_s.b * grp_s.t, d)
    rows_p, rows_s = [], []
    w_uv_b = mla_w_uv.astype(BF16)
    big = (_pack_w_in(w_in, dm), w_out.astype(BF16), mla_w_uk.astype(BF16), w_uv_b, jnp.swapaxes(w_uv_b, 1, 2),
           ffn_w_gu.astype(BF16), ffn_w_down.astype(BF16))
    for l in range(depth):
        small = (conv_b_w[l], mla_kv_norm[l], ln1_g[l], ln1_b[l], ln2_g[l], ln2_b[l], ffn_conv_w[l])
        xp, rp = _layer(xp, mod[l, :grp_p.b], tab_p, grp_p, None, l, big, small, dm, alpha)
        past_l = (cache_a_k[l], cache_a_v[l], cache_idx_k[l], cache_mla_latent[l], cache_mla_krope[l],
                  state_conv_b[l], state_conv_ffn[l])
        xs, rs = _layer(xs, mod[l, grp_p.b:n_c], tab_s, grp_s, past_l, l, big, small, dm, alpha)
        rows_p.append(rp)
        rows_s.append(rs)
    outs_p = [jnp.stack(r) for r in zip(*rows_p)]
    outs_s = [jnp.stack(r) for r in zip(*rows_s)]
    return (xp.reshape(x_prompt.shape), xs.reshape(x_sample.shape), *outs_p, *outs_s)
```

```python
import functools
from typing import NamedTuple

import numpy as np
import jax
import jax.numpy as jnp
from jax import lax
from jax.experimental import pallas as pl
from jax.experimental.pallas import tpu as pltpu

F32, BF16, I32, I16 = jnp.float32, jnp.bfloat16, jnp.int32, jnp.int16

CHUNK = 64
CONV_W = 3
ROPE_THETA = 10000.0
HEAD = 64
IDX_HEADS = 16
TOPK_MAX = 256
C_NOPE = 128
C_V = 128
N_MOD = 6
LN_EPS = 1e-5
RMS_EPS = 1e-6

LANE = 128
SUBLANE = 8
VMEM_LIMIT = 50 * 1024 * 1024

MASKED = -1e30
LOG2E = 1.4426950408889634
INT_MIN = -2 ** 31
I16_MIN = -2 ** 15
KEY_NEG_INF = int(np.array(-np.inf, np.float32).view(np.int32)) ^ 0x7FFFFFFF


class Dims(NamedTuple):
    d: int
    aw: int
    bw: int
    ch: int
    r: int
    dff: int
    depth: int


class Group(NamedTuple):
    b: int
    t: int
    past: int


def _row_tile(m, pref):
    if m <= pref:
        return m
    t = pref - pref % SUBLANE
    while m % t:
        t -= SUBLANE
    return t


def _key_tile(l):
    lp = -(-l // LANE) * LANE
    for tk in (512, 384, 256, 128):
        if lp % tk == 0:
            return lp, tk
    raise AssertionError(lp)


def _params(sem):
    return pltpu.CompilerParams(dimension_semantics=sem, vmem_limit_bytes=VMEM_LIMIT)


def _resident(shape, index_map):
    return pl.BlockSpec(shape, index_map, pipeline_mode=pl.Buffered(1))


def _mod_kernel(c_ref, w_ref, b_ref, o_ref):
    a = jax.nn.silu(c_ref[...]).astype(BF16)
    o_ref[...] = jnp.dot(a, w_ref[...].astype(BF16), preferred_element_type=F32) + b_ref[...]


def _modulation(c, w_mod, b_mod):
    depth, d, n = w_mod.shape
    rows = c.shape[0]
    tn = _row_tile(n, 1024)
    return pl.pallas_call(
        _mod_kernel,
        grid=(depth, n // tn),
        in_specs=[pl.BlockSpec((rows, d), lambda l, j: (0, 0)),
                  pl.BlockSpec((None, d, tn), lambda l, j: (l, 0, j)),
                  pl.BlockSpec((None, 1, tn), lambda l, j: (l, 0, j))],
        out_specs=pl.BlockSpec((None, rows, tn), lambda l, j: (l, 0, j)),
        out_shape=jax.ShapeDtypeStruct((depth, rows, n), F32),
        compiler_params=_params(("arbitrary", "arbitrary")),
        name="modulation",
    )(c, w_mod, b_mod.reshape(depth, 1, n))


def _pack_w_in(w, dm):
    d, aw, bw, ch, r = dm.d, dm.aw, dm.bw, dm.ch, dm.r
    lead = w.shape[:-1]
    o = np.cumsum([0, aw, aw, aw, IDX_HEADS * HEAD, HEAD, IDX_HEADS, bw, bw, bw, ch * (C_NOPE + HEAD), r, HEAD])
    qa, ka, va, qi, ki, wi, bg, cg, xb, qc, lat, kr = [w[..., o[i]:o[i + 1]] for i in range(12)]
    qc = qc.reshape(lead + (ch, C_NOPE + HEAD))
    qcn = qc[..., :C_NOPE].reshape(lead + (ch * C_NOPE,))
    qcr = qc[..., C_NOPE:].reshape(lead + (ch * HEAD,))
    pad = jnp.zeros(lead + (LANE - IDX_HEADS,), w.dtype)
    return jnp.concatenate([qa, ka, qi, qcr, ki, ki, kr, kr, va, bg, cg, xb, qcn, lat, wi, pad],
                           axis=-1).astype(BF16)


def _rope(acc, cos, s1, s2):
    outs = []
    for s in range(acc.shape[1] // LANE):
        xs = acc[:, s * LANE:(s + 1) * LANE]
        outs.append(xs * cos + pltpu.roll(xs, HEAD // 2, 1) * s1 + pltpu.roll(xs, LANE - HEAD // 2, 1) * s2)
    return outs[0] if len(outs) == 1 else jnp.concatenate(outs, axis=1)


def _inproj_kernel(x_ref, sc_ref, sh_ref, w_ref, cos_ref, s1_ref, s2_ref, nrm_ref, *rest, dm, n_alias):
    (qa_ref, ka_ref, kab_ref, qi_ref, qcr_ref, ki_ref, kib_ref, kr_ref, krb_ref,
     va_ref, vab_ref, bg_ref, u_ref, qcn_ref, lat_ref, latb_ref, wi_ref, vat_ref) = rest[n_alias:]
    aw, bw, ch, r = dm.aw, dm.bw, dm.ch, dm.r
    h = (x_ref[...] * (1.0 + sc_ref[...]) + sh_ref[...]).astype(BF16)
    cos, s1, s2 = cos_ref[...], s1_ref[...], s2_ref[...]
    col = [0]

    def proj(width):
        c0 = col[0]
        col[0] = c0 + width
        return jnp.dot(h, w_ref[:, c0:c0 + width], preferred_element_type=F32)

    def pieces(width, step=512):
        return [(o, min(step, width - o)) for o in range(0, width, step)]

    for o, wd in pieces(aw):
        qa_ref[:, o:o + wd] = _rope(proj(wd), cos, s1, s2).astype(BF16)
    for o, wd in pieces(aw):
        y = _rope(proj(wd), cos, s1, s2)
        ka_ref[:, o:o + wd] = y
        kab_ref[:, o:o + wd] = y.astype(BF16)
    for o, wd in pieces(IDX_HEADS * HEAD):
        qi_ref[:, o:o + wd] = _rope(proj(wd), cos, s1, s2).astype(BF16)
    for o, wd in pieces(ch * HEAD):
        qcr_ref[:, o:o + wd] = _rope(proj(wd), cos, s1, s2).astype(BF16)
    y = _rope(proj(2 * LANE), cos, s1, s2)
    for c, (f32_ref, b16_ref) in enumerate(((ki_ref, kib_ref), (kr_ref, krb_ref))):
        f32_ref[...] = y[:, c * LANE:c * LANE + HEAD]
        b16_ref[...] = y[:, c * LANE:(c + 1) * LANE].astype(BF16)
    for o, wd in pieces(aw):
        y = proj(wd)
        va_ref[:, o:o + wd] = y
        vab_ref[:, o:o + wd] = y.astype(BF16)
        vat_ref[o:o + wd, :] = y.T.astype(BF16)
    for o, wd in pieces(bw):
        bg_ref[:, o:o + wd] = proj(wd)
    c_cg = col[0]
    for o, wd in pieces(bw):
        cg = jnp.dot(h, w_ref[:, c_cg + o:c_cg + o + wd], preferred_element_type=F32)
        xb = jnp.dot(h, w_ref[:, c_cg + bw + o:c_cg + bw + o + wd], preferred_element_type=F32)
        u_ref[:, o:o + wd] = cg * xb
    col[0] = c_cg + 2 * bw
    for o, wd in pieces(ch * C_NOPE):
        qcn_ref[:, o:o + wd] = proj(wd).astype(BF16)
    lat = proj(r)
    lat = lat * lax.rsqrt(jnp.mean(lat * lat, axis=-1, keepdims=True) + RMS_EPS) * nrm_ref[...]
    lat_ref[...] = lat
    latb_ref[...] = lat.astype(BF16)
    wi_ref[...] = proj(LANE) * (IDX_HEADS ** -0.5)


STATE_OUTPUTS = (1, 5, 7, 9, 14)


def _in_projection(x, sc, sh, w_packed, layer, tables, kv_norm, stacks, dm, tm):
    m, d = x.shape
    aw, bw, ch, r = dm.aw, dm.bw, dm.ch, dm.r
    npk = w_packed.shape[2]
    n_i = m // tm
    tiles_per_mod = n_i // sc.shape[0]
    mod_rows = sc.shape[1]
    tab_tiles = tables[0].shape[0] // tm
    widths = [(aw, BF16), (aw, F32), (aw, BF16), (IDX_HEADS * HEAD, BF16), (ch * HEAD, BF16),
              (HEAD, F32), (LANE, BF16), (HEAD, F32), (LANE, BF16),
              (aw, F32), (aw, BF16), (bw, F32), (bw, F32), (ch * C_NOPE, BF16), (r, F32), (r, BF16),
              (LANE, F32)]
    row = lambda i: (i, 0)
    mod_spec = pl.BlockSpec((None, mod_rows, d), lambda i: (i // tiles_per_mod, 0, 0))
    tab_spec = pl.BlockSpec((tm, LANE), lambda i: (i % tab_tiles, 0))
    in_specs = [pl.BlockSpec((tm, d), row), mod_spec, mod_spec,
                _resident((None, d, npk), lambda i: (layer, 0, 0)),
                tab_spec, tab_spec, tab_spec,
                pl.BlockSpec((1, r), lambda i: (0, 0))]
    args = [x, sc, sh, w_packed, *tables, kv_norm.reshape(1, r)]
    out_specs = [pl.BlockSpec((tm, w), row) for w, _ in widths] + [pl.BlockSpec((aw, tm), lambda i: (0, i))]
    out_shape = [jax.ShapeDtypeStruct((m, w), dt) for w, dt in widths] + [jax.ShapeDtypeStruct((aw, m), BF16)]
    for o in STATE_OUTPUTS:
        w, dt = widths[o]
        out_specs[o] = pl.BlockSpec((None, tm, w), lambda i: (layer, i, 0))
        out_shape[o] = jax.ShapeDtypeStruct((dm.depth, m, w), dt)
    aliases = {}
    if stacks is not None:
        aliases = {len(args) + k: o for k, o in enumerate(STATE_OUTPUTS)}
        in_specs += [pl.BlockSpec(memory_space=pl.ANY)] * len(stacks)
        args += list(stacks)
    return pl.pallas_call(
        functools.partial(_inproj_kernel, dm=dm, n_alias=len(aliases)),
        grid=(n_i,),
        in_specs=in_specs,
        out_specs=out_specs,
        out_shape=out_shape,
        input_output_aliases=aliases,
        compiler_params=_params(("arbitrary",)),
        name="in_projection",
    )(*args)


def _kvup_kernel(l_ref, wk_ref, wvt_ref, k_ref, vt_ref):
    lat = l_ref[...]
    k_ref[...] = jnp.dot(lat, wk_ref[...], preferred_element_type=F32).astype(BF16)
    vt_ref[...] = lax.dot_general(wvt_ref[...], lat, _NT, preferred_element_type=F32).astype(BF16)


def _kv_up(lat, w_uk, w_uv_t, layer):
    m, r = lat.shape
    n = w_uk.shape[2]
    tm = _row_tile(m, 512)
    return pl.pallas_call(
        _kvup_kernel,
        grid=(m // tm,),
        in_specs=[pl.BlockSpec((tm, r), lambda i: (i, 0)),
                  pl.BlockSpec((None, r, n), lambda i: (layer, 0, 0)),
                  pl.BlockSpec((None, n, r), lambda i: (layer, 0, 0))],
        out_specs=[pl.BlockSpec((tm, n), lambda i: (i, 0)), pl.BlockSpec((n, tm), lambda i: (0, i))],
        out_shape=[jax.ShapeDtypeStruct((m, n), BF16), jax.ShapeDtypeStruct((n, m), BF16)],
        compiler_params=_params(("arbitrary",)),
        name="latent_up_projection",
    )(lat, w_uk, w_uv_t)


_NT = (((1,), (1,)), ((), ()))


def _visible_tiles(qpos0, tq, l, tk):
    last_visible = ((qpos0 + tq - 1) // CHUNK + 1) * CHUNK
    nvis = min(l, last_visible) if isinstance(qpos0, int) else jnp.minimum(l, last_visible)
    return (nvis + tk - 1) // tk


def _for(n, body, init):
    if isinstance(n, int):
        for i in range(n):
            init = body(i, init)
        return init
    return lax.fori_loop(0, n, body, init)


def _tile_start(kt, tk):
    return kt * tk if isinstance(kt, int) else pl.multiple_of(kt * tk, tk)


def _flash_step(carry, s, v):
    m, l, acc = carry
    m_new = jnp.maximum(m, jnp.max(s, axis=1, keepdims=True))
    alpha = jnp.exp(m - m_new)
    p = jnp.exp(s - m_new)
    l = alpha * l + jnp.sum(p, axis=1, keepdims=True)
    acc = alpha * acc + jnp.dot(p.astype(BF16), v, preferred_element_type=F32)
    return m_new, l, acc


def _flash_init(rows, width):
    return (jnp.full((rows, 1), MASKED, F32), jnp.zeros((rows, 1), F32), jnp.zeros((rows, width), F32))


def _ordered_key(x):
    b = pltpu.bitcast(x, I32)
    return jnp.where(b < 0, b ^ 0x7FFFFFFF, b)


def _dsa_kernel(qi_ref, wi_ref, ki_ref, qa_ref, k_ref, v_ref, o_ref, key_scr, bias_scr, thr2_scr,
                *, tq, tk, l, lp, past, topk, aw, one_block):
    qpos0 = past if one_block else past + pl.program_id(1) * tq
    ntiles = _visible_tiles(qpos0, tq, l, tk)
    qchunk = (qpos0 + lax.broadcasted_iota(I32, (tq, 1), 0)) // CHUNK
    lane = lax.broadcasted_iota(I32, (1, LANE), 1)
    lo_half = lane < HEAD
    kf = jnp.float32(topk)

    def split_heads(qs):
        zero = jnp.zeros_like(qs)
        return jnp.concatenate([jnp.where(lo_half, qs, zero), jnp.where(lo_half, zero, qs)], axis=0)

    wi = wi_ref[...] * (HEAD ** -0.5)
    q_idx = jnp.concatenate([split_heads(qi_ref[:, s * LANE:(s + 1) * LANE]) for s in range(IDX_HEADS // 2)],
                            axis=0)
    w_idx = [wi[:, h:h + 1] for h in range(IDX_HEADS)]

    def score_body(kt, _):
        k0 = _tile_start(kt, tk)
        s_all = lax.dot_general(q_idx, ki_ref[pl.ds(k0, tk), :], _NT, preferred_element_type=F32)
        score = jnp.zeros((tq, tk), F32)
        for h in range(IDX_HEADS):
            score = score + jnp.maximum(s_all[h * tq:(h + 1) * tq], 0.0) * w_idx[h]
        kpos = k0 + lax.broadcasted_iota(I32, (1, tk), 1)
        vis = (kpos // CHUNK <= qchunk) & (kpos < l)
        key_scr[kt] = _ordered_key(jnp.where(vis, score, -jnp.inf))
        return 0

    _for(ntiles, score_body, 0)

    def count(pred):
        def body(kt, acc):
            for j in range(tk // LANE):
                ks = key_scr[kt, :, j * LANE:(j + 1) * LANE]
                acc = acc + jnp.where(pred(ks, kt * tk + j * LANE), 1.0, 0.0)
            return acc
        acc = _for(ntiles, body, jnp.zeros((tq, LANE), F32))
        return jnp.sum(acc, axis=1, keepdims=True)

    def wide(col):
        return jnp.broadcast_to(col, (tq, LANE))

    def bit_body(b, lo):
        cand = lo + jnp.left_shift(jnp.int32(1), 31 - b)
        cand_w = wide(cand)
        c = count(lambda ks, _: ks >= cand_w)
        return jnp.where(c >= kf, cand, lo)

    thr = lax.fori_loop(0, 32, bit_body, jnp.full((tq, 1), INT_MIN, I32))
    thr_w = wide(thr)

    n_ge = count(lambda ks, _: ks >= thr_w)
    n_gt = count(lambda ks, _: ks > thr_w)
    excess = (n_ge > kf) & (thr > KEY_NEG_INF)
    thr2_scr[...] = jnp.zeros((tq, 1), I32)

    @pl.when(jnp.max(jnp.where(excess, 1.0, 0.0)) > 0.0)
    def _():
        need = kf - n_gt
        nbits = lp.bit_length()

        def bit2_body(b, lo):
            cand = lo + jnp.left_shift(jnp.int32(1), nbits - 1 - b)
            cand_w = wide(cand)
            c = count(lambda ks, base: jnp.where(ks == thr_w, lp - (base + lane), 0) >= cand_w)
            return jnp.where(c >= need, cand, lo)

        thr2_scr[...] = lax.fori_loop(0, nbits, bit2_body, jnp.zeros((tq, 1), I32))

    thr2_w = wide(thr2_scr[...])

    def bias_body(kt, _):
        for j in range(tk // LANE):
            ks = key_scr[kt, :, j * LANE:(j + 1) * LANE]
            tie = jnp.where(lp - (kt * tk + j * LANE + lane) >= thr2_w, 0.0, MASKED)
            bias = jnp.where(ks > thr_w, 0.0, jnp.where(ks == thr_w, tie, MASKED))
            bias_scr[kt, :, j * LANE:(j + 1) * LANE] = jnp.where(ks > KEY_NEG_INF, bias, MASKED)
        return 0

    _for(ntiles, bias_body, 0)

    for pr in range(aw // LANE):
        cols = slice(pr * LANE, (pr + 1) * LANE)
        q2 = split_heads(qa_ref[:, cols])

        def att_body(kt, carry, cols=cols, q2=q2):
            k0 = _tile_start(kt, tk)
            s = lax.dot_general(q2, k_ref[pl.ds(k0, tk), cols], _NT, preferred_element_type=F32)
            bias = bias_scr[kt]
            s = s * (HEAD ** -0.5) + jnp.concatenate([bias, bias], axis=0)
            return _flash_step(carry, s, v_ref[pl.ds(k0, tk), cols])

        _, den, acc = _for(ntiles, att_body, _flash_init(2 * tq, LANE))
        o2 = acc / den
        o_ref[:, cols] = jnp.where(lo_half, o2[:tq], o2[tq:]).astype(BF16)


def _dsa_attention(qi, wi, ki, qa, k, v, grp, l, tq):
    b, t, aw = qa.shape
    lp = k.shape[1]
    _, tk = _key_tile(l)
    topk = min(TOPK_MAX, l // 4)
    qblk = lambda w: pl.BlockSpec((None, tq, w), lambda i, j: (i, j, 0))
    kblk = lambda w: pl.BlockSpec((None, lp, w), lambda i, j: (i, 0, 0))
    return pl.pallas_call(
        functools.partial(_dsa_kernel, tq=tq, tk=tk, l=l, lp=lp, past=grp.past, topk=topk, aw=aw,
                          one_block=t == tq),
        grid=(b, t // tq),
        in_specs=[qblk(IDX_HEADS * HEAD), qblk(LANE), kblk(LANE), qblk(aw), kblk(aw), kblk(aw)],
        out_specs=qblk(aw),
        out_shape=jax.ShapeDtypeStruct((b, t, aw), BF16),
        scratch_shapes=[pltpu.VMEM((lp // tk, tq, tk), I32),
                        pltpu.VMEM((lp // tk, tq, tk), F32),
                        pltpu.VMEM((tq, 1), I32)],
        compiler_params=_params(("arbitrary", "arbitrary")),
        name="dsa_attention",
    )(qi, wi, ki, qa, k, v)


def _mla_latent_kernel(qn_ref, qr_ref, lat_ref, kr_ref, wuk_ref, wuv_ref, o_ref, *, t, l, past, ch):
    lo_half = lax.broadcasted_iota(I32, (1, LANE), 1) < HEAD
    scale = (C_NOPE + HEAD) ** -0.5
    q_lat, q_rope = [], []
    for h in range(ch):
        cols = slice(h * C_NOPE, (h + 1) * C_NOPE)
        q_lat.append(lax.dot_general(qn_ref[:, cols], wuk_ref[:, cols], _NT, preferred_element_type=F32))
        qs = qr_ref[:, (h // 2) * LANE:(h // 2 + 1) * LANE]
        zero = jnp.zeros_like(qs)
        q_rope.append(jnp.where(lo_half, qs, zero) if h % 2 == 0 else jnp.where(lo_half, zero, qs))
    q_lat = jnp.concatenate(q_lat, axis=0).astype(BF16)
    q_rope = jnp.concatenate(q_rope, axis=0)
    lat = lat_ref[...]
    s = (lax.dot_general(q_lat, lat, _NT, preferred_element_type=F32)
         + lax.dot_general(q_rope, kr_ref[...], _NT, preferred_element_type=F32)) * scale
    qchunk = (past + lax.broadcasted_iota(I32, (t, 1), 0)) // CHUNK
    kpos = lax.broadcasted_iota(I32, (1, lat.shape[0]), 1)
    vis = (kpos // CHUNK <= jnp.concatenate([qchunk] * ch, axis=0)) & (kpos < l)
    s = jnp.where(vis, s, MASKED)
    p = jnp.exp(s - jnp.max(s, axis=1, keepdims=True))
    den = jnp.sum(p, axis=1, keepdims=True)
    o_lat = (jnp.dot(p.astype(BF16), lat, preferred_element_type=F32) / den).astype(BF16)
    for h in range(ch):
        cols = slice(h * C_V, (h + 1) * C_V)
        o_ref[:, cols] = jnp.dot(o_lat[h * t:(h + 1) * t, :], wuv_ref[:, cols],
                                 preferred_element_type=F32).astype(BF16)


def _mla_attention_latent(qn, qr, lat, kr, w_uk, w_uv, layer, grp, l):
    b, t, wn = qn.shape
    ch = wn // C_NOPE
    lp, r = lat.shape[1:]
    qblk = lambda w: pl.BlockSpec((None, t, w), lambda i: (i, 0, 0))
    kblk = lambda w: pl.BlockSpec((None, lp, w), lambda i: (i, 0, 0))
    wblk = pl.BlockSpec((None, r, wn), lambda i: (layer, 0, 0))
    return pl.pallas_call(
        functools.partial(_mla_latent_kernel, t=t, l=l, past=grp.past, ch=ch),
        grid=(b,),
        in_specs=[qblk(wn), qblk(ch * HEAD), kblk(r), kblk(LANE), wblk, wblk],
        out_specs=qblk(wn),
        out_shape=jax.ShapeDtypeStruct((b, t, wn), BF16),
        compiler_params=_params(("arbitrary",)),
        name="mla_attention_latent",
    )(qn, qr, lat, kr, w_uk, w_uv)


ONES_ROWS = 16


def _flash_t_init(m_scr, acc_scr):
    m_scr[...] = jnp.full(m_scr.shape, MASKED, F32)
    acc_scr[...] = jnp.zeros(acc_scr.shape, F32)


def _flash_t_stage(slot, h, s, s_scr):
    s_scr[slot, h] = s
    return jnp.max(s, axis=0, keepdims=True)


def _flash_t_step(slot, h, s_max, vt, c, s_scr, m_scr, acc_scr):
    m_old = m_scr[h]
    m_new = jnp.maximum(m_old, s_max)
    alpha = jnp.exp2((m_old - m_new) * c)
    p = jnp.exp2((s_scr[slot, h] - m_new) * c).astype(BF16)
    vt_ones = jnp.concatenate([vt, jnp.ones((ONES_ROWS, vt.shape[1]), BF16)], axis=0)
    acc_scr[h] = alpha * acc_scr[h] + jnp.dot(vt_ones, p, preferred_element_type=F32)
    m_scr[h] = m_new


def _flash_t_out(o_ref, acc_scr):
    heads, width = acc_scr.shape[0], acc_scr.shape[1] - ONES_ROWS
    ot = jnp.concatenate([acc_scr[h, :width, :] / acc_scr[h, width:width + 1, :] for h in range(heads)], axis=0)
    o_ref[...] = ot.T.astype(BF16)


def _split_heads_t(slab_t):
    row_lo = lax.broadcasted_iota(I32, (LANE, 1), 0) < HEAD
    zero = jnp.zeros_like(slab_t)
    return jnp.where(row_lo, slab_t, zero), jnp.where(row_lo, zero, slab_t)


def _dsa_t_kernel(qi_ref, wi_ref, ki_ref, qa_ref, k_ref, vt_ref, o_ref,
                  key_scr, hi_scr, lo_scr, bias_scr, thr2_scr, qit_scr, qat_scr, s_scr, m_scr, acc_scr,
                  *, tq, tk, l, past, topk):
    heads = acc_scr.shape[0]
    qpos0 = past + pl.program_id(1) * tq
    ntiles = _visible_tiles(qpos0, tq, l, tk)
    qchunk = (qpos0 + lax.broadcasted_iota(I32, (1, tq), 1)) // CHUNK
    krow = lax.broadcasted_iota(I32, (tk, 1), 0)

    for s in range(IDX_HEADS // 2):
        qit_scr[2 * s], qit_scr[2 * s + 1] = _split_heads_t(qi_ref[:, s * LANE:(s + 1) * LANE].T)
    for s in range(heads // 2):
        slab_t = (qa_ref[:, s * LANE:(s + 1) * LANE].astype(F32) * (HEAD ** -0.5)).astype(BF16).T
        qat_scr[2 * s], qat_scr[2 * s + 1] = _split_heads_t(slab_t)
    w_t = wi_ref[...].T * (HEAD ** -0.5)
    w_rows = [w_t[h:h + 1, :] for h in range(IDX_HEADS)]

    def score_body(kt, _):
        k0 = pl.multiple_of(kt * tk, tk)
        ki_tile = ki_ref[pl.ds(k0, tk), :]
        score = jnp.zeros((tk, tq), F32)
        for h in range(IDX_HEADS):
            s = jnp.dot(ki_tile, qit_scr[h], preferred_element_type=F32)
            score = score + jnp.maximum(s, 0.0) * w_rows[h]
        kpos = k0 + krow
        vis = (kpos // CHUNK <= qchunk) & (kpos < l)
        key = _ordered_key(jnp.where(vis, score, -jnp.inf))
        key_scr[kt] = key
        hi_scr[kt] = (key >> 16).astype(I16)
        lo_scr[kt] = ((key & 0xFFFF) + I16_MIN).astype(I16)
        return 0

    lax.fori_loop(0, ntiles, score_body, 0)

    def count(pred):
        def body(kt, acc):
            hit = jnp.where(pred(key_scr[kt], kt * tk), 1.0, 0.0)
            return acc + hit.reshape(tk // SUBLANE, SUBLANE, tq).sum(axis=0)
        acc = lax.fori_loop(0, ntiles, body, jnp.zeros((SUBLANE, tq), F32))
        return jnp.sum(acc, axis=0, keepdims=True)

    def count16(half_scr, cand, strict):
        rows = 2 * SUBLANE
        cand16 = cand.astype(I16)

        def body(kt, acc):
            half = half_scr[kt]
            hit = jnp.where(half > cand16 if strict else half >= cand16, jnp.int16(1), jnp.int16(0))
            parts = hit.reshape(tk // (4 * rows), 4, rows, tq)
            for g in range(parts.shape[0]):
                acc = acc + parts[g]
            return acc

        acc = lax.fori_loop(0, ntiles, body, jnp.zeros((4, rows, tq), I16))
        return acc.astype(I32).sum(axis=0).sum(axis=0, keepdims=True)

    def kth_largest16(half_scr, k_need):
        def bit_body(b, lo):
            cand = lo + jnp.left_shift(jnp.int32(1), 15 - b)
            return jnp.where(count16(half_scr, cand, False) >= k_need, cand, lo)
        return lax.fori_loop(0, 16, bit_body, jnp.full((1, tq), I16_MIN, I32))

    thr_hi = kth_largest16(hi_scr, jnp.full((1, tq), topk, I32))
    n_gt_hi = count16(hi_scr, thr_hi, True)
    thr_hi16 = thr_hi.astype(I16)

    def mark_body(kt, _):
        lo_scr[kt] = jnp.where(hi_scr[kt] == thr_hi16, lo_scr[kt], jnp.int16(I16_MIN))
        return 0

    lax.fori_loop(0, ntiles, mark_body, 0)
    thr_lo = kth_largest16(lo_scr, topk - n_gt_hi)
    thr = thr_hi * 65536 + (thr_lo - I16_MIN)

    lp = key_scr.shape[0] * tk
    n_gt = n_gt_hi + count16(lo_scr, thr_lo, True)
    n_ge = jnp.where(thr_lo > I16_MIN, n_gt_hi + count16(lo_scr, thr_lo, False), count16(hi_scr, thr_hi, False))
    excess = (n_ge > topk) & (thr > KEY_NEG_INF)
    thr2_scr[...] = jnp.zeros((1, tq), I32)

    @pl.when(jnp.max(jnp.where(excess, 1.0, 0.0)) > 0.0)
    def _():
        need = (topk - n_gt).astype(F32)
        nbits = lp.bit_length()

        def bit2_body(b, lo):
            cand = lo + jnp.left_shift(jnp.int32(1), nbits - 1 - b)
            c = count(lambda ks, base: jnp.where(ks == thr, lp - (base + krow), 0) >= cand)
            return jnp.where(c >= need, cand, lo)

        thr2_scr[...] = lax.fori_loop(0, nbits, bit2_body, jnp.zeros((1, tq), I32))

    thr2 = thr2_scr[...]

    def bias_body(kt, _):
        ks = key_scr[kt]
        tie = jnp.where(lp - (kt * tk + krow) >= thr2, 0.0, MASKED)
        bias = jnp.where(ks > thr, 0.0, jnp.where(ks == thr, tie, MASKED))
        bias_scr[kt] = jnp.where(ks > KEY_NEG_INF, bias, MASKED)
        return 0

    lax.fori_loop(0, ntiles, bias_body, 0)

    _flash_t_init(m_scr, acc_scr)

    def att_body(kt, _):
        k0 = pl.multiple_of(kt * tk, tk)
        slot = kt % 2
        s_max = []
        for h in range(heads):
            cols = slice((h // 2) * LANE, (h // 2 + 1) * LANE)
            s = jnp.dot(k_ref[pl.ds(k0, tk), cols], qat_scr[h], preferred_element_type=F32) + bias_scr[kt]
            s_max.append(_flash_t_stage(slot, h, s, s_scr))
        for h in range(heads):
            _flash_t_step(slot, h, s_max[h], vt_ref[h * HEAD:(h + 1) * HEAD, pl.ds(k0, tk)], LOG2E,
                          s_scr, m_scr, acc_scr)
        return 0

    lax.fori_loop(0, ntiles, att_body, 0)
    _flash_t_out(o_ref, acc_scr)


def _dsa_attention_t(qi, wi, ki, qa, k, vt, grp, tq):
    b, t, aw = qa.shape
    lp, tk = _key_tile(t)
    assert lp == t
    heads = aw // HEAD
    topk = min(TOPK_MAX, t // 4)
    qblk = lambda w: pl.BlockSpec((None, tq, w), lambda i, j: (i, j, 0))
    kblk = lambda w: pl.BlockSpec((None, t, w), lambda i, j: (i, 0, 0))
    return pl.pallas_call(
        functools.partial(_dsa_t_kernel, tq=tq, tk=tk, l=t, past=grp.past, topk=topk),
        grid=(b, t // tq),
        in_specs=[qblk(IDX_HEADS * HEAD), qblk(LANE), kblk(LANE), qblk(aw), kblk(aw),
                  pl.BlockSpec((aw, t), lambda i, j: (0, i))],
        out_specs=qblk(aw),
        out_shape=jax.ShapeDtypeStruct((b, t, aw), BF16),
        scratch_shapes=[pltpu.VMEM((t // tk, tk, tq), I32),
                        pltpu.VMEM((t // tk, tk, tq), I16),
                        pltpu.VMEM((t // tk, tk, tq), I16),
                        pltpu.VMEM((t // tk, tk, tq), F32),
                        pltpu.VMEM((1, tq), I32),
                        pltpu.VMEM((IDX_HEADS, LANE, tq), BF16),
                        pltpu.VMEM((heads, LANE, tq), BF16),
                        pltpu.VMEM((2, heads, tk, tq), F32),
                        pltpu.VMEM((heads, 1, tq), F32),
                        pltpu.VMEM((heads, HEAD + ONES_ROWS, tq), F32)],
        compiler_params=_params(("arbitrary", "arbitrary")),
        name="dsa_attention_t",
    )(qi, wi, ki, qa, k, vt)


def _mla_t_kernel(qn_ref, qr_ref, kn_ref, vt_ref, kr_ref, o_ref, qt_scr, s_scr, m_scr, acc_scr,
                  *, tq, tk, l, past):
    ch = acc_scr.shape[0]
    qpos0 = past + pl.program_id(1) * tq
    ntiles = _visible_tiles(qpos0, tq, l, tk)
    nfull = jnp.minimum(l, (qpos0 // CHUNK + 1) * CHUNK) // tk
    qchunk = (qpos0 + lax.broadcasted_iota(I32, (1, tq), 1)) // CHUNK
    krow = lax.broadcasted_iota(I32, (tk, 1), 0)
    c = (C_NOPE + HEAD) ** -0.5 * LOG2E
    for s in range(ch // 2):
        pair = _split_heads_t(qr_ref[:, s * LANE:(s + 1) * LANE].T)
        for half in range(2):
            h = 2 * s + half
            qt_scr[h] = jnp.concatenate([qn_ref[:, h * LANE:(h + 1) * LANE].T, pair[half]], axis=0)
    _flash_t_init(m_scr, acc_scr)

    def tile(kt, masked):
        k0 = pl.multiple_of(kt * tk, tk)
        k_rope = kr_ref[pl.ds(k0, tk), :]
        slot = kt % 2
        if masked:
            kpos = k0 + krow
            bias = jnp.where((kpos // CHUNK <= qchunk) & (kpos < l), 0.0, MASKED)
        s_max = []
        for h in range(ch):
            kcat = jnp.concatenate([kn_ref[pl.ds(k0, tk), h * LANE:(h + 1) * LANE], k_rope], axis=1)
            s = jnp.dot(kcat, qt_scr[h], preferred_element_type=F32)
            s_max.append(_flash_t_stage(slot, h, s + bias if masked else s, s_scr))
        for h in range(ch):
            _flash_t_step(slot, h, s_max[h], vt_ref[h * C_V:(h + 1) * C_V, pl.ds(k0, tk)], c,
                          s_scr, m_scr, acc_scr)
        return 0

    lax.fori_loop(0, nfull, lambda kt, _: tile(kt, False), 0)
    lax.fori_loop(nfull, ntiles, lambda kt, _: tile(kt, True), 0)
    _flash_t_out(o_ref, acc_scr)


def _mla_attention_t(qn, qr, kn, vt, kr, grp, tq):
    b, t, wn = qn.shape
    ch = wn // C_NOPE
    lp, tk = _key_tile(t)
    assert lp == t and ch % 2 == 0
    qblk = lambda w: pl.BlockSpec((None, tq, w), lambda i, j: (i, j, 0))
    kblk = lambda w: _resident((None, t, w), lambda i, j: (i, 0, 0))
    return pl.pallas_call(
        functools.partial(_mla_t_kernel, tq=tq, tk=tk, l=t, past=grp.past),
        grid=(b, t // tq),
        in_specs=[qblk(wn), qblk(ch * HEAD), kblk(wn), _resident((wn, t), lambda i, j: (0, i)), kblk(LANE)],
        out_specs=qblk(wn),
        out_shape=jax.ShapeDtypeStruct((b, t, wn), BF16),
        scratch_shapes=[pltpu.VMEM((ch, 2 * LANE, tq), BF16),
                        pltpu.VMEM((2, ch, tk, tq), F32),
                        pltpu.VMEM((ch, 1, tq), F32),
                        pltpu.VMEM((ch, C_V + ONES_ROWS, tq), F32)],
        compiler_params=_params(("arbitrary", "arbitrary")),
        name="mla_attention_t",
    )(qn, qr, kn, vt, kr)


def _causal_conv(u, e0, e1, w, seg):
    rmod = lax.broadcasted_iota(I32, (u.shape[0], 1), 0) % seg
    u1 = jnp.where(rmod == 0, e1, pltpu.roll(u, 1, 0))
    u2 = jnp.where(rmod == 0, e0, jnp.where(rmod == 1, e1, pltpu.roll(u, 2, 0)))
    return u2 * w[0:1] + u1 * w[1:2] + u * w[2:3]


def _layer_norm(z, g, b):
    mu = jnp.mean(z, axis=-1, keepdims=True)
    zc = z - mu
    var = jnp.mean(zc * zc, axis=-1, keepdims=True)
    return zc * lax.rsqrt(var + LN_EPS) * g + b


def _outproj_kernel(*refs, alpha, seq_tiles, seg, carried):
    if carried:
        (x_ref, oa_ref, bg_ref, u_ref, oc_ref, w_ref, cw_ref, g1_ref, lng_ref, lnb_ref, sc2_ref, sh2_ref,
         x1_ref, h2_ref, prev_scr) = refs

        @pl.when(pl.program_id(0) % seq_tiles == 0)
        def _():
            prev_scr[...] = jnp.zeros_like(prev_scr)

        e0, e1 = prev_scr[SUBLANE - 2:SUBLANE - 1, :], prev_scr[SUBLANE - 1:SUBLANE, :]
    else:
        (x_ref, oa_ref, bg_ref, u_ref, oc_ref, w_ref, cw_ref, g1_ref, lng_ref, lnb_ref, sc2_ref, sh2_ref,
         e0_ref, e1_ref, x1_ref, h2_ref) = refs
        e0, e1 = e0_ref[...], e1_ref[...]
    u = u_ref[...]
    yb = bg_ref[...] * _causal_conv(u, e0, e1, cw_ref[...], seg)
    if carried:
        prev_scr[...] = u[u.shape[0] - SUBLANE:, :]
    mixed = jnp.concatenate([oa_ref[...], yb.astype(BF16), oc_ref[...]], axis=1)
    tm = mixed.shape[0]
    n_split = 2 if tm % (2 * 2 * SUBLANE) == 0 else 1

    def rows_of(ref, rows):
        return ref[...] if ref.shape[0] == 1 else ref[rows, :]

    halves = [slice(c * tm // n_split, (c + 1) * tm // n_split) for c in range(n_split)]
    mixes = [jnp.dot(mixed[rows, :], w_ref[...], preferred_element_type=F32) for rows in halves]
    for rows, mix in zip(halves, mixes):
        x1 = _layer_norm(alpha * x_ref[rows, :] + (1.0 + rows_of(g1_ref, rows)) * mix, lng_ref[...], lnb_ref[...])
        x1_ref[rows, :] = x1
        h2_ref[rows, :] = (x1 * (1.0 + rows_of(sc2_ref, rows)) + rows_of(sh2_ref, rows)).astype(BF16)


def _out_projection(x, oa, bg, u, oc, w_out, layer, conv_w, g1, ln_g, ln_b, sc2, sh2, prev, grp, alpha, tm):
    m, d = x.shape
    aw, bw, cw = oa.shape[1], bg.shape[1], oc.shape[1]
    n_i = m // tm
    tiles_per_mod = n_i // g1.shape[0]
    mod_rows = g1.shape[1]
    row = lambda i: (i, 0)
    fix = lambda i: (0, 0)
    mod_spec = pl.BlockSpec((None, mod_rows, d), lambda i: (i // tiles_per_mod, 0, 0))
    in_specs = [pl.BlockSpec((tm, d), row), pl.BlockSpec((tm, aw), row), pl.BlockSpec((tm, bw), row),
                pl.BlockSpec((tm, bw), row), pl.BlockSpec((tm, cw), row),
                _resident((None, aw + bw + cw, d), lambda i: (layer, 0, 0)), pl.BlockSpec((CONV_W, bw), fix),
                mod_spec, pl.BlockSpec((1, d), fix), pl.BlockSpec((1, d), fix), mod_spec, mod_spec]
    args = [x, oa, bg, u, oc, w_out, conv_w, g1, ln_g.reshape(1, d), ln_b.reshape(1, d), sc2, sh2]
    scratch = []
    if prev is None:
        assert grp.t % tm == 0
        scratch = [pltpu.VMEM((SUBLANE, bw), F32)]
    else:
        assert tm % grp.t == 0
        in_specs += [pl.BlockSpec((tm, bw), row), pl.BlockSpec((tm, bw), row)]
        args += [prev[0], prev[1]]
    return pl.pallas_call(
        functools.partial(_outproj_kernel, alpha=alpha, seq_tiles=max(grp.t // tm, 1), seg=min(grp.t, tm),
                          carried=prev is None),
        grid=(n_i,),
        in_specs=in_specs,
        out_specs=[pl.BlockSpec((tm, d), row), pl.BlockSpec((tm, d), row)],
        out_shape=[jax.ShapeDtypeStruct((m, d), F32), jax.ShapeDtypeStruct((m, d), BF16)],
        scratch_shapes=scratch,
        compiler_params=_params(("arbitrary",)),
        name="out_projection",
    )(*args)


def _ffn_kernel(*refs, alpha, seq_tiles, seg, carried):
    if carried:
        (h_ref, x_ref, wg_ref, wu_ref, wd_ref, cw_ref, g2_ref, lng_ref, lnb_ref,
         o_ref, gt_ref, acc_scr, prev_scr) = refs
    else:
        (h_ref, x_ref, wg_ref, wu_ref, wd_ref, cw_ref, g2_ref, lng_ref, lnb_ref, e0_ref, e1_ref,
         o_ref, gt_ref, acc_scr) = refs
    f = pl.program_id(1)
    if carried:
        @pl.when(pl.program_id(0) % seq_tiles == 0)
        def _():
            prev_scr[f] = jnp.zeros(prev_scr.shape[1:], F32)

    @pl.when(f == 0)
    def _():
        acc_scr[...] = jnp.zeros_like(acc_scr)

    h = h_ref[...]
    tm, tf = h.shape[0], wg_ref.shape[1]
    n_split = 2 if tf % (2 * LANE) == 0 else 1
    halves = [slice(c * tf // n_split, (c + 1) * tf // n_split) for c in range(n_split)]
    gates = [jnp.dot(h, wg_ref[:, cols], preferred_element_type=F32) for cols in halves]
    ups = [jnp.dot(h, wu_ref[:, cols], preferred_element_type=F32) for cols in halves]
    for cols, gate, up in zip(halves, gates, ups):
        if carried:
            e0, e1 = prev_scr[f, SUBLANE - 2:SUBLANE - 1, cols], prev_scr[f, SUBLANE - 1:SUBLANE, cols]
        else:
            e0, e1 = e0_ref[:, cols], e1_ref[:, cols]
        conv = _causal_conv(gate, e0, e1, cw_ref[:, cols], seg)
        if carried:
            prev_scr[f, :, cols] = gate[tm - SUBLANE:, :]
        gt_ref[:, cols] = gate[tm - gt_ref.shape[0]:, :]
        act = (jax.nn.silu(conv) * up).astype(BF16)
        acc_scr[...] = jnp.dot(act, wd_ref[cols, :], preferred_element_type=F32) + acc_scr[...]

    @pl.when(f == pl.num_programs(1) - 1)
    def _():
        z = alpha * x_ref[...] + (1.0 + g2_ref[...]) * acc_scr[...]
        o_ref[...] = _layer_norm(z, lng_ref[...], lnb_ref[...])


def _channel_mixer(h2, x1, w_gu, w_down, layer, conv_w, g2, ln_g, ln_b, prev, grp, alpha, tm, tf):
    m, d = x1.shape
    dff = w_down.shape[1]
    n_i, n_f = m // tm, dff // tf
    tiles_per_mod = n_i // g2.shape[0]
    mod_rows = g2.shape[1]
    fix = lambda i, f: (0, 0)
    in_specs = [pl.BlockSpec((tm, d), lambda i, f: (i, 0)), pl.BlockSpec((tm, d), lambda i, f: (i, 0)),
                pl.BlockSpec((None, d, tf), lambda i, f: (layer, 0, f)),
                pl.BlockSpec((None, d, tf), lambda i, f: (layer, 0, n_f + f)),
                pl.BlockSpec((None, tf, d), lambda i, f: (layer, f, 0)),
                pl.BlockSpec((CONV_W, tf), lambda i, f: (0, f)),
                pl.BlockSpec((None, mod_rows, d), lambda i, f: (i // tiles_per_mod, 0, 0)),
                pl.BlockSpec((1, d), fix), pl.BlockSpec((1, d), fix)]
    args = [h2, x1, w_gu, w_gu, w_down, conv_w, g2, ln_g.reshape(1, d), ln_b.reshape(1, d)]
    scratch = [pltpu.VMEM((tm, d), F32)]
    if prev is None:
        assert grp.t % tm == 0
        scratch.append(pltpu.VMEM((n_f, SUBLANE, tf), F32))
        gt_spec = pl.BlockSpec((None, SUBLANE, tf), lambda i, f: (i, 0, f))
        gt_shape = jax.ShapeDtypeStruct((n_i, SUBLANE, dff), F32)
    else:
        assert tm % grp.t == 0
        in_specs += [pl.BlockSpec((tm, tf), lambda i, f: (i, f))] * 2
        args += [prev[0], prev[1]]
        gt_spec = pl.BlockSpec((tm, tf), lambda i, f: (i, f))
        gt_shape = jax.ShapeDtypeStruct((m, dff), F32)
    return pl.pallas_call(
        functools.partial(_ffn_kernel, alpha=alpha, seq_tiles=max(grp.t // tm, 1), seg=min(grp.t, tm),
                          carried=prev is None),
        grid=(n_i, n_f),
        in_specs=in_specs,
        out_specs=[pl.BlockSpec((tm, d), lambda i, f: (i, 0)), gt_spec],
        out_shape=[jax.ShapeDtypeStruct((m, d), F32), gt_shape],
        scratch_shapes=scratch,
        compiler_params=_params(("arbitrary", "arbitrary")),
        name="channel_mixer",
    )(*args)


def _rope_tables(pos):
    half = HEAD // 2
    inv = jnp.power(jnp.float32(ROPE_THETA), -jnp.arange(half, dtype=F32) / half)
    ang = pos.astype(F32)[:, None] * inv[None, :]
    cos, sin = jnp.cos(ang), jnp.sin(ang)
    zero = jnp.zeros_like(sin)
    reps = LANE // HEAD
    return (jnp.tile(jnp.concatenate([cos, cos], axis=1), (1, reps)),
            jnp.tile(jnp.concatenate([zero, sin], axis=1), (1, reps)),
            jnp.tile(jnp.concatenate([-sin, zero], axis=1), (1, reps)))


def _with_past(past, new, lp, twice=False):
    b, t, w = new.shape
    parts = [new]
    if past is not None:
        p = past.reshape(b, past.shape[1], -1).astype(BF16)
        parts = [jnp.concatenate([p, p], axis=-1) if twice else p, new]
    n = sum(a.shape[1] for a in parts)
    if lp > n:
        parts.append(jnp.zeros((b, lp - n, w), BF16))
    return parts[0] if len(parts) == 1 else jnp.concatenate(parts, axis=1)


def _layer(x, mod, tables, grp, past, layer, big, small, stacks, dm, alpha):
    w_in_p, w_out, w_uk, w_uv, w_uv_t, w_gu, w_down = big
    conv_b_w, kv_norm, ln1_g, ln1_b, ln2_g, ln2_b, conv_f_w = small
    b, t = grp.b, grp.t
    m, d = x.shape
    carried = past is None
    tm = _row_tile(t, 256) if carried else m
    if carried:
        mods = [a.reshape(b, 1, d) for a in jnp.split(mod, N_MOD, axis=-1)]
    else:
        mods = [jnp.repeat(a, t, axis=0).reshape(1, m, d) for a in jnp.split(mod, N_MOD, axis=-1)]
    sh1, sc1, g1, sh2, sc2, g2 = mods

    outs = _in_projection(x, sc1, sh1, w_in_p, layer, tables, kv_norm, stacks, dm, tm)
    stacks = tuple(outs[o] for o in STATE_OUTPUTS)
    qa, _, kab, qi, qcr, _, kib, _, krb, _, vab, bg, u, qcn, _, latb, wi, vat = outs

    three = lambda a: a.reshape(b, t, a.shape[-1])
    if carried:
        tq = _row_tile(t, 256)
        kn, vct = _kv_up(latb, w_uk, w_uv_t, layer)
        oa = _dsa_attention_t(three(qi), three(wi), three(kib), three(qa), three(kab), vat, grp, tq)
        oc = _mla_attention_t(three(qcn), three(qcr), three(kn), vct, three(krb), grp, tq)
        prev_b = prev_f = None
    else:
        l = grp.past + t
        lp, _ = _key_tile(l)
        p_ak, p_av, p_ik, p_lat, p_kr, prev_b, prev_f = past
        k_all = _with_past(p_ak, three(kab), lp)
        v_all = _with_past(p_av, three(vab), lp)
        ki_all = _with_past(p_ik, three(kib), lp, twice=True)
        kr_all = _with_past(p_kr, three(krb), lp, twice=True)
        lat_all = _with_past(p_lat, three(latb), lp)
        oa = _dsa_attention(three(qi), three(wi), ki_all, three(qa), k_all, v_all, grp, l, t)
        oc = _mla_attention_latent(three(qcn), three(qcr), lat_all, kr_all, w_uk, w_uv, layer, grp, l)

    def expand(state):
        return jnp.repeat(state[:, 0], t, axis=0), jnp.repeat(state[:, 1], t, axis=0)

    x1, h2 = _out_projection(x, oa.reshape(m, -1), bg, u, oc.reshape(m, -1), w_out, layer, conv_b_w, g1,
                             ln1_g, ln1_b, sc2, sh2, None if carried else expand(prev_b), grp, alpha, tm)
    tm_f = _row_tile(t, 512) if carried else m
    tf = _row_tile(dm.dff, 512)
    x2, gate_rows = _channel_mixer(h2, x1, w_gu, w_down, layer, conv_f_w, g2, ln2_g, ln2_b,
                                   None if carried else expand(prev_f), grp, alpha, tm_f, tf)
    if carried:
        new_f = gate_rows.reshape(b, t // tm_f, SUBLANE, dm.dff)[:, -1, SUBLANE - (CONV_W - 1):, :]
    else:
        new_f = gate_rows.reshape(b, t, dm.dff)[:, t - (CONV_W - 1):, :]
    new_b = u.reshape(b, t, dm.bw)[:, t - (CONV_W - 1):, :]
    return x2, stacks, (new_b, new_f)


def _state_outputs(stacks, conv_rows, grp, dm):
    ka, ki, kr, va, lat = stacks
    b, t, heads = grp.b, grp.t, dm.aw // HEAD
    new_b, new_f = [jnp.stack(r) for r in zip(*conv_rows)]
    return (ka.reshape(dm.depth, b, t, heads, HEAD), va.reshape(dm.depth, b, t, heads, HEAD),
            ki.reshape(dm.depth, b, t, HEAD), lat.reshape(dm.depth, b, t, dm.r), kr.reshape(dm.depth, b, t, HEAD),
            new_b, new_f)


def kernel(x_prompt, x_sample, c_prompt, c_sample, cache_a_k, cache_a_v, cache_idx_k, cache_mla_latent,
           cache_mla_krope, state_conv_b, state_conv_ffn, w_in, w_out, conv_b_w, mla_kv_norm, mla_w_uk,
           mla_w_uv, w_mod, b_mod, ln1_g, ln1_b, ln2_g, ln2_b, ffn_w_gu, ffn_conv_w, ffn_w_down):
    depth, d, _ = w_in.shape
    a_heads = cache_a_k.shape[3]
    dm = Dims(d=d, aw=a_heads * HEAD, bw=conv_b_w.shape[2], ch=mla_w_uk.shape[2] // C_NOPE,
              r=mla_w_uk.shape[1], dff=ffn_w_down.shape[1], depth=depth)
    alpha = (2 * depth) ** 0.25
    grp_p = Group(b=x_prompt.shape[0], t=x_prompt.shape[1], past=0)
    grp_s = Group(b=x_sample.shape[0], t=x_sample.shape[1], past=cache_a_k.shape[2])

    n_c = grp_p.b + grp_s.b
    c_all = jnp.concatenate([c_prompt, c_sample, jnp.zeros((-n_c % SUBLANE, d), F32)], axis=0)
    mod = _modulation(c_all, w_mod, b_mod)

    tab_p = _rope_tables(jnp.arange(grp_p.t, dtype=I32))
    tab_s = tuple(jnp.tile(a, (grp_s.b, 1)) for a in _rope_tables(grp_s.past + jnp.arange(grp_s.t, dtype=I32)))

    xp = x_prompt.reshape(grp_p.b * grp_p.t, d)
    xs = x_sample.reshape(grp_s.b * grp_s.t, d)
    conv_p, conv_s = [], []
    stacks_p = stacks_s = None
    w_uv_b = mla_w_uv.astype(BF16)
    big = (_pack_w_in(w_in, dm), w_out.astype(BF16), mla_w_uk.astype(BF16), w_uv_b, jnp.swapaxes(w_uv_b, 1, 2),
           ffn_w_gu.astype(BF16), ffn_w_down.astype(BF16))
    for l in range(depth):
        small = (conv_b_w[l], mla_kv_norm[l], ln1_g[l], ln1_b[l], ln2_g[l], ln2_b[l], ffn_conv_w[l])
        xp, stacks_p, rp = _layer(xp, mod[l, :grp_p.b], tab_p, grp_p, None, l, big, small, stacks_p, dm, alpha)
        past_l = (cache_a_k[l], cache_a_v[l], cache_idx_k[l], cache_mla_latent[l], cache_mla_krope[l],
                  state_conv_b[l], state_conv_ffn[l])
        xs, stacks_s, rs = _layer(xs, mod[l, grp_p.b:n_c], tab_s, grp_s, past_l, l, big, small, stacks_s, dm,
                                  alpha)
        conv_p.append(rp)
        conv_s.append(rs)
    return (xp.reshape(x_prompt.shape), xs.reshape(x_sample.shape),
            *_state_outputs(stacks_p, conv_p, grp_p, dm), *_state_outputs(stacks_s, conv_s, grp_s, dm))
```

```python
import functools
from typing import NamedTuple

import numpy as np
import jax
import jax.numpy as jnp
from jax import lax
from jax.experimental import pallas as pl
from jax.experimental.pallas import tpu as pltpu

F32, BF16, I32, I16 = jnp.float32, jnp.bfloat16, jnp.int32, jnp.int16

CHUNK = 64
CONV_W = 3
ROPE_THETA = 10000.0
HEAD = 64
IDX_HEADS = 16
TOPK_MAX = 256
C_NOPE = 128
C_V = 128
N_MOD = 6
LN_EPS = 1e-5
RMS_EPS = 1e-6

LANE = 128
SUBLANE = 8
VMEM_LIMIT = 50 * 1024 * 1024

MASKED = -1e30
LOG2E = 1.4426950408889634
INT_MIN = -2 ** 31
I16_MIN = -2 ** 15
KEY_NEG_INF = int(np.array(-np.inf, np.float32).view(np.int32)) ^ 0x7FFFFFFF


class Dims(NamedTuple):
    d: int
    aw: int
    bw: int
    ch: int
    r: int
    dff: int
    depth: int


class Group(NamedTuple):
    b: int
    t: int
    past: int


def _row_tile(m, pref):
    if m <= pref:
        return m
    t = pref - pref % SUBLANE
    while m % t:
        t -= SUBLANE
    return t


def _key_tile(l):
    lp = -(-l // LANE) * LANE
    for tk in (512, 384, 256, 128):
        if lp % tk == 0:
            return lp, tk
    raise AssertionError(lp)


def _params(sem):
    return pltpu.CompilerParams(dimension_semantics=sem, vmem_limit_bytes=VMEM_LIMIT)


def _resident(shape, index_map):
    return pl.BlockSpec(shape, index_map, pipeline_mode=pl.Buffered(1))


def _mod_kernel(c_ref, w_ref, b_ref, o_ref):
    a = jax.nn.silu(c_ref[...]).astype(BF16)
    o_ref[...] = jnp.dot(a, w_ref[...].astype(BF16), preferred_element_type=F32) + b_ref[...]


def _modulation(c, w_mod, b_mod):
    depth, d, n = w_mod.shape
    rows = c.shape[0]
    tn = _row_tile(n, 1024)
    return pl.pallas_call(
        _mod_kernel,
        grid=(depth, n // tn),
        in_specs=[pl.BlockSpec((rows, d), lambda l, j: (0, 0)),
                  pl.BlockSpec((None, d, tn), lambda l, j: (l, 0, j)),
                  pl.BlockSpec((None, 1, tn), lambda l, j: (l, 0, j))],
        out_specs=pl.BlockSpec((None, rows, tn), lambda l, j: (l, 0, j)),
        out_shape=jax.ShapeDtypeStruct((depth, rows, n), F32),
        compiler_params=_params(("arbitrary", "arbitrary")),
        name="modulation",
    )(c, w_mod, b_mod.reshape(depth, 1, n))


def _pack_w_in(w, dm):
    d, aw, bw, ch, r = dm.d, dm.aw, dm.bw, dm.ch, dm.r
    lead = w.shape[:-1]
    o = np.cumsum([0, aw, aw, aw, IDX_HEADS * HEAD, HEAD, IDX_HEADS, bw, bw, bw, ch * (C_NOPE + HEAD), r, HEAD])
    qa, ka, va, qi, ki, wi, bg, cg, xb, qc, lat, kr = [w[..., o[i]:o[i + 1]] for i in range(12)]
    qc = qc.reshape(lead + (ch, C_NOPE + HEAD))
    qcn = qc[..., :C_NOPE].reshape(lead + (ch * C_NOPE,))
    qcr = qc[..., C_NOPE:].reshape(lead + (ch * HEAD,))
    pad = jnp.zeros(lead + (LANE - IDX_HEADS,), w.dtype)
    return jnp.concatenate([qa, ka, qi, qcr, ki, ki, kr, kr, va, bg, cg, xb, qcn, lat, wi, pad],
                           axis=-1).astype(BF16)


def _rope(acc, cos, s1, s2):
    outs = []
    for s in range(acc.shape[1] // LANE):
        xs = acc[:, s * LANE:(s + 1) * LANE]
        outs.append(xs * cos + pltpu.roll(xs, HEAD // 2, 1) * s1 + pltpu.roll(xs, LANE - HEAD // 2, 1) * s2)
    return outs[0] if len(outs) == 1 else jnp.concatenate(outs, axis=1)


def _inproj_kernel(x_ref, sc_ref, sh_ref, w_ref, cos_ref, s1_ref, s2_ref, nrm_ref, *rest, dm, n_alias):
    (qa_ref, ka_ref, kab_ref, qi_ref, qcr_ref, ki_ref, kib_ref, kr_ref, krb_ref,
     va_ref, vab_ref, bg_ref, u_ref, qcn_ref, lat_ref, latb_ref, wi_ref, vat_ref) = rest[n_alias:]
    aw, bw, ch, r = dm.aw, dm.bw, dm.ch, dm.r
    h = (x_ref[...] * (1.0 + sc_ref[...]) + sh_ref[...]).astype(BF16)
    cos, s1, s2 = cos_ref[...], s1_ref[...], s2_ref[...]
    col = [0]

    def proj(width):
        c0 = col[0]
        col[0] = c0 + width
        return jnp.dot(h, w_ref[:, c0:c0 + width], preferred_element_type=F32)

    def pieces(width, step=512):
        return [(o, min(step, width - o)) for o in range(0, width, step)]

    for o, wd in pieces(aw):
        qa_ref[:, o:o + wd] = _rope(proj(wd), cos, s1, s2).astype(BF16)
    for o, wd in pieces(aw):
        y = _rope(proj(wd), cos, s1, s2)
        ka_ref[:, o // HEAD:(o + wd) // HEAD, :] = y.reshape(y.shape[0], wd // HEAD, HEAD)
        kab_ref[:, o:o + wd] = y.astype(BF16)
    for o, wd in pieces(IDX_HEADS * HEAD):
        qi_ref[:, o:o + wd] = _rope(proj(wd), cos, s1, s2).astype(BF16)
    for o, wd in pieces(ch * HEAD):
        qcr_ref[:, o:o + wd] = _rope(proj(wd), cos, s1, s2).astype(BF16)
    y = _rope(proj(2 * LANE), cos, s1, s2)
    for c, (f32_ref, b16_ref) in enumerate(((ki_ref, kib_ref), (kr_ref, krb_ref))):
        f32_ref[...] = y[:, c * LANE:c * LANE + HEAD]
        b16_ref[...] = y[:, c * LANE:(c + 1) * LANE].astype(BF16)
    for o, wd in pieces(aw):
        y = proj(wd)
        va_ref[:, o // HEAD:(o + wd) // HEAD, :] = y.reshape(y.shape[0], wd // HEAD, HEAD)
        vab_ref[:, o:o + wd] = y.astype(BF16)
        vat_ref[o:o + wd, :] = y.T.astype(BF16)
    for o, wd in pieces(bw):
        bg_ref[:, o:o + wd] = proj(wd)
    c_cg = col[0]
    for o, wd in pieces(bw):
        cg = jnp.dot(h, w_ref[:, c_cg + o:c_cg + o + wd], preferred_element_type=F32)
        xb = jnp.dot(h, w_ref[:, c_cg + bw + o:c_cg + bw + o + wd], preferred_element_type=F32)
        u_ref[:, o:o + wd] = cg * xb
    col[0] = c_cg + 2 * bw
    for o, wd in pieces(ch * C_NOPE):
        qcn_ref[:, o:o + wd] = proj(wd).astype(BF16)
    lat = proj(r)
    lat = lat * lax.rsqrt(jnp.mean(lat * lat, axis=-1, keepdims=True) + RMS_EPS) * nrm_ref[...]
    lat_ref[...] = lat
    latb_ref[...] = lat.astype(BF16)
    wi_ref[...] = proj(LANE) * (IDX_HEADS ** -0.5)


STATE_OUTPUTS = (1, 5, 7, 9, 14)
HEAD_OUTPUTS = (1, 9)


def _in_projection(x, sc, sh, w_packed, layer, tables, kv_norm, stacks, dm, tm):
    m, d = x.shape
    aw, bw, ch, r = dm.aw, dm.bw, dm.ch, dm.r
    npk = w_packed.shape[2]
    n_i = m // tm
    tiles_per_mod = n_i // sc.shape[0]
    mod_rows = sc.shape[1]
    tab_tiles = tables[0].shape[0] // tm
    widths = [(aw, BF16), (aw, F32), (aw, BF16), (IDX_HEADS * HEAD, BF16), (ch * HEAD, BF16),
              (HEAD, F32), (LANE, BF16), (HEAD, F32), (LANE, BF16),
              (aw, F32), (aw, BF16), (bw, F32), (bw, F32), (ch * C_NOPE, BF16), (r, F32), (r, BF16),
              (LANE, F32)]
    row = lambda i: (i, 0)
    mod_spec = pl.BlockSpec((None, mod_rows, d), lambda i: (i // tiles_per_mod, 0, 0))
    tab_spec = pl.BlockSpec((tm, LANE), lambda i: (i % tab_tiles, 0))
    in_specs = [pl.BlockSpec((tm, d), row), mod_spec, mod_spec,
                _resident((None, d, npk), lambda i: (layer, 0, 0)),
                tab_spec, tab_spec, tab_spec,
                pl.BlockSpec((1, r), lambda i: (0, 0))]
    args = [x, sc, sh, w_packed, *tables, kv_norm.reshape(1, r)]
    out_specs = [pl.BlockSpec((tm, w), row) for w, _ in widths] + [pl.BlockSpec((aw, tm), lambda i: (0, i))]
    out_shape = [jax.ShapeDtypeStruct((m, w), dt) for w, dt in widths] + [jax.ShapeDtypeStruct((aw, m), BF16)]
    for o in STATE_OUTPUTS:
        w, dt = widths[o]
        tail = (w // HEAD, HEAD) if o in HEAD_OUTPUTS else (w,)
        out_specs[o] = pl.BlockSpec((None, tm) + tail, lambda i, n=len(tail): (layer, i) + (0,) * n)
        out_shape[o] = jax.ShapeDtypeStruct((dm.depth, m) + tail, dt)
    aliases = {}
    if stacks is not None:
        aliases = {len(args) + k: o for k, o in enumerate(STATE_OUTPUTS)}
        in_specs += [pl.BlockSpec(memory_space=pl.ANY)] * len(stacks)
        args += list(stacks)
    return pl.pallas_call(
        functools.partial(_inproj_kernel, dm=dm, n_alias=len(aliases)),
        grid=(n_i,),
        in_specs=in_specs,
        out_specs=out_specs,
        out_shape=out_shape,
        input_output_aliases=aliases,
        compiler_params=_params(("arbitrary",)),
        name="in_projection",
    )(*args)


def _kvup_kernel(l_ref, wk_ref, wvt_ref, k_ref, vt_ref):
    lat = l_ref[...]
    k_ref[...] = jnp.dot(lat, wk_ref[...], preferred_element_type=F32).astype(BF16)
    vt_ref[...] = lax.dot_general(wvt_ref[...], lat, _NT, preferred_element_type=F32).astype(BF16)


def _kv_up(lat, w_uk, w_uv_t, layer):
    m, r = lat.shape
    n = w_uk.shape[2]
    tm = _row_tile(m, 512)
    return pl.pallas_call(
        _kvup_kernel,
        grid=(m // tm,),
        in_specs=[pl.BlockSpec((tm, r), lambda i: (i, 0)),
                  pl.BlockSpec((None, r, n), lambda i: (layer, 0, 0)),
                  pl.BlockSpec((None, n, r), lambda i: (layer, 0, 0))],
        out_specs=[pl.BlockSpec((tm, n), lambda i: (i, 0)), pl.BlockSpec((n, tm), lambda i: (0, i))],
        out_shape=[jax.ShapeDtypeStruct((m, n), BF16), jax.ShapeDtypeStruct((n, m), BF16)],
        compiler_params=_params(("arbitrary",)),
        name="latent_up_projection",
    )(lat, w_uk, w_uv_t)


_NT = (((1,), (1,)), ((), ()))


def _visible_tiles(qpos0, tq, l, tk):
    last_visible = ((qpos0 + tq - 1) // CHUNK + 1) * CHUNK
    nvis = min(l, last_visible) if isinstance(qpos0, int) else jnp.minimum(l, last_visible)
    return (nvis + tk - 1) // tk


def _for(n, body, init):
    if isinstance(n, int):
        for i in range(n):
            init = body(i, init)
        return init
    return lax.fori_loop(0, n, body, init)


def _tile_start(kt, tk):
    return kt * tk if isinstance(kt, int) else pl.multiple_of(kt * tk, tk)


def _flash_step(carry, s, v):
    m, l, acc = carry
    m_new = jnp.maximum(m, jnp.max(s, axis=1, keepdims=True))
    alpha = jnp.exp(m - m_new)
    p = jnp.exp(s - m_new)
    l = alpha * l + jnp.sum(p, axis=1, keepdims=True)
    acc = alpha * acc + jnp.dot(p.astype(BF16), v, preferred_element_type=F32)
    return m_new, l, acc


def _flash_init(rows, width):
    return (jnp.full((rows, 1), MASKED, F32), jnp.zeros((rows, 1), F32), jnp.zeros((rows, width), F32))


def _ordered_key(x):
    b = pltpu.bitcast(x, I32)
    return jnp.where(b < 0, b ^ 0x7FFFFFFF, b)


def _dsa_kernel(qi_ref, wi_ref, ki_ref, qa_ref, k_ref, v_ref, o_ref, key_scr, bias_scr, thr2_scr,
                *, tq, tk, l, lp, past, topk, aw, one_block):
    qpos0 = past if one_block else past + pl.program_id(1) * tq
    ntiles = _visible_tiles(qpos0, tq, l, tk)
    qchunk = (qpos0 + lax.broadcasted_iota(I32, (tq, 1), 0)) // CHUNK
    lane = lax.broadcasted_iota(I32, (1, LANE), 1)
    lo_half = lane < HEAD
    kf = jnp.float32(topk)

    def split_heads(qs):
        zero = jnp.zeros_like(qs)
        return jnp.concatenate([jnp.where(lo_half, qs, zero), jnp.where(lo_half, zero, qs)], axis=0)

    wi = wi_ref[...] * (HEAD ** -0.5)
    q_idx = jnp.concatenate([split_heads(qi_ref[:, s * LANE:(s + 1) * LANE]) for s in range(IDX_HEADS // 2)],
                            axis=0)
    w_idx = [wi[:, h:h + 1] for h in range(IDX_HEADS)]

    def score_body(kt, _):
        k0 = _tile_start(kt, tk)
        s_all = lax.dot_general(q_idx, ki_ref[pl.ds(k0, tk), :], _NT, preferred_element_type=F32)
        score = jnp.zeros((tq, tk), F32)
        for h in range(IDX_HEADS):
            score = score + jnp.maximum(s_all[h * tq:(h + 1) * tq], 0.0) * w_idx[h]
        kpos = k0 + lax.broadcasted_iota(I32, (1, tk), 1)
        vis = (kpos // CHUNK <= qchunk) & (kpos < l)
        key_scr[kt] = _ordered_key(jnp.where(vis, score, -jnp.inf))
        return 0

    _for(ntiles, score_body, 0)

    def count(pred):
        def body(kt, acc):
            for j in range(tk // LANE):
                ks = key_scr[kt, :, j * LANE:(j + 1) * LANE]
                acc = acc + jnp.where(pred(ks, kt * tk + j * LANE), 1.0, 0.0)
            return acc
        acc = _for(ntiles, body, jnp.zeros((tq, LANE), F32))
        return jnp.sum(acc, axis=1, keepdims=True)

    def wide(col):
        return jnp.broadcast_to(col, (tq, LANE))

    def bit_body(b, lo):
        cand = lo + jnp.left_shift(jnp.int32(1), 31 - b)
        cand_w = wide(cand)
        c = count(lambda ks, _: ks >= cand_w)
        return jnp.where(c >= kf, cand, lo)

    thr = lax.fori_loop(0, 32, bit_body, jnp.full((tq, 1), INT_MIN, I32))
    thr_w = wide(thr)

    n_ge = count(lambda ks, _: ks >= thr_w)
    n_gt = count(lambda ks, _: ks > thr_w)
    excess = (n_ge > kf) & (thr > KEY_NEG_INF)
    thr2_scr[...] = jnp.zeros((tq, 1), I32)

    @pl.when(jnp.max(jnp.where(excess, 1.0, 0.0)) > 0.0)
    def _():
        need = kf - n_gt
        nbits = lp.bit_length()

        def bit2_body(b, lo):
            cand = lo + jnp.left_shift(jnp.int32(1), nbits - 1 - b)
            cand_w = wide(cand)
            c = count(lambda ks, base: jnp.where(ks == thr_w, lp - (base + lane), 0) >= cand_w)
            return jnp.where(c >= need, cand, lo)

        thr2_scr[...] = lax.fori_loop(0, nbits, bit2_body, jnp.zeros((tq, 1), I32))

    thr2_w = wide(thr2_scr[...])

    def bias_body(kt, _):
        for j in range(tk // LANE):
            ks = key_scr[kt, :, j * LANE:(j + 1) * LANE]
            tie = jnp.where(lp - (kt * tk + j * LANE + lane) >= thr2_w, 0.0, MASKED)
            bias = jnp.where(ks > thr_w, 0.0, jnp.where(ks == thr_w, tie, MASKED))
            bias_scr[kt, :, j * LANE:(j + 1) * LANE] = jnp.where(ks > KEY_NEG_INF, bias, MASKED)
        return 0

    _for(ntiles, bias_body, 0)

    for pr in range(aw // LANE):
        cols = slice(pr * LANE, (pr + 1) * LANE)
        q2 = split_heads(qa_ref[:, cols])

        def att_body(kt, carry, cols=cols, q2=q2):
            k0 = _tile_start(kt, tk)
            s = lax.dot_general(q2, k_ref[pl.ds(k0, tk), cols], _NT, preferred_element_type=F32)
            bias = bias_scr[kt]
            s = s * (HEAD ** -0.5) + jnp.concatenate([bias, bias], axis=0)
            return _flash_step(carry, s, v_ref[pl.ds(k0, tk), cols])

        _, den, acc = _for(ntiles, att_body, _flash_init(2 * tq, LANE))
        o2 = acc / den
        o_ref[:, cols] = jnp.where(lo_half, o2[:tq], o2[tq:]).astype(BF16)


def _dsa_attention(qi, wi, ki, qa, k, v, grp, l, tq):
    b, t, aw = qa.shape
    lp = k.shape[1]
    _, tk = _key_tile(l)
    topk = min(TOPK_MAX, l // 4)
    qblk = lambda w: pl.BlockSpec((None, tq, w), lambda i, j: (i, j, 0))
    kblk = lambda w: pl.BlockSpec((None, lp, w), lambda i, j: (i, 0, 0))
    return pl.pallas_call(
        functools.partial(_dsa_kernel, tq=tq, tk=tk, l=l, lp=lp, past=grp.past, topk=topk, aw=aw,
                          one_block=t == tq),
        grid=(b, t // tq),
        in_specs=[qblk(IDX_HEADS * HEAD), qblk(LANE), kblk(LANE), qblk(aw), kblk(aw), kblk(aw)],
        out_specs=qblk(aw),
        out_shape=jax.ShapeDtypeStruct((b, t, aw), BF16),
        scratch_shapes=[pltpu.VMEM((lp // tk, tq, tk), I32),
                        pltpu.VMEM((lp // tk, tq, tk), F32),
                        pltpu.VMEM((tq, 1), I32)],
        compiler_params=_params(("arbitrary", "arbitrary")),
        name="dsa_attention",
    )(qi, wi, ki, qa, k, v)


def _mla_latent_kernel(qn_ref, qr_ref, lat_ref, kr_ref, wuk_ref, wuv_ref, o_ref, *, t, l, past, ch):
    lo_half = lax.broadcasted_iota(I32, (1, LANE), 1) < HEAD
    scale = (C_NOPE + HEAD) ** -0.5
    q_lat, q_rope = [], []
    for h in range(ch):
        cols = slice(h * C_NOPE, (h + 1) * C_NOPE)
        q_lat.append(lax.dot_general(qn_ref[:, cols], wuk_ref[:, cols], _NT, preferred_element_type=F32))
        qs = qr_ref[:, (h // 2) * LANE:(h // 2 + 1) * LANE]
        zero = jnp.zeros_like(qs)
        q_rope.append(jnp.where(lo_half, qs, zero) if h % 2 == 0 else jnp.where(lo_half, zero, qs))
    q_lat = jnp.concatenate(q_lat, axis=0).astype(BF16)
    q_rope = jnp.concatenate(q_rope, axis=0)
    lat = lat_ref[...]
    s = (lax.dot_general(q_lat, lat, _NT, preferred_element_type=F32)
         + lax.dot_general(q_rope, kr_ref[...], _NT, preferred_element_type=F32)) * scale
    qchunk = (past + lax.broadcasted_iota(I32, (t, 1), 0)) // CHUNK
    kpos = lax.broadcasted_iota(I32, (1, lat.shape[0]), 1)
    vis = (kpos // CHUNK <= jnp.concatenate([qchunk] * ch, axis=0)) & (kpos < l)
    s = jnp.where(vis, s, MASKED)
    p = jnp.exp(s - jnp.max(s, axis=1, keepdims=True))
    den = jnp.sum(p, axis=1, keepdims=True)
    o_lat = (jnp.dot(p.astype(BF16), lat, preferred_element_type=F32) / den).astype(BF16)
    for h in range(ch):
        cols = slice(h * C_V, (h + 1) * C_V)
        o_ref[:, cols] = jnp.dot(o_lat[h * t:(h + 1) * t, :], wuv_ref[:, cols],
                                 preferred_element_type=F32).astype(BF16)


def _mla_attention_latent(qn, qr, lat, kr, w_uk, w_uv, layer, grp, l):
    b, t, wn = qn.shape
    ch = wn // C_NOPE
    lp, r = lat.shape[1:]
    qblk = lambda w: pl.BlockSpec((None, t, w), lambda i: (i, 0, 0))
    kblk = lambda w: pl.BlockSpec((None, lp, w), lambda i: (i, 0, 0))
    wblk = pl.BlockSpec((None, r, wn), lambda i: (layer, 0, 0))
    return pl.pallas_call(
        functools.partial(_mla_latent_kernel, t=t, l=l, past=grp.past, ch=ch),
        grid=(b,),
        in_specs=[qblk(wn), qblk(ch * HEAD), kblk(r), kblk(LANE), wblk, wblk],
        out_specs=qblk(wn),
        out_shape=jax.ShapeDtypeStruct((b, t, wn), BF16),
        compiler_params=_params(("arbitrary",)),
        name="mla_attention_latent",
    )(qn, qr, lat, kr, w_uk, w_uv)


ONES_ROWS = 16


def _flash_t_init(m_scr, acc_scr):
    m_scr[...] = jnp.full(m_scr.shape, MASKED, F32)
    acc_scr[...] = jnp.zeros(acc_scr.shape, F32)


def _flash_t_stage(slot, h, s, s_scr):
    s_scr[slot, h] = s
    return jnp.max(s, axis=0, keepdims=True)


def _flash_t_step(slot, h, s_max, vt, c, s_scr, m_scr, acc_scr):
    m_old = m_scr[h]
    m_new = jnp.maximum(m_old, s_max)
    alpha = jnp.exp2((m_old - m_new) * c)
    p = jnp.exp2((s_scr[slot, h] - m_new) * c).astype(BF16)
    vt_ones = jnp.concatenate([vt, jnp.ones((ONES_ROWS, vt.shape[1]), BF16)], axis=0)
    acc_scr[h] = alpha * acc_scr[h] + jnp.dot(vt_ones, p, preferred_element_type=F32)
    m_scr[h] = m_new


def _flash_t_out(o_ref, acc_scr):
    heads, width = acc_scr.shape[0], acc_scr.shape[1] - ONES_ROWS
    ot = jnp.concatenate([acc_scr[h, :width, :] / acc_scr[h, width:width + 1, :] for h in range(heads)], axis=0)
    o_ref[...] = ot.T.astype(BF16)


def _split_heads_t(slab_t):
    row_lo = lax.broadcasted_iota(I32, (LANE, 1), 0) < HEAD
    zero = jnp.zeros_like(slab_t)
    return jnp.where(row_lo, slab_t, zero), jnp.where(row_lo, zero, slab_t)


def _dsa_t_kernel(qi_ref, wi_ref, ki_ref, qa_ref, k_ref, vt_ref, o_ref,
                  key_scr, hi_scr, lo_scr, bias_scr, thr2_scr, qit_scr, qat_scr, s_scr, m_scr, acc_scr,
                  *, tq, tk, l, past, topk):
    heads = acc_scr.shape[0]
    qpos0 = past + pl.program_id(1) * tq
    ntiles = _visible_tiles(qpos0, tq, l, tk)
    qchunk = (qpos0 + lax.broadcasted_iota(I32, (1, tq), 1)) // CHUNK
    krow = lax.broadcasted_iota(I32, (tk, 1), 0)

    for s in range(IDX_HEADS // 2):
        qit_scr[2 * s], qit_scr[2 * s + 1] = _split_heads_t(qi_ref[:, s * LANE:(s + 1) * LANE].T)
    for s in range(heads // 2):
        slab_t = (qa_ref[:, s * LANE:(s + 1) * LANE].astype(F32) * (HEAD ** -0.5)).astype(BF16).T
        qat_scr[2 * s], qat_scr[2 * s + 1] = _split_heads_t(slab_t)
    w_t = wi_ref[...].T * (HEAD ** -0.5)
    w_rows = [w_t[h:h + 1, :] for h in range(IDX_HEADS)]

    def score_body(kt, _):
        k0 = pl.multiple_of(kt * tk, tk)
        ki_tile = ki_ref[pl.ds(k0, tk), :]
        score = jnp.zeros((tk, tq), F32)
        for h in range(IDX_HEADS):
            s = jnp.dot(ki_tile, qit_scr[h], preferred_element_type=F32)
            score = score + jnp.maximum(s, 0.0) * w_rows[h]
        kpos = k0 + krow
        vis = (kpos // CHUNK <= qchunk) & (kpos < l)
        key = _ordered_key(jnp.where(vis, score, -jnp.inf))
        key_scr[kt] = key
        hi_scr[kt] = (key >> 16).astype(I16)
        lo_scr[kt] = ((key & 0xFFFF) + I16_MIN).astype(I16)
        return 0

    lax.fori_loop(0, ntiles, score_body, 0)

    def count(pred):
        def body(kt, acc):
            hit = jnp.where(pred(key_scr[kt], kt * tk), 1.0, 0.0)
            return acc + hit.reshape(tk // SUBLANE, SUBLANE, tq).sum(axis=0)
        acc = lax.fori_loop(0, ntiles, body, jnp.zeros((SUBLANE, tq), F32))
        return jnp.sum(acc, axis=0, keepdims=True)

    def count16(half_scr, cand, strict):
        rows = 2 * SUBLANE
        cand16 = cand.astype(I16)

        def body(kt, acc):
            half = half_scr[kt]
            hit = jnp.where(half > cand16 if strict else half >= cand16, jnp.int16(1), jnp.int16(0))
            parts = hit.reshape(tk // (4 * rows), 4, rows, tq)
            for g in range(parts.shape[0]):
                acc = acc + parts[g]
            return acc

        acc = lax.fori_loop(0, ntiles, body, jnp.zeros((4, rows, tq), I16))
        return acc.astype(I32).sum(axis=0).sum(axis=0, keepdims=True)

    def kth_largest16(half_scr, k_need):
        def bit_body(b, lo):
            cand = lo + jnp.left_shift(jnp.int32(1), 15 - b)
            return jnp.where(count16(half_scr, cand, False) >= k_need, cand, lo)
        return lax.fori_loop(0, 16, bit_body, jnp.full((1, tq), I16_MIN, I32))

    thr_hi = kth_largest16(hi_scr, jnp.full((1, tq), topk, I32))
    n_gt_hi = count16(hi_scr, thr_hi, True)
    thr_hi16 = thr_hi.astype(I16)

    def mark_body(kt, _):
        lo_scr[kt] = jnp.where(hi_scr[kt] == thr_hi16, lo_scr[kt], jnp.int16(I16_MIN))
        return 0

    lax.fori_loop(0, ntiles, mark_body, 0)
    thr_lo = kth_largest16(lo_scr, topk - n_gt_hi)
    thr = thr_hi * 65536 + (thr_lo - I16_MIN)

    lp = key_scr.shape[0] * tk
    n_gt = n_gt_hi + count16(lo_scr, thr_lo, True)
    n_ge = jnp.where(thr_lo > I16_MIN, n_gt_hi + count16(lo_scr, thr_lo, False), count16(hi_scr, thr_hi, False))
    excess = (n_ge > topk) & (thr > KEY_NEG_INF)
    thr2_scr[...] = jnp.zeros((1, tq), I32)

    @pl.when(jnp.max(jnp.where(excess, 1.0, 0.0)) > 0.0)
    def _():
        need = (topk - n_gt).astype(F32)
        nbits = lp.bit_length()

        def bit2_body(b, lo):
            cand = lo + jnp.left_shift(jnp.int32(1), nbits - 1 - b)
            c = count(lambda ks, base: jnp.where(ks == thr, lp - (base + krow), 0) >= cand)
            return jnp.where(c >= need, cand, lo)

        thr2_scr[...] = lax.fori_loop(0, nbits, bit2_body, jnp.zeros((1, tq), I32))

    thr2 = thr2_scr[...]

    def bias_body(kt, _):
        ks = key_scr[kt]
        tie = jnp.where(lp - (kt * tk + krow) >= thr2, 0.0, MASKED)
        bias = jnp.where(ks > thr, 0.0, jnp.where(ks == thr, tie, MASKED))
        bias_scr[kt] = jnp.where(ks > KEY_NEG_INF, bias, MASKED)
        return 0

    lax.fori_loop(0, ntiles, bias_body, 0)

    _flash_t_init(m_scr, acc_scr)

    def att_body(kt, _):
        k0 = pl.multiple_of(kt * tk, tk)
        slot = kt % 2
        s_max = []
        for h in range(heads):
            cols = slice((h // 2) * LANE, (h // 2 + 1) * LANE)
            s = jnp.dot(k_ref[pl.ds(k0, tk), cols], qat_scr[h], preferred_element_type=F32) + bias_scr[kt]
            s_max.append(_flash_t_stage(slot, h, s, s_scr))
        for h in range(heads):
            _flash_t_step(slot, h, s_max[h], vt_ref[h * HEAD:(h + 1) * HEAD, pl.ds(k0, tk)], LOG2E,
                          s_scr, m_scr, acc_scr)
        return 0

    lax.fori_loop(0, ntiles, att_body, 0)
    _flash_t_out(o_ref, acc_scr)


def _dsa_attention_t(qi, wi, ki, qa, k, vt, grp, tq):
    b, t, aw = qa.shape
    lp, tk = _key_tile(t)
    assert lp == t
    heads = aw // HEAD
    topk = min(TOPK_MAX, t // 4)
    qblk = lambda w: pl.BlockSpec((None, tq, w), lambda i, j: (i, j, 0))
    kblk = lambda w: pl.BlockSpec((None, t, w), lambda i, j: (i, 0, 0))
    return pl.pallas_call(
        functools.partial(_dsa_t_kernel, tq=tq, tk=tk, l=t, past=grp.past, topk=topk),
        grid=(b, t // tq),
        in_specs=[qblk(IDX_HEADS * HEAD), qblk(LANE), kblk(LANE), qblk(aw), kblk(aw),
                  pl.BlockSpec((aw, t), lambda i, j: (0, i))],
        out_specs=qblk(aw),
        out_shape=jax.ShapeDtypeStruct((b, t, aw), BF16),
        scratch_shapes=[pltpu.VMEM((t // tk, tk, tq), I32),
                        pltpu.VMEM((t // tk, tk, tq), I16),
                        pltpu.VMEM((t // tk, tk, tq), I16),
                        pltpu.VMEM((t // tk, tk, tq), F32),
                        pltpu.VMEM((1, tq), I32),
                        pltpu.VMEM((IDX_HEADS, LANE, tq), BF16),
                        pltpu.VMEM((heads, LANE, tq), BF16),
                        pltpu.VMEM((2, heads, tk, tq), F32),
                        pltpu.VMEM((heads, 1, tq), F32),
                        pltpu.VMEM((heads, HEAD + ONES_ROWS, tq), F32)],
        compiler_params=_params(("arbitrary", "arbitrary")),
        name="dsa_attention_t",
    )(qi, wi, ki, qa, k, vt)


def _mla_t_kernel(qn_ref, qr_ref, kn_ref, vt_ref, kr_ref, o_ref, qt_scr, s_scr, m_scr, acc_scr,
                  *, tq, tk, l, past):
    ch = acc_scr.shape[0]
    qpos0 = past + pl.program_id(1) * tq
    ntiles = _visible_tiles(qpos0, tq, l, tk)
    nfull = jnp.minimum(l, (qpos0 // CHUNK + 1) * CHUNK) // tk
    qchunk = (qpos0 + lax.broadcasted_iota(I32, (1, tq), 1)) // CHUNK
    krow = lax.broadcasted_iota(I32, (tk, 1), 0)
    c = (C_NOPE + HEAD) ** -0.5 * LOG2E
    for s in range(ch // 2):
        pair = _split_heads_t(qr_ref[:, s * LANE:(s + 1) * LANE].T)
        for half in range(2):
            h = 2 * s + half
            qt_scr[h] = jnp.concatenate([qn_ref[:, h * LANE:(h + 1) * LANE].T, pair[half]], axis=0)
    _flash_t_init(m_scr, acc_scr)

    def tile(kt, masked):
        k0 = pl.multiple_of(kt * tk, tk)
        k_rope = kr_ref[pl.ds(k0, tk), :]
        slot = kt % 2
        if masked:
            kpos = k0 + krow
            bias = jnp.where((kpos // CHUNK <= qchunk) & (kpos < l), 0.0, MASKED)
        s_max = []
        for h in range(ch):
            kcat = jnp.concatenate([kn_ref[pl.ds(k0, tk), h * LANE:(h + 1) * LANE], k_rope], axis=1)
            s = jnp.dot(kcat, qt_scr[h], preferred_element_type=F32)
            s_max.append(_flash_t_stage(slot, h, s + bias if masked else s, s_scr))
        for h in range(ch):
            _flash_t_step(slot, h, s_max[h], vt_ref[h * C_V:(h + 1) * C_V, pl.ds(k0, tk)], c,
                          s_scr, m_scr, acc_scr)
        return 0

    lax.fori_loop(0, nfull, lambda kt, _: tile(kt, False), 0)
    lax.fori_loop(nfull, ntiles, lambda kt, _: tile(kt, True), 0)
    _flash_t_out(o_ref, acc_scr)


def _mla_attention_t(qn, qr, kn, vt, kr, grp, tq):
    b, t, wn = qn.shape
    ch = wn // C_NOPE
    lp, tk = _key_tile(t)
    assert lp == t and ch % 2 == 0
    qblk = lambda w: pl.BlockSpec((None, tq, w), lambda i, j: (i, j, 0))
    kblk = lambda w: _resident((None, t, w), lambda i, j: (i, 0, 0))
    return pl.pallas_call(
        functools.partial(_mla_t_kernel, tq=tq, tk=tk, l=t, past=grp.past),
        grid=(b, t // tq),
        in_specs=[qblk(wn), qblk(ch * HEAD), kblk(wn), _resident((wn, t), lambda i, j: (0, i)), kblk(LANE)],
        out_specs=qblk(wn),
        out_shape=jax.ShapeDtypeStruct((b, t, wn), BF16),
        scratch_shapes=[pltpu.VMEM((ch, 2 * LANE, tq), BF16),
                        pltpu.VMEM((2, ch, tk, tq), F32),
                        pltpu.VMEM((ch, 1, tq), F32),
                        pltpu.VMEM((ch, C_V + ONES_ROWS, tq), F32)],
        compiler_params=_params(("arbitrary", "arbitrary")),
        name="mla_attention_t",
    )(qn, qr, kn, vt, kr)


def _causal_conv(u, e0, e1, w, seg):
    rmod = lax.broadcasted_iota(I32, (u.shape[0], 1), 0) % seg
    u1 = jnp.where(rmod == 0, e1, pltpu.roll(u, 1, 0))
    u2 = jnp.where(rmod == 0, e0, jnp.where(rmod == 1, e1, pltpu.roll(u, 2, 0)))
    return u2 * w[0:1] + u1 * w[1:2] + u * w[2:3]


def _layer_norm(z, g, b):
    mu = jnp.mean(z, axis=-1, keepdims=True)
    zc = z - mu
    var = jnp.mean(zc * zc, axis=-1, keepdims=True)
    return zc * lax.rsqrt(var + LN_EPS) * g + b


def _outproj_kernel(*refs, alpha, seq_tiles, seg, carried):
    if carried:
        (x_ref, oa_ref, bg_ref, u_ref, oc_ref, w_ref, cw_ref, g1_ref, lng_ref, lnb_ref, sc2_ref, sh2_ref,
         x1_ref, h2_ref, prev_scr) = refs

        @pl.when(pl.program_id(0) % seq_tiles == 0)
        def _():
            prev_scr[...] = jnp.zeros_like(prev_scr)

        e0, e1 = prev_scr[SUBLANE - 2:SUBLANE - 1, :], prev_scr[SUBLANE - 1:SUBLANE, :]
    else:
        (x_ref, oa_ref, bg_ref, u_ref, oc_ref, w_ref, cw_ref, g1_ref, lng_ref, lnb_ref, sc2_ref, sh2_ref,
         e0_ref, e1_ref, x1_ref, h2_ref) = refs
        e0, e1 = e0_ref[...], e1_ref[...]
    u = u_ref[...]
    yb = bg_ref[...] * _causal_conv(u, e0, e1, cw_ref[...], seg)
    if carried:
        prev_scr[...] = u[u.shape[0] - SUBLANE:, :]
    mixed = jnp.concatenate([oa_ref[...], yb.astype(BF16), oc_ref[...]], axis=1)
    tm = mixed.shape[0]
    n_split = 2 if tm % (2 * 2 * SUBLANE) == 0 else 1

    def rows_of(ref, rows):
        return ref[...] if ref.shape[0] == 1 else ref[rows, :]

    halves = [slice(c * tm // n_split, (c + 1) * tm // n_split) for c in range(n_split)]
    mixes = [jnp.dot(mixed[rows, :], w_ref[...], preferred_element_type=F32) for rows in halves]
    for rows, mix in zip(halves, mixes):
        x1 = _layer_norm(alpha * x_ref[rows, :] + (1.0 + rows_of(g1_ref, rows)) * mix, lng_ref[...], lnb_ref[...])
        x1_ref[rows, :] = x1
        h2_ref[rows, :] = (x1 * (1.0 + rows_of(sc2_ref, rows)) + rows_of(sh2_ref, rows)).astype(BF16)


def _out_projection(x, oa, bg, u, oc, w_out, layer, conv_w, g1, ln_g, ln_b, sc2, sh2, prev, grp, alpha, tm):
    m, d = x.shape
    aw, bw, cw = oa.shape[1], bg.shape[1], oc.shape[1]
    n_i = m // tm
    tiles_per_mod = n_i // g1.shape[0]
    mod_rows = g1.shape[1]
    row = lambda i: (i, 0)
    fix = lambda i: (0, 0)
    mod_spec = pl.BlockSpec((None, mod_rows, d), lambda i: (i // tiles_per_mod, 0, 0))
    in_specs = [pl.BlockSpec((tm, d), row), pl.BlockSpec((tm, aw), row), pl.BlockSpec((tm, bw), row),
                pl.BlockSpec((tm, bw), row), pl.BlockSpec((tm, cw), row),
                _resident((None, aw + bw + cw, d), lambda i: (layer, 0, 0)), pl.BlockSpec((CONV_W, bw), fix),
                mod_spec, pl.BlockSpec((1, d), fix), pl.BlockSpec((1, d), fix), mod_spec, mod_spec]
    args = [x, oa, bg, u, oc, w_out, conv_w, g1, ln_g.reshape(1, d), ln_b.reshape(1, d), sc2, sh2]
    scratch = []
    if prev is None:
        assert grp.t % tm == 0
        scratch = [pltpu.VMEM((SUBLANE, bw), F32)]
    else:
        assert tm % grp.t == 0
        in_specs += [pl.BlockSpec((tm, bw), row), pl.BlockSpec((tm, bw), row)]
        args += [prev[0], prev[1]]
    return pl.pallas_call(
        functools.partial(_outproj_kernel, alpha=alpha, seq_tiles=max(grp.t // tm, 1), seg=min(grp.t, tm),
                          carried=prev is None),
        grid=(n_i,),
        in_specs=in_specs,
        out_specs=[pl.BlockSpec((tm, d), row), pl.BlockSpec((tm, d), row)],
        out_shape=[jax.ShapeDtypeStruct((m, d), F32), jax.ShapeDtypeStruct((m, d), BF16)],
        scratch_shapes=scratch,
        compiler_params=_params(("arbitrary",)),
        name="out_projection",
    )(*args)


def _ffn_kernel(*refs, alpha, seq_tiles, seg, carried):
    if carried:
        (h_ref, x_ref, wg_ref, wu_ref, wd_ref, cw_ref, g2_ref, lng_ref, lnb_ref,
         o_ref, gt_ref, acc_scr, prev_scr) = refs
    else:
        (h_ref, x_ref, wg_ref, wu_ref, wd_ref, cw_ref, g2_ref, lng_ref, lnb_ref, e0_ref, e1_ref,
         o_ref, gt_ref, acc_scr) = refs
    f = pl.program_id(1)
    if carried:
        @pl.when(pl.program_id(0) % seq_tiles == 0)
        def _():
            prev_scr[f] = jnp.zeros(prev_scr.shape[1:], F32)

    @pl.when(f == 0)
    def _():
        acc_scr[...] = jnp.zeros_like(acc_scr)

    h = h_ref[...]
    tm, tf = h.shape[0], wg_ref.shape[1]
    n_split = 2 if tf % (2 * LANE) == 0 else 1
    halves = [slice(c * tf // n_split, (c + 1) * tf // n_split) for c in range(n_split)]
    gates = [jnp.dot(h, wg_ref[:, cols], preferred_element_type=F32) for cols in halves]
    ups = [jnp.dot(h, wu_ref[:, cols], preferred_element_type=F32) for cols in halves]
    for cols, gate, up in zip(halves, gates, ups):
        if carried:
            e0, e1 = prev_scr[f, SUBLANE - 2:SUBLANE - 1, cols], prev_scr[f, SUBLANE - 1:SUBLANE, cols]
        else:
            e0, e1 = e0_ref[:, cols], e1_ref[:, cols]
        conv = _causal_conv(gate, e0, e1, cw_ref[:, cols], seg)
        if carried:
            prev_scr[f, :, cols] = gate[tm - SUBLANE:, :]
        gt_ref[:, cols] = gate[tm - gt_ref.shape[0]:, :]
        act = (jax.nn.silu(conv) * up).astype(BF16)
        acc_scr[...] = jnp.dot(act, wd_ref[cols, :], preferred_element_type=F32) + acc_scr[...]

    @pl.when(f == pl.num_programs(1) - 1)
    def _():
        z = alpha * x_ref[...] + (1.0 + g2_ref[...]) * acc_scr[...]
        o_ref[...] = _layer_norm(z, lng_ref[...], lnb_ref[...])


def _channel_mixer(h2, x1, w_gu, w_down, layer, conv_w, g2, ln_g, ln_b, prev, grp, alpha, tm, tf):
    m, d = x1.shape
    dff = w_down.shape[1]
    n_i, n_f = m // tm, dff // tf
    tiles_per_mod = n_i // g2.shape[0]
    mod_rows = g2.shape[1]
    fix = lambda i, f: (0, 0)
    in_specs = [pl.BlockSpec((tm, d), lambda i, f: (i, 0)), pl.BlockSpec((tm, d), lambda i, f: (i, 0)),
                pl.BlockSpec((None, d, tf), lambda i, f: (layer, 0, f)),
                pl.BlockSpec((None, d, tf), lambda i, f: (layer, 0, n_f + f)),
                pl.BlockSpec((None, tf, d), lambda i, f: (layer, f, 0)),
                pl.BlockSpec((CONV_W, tf), lambda i, f: (0, f)),
                pl.BlockSpec((None, mod_rows, d), lambda i, f: (i // tiles_per_mod, 0, 0)),
                pl.BlockSpec((1, d), fix), pl.BlockSpec((1, d), fix)]
    args = [h2, x1, w_gu, w_gu, w_down, conv_w, g2, ln_g.reshape(1, d), ln_b.reshape(1, d)]
    scratch = [pltpu.VMEM((tm, d), F32)]
    if prev is None:
        assert grp.t % tm == 0
        scratch.append(pltpu.VMEM((n_f, SUBLANE, tf), F32))
        gt_spec = pl.BlockSpec((None, SUBLANE, tf), lambda i, f: (i, 0, f))
        gt_shape = jax.ShapeDtypeStruct((n_i, SUBLANE, dff), F32)
    else:
        assert tm % grp.t == 0
        in_specs += [pl.BlockSpec((tm, tf), lambda i, f: (i, f))] * 2
        args += [prev[0], prev[1]]
        gt_spec = pl.BlockSpec((tm, tf), lambda i, f: (i, f))
        gt_shape = jax.ShapeDtypeStruct((m, dff), F32)
    return pl.pallas_call(
        functools.partial(_ffn_kernel, alpha=alpha, seq_tiles=max(grp.t // tm, 1), seg=min(grp.t, tm),
                          carried=prev is None),
        grid=(n_i, n_f),
        in_specs=in_specs,
        out_specs=[pl.BlockSpec((tm, d), lambda i, f: (i, 0)), gt_spec],
        out_shape=[jax.ShapeDtypeStruct((m, d), F32), gt_shape],
        scratch_shapes=scratch,
        compiler_params=_params(("arbitrary", "arbitrary")),
        name="channel_mixer",
    )(*args)


def _rope_tables(pos):
    half = HEAD // 2
    inv = jnp.power(jnp.float32(ROPE_THETA), -jnp.arange(half, dtype=F32) / half)
    ang = pos.astype(F32)[:, None] * inv[None, :]
    cos, sin = jnp.cos(ang), jnp.sin(ang)
    zero = jnp.zeros_like(sin)
    reps = LANE // HEAD
    return (jnp.tile(jnp.concatenate([cos, cos], axis=1), (1, reps)),
            jnp.tile(jnp.concatenate([zero, sin], axis=1), (1, reps)),
            jnp.tile(jnp.concatenate([-sin, zero], axis=1), (1, reps)))


def _with_past(past, new, lp, twice=False):
    b, t, w = new.shape
    parts = [new]
    if past is not None:
        p = past.reshape(b, past.shape[1], -1).astype(BF16)
        parts = [jnp.concatenate([p, p], axis=-1) if twice else p, new]
    n = sum(a.shape[1] for a in parts)
    if lp > n:
        parts.append(jnp.zeros((b, lp - n, w), BF16))
    return parts[0] if len(parts) == 1 else jnp.concatenate(parts, axis=1)


def _layer(x, mod, tables, grp, past, layer, big, small, stacks, dm, alpha):
    w_in_p, w_out, w_uk, w_uv, w_uv_t, w_gu, w_down = big
    conv_b_w, kv_norm, ln1_g, ln1_b, ln2_g, ln2_b, conv_f_w = small
    b, t = grp.b, grp.t
    m, d = x.shape
    carried = past is None
    tm = _row_tile(t, 256) if carried else m
    if carried:
        mods = [a.reshape(b, 1, d) for a in jnp.split(mod, N_MOD, axis=-1)]
    else:
        mods = [jnp.repeat(a, t, axis=0).reshape(1, m, d) for a in jnp.split(mod, N_MOD, axis=-1)]
    sh1, sc1, g1, sh2, sc2, g2 = mods

    outs = _in_projection(x, sc1, sh1, w_in_p, layer, tables, kv_norm, stacks, dm, tm)
    stacks = tuple(outs[o] for o in STATE_OUTPUTS)
    qa, _, kab, qi, qcr, _, kib, _, krb, _, vab, bg, u, qcn, _, latb, wi, vat = outs

    three = lambda a: a.reshape(b, t, a.shape[-1])
    if carried:
        tq = _row_tile(t, 256)
        kn, vct = _kv_up(latb, w_uk, w_uv_t, layer)
        oa = _dsa_attention_t(three(qi), three(wi), three(kib), three(qa), three(kab), vat, grp, tq)
        oc = _mla_attention_t(three(qcn), three(qcr), three(kn), vct, three(krb), grp, tq)
        prev_b = prev_f = None
    else:
        l = grp.past + t
        lp, _ = _key_tile(l)
        p_ak, p_av, p_ik, p_lat, p_kr, prev_b, prev_f = past
        k_all = _with_past(p_ak, three(kab), lp)
        v_all = _with_past(p_av, three(vab), lp)
        ki_all = _with_past(p_ik, three(kib), lp, twice=True)
        kr_all = _with_past(p_kr, three(krb), lp, twice=True)
        lat_all = _with_past(p_lat, three(latb), lp)
        oa = _dsa_attention(three(qi), three(wi), ki_all, three(qa), k_all, v_all, grp, l, t)
        oc = _mla_attention_latent(three(qcn), three(qcr), lat_all, kr_all, w_uk, w_uv, layer, grp, l)

    def expand(state):
        return jnp.repeat(state[:, 0], t, axis=0), jnp.repeat(state[:, 1], t, axis=0)

    x1, h2 = _out_projection(x, oa.reshape(m, -1), bg, u, oc.reshape(m, -1), w_out, layer, conv_b_w, g1,
                             ln1_g, ln1_b, sc2, sh2, None if carried else expand(prev_b), grp, alpha, tm)
    tm_f = _row_tile(t, 512) if carried else m
    tf = _row_tile(dm.dff, 512)
    x2, gate_rows = _channel_mixer(h2, x1, w_gu, w_down, layer, conv_f_w, g2, ln2_g, ln2_b,
                                   None if carried else expand(prev_f), grp, alpha, tm_f, tf)
    if carried:
        new_f = gate_rows.reshape(b, t // tm_f, SUBLANE, dm.dff)[:, -1, SUBLANE - (CONV_W - 1):, :]
    else:
        new_f = gate_rows.reshape(b, t, dm.dff)[:, t - (CONV_W - 1):, :]
    new_b = u.reshape(b, t, dm.bw)[:, t - (CONV_W - 1):, :]
    return x2, stacks, (new_b, new_f)


def _state_outputs(stacks, conv_rows, grp, dm):
    ka, ki, kr, va, lat = stacks
    b, t, heads = grp.b, grp.t, dm.aw // HEAD
    new_b, new_f = [jnp.stack(r) for r in zip(*conv_rows)]
    return (ka.reshape(dm.depth, b, t, heads, HEAD), va.reshape(dm.depth, b, t, heads, HEAD),
            ki.reshape(dm.depth, b, t, HEAD), lat.reshape(dm.depth, b, t, dm.r), kr.reshape(dm.depth, b, t, HEAD),
            new_b, new_f)


def kernel(x_prompt, x_sample, c_prompt, c_sample, cache_a_k, cache_a_v, cache_idx_k, cache_mla_latent,
           cache_mla_krope, state_conv_b, state_conv_ffn, w_in, w_out, conv_b_w, mla_kv_norm, mla_w_uk,
           mla_w_uv, w_mod, b_mod, ln1_g, ln1_b, ln2_g, ln2_b, ffn_w_gu, ffn_conv_w, ffn_w_down):
    depth, d, _ = w_in.shape
    a_heads = cache_a_k.shape[3]
    dm = Dims(d=d, aw=a_heads * HEAD, bw=conv_b_w.shape[2], ch=mla_w_uk.shape[2] // C_NOPE,
              r=mla_w_uk.shape[1], dff=ffn_w_down.shape[1], depth=depth)
    alpha = (2 * depth) ** 0.25
    grp_p = Group(b=x_prompt.shape[0], t=x_prompt.shape[1], past=0)
    grp_s = Group(b=x_sample.shape[0], t=x_sample.shape[1], past=cache_a_k.shape[2])

    n_c = grp_p.b + grp_s.b
    c_all = jnp.concatenate([c_prompt, c_sample, jnp.zeros((-n_c % SUBLANE, d), F32)], axis=0)
    mod = _modulation(c_all, w_mod, b_mod)

    tab_p = _rope_tables(jnp.arange(grp_p.t, dtype=I32))
    tab_s = tuple(jnp.tile(a, (grp_s.b, 1)) for a in _rope_tables(grp_s.past + jnp.arange(grp_s.t, dtype=I32)))

    xp = x_prompt.reshape(grp_p.b * grp_p.t, d)
    xs = x_sample.reshape(grp_s.b * grp_s.t, d)
    conv_p, conv_s = [], []
    stacks_p = stacks_s = None
    w_uv_b = mla_w_uv.astype(BF16)
    big = (_pack_w_in(w_in, dm), w_out.astype(BF16), mla_w_uk.astype(BF16), w_uv_b, jnp.swapaxes(w_uv_b, 1, 2),
           ffn_w_gu.astype(BF16), ffn_w_down.astype(BF16))
    for l in range(depth):
        small = (conv_b_w[l], mla_kv_norm[l], ln1_g[l], ln1_b[l], ln2_g[l], ln2_b[l], ffn_conv_w[l])
        xp, stacks_p, rp = _layer(xp, mod[l, :grp_p.b], tab_p, grp_p, None, l, big, small, stacks_p, dm, alpha)
        past_l = (cache_a_k[l], cache_a_v[l], cache_idx_k[l], cache_mla_latent[l], cache_mla_krope[l],
                  state_conv_b[l], state_conv_ffn[l])
        xs, stacks_s, rs = _layer(xs, mod[l, grp_p.b:n_c], tab_s, grp_s, past_l, l, big, small, stacks_s, dm,
                                  alpha)
        conv_p.append(rp)
        conv_s.append(rs)
    return (xp.reshape(x_prompt.shape), xs.reshape(x_sample.shape),
            *_state_outputs(stacks_p, conv_p, grp_p, dm), *_state_outputs(stacks_s, conv_s, grp_s, dm))
```

```python
import functools
from typing import NamedTuple

import numpy as np
import jax
import jax.numpy as jnp
from jax import lax
from jax.experimental import pallas as pl
from jax.experimental.pallas import tpu as pltpu

F32, BF16, I32, I16 = jnp.float32, jnp.bfloat16, jnp.int32, jnp.int16

CHUNK = 64
CONV_W = 3
ROPE_THETA = 10000.0
HEAD = 64
IDX_HEADS = 16
TOPK_MAX = 256
C_NOPE = 128
C_V = 128
N_MOD = 6
LN_EPS = 1e-5
RMS_EPS = 1e-6

LANE = 128
SUBLANE = 8
VMEM_LIMIT = 50 * 1024 * 1024

MASKED = -1e30
LOG2E = 1.4426950408889634
INT_MIN = -2 ** 31
I16_MIN = -2 ** 15
KEY_NEG_INF = int(np.array(-np.inf, np.float32).view(np.int32)) ^ 0x7FFFFFFF


class Dims(NamedTuple):
    d: int
    aw: int
    bw: int
    ch: int
    r: int
    dff: int
    depth: int


class Group(NamedTuple):
    b: int
    t: int
    past: int


def _row_tile(m, pref):
    if m <= pref:
        return m
    t = pref - pref % SUBLANE
    while m % t:
        t -= SUBLANE
    return t


def _key_tile(l):
    lp = -(-l // LANE) * LANE
    for tk in (512, 384, 256, 128):
        if lp % tk == 0:
            return lp, tk
    raise AssertionError(lp)


def _params(sem):
    return pltpu.CompilerParams(dimension_semantics=sem, vmem_limit_bytes=VMEM_LIMIT)


def _resident(shape, index_map):
    return pl.BlockSpec(shape, index_map, pipeline_mode=pl.Buffered(1))


def _mod_kernel(c_ref, w_ref, b_ref, o_ref):
    a = jax.nn.silu(c_ref[...]).astype(BF16)
    o_ref[...] = jnp.dot(a, w_ref[...].astype(BF16), preferred_element_type=F32) + b_ref[...]


def _modulation(c, w_mod, b_mod):
    depth, d, n = w_mod.shape
    rows = c.shape[0]
    tn = _row_tile(n, 1024)
    return pl.pallas_call(
        _mod_kernel,
        grid=(depth, n // tn),
        in_specs=[pl.BlockSpec((rows, d), lambda l, j: (0, 0)),
                  pl.BlockSpec((None, d, tn), lambda l, j: (l, 0, j)),
                  pl.BlockSpec((None, 1, tn), lambda l, j: (l, 0, j))],
        out_specs=pl.BlockSpec((None, rows, tn), lambda l, j: (l, 0, j)),
        out_shape=jax.ShapeDtypeStruct((depth, rows, n), F32),
        compiler_params=_params(("arbitrary", "arbitrary")),
        name="modulation",
    )(c, w_mod, b_mod.reshape(depth, 1, n))


def _pack_w_in(w, dm):
    d, aw, bw, ch, r = dm.d, dm.aw, dm.bw, dm.ch, dm.r
    lead = w.shape[:-1]
    o = np.cumsum([0, aw, aw, aw, IDX_HEADS * HEAD, HEAD, IDX_HEADS, bw, bw, bw, ch * (C_NOPE + HEAD), r, HEAD])
    qa, ka, va, qi, ki, wi, bg, cg, xb, qc, lat, kr = [w[..., o[i]:o[i + 1]] for i in range(12)]
    qc = qc.reshape(lead + (ch, C_NOPE + HEAD))
    qcn = qc[..., :C_NOPE].reshape(lead + (ch * C_NOPE,))
    qcr = qc[..., C_NOPE:].reshape(lead + (ch * HEAD,))
    pad = jnp.zeros(lead + (LANE - IDX_HEADS,), w.dtype)
    return jnp.concatenate([qa, ka, qi, qcr, ki, ki, kr, kr, va, bg, cg, xb, qcn, lat, wi, pad],
                           axis=-1).astype(BF16)


def _rope(acc, cos, s1, s2):
    outs = []
    for s in range(acc.shape[1] // LANE):
        xs = acc[:, s * LANE:(s + 1) * LANE]
        outs.append(xs * cos + pltpu.roll(xs, HEAD // 2, 1) * s1 + pltpu.roll(xs, LANE - HEAD // 2, 1) * s2)
    return outs[0] if len(outs) == 1 else jnp.concatenate(outs, axis=1)


def _inproj_kernel(x_ref, sc_ref, sh_ref, w_ref, cos_ref, s1_ref, s2_ref, nrm_ref, *rest, dm, n_alias):
    (qa_ref, ka_ref, kab_ref, qi_ref, qcr_ref, ki_ref, kib_ref, kr_ref, krb_ref,
     va_ref, vab_ref, bg_ref, u_ref, qcn_ref, lat_ref, latb_ref, wi_ref, vat_ref) = rest[n_alias:]
    aw, bw, ch, r = dm.aw, dm.bw, dm.ch, dm.r
    h = (x_ref[...] * (1.0 + sc_ref[...]) + sh_ref[...]).astype(BF16)
    cos, s1, s2 = cos_ref[...], s1_ref[...], s2_ref[...]
    col = [0]

    def proj(width):
        c0 = col[0]
        col[0] = c0 + width
        return jnp.dot(h, w_ref[:, c0:c0 + width], preferred_element_type=F32)

    def pieces(width, step=512):
        return [(o, min(step, width - o)) for o in range(0, width, step)]

    for o, wd in pieces(aw):
        qa_ref[:, o:o + wd] = _rope(proj(wd), cos, s1, s2).astype(BF16)
    for o, wd in pieces(aw):
        y = _rope(proj(wd), cos, s1, s2)
        ka_ref[:, o // HEAD:(o + wd) // HEAD, :] = y.reshape(y.shape[0], wd // HEAD, HEAD)
        kab_ref[:, o:o + wd] = y.astype(BF16)
    for o, wd in pieces(IDX_HEADS * HEAD):
        qi_ref[:, o:o + wd] = _rope(proj(wd), cos, s1, s2).astype(BF16)
    for o, wd in pieces(ch * HEAD):
        qcr_ref[:, o:o + wd] = _rope(proj(wd), cos, s1, s2).astype(BF16)
    y = _rope(proj(2 * LANE), cos, s1, s2)
    for c, (f32_ref, b16_ref) in enumerate(((ki_ref, kib_ref), (kr_ref, krb_ref))):
        f32_ref[...] = y[:, c * LANE:c * LANE + HEAD]
        b16_ref[...] = y[:, c * LANE:(c + 1) * LANE].astype(BF16)
    for o, wd in pieces(aw):
        y = proj(wd)
        va_ref[:, o // HEAD:(o + wd) // HEAD, :] = y.reshape(y.shape[0], wd // HEAD, HEAD)
        vab_ref[:, o:o + wd] = y.astype(BF16)
        vat_ref[o:o + wd, :] = y.T.astype(BF16)
    for o, wd in pieces(bw):
        bg_ref[:, o:o + wd] = proj(wd)
    c_cg = col[0]
    for o, wd in pieces(bw):
        cg = jnp.dot(h, w_ref[:, c_cg + o:c_cg + o + wd], preferred_element_type=F32)
        xb = jnp.dot(h, w_ref[:, c_cg + bw + o:c_cg + bw + o + wd], preferred_element_type=F32)
        u_ref[:, o:o + wd] = cg * xb
    col[0] = c_cg + 2 * bw
    for o, wd in pieces(ch * C_NOPE):
        qcn_ref[:, o:o + wd] = proj(wd).astype(BF16)
    lat = proj(r)
    lat = lat * lax.rsqrt(jnp.mean(lat * lat, axis=-1, keepdims=True) + RMS_EPS) * nrm_ref[...]
    lat_ref[...] = lat
    latb_ref[...] = lat.astype(BF16)
    wi_ref[...] = proj(LANE) * (IDX_HEADS ** -0.5)


STATE_OUTPUTS = (1, 5, 7, 9, 14)
HEAD_OUTPUTS = (1, 9)


def _in_projection(x, sc, sh, w_packed, layer, tables, kv_norm, stacks, dm, tm):
    m, d = x.shape
    aw, bw, ch, r = dm.aw, dm.bw, dm.ch, dm.r
    npk = w_packed.shape[2]
    n_i = m // tm
    tiles_per_mod = n_i // sc.shape[0]
    mod_rows = sc.shape[1]
    tab_tiles = tables[0].shape[0] // tm
    widths = [(aw, BF16), (aw, F32), (aw, BF16), (IDX_HEADS * HEAD, BF16), (ch * HEAD, BF16),
              (HEAD, F32), (LANE, BF16), (HEAD, F32), (LANE, BF16),
              (aw, F32), (aw, BF16), (bw, F32), (bw, F32), (ch * C_NOPE, BF16), (r, F32), (r, BF16),
              (LANE, F32)]
    row = lambda i: (i, 0)
    mod_spec = pl.BlockSpec((None, mod_rows, d), lambda i: (i // tiles_per_mod, 0, 0))
    tab_spec = pl.BlockSpec((tm, LANE), lambda i: (i % tab_tiles, 0))
    in_specs = [pl.BlockSpec((tm, d), row), mod_spec, mod_spec,
                _resident((None, d, npk), lambda i: (layer, 0, 0)),
                tab_spec, tab_spec, tab_spec,
                pl.BlockSpec((1, r), lambda i: (0, 0))]
    args = [x, sc, sh, w_packed, *tables, kv_norm.reshape(1, r)]
    out_specs = [pl.BlockSpec((tm, w), row) for w, _ in widths] + [pl.BlockSpec((aw, tm), lambda i: (0, i))]
    out_shape = [jax.ShapeDtypeStruct((m, w), dt) for w, dt in widths] + [jax.ShapeDtypeStruct((aw, m), BF16)]
    for o in STATE_OUTPUTS:
        w, dt = widths[o]
        tail = (w // HEAD, HEAD) if o in HEAD_OUTPUTS else (w,)
        out_specs[o] = pl.BlockSpec((None, tm) + tail, lambda i, n=len(tail): (layer, i) + (0,) * n)
        out_shape[o] = jax.ShapeDtypeStruct((dm.depth, m) + tail, dt)
    aliases = {}
    if stacks is not None:
        aliases = {len(args) + k: o for k, o in enumerate(STATE_OUTPUTS)}
        in_specs += [pl.BlockSpec(memory_space=pl.ANY)] * len(stacks)
        args += list(stacks)
    return pl.pallas_call(
        functools.partial(_inproj_kernel, dm=dm, n_alias=len(aliases)),
        grid=(n_i,),
        in_specs=in_specs,
        out_specs=out_specs,
        out_shape=out_shape,
        input_output_aliases=aliases,
        compiler_params=_params(("arbitrary",)),
        name="in_projection",
    )(*args)


def _kvup_kernel(l_ref, wk_ref, wvt_ref, k_ref, vt_ref):
    lat = l_ref[...]
    k_ref[...] = jnp.dot(lat, wk_ref[...], preferred_element_type=F32).astype(BF16)
    vt_ref[...] = lax.dot_general(wvt_ref[...], lat, _NT, preferred_element_type=F32).astype(BF16)


def _kv_up(lat, w_uk, w_uv_t, layer):
    m, r = lat.shape
    n = w_uk.shape[2]
    tm = _row_tile(m, 512)
    return pl.pallas_call(
        _kvup_kernel,
        grid=(m // tm,),
        in_specs=[pl.BlockSpec((tm, r), lambda i: (i, 0)),
                  pl.BlockSpec((None, r, n), lambda i: (layer, 0, 0)),
                  pl.BlockSpec((None, n, r), lambda i: (layer, 0, 0))],
        out_specs=[pl.BlockSpec((tm, n), lambda i: (i, 0)), pl.BlockSpec((n, tm), lambda i: (0, i))],
        out_shape=[jax.ShapeDtypeStruct((m, n), BF16), jax.ShapeDtypeStruct((n, m), BF16)],
        compiler_params=_params(("arbitrary",)),
        name="latent_up_projection",
    )(lat, w_uk, w_uv_t)


_NT = (((1,), (1,)), ((), ()))


def _visible_tiles(qpos0, tq, l, tk):
    nvis = jnp.minimum(l, ((qpos0 + tq - 1) // CHUNK + 1) * CHUNK)
    return (nvis + tk - 1) // tk


def _flash_step(carry, s, v):
    m, l, acc = carry
    m_new = jnp.maximum(m, jnp.max(s, axis=1, keepdims=True))
    alpha = jnp.exp(m - m_new)
    p = jnp.exp(s - m_new)
    l = alpha * l + jnp.sum(p, axis=1, keepdims=True)
    acc = alpha * acc + jnp.dot(p.astype(BF16), v, preferred_element_type=F32)
    return m_new, l, acc


def _flash_init(rows, width):
    return (jnp.full((rows, 1), MASKED, F32), jnp.zeros((rows, 1), F32), jnp.zeros((rows, width), F32))


def _ordered_key(x):
    b = pltpu.bitcast(x, I32)
    return jnp.where(b < 0, b ^ 0x7FFFFFFF, b)


def _dsa_kernel(qi_ref, wi_ref, qa_ref, kip_ref, kp_ref, vp_ref, kin_ref, kn_ref, vn_ref, o_ref,
                key_scr, bias_scr, thr2_scr, *, tq, tk, past, topk, aw):
    l = past + tq
    tn = kn_ref.shape[0]
    span = past + tn
    tiles = [(kip_ref, kp_ref, vp_ref, r0, tk, r0) for r0 in range(0, past, tk)]
    tiles.append((kin_ref, kn_ref, vn_ref, 0, tn, past))
    qchunk = (past + lax.broadcasted_iota(I32, (tq, 1), 0)) // CHUNK
    lane = lax.broadcasted_iota(I32, (1, LANE), 1)
    lo_half = lane < HEAD
    kf = jnp.float32(topk)

    def split_heads(qs):
        zero = jnp.zeros_like(qs)
        return jnp.concatenate([jnp.where(lo_half, qs, zero), jnp.where(lo_half, zero, qs)], axis=0)

    wi = wi_ref[...] * (HEAD ** -0.5)
    q_idx = jnp.concatenate([split_heads(qi_ref[:, s * LANE:(s + 1) * LANE]) for s in range(IDX_HEADS // 2)],
                            axis=0)
    w_idx = [wi[:, h:h + 1] for h in range(IDX_HEADS)]

    for kt, (ki_ref, _, _, r0, width, pos0) in enumerate(tiles):
        s_all = lax.dot_general(q_idx, ki_ref[r0:r0 + width, :], _NT, preferred_element_type=F32)
        score = jnp.zeros((tq, width), F32)
        for h in range(IDX_HEADS):
            score = score + jnp.maximum(s_all[h * tq:(h + 1) * tq], 0.0) * w_idx[h]
        kpos = pos0 + lax.broadcasted_iota(I32, (1, width), 1)
        vis = (kpos // CHUNK <= qchunk) & (kpos < l)
        key_scr[kt, :, :width] = _ordered_key(jnp.where(vis, score, -jnp.inf))

    def slabs():
        return [(kt, j, pos0 + j) for kt, (_, _, _, _, width, pos0) in enumerate(tiles)
                for j in range(0, width, LANE)]

    def count(pred):
        acc = jnp.zeros((tq, LANE), F32)
        for kt, j, pos in slabs():
            acc = acc + jnp.where(pred(key_scr[kt, :, j:j + LANE], pos), 1.0, 0.0)
        return jnp.sum(acc, axis=1, keepdims=True)

    def wide(col):
        return jnp.broadcast_to(col, (tq, LANE))

    def bit_body(b, lo):
        cand = lo + jnp.left_shift(jnp.int32(1), 31 - b)
        cand_w = wide(cand)
        c = count(lambda ks, _: ks >= cand_w)
        return jnp.where(c >= kf, cand, lo)

    thr = lax.fori_loop(0, 32, bit_body, jnp.full((tq, 1), INT_MIN, I32))
    thr_w = wide(thr)

    n_ge = count(lambda ks, _: ks >= thr_w)
    n_gt = count(lambda ks, _: ks > thr_w)
    excess = (n_ge > kf) & (thr > KEY_NEG_INF)
    thr2_scr[...] = jnp.zeros((tq, 1), I32)

    @pl.when(jnp.max(jnp.where(excess, 1.0, 0.0)) > 0.0)
    def _():
        need = kf - n_gt
        nbits = span.bit_length()

        def bit2_body(b, lo):
            cand = lo + jnp.left_shift(jnp.int32(1), nbits - 1 - b)
            cand_w = wide(cand)
            c = count(lambda ks, pos: jnp.where(ks == thr_w, span - (pos + lane), 0) >= cand_w)
            return jnp.where(c >= need, cand, lo)

        thr2_scr[...] = lax.fori_loop(0, nbits, bit2_body, jnp.zeros((tq, 1), I32))

    thr2_w = wide(thr2_scr[...])

    for kt, j, pos in slabs():
        ks = key_scr[kt, :, j:j + LANE]
        tie = jnp.where(span - (pos + lane) >= thr2_w, 0.0, MASKED)
        bias = jnp.where(ks > thr_w, 0.0, jnp.where(ks == thr_w, tie, MASKED))
        bias_scr[kt, :, j:j + LANE] = jnp.where(ks > KEY_NEG_INF, bias, MASKED)

    for pr in range(aw // LANE):
        cols = slice(pr * LANE, (pr + 1) * LANE)
        q2 = split_heads(qa_ref[:, cols])
        carry = _flash_init(2 * tq, LANE)
        for kt, (_, k_ref, v_ref, r0, width, _) in enumerate(tiles):
            s = lax.dot_general(q2, k_ref[r0:r0 + width, cols], _NT, preferred_element_type=F32)
            bias = bias_scr[kt, :, :width]
            s = s * (HEAD ** -0.5) + jnp.concatenate([bias, bias], axis=0)
            carry = _flash_step(carry, s, v_ref[r0:r0 + width, cols])
        _, den, acc = carry
        o2 = acc / den
        o_ref[:, cols] = jnp.where(lo_half, o2[:tq], o2[tq:]).astype(BF16)


def _dsa_attention(qi, wi, qa, ki_cache, k_cache, v_cache, ki_new, k_new, v_new, layer, grp):
    b, t, aw = qa.shape
    past, tn = k_cache.shape[2], k_new.shape[1]
    assert past % LANE == 0 and tn % LANE == 0
    tk = next(c for c in (512, 384, 256, 128) if past % c == 0)
    width = max(tk, tn)
    topk = min(TOPK_MAX, (past + t) // 4)
    qblk = lambda w: pl.BlockSpec((None, t, w), lambda i: (i, 0, 0))
    cblk = lambda w: pl.BlockSpec((None, None, past, w), lambda i: (layer, i, 0, 0))
    nblk = lambda w: pl.BlockSpec((None, tn, w), lambda i: (i, 0, 0))
    return pl.pallas_call(
        functools.partial(_dsa_kernel, tq=t, tk=tk, past=past, topk=topk, aw=aw),
        grid=(b,),
        in_specs=[qblk(IDX_HEADS * HEAD), qblk(LANE), qblk(aw), cblk(LANE), cblk(aw), cblk(aw),
                  nblk(LANE), nblk(aw), nblk(aw)],
        out_specs=qblk(aw),
        out_shape=jax.ShapeDtypeStruct((b, t, aw), BF16),
        scratch_shapes=[pltpu.VMEM((past // tk + 1, t, width), I32),
                        pltpu.VMEM((past // tk + 1, t, width), F32),
                        pltpu.VMEM((t, 1), I32)],
        compiler_params=_params(("arbitrary",)),
        name="dsa_attention",
    )(qi, wi, qa, ki_cache, k_cache, v_cache, ki_new, k_new, v_new)


def _mla_latent_kernel(qn_ref, qr_ref, latp_ref, krp_ref, latn_ref, krn_ref, wuk_ref, wuv_ref, o_ref,
                       *, t, l, past, ch):
    lo_half = lax.broadcasted_iota(I32, (1, LANE), 1) < HEAD
    scale = (C_NOPE + HEAD) ** -0.5
    q_lat, q_rope = [], []
    for h in range(ch):
        cols = slice(h * C_NOPE, (h + 1) * C_NOPE)
        q_lat.append(lax.dot_general(qn_ref[:, cols], wuk_ref[:, cols], _NT, preferred_element_type=F32))
        qs = qr_ref[:, (h // 2) * LANE:(h // 2 + 1) * LANE]
        zero = jnp.zeros_like(qs)
        q_rope.append(jnp.where(lo_half, qs, zero) if h % 2 == 0 else jnp.where(lo_half, zero, qs))
    q_lat = jnp.concatenate(q_lat, axis=0).astype(BF16)
    q_rope = jnp.concatenate(q_rope, axis=0)
    qchunk = jnp.concatenate([(past + lax.broadcasted_iota(I32, (t, 1), 0)) // CHUNK] * ch, axis=0)
    kr_past = krp_ref[...].astype(BF16)
    segments = ((latp_ref[...].astype(BF16), jnp.concatenate([kr_past, kr_past], axis=1), 0, past),
                (latn_ref[...], krn_ref[...], past, l))
    scores = []
    for lat, kr, pos0, pos_end in segments:
        s = (lax.dot_general(q_lat, lat, _NT, preferred_element_type=F32)
             + lax.dot_general(q_rope, kr, _NT, preferred_element_type=F32)) * scale
        kpos = pos0 + lax.broadcasted_iota(I32, (1, lat.shape[0]), 1)
        scores.append(jnp.where((kpos // CHUNK <= qchunk) & (kpos < pos_end), s, MASKED))
    m = functools.reduce(jnp.maximum, [jnp.max(s, axis=1, keepdims=True) for s in scores])
    ps = [jnp.exp(s - m) for s in scores]
    den = sum(jnp.sum(p, axis=1, keepdims=True) for p in ps)
    acc = sum(jnp.dot(p.astype(BF16), seg[0], preferred_element_type=F32) for p, seg in zip(ps, segments))
    o_lat = (acc / den).astype(BF16)
    for h in range(ch):
        cols = slice(h * C_V, (h + 1) * C_V)
        o_ref[:, cols] = jnp.dot(o_lat[h * t:(h + 1) * t, :], wuv_ref[:, cols],
                                 preferred_element_type=F32).astype(BF16)


def _mla_attention_latent(qn, qr, lat_cache, kr_cache, lat_new, kr_new, w_uk, w_uv, layer, grp):
    b, t, wn = qn.shape
    ch = wn // C_NOPE
    past, r = lat_cache.shape[2:]
    tn = lat_new.shape[1]
    qblk = lambda w: pl.BlockSpec((None, t, w), lambda i: (i, 0, 0))
    cblk = lambda w: pl.BlockSpec((None, None, past, w), lambda i: (layer, i, 0, 0))
    nblk = lambda w: pl.BlockSpec((None, tn, w), lambda i: (i, 0, 0))
    wblk = pl.BlockSpec((None, r, wn), lambda i: (layer, 0, 0))
    return pl.pallas_call(
        functools.partial(_mla_latent_kernel, t=t, l=past + t, past=past, ch=ch),
        grid=(b,),
        in_specs=[qblk(wn), qblk(ch * HEAD), cblk(r), cblk(HEAD), nblk(r), nblk(LANE), wblk, wblk],
        out_specs=qblk(wn),
        out_shape=jax.ShapeDtypeStruct((b, t, wn), BF16),
        compiler_params=_params(("arbitrary",)),
        name="mla_attention_latent",
    )(qn, qr, lat_cache, kr_cache, lat_new, kr_new, w_uk, w_uv)


ONES_ROWS = 16


def _flash_t_init(m_scr, acc_scr):
    m_scr[...] = jnp.full(m_scr.shape, MASKED, F32)
    acc_scr[...] = jnp.zeros(acc_scr.shape, F32)


def _flash_t_stage(slot, h, s, s_scr):
    s_scr[slot, h] = s
    return jnp.max(s, axis=0, keepdims=True)


def _flash_t_step(slot, h, s_max, vt, c, s_scr, m_scr, acc_scr):
    m_old = m_scr[h]
    m_new = jnp.maximum(m_old, s_max)
    alpha = jnp.exp2((m_old - m_new) * c)
    p = jnp.exp2((s_scr[slot, h] - m_new) * c).astype(BF16)
    vt_ones = jnp.concatenate([vt, jnp.ones((ONES_ROWS, vt.shape[1]), BF16)], axis=0)
    acc_scr[h] = alpha * acc_scr[h] + jnp.dot(vt_ones, p, preferred_element_type=F32)
    m_scr[h] = m_new


def _flash_t_out(o_ref, acc_scr):
    heads, width = acc_scr.shape[0], acc_scr.shape[1] - ONES_ROWS
    ot = jnp.concatenate([acc_scr[h, :width, :] / acc_scr[h, width:width + 1, :] for h in range(heads)], axis=0)
    o_ref[...] = ot.T.astype(BF16)


def _split_heads_t(slab_t):
    row_lo = lax.broadcasted_iota(I32, (LANE, 1), 0) < HEAD
    zero = jnp.zeros_like(slab_t)
    return jnp.where(row_lo, slab_t, zero), jnp.where(row_lo, zero, slab_t)


def _dsa_t_kernel(qi_ref, wi_ref, ki_ref, qa_ref, k_ref, vt_ref, o_ref,
                  key_scr, hi_scr, lo_scr, bias_scr, thr2_scr, qit_scr, qat_scr, s_scr, m_scr, acc_scr,
                  *, tq, tk, l, past, topk):
    heads = acc_scr.shape[0]
    qpos0 = past + pl.program_id(1) * tq
    ntiles = _visible_tiles(qpos0, tq, l, tk)
    qchunk = (qpos0 + lax.broadcasted_iota(I32, (1, tq), 1)) // CHUNK
    krow = lax.broadcasted_iota(I32, (tk, 1), 0)

    for s in range(IDX_HEADS // 2):
        qit_scr[2 * s], qit_scr[2 * s + 1] = _split_heads_t(qi_ref[:, s * LANE:(s + 1) * LANE].T)
    for s in range(heads // 2):
        slab_t = (qa_ref[:, s * LANE:(s + 1) * LANE].astype(F32) * (HEAD ** -0.5)).astype(BF16).T
        qat_scr[2 * s], qat_scr[2 * s + 1] = _split_heads_t(slab_t)
    w_t = wi_ref[...].T * (HEAD ** -0.5)
    w_rows = [w_t[h:h + 1, :] for h in range(IDX_HEADS)]

    def score_body(kt, _):
        k0 = pl.multiple_of(kt * tk, tk)
        ki_tile = ki_ref[pl.ds(k0, tk), :]
        score = jnp.zeros((tk, tq), F32)
        for h in range(IDX_HEADS):
            s = jnp.dot(ki_tile, qit_scr[h], preferred_element_type=F32)
            score = score + jnp.maximum(s, 0.0) * w_rows[h]
        kpos = k0 + krow
        vis = (kpos // CHUNK <= qchunk) & (kpos < l)
        key = _ordered_key(jnp.where(vis, score, -jnp.inf))
        key_scr[kt] = key
        hi_scr[kt] = (key >> 16).astype(I16)
        lo_scr[kt] = ((key & 0xFFFF) + I16_MIN).astype(I16)
        return 0

    lax.fori_loop(0, ntiles, score_body, 0)

    def count(pred):
        def body(kt, acc):
            hit = jnp.where(pred(key_scr[kt], kt * tk), 1.0, 0.0)
            return acc + hit.reshape(tk // SUBLANE, SUBLANE, tq).sum(axis=0)
        acc = lax.fori_loop(0, ntiles, body, jnp.zeros((SUBLANE, tq), F32))
        return jnp.sum(acc, axis=0, keepdims=True)

    def count16(half_scr, cand, strict):
        rows = 2 * SUBLANE
        cand16 = cand.astype(I16)

        def body(kt, acc):
            half = half_scr[kt]
            hit = jnp.where(half > cand16 if strict else half >= cand16, jnp.int16(1), jnp.int16(0))
            parts = hit.reshape(tk // (4 * rows), 4, rows, tq)
            for g in range(parts.shape[0]):
                acc = acc + parts[g]
            return acc

        acc = lax.fori_loop(0, ntiles, body, jnp.zeros((4, rows, tq), I16))
        return acc.astype(I32).sum(axis=0).sum(axis=0, keepdims=True)

    def kth_largest16(half_scr, k_need):
        def bit_body(b, lo):
            cand = lo + jnp.left_shift(jnp.int32(1), 15 - b)
            return jnp.where(count16(half_scr, cand, False) >= k_need, cand, lo)
        return lax.fori_loop(0, 16, bit_body, jnp.full((1, tq), I16_MIN, I32))

    thr_hi = kth_largest16(hi_scr, jnp.full((1, tq), topk, I32))
    n_gt_hi = count16(hi_scr, thr_hi, True)
    thr_hi16 = thr_hi.astype(I16)

    def mark_body(kt, _):
        lo_scr[kt] = jnp.where(hi_scr[kt] == thr_hi16, lo_scr[kt], jnp.int16(I16_MIN))
        return 0

    lax.fori_loop(0, ntiles, mark_body, 0)
    thr_lo = kth_largest16(lo_scr, topk - n_gt_hi)
    thr = thr_hi * 65536 + (thr_lo - I16_MIN)

    lp = key_scr.shape[0] * tk
    n_gt = n_gt_hi + count16(lo_scr, thr_lo, True)
    n_ge = jnp.where(thr_lo > I16_MIN, n_gt_hi + count16(lo_scr, thr_lo, False), count16(hi_scr, thr_hi, False))
    excess = (n_ge > topk) & (thr > KEY_NEG_INF)
    thr2_scr[...] = jnp.zeros((1, tq), I32)

    @pl.when(jnp.max(jnp.where(excess, 1.0, 0.0)) > 0.0)
    def _():
        need = (topk - n_gt).astype(F32)
        nbits = lp.bit_length()

        def bit2_body(b, lo):
            cand = lo + jnp.left_shift(jnp.int32(1), nbits - 1 - b)
            c = count(lambda ks, base: jnp.where(ks == thr, lp - (base + krow), 0) >= cand)
            return jnp.where(c >= need, cand, lo)

        thr2_scr[...] = lax.fori_loop(0, nbits, bit2_body, jnp.zeros((1, tq), I32))

    thr2 = thr2_scr[...]

    def bias_body(kt, _):
        ks = key_scr[kt]
        tie = jnp.where(lp - (kt * tk + krow) >= thr2, 0.0, MASKED)
        bias = jnp.where(ks > thr, 0.0, jnp.where(ks == thr, tie, MASKED))
        bias_scr[kt] = jnp.where(ks > KEY_NEG_INF, bias, MASKED)
        return 0

    lax.fori_loop(0, ntiles, bias_body, 0)

    _flash_t_init(m_scr, acc_scr)

    def att_body(kt, _):
        k0 = pl.multiple_of(kt * tk, tk)
        slot = kt % 2
        s_max = []
        for h in range(heads):
            cols = slice((h // 2) * LANE, (h // 2 + 1) * LANE)
            s = jnp.dot(k_ref[pl.ds(k0, tk), cols], qat_scr[h], preferred_element_type=F32) + bias_scr[kt]
            s_max.append(_flash_t_stage(slot, h, s, s_scr))
        for h in range(heads):
            _flash_t_step(slot, h, s_max[h], vt_ref[h * HEAD:(h + 1) * HEAD, pl.ds(k0, tk)], LOG2E,
                          s_scr, m_scr, acc_scr)
        return 0

    lax.fori_loop(0, ntiles, att_body, 0)
    _flash_t_out(o_ref, acc_scr)


def _dsa_attention_t(qi, wi, ki, qa, k, vt, grp, tq):
    b, t, aw = qa.shape
    lp, tk = _key_tile(t)
    assert lp == t
    heads = aw // HEAD
    topk = min(TOPK_MAX, t // 4)
    qblk = lambda w: pl.BlockSpec((None, tq, w), lambda i, j: (i, j, 0))
    kblk = lambda w: pl.BlockSpec((None, t, w), lambda i, j: (i, 0, 0))
    return pl.pallas_call(
        functools.partial(_dsa_t_kernel, tq=tq, tk=tk, l=t, past=grp.past, topk=topk),
        grid=(b, t // tq),
        in_specs=[qblk(IDX_HEADS * HEAD), qblk(LANE), kblk(LANE), qblk(aw), kblk(aw),
                  pl.BlockSpec((aw, t), lambda i, j: (0, i))],
        out_specs=qblk(aw),
        out_shape=jax.ShapeDtypeStruct((b, t, aw), BF16),
        scratch_shapes=[pltpu.VMEM((t // tk, tk, tq), I32),
                        pltpu.VMEM((t // tk, tk, tq), I16),
                        pltpu.VMEM((t // tk, tk, tq), I16),
                        pltpu.VMEM((t // tk, tk, tq), F32),
                        pltpu.VMEM((1, tq), I32),
                        pltpu.VMEM((IDX_HEADS, LANE, tq), BF16),
                        pltpu.VMEM((heads, LANE, tq), BF16),
                        pltpu.VMEM((2, heads, tk, tq), F32),
                        pltpu.VMEM((heads, 1, tq), F32),
                        pltpu.VMEM((heads, HEAD + ONES_ROWS, tq), F32)],
        compiler_params=_params(("arbitrary", "arbitrary")),
        name="dsa_attention_t",
    )(qi, wi, ki, qa, k, vt)


def _mla_t_kernel(qn_ref, qr_ref, kn_ref, vt_ref, kr_ref, o_ref, qt_scr, s_scr, m_scr, acc_scr,
                  *, tq, tk, l, past):
    ch = acc_scr.shape[0]
    qpos0 = past + pl.program_id(1) * tq
    ntiles = _visible_tiles(qpos0, tq, l, tk)
    nfull = jnp.minimum(l, (qpos0 // CHUNK + 1) * CHUNK) // tk
    qchunk = (qpos0 + lax.broadcasted_iota(I32, (1, tq), 1)) // CHUNK
    krow = lax.broadcasted_iota(I32, (tk, 1), 0)
    c = (C_NOPE + HEAD) ** -0.5 * LOG2E
    for s in range(ch // 2):
        pair = _split_heads_t(qr_ref[:, s * LANE:(s + 1) * LANE].T)
        for half in range(2):
            h = 2 * s + half
            qt_scr[h] = jnp.concatenate([qn_ref[:, h * LANE:(h + 1) * LANE].T, pair[half]], axis=0)
    _flash_t_init(m_scr, acc_scr)

    def tile(kt, masked):
        k0 = pl.multiple_of(kt * tk, tk)
        k_rope = kr_ref[pl.ds(k0, tk), :]
        slot = kt % 2
        if masked:
            kpos = k0 + krow
            bias = jnp.where((kpos // CHUNK <= qchunk) & (kpos < l), 0.0, MASKED)
        s_max = []
        for h in range(ch):
            kcat = jnp.concatenate([kn_ref[pl.ds(k0, tk), h * LANE:(h + 1) * LANE], k_rope], axis=1)
            s = jnp.dot(kcat, qt_scr[h], preferred_element_type=F32)
            s_max.append(_flash_t_stage(slot, h, s + bias if masked else s, s_scr))
        for h in range(ch):
            _flash_t_step(slot, h, s_max[h], vt_ref[h * C_V:(h + 1) * C_V, pl.ds(k0, tk)], c,
                          s_scr, m_scr, acc_scr)
        return 0

    lax.fori_loop(0, nfull, lambda kt, _: tile(kt, False), 0)
    lax.fori_loop(nfull, ntiles, lambda kt, _: tile(kt, True), 0)
    _flash_t_out(o_ref, acc_scr)


def _mla_attention_t(qn, qr, kn, vt, kr, grp, tq):
    b, t, wn = qn.shape
    ch = wn // C_NOPE
    lp, tk = _key_tile(t)
    assert lp == t and ch % 2 == 0
    qblk = lambda w: pl.BlockSpec((None, tq, w), lambda i, j: (i, j, 0))
    kblk = lambda w: _resident((None, t, w), lambda i, j: (i, 0, 0))
    return pl.pallas_call(
        functools.partial(_mla_t_kernel, tq=tq, tk=tk, l=t, past=grp.past),
        grid=(b, t // tq),
        in_specs=[qblk(wn), qblk(ch * HEAD), kblk(wn), _resident((wn, t), lambda i, j: (0, i)), kblk(LANE)],
        out_specs=qblk(wn),
        out_shape=jax.ShapeDtypeStruct((b, t, wn), BF16),
        scratch_shapes=[pltpu.VMEM((ch, 2 * LANE, tq), BF16),
                        pltpu.VMEM((2, ch, tk, tq), F32),
                        pltpu.VMEM((ch, 1, tq), F32),
                        pltpu.VMEM((ch, C_V + ONES_ROWS, tq), F32)],
        compiler_params=_params(("arbitrary", "arbitrary")),
        name="mla_attention_t",
    )(qn, qr, kn, vt, kr)


def _causal_conv(u, e0, e1, w, seg):
    rmod = lax.broadcasted_iota(I32, (u.shape[0], 1), 0) % seg
    u1 = jnp.where(rmod == 0, e1, pltpu.roll(u, 1, 0))
    u2 = jnp.where(rmod == 0, e0, jnp.where(rmod == 1, e1, pltpu.roll(u, 2, 0)))
    return u2 * w[0:1] + u1 * w[1:2] + u * w[2:3]


def _layer_norm(z, g, b):
    mu = jnp.mean(z, axis=-1, keepdims=True)
    zc = z - mu
    var = jnp.mean(zc * zc, axis=-1, keepdims=True)
    return zc * lax.rsqrt(var + LN_EPS) * g + b


def _outproj_kernel(*refs, alpha, seq_tiles, seg, carried):
    if carried:
        (x_ref, oa_ref, bg_ref, u_ref, oc_ref, w_ref, cw_ref, g1_ref, lng_ref, lnb_ref, sc2_ref, sh2_ref,
         x1_ref, h2_ref, prev_scr) = refs

        @pl.when(pl.program_id(0) % seq_tiles == 0)
        def _():
            prev_scr[...] = jnp.zeros_like(prev_scr)

        e0, e1 = prev_scr[SUBLANE - 2:SUBLANE - 1, :], prev_scr[SUBLANE - 1:SUBLANE, :]
    else:
        (x_ref, oa_ref, bg_ref, u_ref, oc_ref, w_ref, cw_ref, g1_ref, lng_ref, lnb_ref, sc2_ref, sh2_ref,
         e0_ref, e1_ref, x1_ref, h2_ref) = refs
        e0, e1 = e0_ref[...], e1_ref[...]
    u = u_ref[...]
    yb = bg_ref[...] * _causal_conv(u, e0, e1, cw_ref[...], seg)
    if carried:
        prev_scr[...] = u[u.shape[0] - SUBLANE:, :]
    mixed = jnp.concatenate([oa_ref[...], yb.astype(BF16), oc_ref[...]], axis=1)
    tm = mixed.shape[0]
    n_split = 4 if tm % (4 * 2 * SUBLANE) == 0 else 1

    def rows_of(ref, rows):
        return ref[...] if ref.shape[0] == 1 else ref[rows, :]

    halves = [slice(c * tm // n_split, (c + 1) * tm // n_split) for c in range(n_split)]
    mixes = [jnp.dot(mixed[rows, :], w_ref[...], preferred_element_type=F32) for rows in halves]
    for rows, mix in zip(halves, mixes):
        x1 = _layer_norm(alpha * x_ref[rows, :] + (1.0 + rows_of(g1_ref, rows)) * mix, lng_ref[...], lnb_ref[...])
        x1_ref[rows, :] = x1
        h2_ref[rows, :] = (x1 * (1.0 + rows_of(sc2_ref, rows)) + rows_of(sh2_ref, rows)).astype(BF16)


def _out_projection(x, oa, bg, u, oc, w_out, layer, conv_w, g1, ln_g, ln_b, sc2, sh2, prev, grp, alpha, tm):
    m, d = x.shape
    aw, bw, cw = oa.shape[1], bg.shape[1], oc.shape[1]
    n_i = m // tm
    tiles_per_mod = n_i // g1.shape[0]
    mod_rows = g1.shape[1]
    row = lambda i: (i, 0)
    fix = lambda i: (0, 0)
    mod_spec = pl.BlockSpec((None, mod_rows, d), lambda i: (i // tiles_per_mod, 0, 0))
    in_specs = [pl.BlockSpec((tm, d), row), pl.BlockSpec((tm, aw), row), pl.BlockSpec((tm, bw), row),
                pl.BlockSpec((tm, bw), row), pl.BlockSpec((tm, cw), row),
                _resident((None, aw + bw + cw, d), lambda i: (layer, 0, 0)), pl.BlockSpec((CONV_W, bw), fix),
                mod_spec, pl.BlockSpec((1, d), fix), pl.BlockSpec((1, d), fix), mod_spec, mod_spec]
    args = [x, oa, bg, u, oc, w_out, conv_w, g1, ln_g.reshape(1, d), ln_b.reshape(1, d), sc2, sh2]
    scratch = []
    if prev is None:
        assert grp.t % tm == 0
        scratch = [pltpu.VMEM((SUBLANE, bw), F32)]
    else:
        assert tm % grp.t == 0
        in_specs += [pl.BlockSpec((tm, bw), row), pl.BlockSpec((tm, bw), row)]
        args += [prev[0], prev[1]]
    return pl.pallas_call(
        functools.partial(_outproj_kernel, alpha=alpha, seq_tiles=max(grp.t // tm, 1), seg=min(grp.t, tm),
                          carried=prev is None),
        grid=(n_i,),
        in_specs=in_specs,
        out_specs=[pl.BlockSpec((tm, d), row), pl.BlockSpec((tm, d), row)],
        out_shape=[jax.ShapeDtypeStruct((m, d), F32), jax.ShapeDtypeStruct((m, d), BF16)],
        scratch_shapes=scratch,
        compiler_params=_params(("arbitrary",)),
        name="out_projection",
    )(*args)


def _ffn_kernel(*refs, alpha, seq_tiles, seg, carried):
    if carried:
        (h_ref, x_ref, wg_ref, wu_ref, wd_ref, cw_ref, g2_ref, lng_ref, lnb_ref,
         o_ref, gt_ref, acc_scr, prev_scr) = refs
    else:
        (h_ref, x_ref, wg_ref, wu_ref, wd_ref, cw_ref, g2_ref, lng_ref, lnb_ref, e0_ref, e1_ref,
         o_ref, gt_ref, acc_scr) = refs
    f = pl.program_id(1)
    if carried:
        @pl.when(pl.program_id(0) % seq_tiles == 0)
        def _():
            prev_scr[f] = jnp.zeros(prev_scr.shape[1:], F32)

    @pl.when(f == 0)
    def _():
        acc_scr[...] = jnp.zeros_like(acc_scr)

    h = h_ref[...]
    tm, tf = h.shape[0], wg_ref.shape[1]
    n_split = 2 if tf % (2 * LANE) == 0 else 1
    halves = [slice(c * tf // n_split, (c + 1) * tf // n_split) for c in range(n_split)]
    gates = [jnp.dot(h, wg_ref[:, cols], preferred_element_type=F32) for cols in halves]
    ups = [jnp.dot(h, wu_ref[:, cols], preferred_element_type=F32) for cols in halves]
    for cols, gate, up in zip(halves, gates, ups):
        if carried:
            e0, e1 = prev_scr[f, SUBLANE - 2:SUBLANE - 1, cols], prev_scr[f, SUBLANE - 1:SUBLANE, cols]
        else:
            e0, e1 = e0_ref[:, cols], e1_ref[:, cols]
        conv = _causal_conv(gate, e0, e1, cw_ref[:, cols], seg)
        if carried:
            prev_scr[f, :, cols] = gate[tm - SUBLANE:, :]
        gt_ref[:, cols] = gate[tm - gt_ref.shape[0]:, :]
        act = (jax.nn.silu(conv) * up).astype(BF16)
        acc_scr[...] = jnp.dot(act, wd_ref[cols, :], preferred_element_type=F32) + acc_scr[...]

    @pl.when(f == pl.num_programs(1) - 1)
    def _():
        z = alpha * x_ref[...] + (1.0 + g2_ref[...]) * acc_scr[...]
        o_ref[...] = _layer_norm(z, lng_ref[...], lnb_ref[...])


def _channel_mixer(h2, x1, w_gu, w_down, layer, conv_w, g2, ln_g, ln_b, prev, grp, alpha, tm, tf):
    m, d = x1.shape
    dff = w_down.shape[1]
    n_i, n_f = m // tm, dff // tf
    tiles_per_mod = n_i // g2.shape[0]
    mod_rows = g2.shape[1]
    fix = lambda i, f: (0, 0)
    in_specs = [pl.BlockSpec((tm, d), lambda i, f: (i, 0)), pl.BlockSpec((tm, d), lambda i, f: (i, 0)),
                pl.BlockSpec((None, d, tf), lambda i, f: (layer, 0, f)),
                pl.BlockSpec((None, d, tf), lambda i, f: (layer, 0, n_f + f)),
                pl.BlockSpec((None, tf, d), lambda i, f: (layer, f, 0)),
                pl.BlockSpec((CONV_W, tf), lambda i, f: (0, f)),
                pl.BlockSpec((None, mod_rows, d), lambda i, f: (i // tiles_per_mod, 0, 0)),
                pl.BlockSpec((1, d), fix), pl.BlockSpec((1, d), fix)]
    args = [h2, x1, w_gu, w_gu, w_down, conv_w, g2, ln_g.reshape(1, d), ln_b.reshape(1, d)]
    scratch = [pltpu.VMEM((tm, d), F32)]
    if prev is None:
        assert grp.t % tm == 0
        scratch.append(pltpu.VMEM((n_f, SUBLANE, tf), F32))
        gt_spec = pl.BlockSpec((None, SUBLANE, tf), lambda i, f: (i, 0, f))
        gt_shape = jax.ShapeDtypeStruct((n_i, SUBLANE, dff), F32)
    else:
        assert tm % grp.t == 0
        in_specs += [pl.BlockSpec((tm, tf), lambda i, f: (i, f))] * 2
        args += [prev[0], prev[1]]
        gt_spec = pl.BlockSpec((tm, tf), lambda i, f: (i, f))
        gt_shape = jax.ShapeDtypeStruct((m, dff), F32)
    return pl.pallas_call(
        functools.partial(_ffn_kernel, alpha=alpha, seq_tiles=max(grp.t // tm, 1), seg=min(grp.t, tm),
                          carried=prev is None),
        grid=(n_i, n_f),
        in_specs=in_specs,
        out_specs=[pl.BlockSpec((tm, d), lambda i, f: (i, 0)), gt_spec],
        out_shape=[jax.ShapeDtypeStruct((m, d), F32), gt_shape],
        scratch_shapes=scratch,
        compiler_params=_params(("arbitrary", "arbitrary")),
        name="channel_mixer",
    )(*args)


def _rope_tables(pos):
    half = HEAD // 2
    inv = jnp.power(jnp.float32(ROPE_THETA), -jnp.arange(half, dtype=F32) / half)
    ang = pos.astype(F32)[:, None] * inv[None, :]
    cos, sin = jnp.cos(ang), jnp.sin(ang)
    zero = jnp.zeros_like(sin)
    reps = LANE // HEAD
    return (jnp.tile(jnp.concatenate([cos, cos], axis=1), (1, reps)),
            jnp.tile(jnp.concatenate([zero, sin], axis=1), (1, reps)),
            jnp.tile(jnp.concatenate([-sin, zero], axis=1), (1, reps)))


def _layer(x, mod, tables, grp, past, layer, big, small, stacks, dm, alpha):
    w_in_p, w_out, w_uk, w_uv, w_uv_t, w_gu, w_down = big
    conv_b_w, kv_norm, ln1_g, ln1_b, ln2_g, ln2_b, conv_f_w = small
    b, t = grp.b, grp.t
    m, d = x.shape
    carried = past is None
    tm = _row_tile(t, 256) if carried else m
    if carried:
        mods = [a.reshape(b, 1, d) for a in jnp.split(mod, N_MOD, axis=-1)]
    else:
        mods = [jnp.repeat(a, t, axis=0).reshape(1, m, d) for a in jnp.split(mod, N_MOD, axis=-1)]
    sh1, sc1, g1, sh2, sc2, g2 = mods

    outs = _in_projection(x, sc1, sh1, w_in_p, layer, tables, kv_norm, stacks, dm, tm)
    stacks = tuple(outs[o] for o in STATE_OUTPUTS)
    qa, _, kab, qi, qcr, _, kib, _, krb, _, vab, bg, u, qcn, _, latb, wi, vat = outs

    three = lambda a: a.reshape(b, t, a.shape[-1])
    if carried:
        tq = _row_tile(t, 256)
        kn, vct = _kv_up(latb, w_uk, w_uv_t, layer)
        oa = _dsa_attention_t(three(qi), three(wi), three(kib), three(qa), three(kab), vat, grp, tq)
        oc = _mla_attention_t(three(qcn), three(qcr), three(kn), vct, three(krb), grp, tq)
        prev_b = prev_f = None
    else:
        c_ak, c_av, c_ik, c_lat, c_kr, prev_b, prev_f = past
        new_rows = lambda a: jnp.pad(three(a), ((0, 0), (0, -t % LANE), (0, 0)))
        oa = _dsa_attention(three(qi), three(wi), three(qa), c_ik, c_ak, c_av,
                            new_rows(kib), new_rows(kab), new_rows(vab), layer, grp)
        oc = _mla_attention_latent(three(qcn), three(qcr), c_lat, c_kr, new_rows(latb), new_rows(krb),
                                   w_uk, w_uv, layer, grp)

    def expand(state):
        return jnp.repeat(state[:, 0], t, axis=0), jnp.repeat(state[:, 1], t, axis=0)

    x1, h2 = _out_projection(x, oa.reshape(m, -1), bg, u, oc.reshape(m, -1), w_out, layer, conv_b_w, g1,
                             ln1_g, ln1_b, sc2, sh2, None if carried else expand(prev_b), grp, alpha,
                             _row_tile(t, 512) if carried else m)
    tm_f = _row_tile(t, 512) if carried else m
    tf = _row_tile(dm.dff, 512)
    x2, gate_rows = _channel_mixer(h2, x1, w_gu, w_down, layer, conv_f_w, g2, ln2_g, ln2_b,
                                   None if carried else expand(prev_f), grp, alpha, tm_f, tf)
    if carried:
        new_f = gate_rows.reshape(b, t // tm_f, SUBLANE, dm.dff)[:, -1, SUBLANE - (CONV_W - 1):, :]
    else:
        new_f = gate_rows.reshape(b, t, dm.dff)[:, t - (CONV_W - 1):, :]
    new_b = u.reshape(b, t, dm.bw)[:, t - (CONV_W - 1):, :]
    return x2, stacks, (new_b, new_f)


def _state_outputs(stacks, conv_rows, grp, dm):
    ka, ki, kr, va, lat = stacks
    b, t, heads = grp.b, grp.t, dm.aw // HEAD
    new_b, new_f = [jnp.stack(r) for r in zip(*conv_rows)]
    return (ka.reshape(dm.depth, b, t, heads, HEAD), va.reshape(dm.depth, b, t, heads, HEAD),
            ki.reshape(dm.depth, b, t, HEAD), lat.reshape(dm.depth, b, t, dm.r), kr.reshape(dm.depth, b, t, HEAD),
            new_b, new_f)


def kernel(x_prompt, x_sample, c_prompt, c_sample, cache_a_k, cache_a_v, cache_idx_k, cache_mla_latent,
           cache_mla_krope, state_conv_b, state_conv_ffn, w_in, w_out, conv_b_w, mla_kv_norm, mla_w_uk,
           mla_w_uv, w_mod, b_mod, ln1_g, ln1_b, ln2_g, ln2_b, ffn_w_gu, ffn_conv_w, ffn_w_down):
    depth, d, _ = w_in.shape
    a_heads = cache_a_k.shape[3]
    dm = Dims(d=d, aw=a_heads * HEAD, bw=conv_b_w.shape[2], ch=mla_w_uk.shape[2] // C_NOPE,
              r=mla_w_uk.shape[1], dff=ffn_w_down.shape[1], depth=depth)
    alpha = (2 * depth) ** 0.25
    grp_p = Group(b=x_prompt.shape[0], t=x_prompt.shape[1], past=0)
    grp_s = Group(b=x_sample.shape[0], t=x_sample.shape[1], past=cache_a_k.shape[2])

    n_c = grp_p.b + grp_s.b
    c_all = jnp.concatenate([c_prompt, c_sample, jnp.zeros((-n_c % SUBLANE, d), F32)], axis=0)
    mod = _modulation(c_all, w_mod, b_mod)

    tab_p = _rope_tables(jnp.arange(grp_p.t, dtype=I32))
    tab_s = tuple(jnp.tile(a, (grp_s.b, 1)) for a in _rope_tables(grp_s.past + jnp.arange(grp_s.t, dtype=I32)))

    xp = x_prompt.reshape(grp_p.b * grp_p.t, d)
    xs = x_sample.reshape(grp_s.b * grp_s.t, d)
    conv_p, conv_s = [], []
    stacks_p = stacks_s = None
    w_uv_b = mla_w_uv.astype(BF16)
    big = (_pack_w_in(w_in, dm), w_out.astype(BF16), mla_w_uk.astype(BF16), w_uv_b, jnp.swapaxes(w_uv_b, 1, 2),
           ffn_w_gu.astype(BF16), ffn_w_down.astype(BF16))
    idx_k = cache_idx_k.astype(BF16)
    caches = (cache_a_k.reshape(cache_a_k.shape[:3] + (-1,)).astype(BF16),
              cache_a_v.reshape(cache_a_v.shape[:3] + (-1,)).astype(BF16),
              jnp.concatenate([idx_k, idx_k], axis=-1))
    for l in range(depth):
        small = (conv_b_w[l], mla_kv_norm[l], ln1_g[l], ln1_b[l], ln2_g[l], ln2_b[l], ffn_conv_w[l])
        xp, stacks_p, rp = _layer(xp, mod[l, :grp_p.b], tab_p, grp_p, None, l, big, small, stacks_p, dm, alpha)
        past_l = (*caches, cache_mla_latent, cache_mla_krope, state_conv_b[l], state_conv_ffn[l])
        xs, stacks_s, rs = _layer(xs, mod[l, grp_p.b:n_c], tab_s, grp_s, past_l, l, big, small, stacks_s, dm,
                                  alpha)
        conv_p.append(rp)
        conv_s.append(rs)
    return (xp.reshape(x_prompt.shape), xs.reshape(x_sample.shape),
            *_state_outputs(stacks_p, conv_p, grp_p, dm), *_state_outputs(stacks_s, conv_s, grp_s, dm))
```

```python
import functools
from typing import NamedTuple

import numpy as np
import jax
import jax.numpy as jnp
from jax import lax
from jax.experimental import pallas as pl
from jax.experimental.pallas import tpu as pltpu

F32, BF16, I32, I16 = jnp.float32, jnp.bfloat16, jnp.int32, jnp.int16

CHUNK = 64
CONV_W = 3
ROPE_THETA = 10000.0
HEAD = 64
IDX_HEADS = 16
TOPK_MAX = 256
C_NOPE = 128
C_V = 128
N_MOD = 6
LN_EPS = 1e-5
RMS_EPS = 1e-6

LANE = 128
SUBLANE = 8
VMEM_LIMIT = 50 * 1024 * 1024

MASKED = -1e30
LOG2E = 1.4426950408889634
INT_MIN = -2 ** 31
I16_MIN = -2 ** 15
KEY_NEG_INF = int(np.array(-np.inf, np.float32).view(np.int32)) ^ 0x7FFFFFFF


class Dims(NamedTuple):
    d: int
    aw: int
    bw: int
    ch: int
    r: int
    dff: int
    depth: int


class Group(NamedTuple):
    b: int
    t: int
    past: int


def _row_tile(m, pref):
    if m <= pref:
        return m
    t = pref - pref % SUBLANE
    while m % t:
        t -= SUBLANE
    return t


def _key_tile(l):
    lp = -(-l // LANE) * LANE
    for tk in (512, 384, 256, 128):
        if lp % tk == 0:
            return lp, tk
    raise AssertionError(lp)


def _params(sem):
    return pltpu.CompilerParams(dimension_semantics=sem, vmem_limit_bytes=VMEM_LIMIT)


def _resident(shape, index_map):
    return pl.BlockSpec(shape, index_map, pipeline_mode=pl.Buffered(1))


def _mod_kernel(c_ref, w_ref, b_ref, o_ref):
    a = jax.nn.silu(c_ref[...]).astype(BF16)
    o_ref[...] = jnp.dot(a, w_ref[...].astype(BF16), preferred_element_type=F32) + b_ref[...]


def _modulation(c, w_mod, b_mod):
    depth, d, n = w_mod.shape
    rows = c.shape[0]
    tn = _row_tile(n, 1024)
    return pl.pallas_call(
        _mod_kernel,
        grid=(depth, n // tn),
        in_specs=[pl.BlockSpec((rows, d), lambda l, j: (0, 0)),
                  pl.BlockSpec((None, d, tn), lambda l, j: (l, 0, j)),
                  pl.BlockSpec((None, 1, tn), lambda l, j: (l, 0, j))],
        out_specs=pl.BlockSpec((None, rows, tn), lambda l, j: (l, 0, j)),
        out_shape=jax.ShapeDtypeStruct((depth, rows, n), F32),
        compiler_params=_params(("arbitrary", "arbitrary")),
        name="modulation",
    )(c, w_mod, b_mod.reshape(depth, 1, n))


def _pack_w_in(w, dm):
    d, aw, bw, ch, r = dm.d, dm.aw, dm.bw, dm.ch, dm.r
    lead = w.shape[:-1]
    o = np.cumsum([0, aw, aw, aw, IDX_HEADS * HEAD, HEAD, IDX_HEADS, bw, bw, bw, ch * (C_NOPE + HEAD), r, HEAD])
    qa, ka, va, qi, ki, wi, bg, cg, xb, qc, lat, kr = [w[..., o[i]:o[i + 1]] for i in range(12)]
    qc = qc.reshape(lead + (ch, C_NOPE + HEAD))
    qcn = qc[..., :C_NOPE].reshape(lead + (ch * C_NOPE,))
    qcr = qc[..., C_NOPE:].reshape(lead + (ch * HEAD,))
    pad = jnp.zeros(lead + (LANE - IDX_HEADS,), w.dtype)
    return jnp.concatenate([qa, ka, qi, qcr, ki, ki, kr, kr, va, bg, cg, xb, qcn, lat, wi, pad],
                           axis=-1).astype(BF16)


def _rope(acc, cos, s1, s2):
    outs = []
    for s in range(acc.shape[1] // LANE):
        xs = acc[:, s * LANE:(s + 1) * LANE]
        outs.append(xs * cos + pltpu.roll(xs, HEAD // 2, 1) * s1 + pltpu.roll(xs, LANE - HEAD // 2, 1) * s2)
    return outs[0] if len(outs) == 1 else jnp.concatenate(outs, axis=1)


def _inproj_kernel(x_ref, sc_ref, sh_ref, w_ref, cos_ref, s1_ref, s2_ref, nrm_ref, *rest, dm, n_alias):
    (qa_ref, ka_ref, kab_ref, qi_ref, qcr_ref, ki_ref, kib_ref, kr_ref, krb_ref,
     va_ref, vab_ref, bg_ref, u_ref, qcn_ref, lat_ref, latb_ref, wi_ref, vat_ref) = rest[n_alias:]
    aw, bw, ch, r = dm.aw, dm.bw, dm.ch, dm.r
    h = (x_ref[...] * (1.0 + sc_ref[...]) + sh_ref[...]).astype(BF16)
    cos, s1, s2 = cos_ref[...], s1_ref[...], s2_ref[...]
    col = [0]

    def proj(width):
        c0 = col[0]
        col[0] = c0 + width
        return jnp.dot(h, w_ref[:, c0:c0 + width], preferred_element_type=F32)

    def pieces(width, step=512):
        return [(o, min(step, width - o)) for o in range(0, width, step)]

    for o, wd in pieces(aw):
        qa_ref[:, o:o + wd] = _rope(proj(wd), cos, s1, s2).astype(BF16)
    for o, wd in pieces(aw):
        y = _rope(proj(wd), cos, s1, s2)
        ka_ref[:, o // HEAD:(o + wd) // HEAD, :] = y.reshape(y.shape[0], wd // HEAD, HEAD)
        kab_ref[:, o:o + wd] = y.astype(BF16)
    for o, wd in pieces(IDX_HEADS * HEAD):
        qi_ref[:, o:o + wd] = _rope(proj(wd), cos, s1, s2).astype(BF16)
    for o, wd in pieces(ch * HEAD):
        qcr_ref[:, o:o + wd] = _rope(proj(wd), cos, s1, s2).astype(BF16)
    y = _rope(proj(2 * LANE), cos, s1, s2)
    for c, (f32_ref, b16_ref) in enumerate(((ki_ref, kib_ref), (kr_ref, krb_ref))):
        f32_ref[...] = y[:, c * LANE:c * LANE + HEAD]
        b16_ref[...] = y[:, c * LANE:(c + 1) * LANE].astype(BF16)
    for o, wd in pieces(aw):
        y = proj(wd)
        va_ref[:, o // HEAD:(o + wd) // HEAD, :] = y.reshape(y.shape[0], wd // HEAD, HEAD)
        vab_ref[:, o:o + wd] = y.astype(BF16)
        vat_ref[o:o + wd, :] = y.T.astype(BF16)
    for o, wd in pieces(bw):
        bg_ref[:, o:o + wd] = proj(wd)
    c_cg = col[0]
    for o, wd in pieces(bw):
        cg = jnp.dot(h, w_ref[:, c_cg + o:c_cg + o + wd], preferred_element_type=F32)
        xb = jnp.dot(h, w_ref[:, c_cg + bw + o:c_cg + bw + o + wd], preferred_element_type=F32)
        u_ref[:, o:o + wd] = cg * xb
    col[0] = c_cg + 2 * bw
    for o, wd in pieces(ch * C_NOPE):
        qcn_ref[:, o:o + wd] = proj(wd).astype(BF16)
    lat = proj(r)
    lat = lat * lax.rsqrt(jnp.mean(lat * lat, axis=-1, keepdims=True) + RMS_EPS) * nrm_ref[...]
    lat_ref[...] = lat
    latb_ref[...] = lat.astype(BF16)
    wi_ref[...] = proj(LANE) * (IDX_HEADS ** -0.5)


STATE_OUTPUTS = (1, 5, 7, 9, 14)
HEAD_OUTPUTS = (1, 9)


def _in_projection(x, sc, sh, w_packed, layer, tables, kv_norm, stacks, dm, tm):
    m, d = x.shape
    aw, bw, ch, r = dm.aw, dm.bw, dm.ch, dm.r
    npk = w_packed.shape[2]
    n_i = m // tm
    tiles_per_mod = n_i // sc.shape[0]
    mod_rows = sc.shape[1]
    tab_tiles = tables[0].shape[0] // tm
    widths = [(aw, BF16), (aw, F32), (aw, BF16), (IDX_HEADS * HEAD, BF16), (ch * HEAD, BF16),
              (HEAD, F32), (LANE, BF16), (HEAD, F32), (LANE, BF16),
              (aw, F32), (aw, BF16), (bw, F32), (bw, F32), (ch * C_NOPE, BF16), (r, F32), (r, BF16),
              (LANE, F32)]
    row = lambda i: (i, 0)
    mod_spec = pl.BlockSpec((None, mod_rows, d), lambda i: (i // tiles_per_mod, 0, 0))
    tab_spec = pl.BlockSpec((tm, LANE), lambda i: (i % tab_tiles, 0))
    in_specs = [pl.BlockSpec((tm, d), row), mod_spec, mod_spec,
                _resident((None, d, npk), lambda i: (layer, 0, 0)),
                tab_spec, tab_spec, tab_spec,
                pl.BlockSpec((1, r), lambda i: (0, 0))]
    args = [x, sc, sh, w_packed, *tables, kv_norm.reshape(1, r)]
    out_specs = [pl.BlockSpec((tm, w), row) for w, _ in widths] + [pl.BlockSpec((aw, tm), lambda i: (0, i))]
    out_shape = [jax.ShapeDtypeStruct((m, w), dt) for w, dt in widths] + [jax.ShapeDtypeStruct((aw, m), BF16)]
    for o in STATE_OUTPUTS:
        w, dt = widths[o]
        tail = (w // HEAD, HEAD) if o in HEAD_OUTPUTS else (w,)
        out_specs[o] = pl.BlockSpec((None, tm) + tail, lambda i, n=len(tail): (layer, i) + (0,) * n)
        out_shape[o] = jax.ShapeDtypeStruct((dm.depth, m) + tail, dt)
    aliases = {}
    if stacks is not None:
        aliases = {len(args) + k: o for k, o in enumerate(STATE_OUTPUTS)}
        in_specs += [pl.BlockSpec(memory_space=pl.ANY)] * len(stacks)
        args += list(stacks)
    return pl.pallas_call(
        functools.partial(_inproj_kernel, dm=dm, n_alias=len(aliases)),
        grid=(n_i,),
        in_specs=in_specs,
        out_specs=out_specs,
        out_shape=out_shape,
        input_output_aliases=aliases,
        compiler_params=_params(("arbitrary",)),
        name="in_projection",
    )(*args)


def _kvup_kernel(l_ref, wk_ref, wvt_ref, k_ref, vt_ref):
    lat = l_ref[...]
    k_ref[...] = jnp.dot(lat, wk_ref[...], preferred_element_type=F32).astype(BF16)
    vt_ref[...] = lax.dot_general(wvt_ref[...], lat, _NT, preferred_element_type=F32).astype(BF16)


def _kv_up(lat, w_uk, w_uv_t, layer):
    m, r = lat.shape
    n = w_uk.shape[2]
    tm = _row_tile(m, 512)
    return pl.pallas_call(
        _kvup_kernel,
        grid=(m // tm,),
        in_specs=[pl.BlockSpec((tm, r), lambda i: (i, 0)),
                  pl.BlockSpec((None, r, n), lambda i: (layer, 0, 0)),
                  pl.BlockSpec((None, n, r), lambda i: (layer, 0, 0))],
        out_specs=[pl.BlockSpec((tm, n), lambda i: (i, 0)), pl.BlockSpec((n, tm), lambda i: (0, i))],
        out_shape=[jax.ShapeDtypeStruct((m, n), BF16), jax.ShapeDtypeStruct((n, m), BF16)],
        compiler_params=_params(("arbitrary",)),
        name="latent_up_projection",
    )(lat, w_uk, w_uv_t)


_NT = (((1,), (1,)), ((), ()))


def _visible_tiles(qpos0, tq, l, tk):
    nvis = jnp.minimum(l, ((qpos0 + tq - 1) // CHUNK + 1) * CHUNK)
    return (nvis + tk - 1) // tk


def _flash_step(carry, s, v):
    m, l, acc = carry
    m_new = jnp.maximum(m, jnp.max(s, axis=1, keepdims=True))
    alpha = jnp.exp(m - m_new)
    p = jnp.exp(s - m_new)
    l = alpha * l + jnp.sum(p, axis=1, keepdims=True)
    acc = alpha * acc + jnp.dot(p.astype(BF16), v, preferred_element_type=F32)
    return m_new, l, acc


def _flash_init(rows, width):
    return (jnp.full((rows, 1), MASKED, F32), jnp.zeros((rows, 1), F32), jnp.zeros((rows, width), F32))


def _ordered_key(x):
    b = pltpu.bitcast(x, I32)
    return jnp.where(b < 0, b ^ 0x7FFFFFFF, b)


def _dsa_kernel(qi_ref, wi_ref, qa_ref, kip_ref, kp_ref, vp_ref, kin_ref, kn_ref, vn_ref, o_ref,
                key_scr, bias_scr, thr2_scr, *, tq, tk, past, topk, aw):
    l = past + tq
    tn = kn_ref.shape[0]
    span = past + tn
    tiles = [(kip_ref, kp_ref, vp_ref, r0, tk, r0) for r0 in range(0, past, tk)]
    tiles.append((kin_ref, kn_ref, vn_ref, 0, tn, past))
    qchunk = (past + lax.broadcasted_iota(I32, (tq, 1), 0)) // CHUNK
    lane = lax.broadcasted_iota(I32, (1, LANE), 1)
    lo_half = lane < HEAD
    kf = jnp.float32(topk)

    def split_heads(qs):
        zero = jnp.zeros_like(qs)
        return jnp.concatenate([jnp.where(lo_half, qs, zero), jnp.where(lo_half, zero, qs)], axis=0)

    wi = wi_ref[...] * (HEAD ** -0.5)
    q_idx = jnp.concatenate([split_heads(qi_ref[:, s * LANE:(s + 1) * LANE]) for s in range(IDX_HEADS // 2)],
                            axis=0)
    w_idx = [wi[:, h:h + 1] for h in range(IDX_HEADS)]

    for kt, (ki_ref, _, _, r0, width, pos0) in enumerate(tiles):
        s_all = lax.dot_general(q_idx, ki_ref[r0:r0 + width, :], _NT, preferred_element_type=F32)
        score = jnp.zeros((tq, width), F32)
        for h in range(IDX_HEADS):
            score = score + jnp.maximum(s_all[h * tq:(h + 1) * tq], 0.0) * w_idx[h]
        kpos = pos0 + lax.broadcasted_iota(I32, (1, width), 1)
        vis = (kpos // CHUNK <= qchunk) & (kpos < l)
        key_scr[kt, :, :width] = _ordered_key(jnp.where(vis, score, -jnp.inf))

    def slabs():
        return [(kt, j, pos0 + j) for kt, (_, _, _, _, width, pos0) in enumerate(tiles)
                for j in range(0, width, LANE)]

    def count(pred):
        acc = jnp.zeros((tq, LANE), F32)
        for kt, j, pos in slabs():
            acc = acc + jnp.where(pred(key_scr[kt, :, j:j + LANE], pos), 1.0, 0.0)
        return jnp.sum(acc, axis=1, keepdims=True)

    def wide(col):
        return jnp.broadcast_to(col, (tq, LANE))

    def bit_body(b, lo):
        cand = lo + jnp.left_shift(jnp.int32(1), 31 - b)
        cand_w = wide(cand)
        c = count(lambda ks, _: ks >= cand_w)
        return jnp.where(c >= kf, cand, lo)

    thr = lax.fori_loop(0, 32, bit_body, jnp.full((tq, 1), INT_MIN, I32))
    thr_w = wide(thr)

    n_ge = count(lambda ks, _: ks >= thr_w)
    n_gt = count(lambda ks, _: ks > thr_w)
    excess = (n_ge > kf) & (thr > KEY_NEG_INF)
    thr2_scr[...] = jnp.zeros((tq, 1), I32)

    @pl.when(jnp.max(jnp.where(excess, 1.0, 0.0)) > 0.0)
    def _():
        need = kf - n_gt
        nbits = span.bit_length()

        def bit2_body(b, lo):
            cand = lo + jnp.left_shift(jnp.int32(1), nbits - 1 - b)
            cand_w = wide(cand)
            c = count(lambda ks, pos: jnp.where(ks == thr_w, span - (pos + lane), 0) >= cand_w)
            return jnp.where(c >= need, cand, lo)

        thr2_scr[...] = lax.fori_loop(0, nbits, bit2_body, jnp.zeros((tq, 1), I32))

    thr2_w = wide(thr2_scr[...])

    for kt, j, pos in slabs():
        ks = key_scr[kt, :, j:j + LANE]
        tie = jnp.where(span - (pos + lane) >= thr2_w, 0.0, MASKED)
        bias = jnp.where(ks > thr_w, 0.0, jnp.where(ks == thr_w, tie, MASKED))
        bias_scr[kt, :, j:j + LANE] = jnp.where(ks > KEY_NEG_INF, bias, MASKED)

    for pr in range(aw // LANE):
        cols = slice(pr * LANE, (pr + 1) * LANE)
        q2 = split_heads(qa_ref[:, cols])
        carry = _flash_init(2 * tq, LANE)
        for kt, (_, k_ref, v_ref, r0, width, _) in enumerate(tiles):
            s = lax.dot_general(q2, k_ref[r0:r0 + width, cols], _NT, preferred_element_type=F32)
            bias = bias_scr[kt, :, :width]
            s = s * (HEAD ** -0.5) + jnp.concatenate([bias, bias], axis=0)
            carry = _flash_step(carry, s, v_ref[r0:r0 + width, cols])
        _, den, acc = carry
        o2 = acc / den
        o_ref[:, cols] = jnp.where(lo_half, o2[:tq], o2[tq:]).astype(BF16)


def _dsa_attention(qi, wi, qa, ki_cache, k_cache, v_cache, ki_new, k_new, v_new, layer, grp):
    b, t, aw = qa.shape
    past, tn = k_cache.shape[2], k_new.shape[1]
    assert past % LANE == 0 and tn % LANE == 0
    tk = next(c for c in (512, 384, 256, 128) if past % c == 0)
    width = max(tk, tn)
    topk = min(TOPK_MAX, (past + t) // 4)
    qblk = lambda w: pl.BlockSpec((None, t, w), lambda i: (i, 0, 0))
    cblk = lambda w: pl.BlockSpec((None, None, past, w), lambda i: (layer, i, 0, 0))
    nblk = lambda w: pl.BlockSpec((None, tn, w), lambda i: (i, 0, 0))
    return pl.pallas_call(
        functools.partial(_dsa_kernel, tq=t, tk=tk, past=past, topk=topk, aw=aw),
        grid=(b,),
        in_specs=[qblk(IDX_HEADS * HEAD), qblk(LANE), qblk(aw), cblk(LANE), cblk(aw), cblk(aw),
                  nblk(LANE), nblk(aw), nblk(aw)],
        out_specs=qblk(aw),
        out_shape=jax.ShapeDtypeStruct((b, t, aw), BF16),
        scratch_shapes=[pltpu.VMEM((past // tk + 1, t, width), I32),
                        pltpu.VMEM((past // tk + 1, t, width), F32),
                        pltpu.VMEM((t, 1), I32)],
        compiler_params=_params(("arbitrary",)),
        name="dsa_attention",
    )(qi, wi, qa, ki_cache, k_cache, v_cache, ki_new, k_new, v_new)


def _mla_latent_kernel(qn_ref, qr_ref, latp_ref, krp_ref, latn_ref, krn_ref, wuk_ref, wuv_ref, o_ref,
                       *, t, l, past, ch):
    lo_half = lax.broadcasted_iota(I32, (1, LANE), 1) < HEAD
    scale = (C_NOPE + HEAD) ** -0.5
    q_lat, q_rope = [], []
    for h in range(ch):
        cols = slice(h * C_NOPE, (h + 1) * C_NOPE)
        q_lat.append(lax.dot_general(qn_ref[:, cols], wuk_ref[:, cols], _NT, preferred_element_type=F32))
        qs = qr_ref[:, (h // 2) * LANE:(h // 2 + 1) * LANE]
        zero = jnp.zeros_like(qs)
        q_rope.append(jnp.where(lo_half, qs, zero) if h % 2 == 0 else jnp.where(lo_half, zero, qs))
    q_lat = jnp.concatenate(q_lat, axis=0).astype(BF16)
    q_rope = jnp.concatenate(q_rope, axis=0)
    qchunk = jnp.concatenate([(past + lax.broadcasted_iota(I32, (t, 1), 0)) // CHUNK] * ch, axis=0)
    kr_past = krp_ref[...].astype(BF16)
    segments = ((latp_ref[...].astype(BF16), jnp.concatenate([kr_past, kr_past], axis=1), 0, past),
                (latn_ref[...], krn_ref[...], past, l))
    scores = []
    for lat, kr, pos0, pos_end in segments:
        s = (lax.dot_general(q_lat, lat, _NT, preferred_element_type=F32)
             + lax.dot_general(q_rope, kr, _NT, preferred_element_type=F32)) * scale
        kpos = pos0 + lax.broadcasted_iota(I32, (1, lat.shape[0]), 1)
        scores.append(jnp.where((kpos // CHUNK <= qchunk) & (kpos < pos_end), s, MASKED))
    m = functools.reduce(jnp.maximum, [jnp.max(s, axis=1, keepdims=True) for s in scores])
    ps = [jnp.exp(s - m) for s in scores]
    den = sum(jnp.sum(p, axis=1, keepdims=True) for p in ps)
    acc = sum(jnp.dot(p.astype(BF16), seg[0], preferred_element_type=F32) for p, seg in zip(ps, segments))
    o_lat = (acc / den).astype(BF16)
    for h in range(ch):
        cols = slice(h * C_V, (h + 1) * C_V)
        o_ref[:, cols] = jnp.dot(o_lat[h * t:(h + 1) * t, :], wuv_ref[:, cols],
                                 preferred_element_type=F32).astype(BF16)


def _mla_attention_latent(qn, qr, lat_cache, kr_cache, lat_new, kr_new, w_uk, w_uv, layer, grp):
    b, t, wn = qn.shape
    ch = wn // C_NOPE
    past, r = lat_cache.shape[2:]
    tn = lat_new.shape[1]
    qblk = lambda w: pl.BlockSpec((None, t, w), lambda i: (i, 0, 0))
    cblk = lambda w: pl.BlockSpec((None, None, past, w), lambda i: (layer, i, 0, 0))
    nblk = lambda w: pl.BlockSpec((None, tn, w), lambda i: (i, 0, 0))
    wblk = pl.BlockSpec((None, r, wn), lambda i: (layer, 0, 0))
    return pl.pallas_call(
        functools.partial(_mla_latent_kernel, t=t, l=past + t, past=past, ch=ch),
        grid=(b,),
        in_specs=[qblk(wn), qblk(ch * HEAD), cblk(r), cblk(HEAD), nblk(r), nblk(LANE), wblk, wblk],
        out_specs=qblk(wn),
        out_shape=jax.ShapeDtypeStruct((b, t, wn), BF16),
        compiler_params=_params(("arbitrary",)),
        name="mla_attention_latent",
    )(qn, qr, lat_cache, kr_cache, lat_new, kr_new, w_uk, w_uv)


ONES_ROWS = 16


def _flash_t_init(m_scr, acc_scr):
    m_scr[...] = jnp.full(m_scr.shape, MASKED, F32)
    acc_scr[...] = jnp.zeros(acc_scr.shape, F32)


def _flash_t_stage(slot, h, s, s_scr):
    s_scr[slot, h] = s
    return jnp.max(s, axis=0, keepdims=True)


def _flash_t_step(slot, h, s_max, vt, c, s_scr, m_scr, acc_scr):
    m_old = m_scr[h]
    m_new = jnp.maximum(m_old, s_max)
    alpha = jnp.exp2((m_old - m_new) * c)
    p = jnp.exp2((s_scr[slot, h] - m_new) * c).astype(BF16)
    vt_ones = jnp.concatenate([vt, jnp.ones((ONES_ROWS, vt.shape[1]), BF16)], axis=0)
    acc_scr[h] = alpha * acc_scr[h] + jnp.dot(vt_ones, p, preferred_element_type=F32)
    m_scr[h] = m_new


def _flash_t_out(o_ref, acc_scr):
    heads, width = acc_scr.shape[0], acc_scr.shape[1] - ONES_ROWS
    ot = jnp.concatenate([acc_scr[h, :width, :] / acc_scr[h, width:width + 1, :] for h in range(heads)], axis=0)
    o_ref[...] = ot.T.astype(BF16)


def _split_heads_t(slab_t):
    row_lo = lax.broadcasted_iota(I32, (LANE, 1), 0) < HEAD
    zero = jnp.zeros_like(slab_t)
    return jnp.where(row_lo, slab_t, zero), jnp.where(row_lo, zero, slab_t)


def _dsa_t_kernel(qi_ref, wi_ref, ki_ref, qa_ref, k_ref, vt_ref, o_ref,
                  key_scr, hi_scr, lo_scr, bias_scr, thr2_scr, qit_scr, qat_scr, s_scr, m_scr, acc_scr,
                  *, tq, tk, l, past, topk):
    heads = acc_scr.shape[0]
    qpos0 = past + pl.program_id(1) * tq
    ntiles = _visible_tiles(qpos0, tq, l, tk)
    qchunk = (qpos0 + lax.broadcasted_iota(I32, (1, tq), 1)) // CHUNK
    krow = lax.broadcasted_iota(I32, (tk, 1), 0)

    for s in range(IDX_HEADS // 2):
        qit_scr[2 * s], qit_scr[2 * s + 1] = _split_heads_t(qi_ref[:, s * LANE:(s + 1) * LANE].T)
    for s in range(heads // 2):
        slab_t = (qa_ref[:, s * LANE:(s + 1) * LANE].astype(F32) * (HEAD ** -0.5)).astype(BF16).T
        qat_scr[2 * s], qat_scr[2 * s + 1] = _split_heads_t(slab_t)
    w_t = wi_ref[...].T * (HEAD ** -0.5)
    w_rows = [w_t[h:h + 1, :] for h in range(IDX_HEADS)]

    nfull = jnp.minimum(l, (qpos0 // CHUNK + 1) * CHUNK) // tk

    def score_body(kt, masked):
        k0 = pl.multiple_of(kt * tk, tk)
        ki_tile = ki_ref[pl.ds(k0, tk), :]
        score = jnp.zeros((tk, tq), F32)
        for h in range(IDX_HEADS):
            s = jnp.dot(ki_tile, qit_scr[h], preferred_element_type=F32)
            score = score + jnp.maximum(s, 0.0) * w_rows[h]
        if masked:
            kpos = k0 + krow
            score = jnp.where((kpos // CHUNK <= qchunk) & (kpos < l), score, -jnp.inf)
        key = _ordered_key(score)
        key_scr[kt] = key
        hi_scr[kt] = (key >> 16).astype(I16)
        lo_scr[kt] = ((key & 0xFFFF) + I16_MIN).astype(I16)
        return 0

    lax.fori_loop(0, nfull, lambda kt, _: score_body(kt, False), 0)
    lax.fori_loop(nfull, ntiles, lambda kt, _: score_body(kt, True), 0)

    def count(pred):
        def body(kt, acc):
            hit = jnp.where(pred(key_scr[kt], kt * tk), 1.0, 0.0)
            return acc + hit.reshape(tk // SUBLANE, SUBLANE, tq).sum(axis=0)
        acc = lax.fori_loop(0, ntiles, body, jnp.zeros((SUBLANE, tq), F32))
        return jnp.sum(acc, axis=0, keepdims=True)

    def count16(half_scr, cand, strict):
        rows = 2 * SUBLANE
        cand16 = cand.astype(I16)

        def body(kt, acc):
            half = half_scr[kt]
            hit = jnp.where(half > cand16 if strict else half >= cand16, jnp.int16(1), jnp.int16(0))
            parts = hit.reshape(tk // (4 * rows), 4, rows, tq)
            for g in range(parts.shape[0]):
                acc = acc + parts[g]
            return acc

        acc = lax.fori_loop(0, ntiles, body, jnp.zeros((4, rows, tq), I16))
        return acc.astype(I32).sum(axis=0).sum(axis=0, keepdims=True)

    def kth_largest16(half_scr, k_need):
        def bit_body(b, lo):
            cand = lo + jnp.left_shift(jnp.int32(1), 15 - b)
            return jnp.where(count16(half_scr, cand, False) >= k_need, cand, lo)
        return lax.fori_loop(0, 16, bit_body, jnp.full((1, tq), I16_MIN, I32))

    thr_hi = kth_largest16(hi_scr, jnp.full((1, tq), topk, I32))
    n_gt_hi = count16(hi_scr, thr_hi, True)
    thr_hi16 = thr_hi.astype(I16)

    def mark_body(kt, _):
        lo_scr[kt] = jnp.where(hi_scr[kt] == thr_hi16, lo_scr[kt], jnp.int16(I16_MIN))
        return 0

    lax.fori_loop(0, ntiles, mark_body, 0)
    thr_lo = kth_largest16(lo_scr, topk - n_gt_hi)
    thr = thr_hi * 65536 + (thr_lo - I16_MIN)

    lp = key_scr.shape[0] * tk
    n_gt = n_gt_hi + count16(lo_scr, thr_lo, True)
    n_ge = jnp.where(thr_lo > I16_MIN, n_gt_hi + count16(lo_scr, thr_lo, False), count16(hi_scr, thr_hi, False))
    excess = (n_ge > topk) & (thr > KEY_NEG_INF)
    thr2_scr[...] = jnp.zeros((1, tq), I32)

    @pl.when(jnp.max(jnp.where(excess, 1.0, 0.0)) > 0.0)
    def _():
        need = (topk - n_gt).astype(F32)
        nbits = lp.bit_length()

        def bit2_body(b, lo):
            cand = lo + jnp.left_shift(jnp.int32(1), nbits - 1 - b)
            c = count(lambda ks, base: jnp.where(ks == thr, lp - (base + krow), 0) >= cand)
            return jnp.where(c >= need, cand, lo)

        thr2_scr[...] = lax.fori_loop(0, nbits, bit2_body, jnp.zeros((1, tq), I32))

    thr2 = thr2_scr[...]

    def bias_body(kt, _):
        ks = key_scr[kt]
        tie = jnp.where(lp - (kt * tk + krow) >= thr2, 0.0, MASKED)
        bias = jnp.where(ks > thr, 0.0, jnp.where(ks == thr, tie, MASKED))
        bias_scr[kt] = jnp.where(ks > KEY_NEG_INF, bias, MASKED)
        return 0

    lax.fori_loop(0, ntiles, bias_body, 0)

    _flash_t_init(m_scr, acc_scr)

    def att_body(kt, _):
        k0 = pl.multiple_of(kt * tk, tk)
        slot = kt % 2
        s_max = []
        for h in range(heads):
            cols = slice((h // 2) * LANE, (h // 2 + 1) * LANE)
            s = jnp.dot(k_ref[pl.ds(k0, tk), cols], qat_scr[h], preferred_element_type=F32) + bias_scr[kt]
            s_max.append(_flash_t_stage(slot, h, s, s_scr))
        for h in range(heads):
            _flash_t_step(slot, h, s_max[h], vt_ref[h * HEAD:(h + 1) * HEAD, pl.ds(k0, tk)], LOG2E,
                          s_scr, m_scr, acc_scr)
        return 0

    lax.fori_loop(0, ntiles, att_body, 0)
    _flash_t_out(o_ref, acc_scr)


def _dsa_attention_t(qi, wi, ki, qa, k, vt, grp, tq):
    b, t, aw = qa.shape
    lp, tk = _key_tile(t)
    assert lp == t
    heads = aw // HEAD
    topk = min(TOPK_MAX, t // 4)
    qblk = lambda w: pl.BlockSpec((None, tq, w), lambda i, j: (i, j, 0))
    kblk = lambda w: pl.BlockSpec((None, t, w), lambda i, j: (i, 0, 0))
    return pl.pallas_call(
        functools.partial(_dsa_t_kernel, tq=tq, tk=tk, l=t, past=grp.past, topk=topk),
        grid=(b, t // tq),
        in_specs=[qblk(IDX_HEADS * HEAD), qblk(LANE), kblk(LANE), qblk(aw), kblk(aw),
                  pl.BlockSpec((aw, t), lambda i, j: (0, i))],
        out_specs=qblk(aw),
        out_shape=jax.ShapeDtypeStruct((b, t, aw), BF16),
        scratch_shapes=[pltpu.VMEM((t // tk, tk, tq), I32),
                        pltpu.VMEM((t // tk, tk, tq), I16),
                        pltpu.VMEM((t // tk, tk, tq), I16),
                        pltpu.VMEM((t // tk, tk, tq), F32),
                        pltpu.VMEM((1, tq), I32),
                        pltpu.VMEM((IDX_HEADS, LANE, tq), BF16),
                        pltpu.VMEM((heads, LANE, tq), BF16),
                        pltpu.VMEM((2, heads, tk, tq), F32),
                        pltpu.VMEM((heads, 1, tq), F32),
                        pltpu.VMEM((heads, HEAD + ONES_ROWS, tq), F32)],
        compiler_params=_params(("arbitrary", "arbitrary")),
        name="dsa_attention_t",
    )(qi, wi, ki, qa, k, vt)


def _mla_t_kernel(qn_ref, qr_ref, kn_ref, vt_ref, kr_ref, o_ref, qt_scr, s_scr, m_scr, acc_scr,
                  *, tq, tk, l, past):
    ch = acc_scr.shape[0]
    qpos0 = past + pl.program_id(1) * tq
    ntiles = _visible_tiles(qpos0, tq, l, tk)
    nfull = jnp.minimum(l, (qpos0 // CHUNK + 1) * CHUNK) // tk
    qchunk = (qpos0 + lax.broadcasted_iota(I32, (1, tq), 1)) // CHUNK
    krow = lax.broadcasted_iota(I32, (tk, 1), 0)
    c = (C_NOPE + HEAD) ** -0.5 * LOG2E
    for s in range(ch // 2):
        pair = _split_heads_t(qr_ref[:, s * LANE:(s + 1) * LANE].T)
        for half in range(2):
            h = 2 * s + half
            qt_scr[h] = jnp.concatenate([qn_ref[:, h * LANE:(h + 1) * LANE].T, pair[half]], axis=0)
    _flash_t_init(m_scr, acc_scr)

    def tile(kt, masked):
        k0 = pl.multiple_of(kt * tk, tk)
        k_rope = kr_ref[pl.ds(k0, tk), :]
        slot = kt % 2
        if masked:
            kpos = k0 + krow
            bias = jnp.where((kpos // CHUNK <= qchunk) & (kpos < l), 0.0, MASKED)
        s_max = []
        for h in range(ch):
            kcat = jnp.concatenate([kn_ref[pl.ds(k0, tk), h * LANE:(h + 1) * LANE], k_rope], axis=1)
            s = jnp.dot(kcat, qt_scr[h], preferred_element_type=F32)
            s_max.append(_flash_t_stage(slot, h, s + bias if masked else s, s_scr))
        for h in range(ch):
            _flash_t_step(slot, h, s_max[h], vt_ref[h * C_V:(h + 1) * C_V, pl.ds(k0, tk)], c,
                          s_scr, m_scr, acc_scr)
        return 0

    lax.fori_loop(0, nfull, lambda kt, _: tile(kt, False), 0)
    lax.fori_loop(nfull, ntiles, lambda kt, _: tile(kt, True), 0)
    _flash_t_out(o_ref, acc_scr)


def _mla_attention_t(qn, qr, kn, vt, kr, grp, tq):
    b, t, wn = qn.shape
    ch = wn // C_NOPE
    lp, tk = _key_tile(t)
    assert lp == t and ch % 2 == 0
    qblk = lambda w: pl.BlockSpec((None, tq, w), lambda i, j: (i, j, 0))
    kblk = lambda w: _resident((None, t, w), lambda i, j: (i, 0, 0))
    return pl.pallas_call(
        functools.partial(_mla_t_kernel, tq=tq, tk=tk, l=t, past=grp.past),
        grid=(b, t // tq),
        in_specs=[qblk(wn), qblk(ch * HEAD), kblk(wn), _resident((wn, t), lambda i, j: (0, i)), kblk(LANE)],
        out_specs=qblk(wn),
        out_shape=jax.ShapeDtypeStruct((b, t, wn), BF16),
        scratch_shapes=[pltpu.VMEM((ch, 2 * LANE, tq), BF16),
                        pltpu.VMEM((2, ch, tk, tq), F32),
                        pltpu.VMEM((ch, 1, tq), F32),
                        pltpu.VMEM((ch, C_V + ONES_ROWS, tq), F32)],
        compiler_params=_params(("arbitrary", "arbitrary")),
        name="mla_attention_t",
    )(qn, qr, kn, vt, kr)


def _causal_conv(u, e0, e1, w, seg):
    rmod = lax.broadcasted_iota(I32, (u.shape[0], 1), 0) % seg
    u1 = jnp.where(rmod == 0, e1, pltpu.roll(u, 1, 0))
    u2 = jnp.where(rmod == 0, e0, jnp.where(rmod == 1, e1, pltpu.roll(u, 2, 0)))
    return u2 * w[0:1] + u1 * w[1:2] + u * w[2:3]


def _layer_norm(z, g, b):
    mu = jnp.mean(z, axis=-1, keepdims=True)
    zc = z - mu
    var = jnp.mean(zc * zc, axis=-1, keepdims=True)
    return zc * lax.rsqrt(var + LN_EPS) * g + b


def _outproj_kernel(*refs, alpha, seq_tiles, seg, carried):
    if carried:
        (x_ref, oa_ref, bg_ref, u_ref, oc_ref, w_ref, cw_ref, g1_ref, lng_ref, lnb_ref, sc2_ref, sh2_ref,
         x1_ref, h2_ref, prev_scr) = refs

        @pl.when(pl.program_id(0) % seq_tiles == 0)
        def _():
            prev_scr[...] = jnp.zeros_like(prev_scr)

        e0, e1 = prev_scr[SUBLANE - 2:SUBLANE - 1, :], prev_scr[SUBLANE - 1:SUBLANE, :]
    else:
        (x_ref, oa_ref, bg_ref, u_ref, oc_ref, w_ref, cw_ref, g1_ref, lng_ref, lnb_ref, sc2_ref, sh2_ref,
         e0_ref, e1_ref, x1_ref, h2_ref) = refs
        e0, e1 = e0_ref[...], e1_ref[...]
    u = u_ref[...]
    yb = bg_ref[...] * _causal_conv(u, e0, e1, cw_ref[...], seg)
    if carried:
        prev_scr[...] = u[u.shape[0] - SUBLANE:, :]
    mixed = jnp.concatenate([oa_ref[...], yb.astype(BF16), oc_ref[...]], axis=1)
    tm = mixed.shape[0]
    n_split = 4 if tm % (4 * 2 * SUBLANE) == 0 else 1

    def rows_of(ref, rows):
        return ref[...] if ref.shape[0] == 1 else ref[rows, :]

    halves = [slice(c * tm // n_split, (c + 1) * tm // n_split) for c in range(n_split)]
    mixes = [jnp.dot(mixed[rows, :], w_ref[...], preferred_element_type=F32) for rows in halves]
    for rows, mix in zip(halves, mixes):
        x1 = _layer_norm(alpha * x_ref[rows, :] + (1.0 + rows_of(g1_ref, rows)) * mix, lng_ref[...], lnb_ref[...])
        x1_ref[rows, :] = x1
        h2_ref[rows, :] = (x1 * (1.0 + rows_of(sc2_ref, rows)) + rows_of(sh2_ref, rows)).astype(BF16)


def _out_projection(x, oa, bg, u, oc, w_out, layer, conv_w, g1, ln_g, ln_b, sc2, sh2, prev, grp, alpha, tm):
    m, d = x.shape
    aw, bw, cw = oa.shape[1], bg.shape[1], oc.shape[1]
    n_i = m // tm
    tiles_per_mod = n_i // g1.shape[0]
    mod_rows = g1.shape[1]
    row = lambda i: (i, 0)
    fix = lambda i: (0, 0)
    mod_spec = pl.BlockSpec((None, mod_rows, d), lambda i: (i // tiles_per_mod, 0, 0))
    in_specs = [pl.BlockSpec((tm, d), row), pl.BlockSpec((tm, aw), row), pl.BlockSpec((tm, bw), row),
                pl.BlockSpec((tm, bw), row), pl.BlockSpec((tm, cw), row),
                _resident((None, aw + bw + cw, d), lambda i: (layer, 0, 0)), pl.BlockSpec((CONV_W, bw), fix),
                mod_spec, pl.BlockSpec((1, d), fix), pl.BlockSpec((1, d), fix), mod_spec, mod_spec]
    args = [x, oa, bg, u, oc, w_out, conv_w, g1, ln_g.reshape(1, d), ln_b.reshape(1, d), sc2, sh2]
    scratch = []
    if prev is None:
        assert grp.t % tm == 0
        scratch = [pltpu.VMEM((SUBLANE, bw), F32)]
    else:
        assert tm % grp.t == 0
        in_specs += [pl.BlockSpec((tm, bw), row), pl.BlockSpec((tm, bw), row)]
        args += [prev[0], prev[1]]
    return pl.pallas_call(
        functools.partial(_outproj_kernel, alpha=alpha, seq_tiles=max(grp.t // tm, 1), seg=min(grp.t, tm),
                          carried=prev is None),
        grid=(n_i,),
        in_specs=in_specs,
        out_specs=[pl.BlockSpec((tm, d), row), pl.BlockSpec((tm, d), row)],
        out_shape=[jax.ShapeDtypeStruct((m, d), F32), jax.ShapeDtypeStruct((m, d), BF16)],
        scratch_shapes=scratch,
        compiler_params=_params(("arbitrary",)),
        name="out_projection",
    )(*args)


def _ffn_kernel(*refs, alpha, seq_tiles, seg, carried):
    if carried:
        (h_ref, x_ref, wg_ref, wu_ref, wd_ref, cw_ref, g2_ref, lng_ref, lnb_ref,
         o_ref, gt_ref, acc_scr, prev_scr) = refs
    else:
        (h_ref, x_ref, wg_ref, wu_ref, wd_ref, cw_ref, g2_ref, lng_ref, lnb_ref, e0_ref, e1_ref,
         o_ref, gt_ref, acc_scr) = refs
    f = pl.program_id(1)
    if carried:
        @pl.when(pl.program_id(0) % seq_tiles == 0)
        def _():
            prev_scr[f] = jnp.zeros(prev_scr.shape[1:], F32)

    @pl.when(f == 0)
    def _():
        acc_scr[...] = jnp.zeros_like(acc_scr)

    h = h_ref[...]
    tm, tf = h.shape[0], wg_ref.shape[1]
    n_split = 2 if tf % (2 * LANE) == 0 else 1
    halves = [slice(c * tf // n_split, (c + 1) * tf // n_split) for c in range(n_split)]
    gates = [jnp.dot(h, wg_ref[:, cols], preferred_element_type=F32) for cols in halves]
    ups = [jnp.dot(h, wu_ref[:, cols], preferred_element_type=F32) for cols in halves]
    for cols, gate, up in zip(halves, gates, ups):
        if carried:
            e0, e1 = prev_scr[f, SUBLANE - 2:SUBLANE - 1, cols], prev_scr[f, SUBLANE - 1:SUBLANE, cols]
        else:
            e0, e1 = e0_ref[:, cols], e1_ref[:, cols]
        conv = _causal_conv(gate, e0, e1, cw_ref[:, cols], seg)
        if carried:
            prev_scr[f, :, cols] = gate[tm - SUBLANE:, :]
        gt_ref[:, cols] = gate[tm - gt_ref.shape[0]:, :]
        act = (jax.nn.silu(conv) * up).astype(BF16)
        acc_scr[...] = jnp.dot(act, wd_ref[cols, :], preferred_element_type=F32) + acc_scr[...]

    @pl.when(f == pl.num_programs(1) - 1)
    def _():
        z = alpha * x_ref[...] + (1.0 + g2_ref[...]) * acc_scr[...]
        o_ref[...] = _layer_norm(z, lng_ref[...], lnb_ref[...])


def _channel_mixer(h2, x1, w_gu, w_down, layer, conv_w, g2, ln_g, ln_b, prev, grp, alpha, tm, tf):
    m, d = x1.shape
    dff = w_down.shape[1]
    n_i, n_f = m // tm, dff // tf
    tiles_per_mod = n_i // g2.shape[0]
    mod_rows = g2.shape[1]
    fix = lambda i, f: (0, 0)
    in_specs = [pl.BlockSpec((tm, d), lambda i, f: (i, 0)), pl.BlockSpec((tm, d), lambda i, f: (i, 0)),
                pl.BlockSpec((None, d, tf), lambda i, f: (layer, 0, f)),
                pl.BlockSpec((None, d, tf), lambda i, f: (layer, 0, n_f + f)),
                pl.BlockSpec((None, tf, d), lambda i, f: (layer, f, 0)),
                pl.BlockSpec((CONV_W, tf), lambda i, f: (0, f)),
                pl.BlockSpec((None, mod_rows, d), lambda i, f: (i // tiles_per_mod, 0, 0)),
                pl.BlockSpec((1, d), fix), pl.BlockSpec((1, d), fix)]
    args = [h2, x1, w_gu, w_gu, w_down, conv_w, g2, ln_g.reshape(1, d), ln_b.reshape(1, d)]
    scratch = [pltpu.VMEM((tm, d), F32)]
    if prev is None:
        assert grp.t % tm == 0
        scratch.append(pltpu.VMEM((n_f, SUBLANE, tf), F32))
        gt_spec = pl.BlockSpec((None, SUBLANE, tf), lambda i, f: (i, 0, f))
        gt_shape = jax.ShapeDtypeStruct((n_i, SUBLANE, dff), F32)
    else:
        assert tm % grp.t == 0
        in_specs += [pl.BlockSpec((tm, tf), lambda i, f: (i, f))] * 2
        args += [prev[0], prev[1]]
        gt_spec = pl.BlockSpec((tm, tf), lambda i, f: (i, f))
        gt_shape = jax.ShapeDtypeStruct((m, dff), F32)
    return pl.pallas_call(
        functools.partial(_ffn_kernel, alpha=alpha, seq_tiles=max(grp.t // tm, 1), seg=min(grp.t, tm),
                          carried=prev is None),
        grid=(n_i, n_f),
        in_specs=in_specs,
        out_specs=[pl.BlockSpec((tm, d), lambda i, f: (i, 0)), gt_spec],
        out_shape=[jax.ShapeDtypeStruct((m, d), F32), gt_shape],
        scratch_shapes=scratch,
        compiler_params=_params(("arbitrary", "arbitrary")),
        name="channel_mixer",
    )(*args)


def _rope_tables(pos):
    half = HEAD // 2
    inv = jnp.power(jnp.float32(ROPE_THETA), -jnp.arange(half, dtype=F32) / half)
    ang = pos.astype(F32)[:, None] * inv[None, :]
    cos, sin = jnp.cos(ang), jnp.sin(ang)
    zero = jnp.zeros_like(sin)
    reps = LANE // HEAD
    return (jnp.tile(jnp.concatenate([cos, cos], axis=1), (1, reps)),
            jnp.tile(jnp.concatenate([zero, sin], axis=1), (1, reps)),
            jnp.tile(jnp.concatenate([-sin, zero], axis=1), (1, reps)))


def _layer(x, mod, tables, grp, past, layer, big, small, stacks, dm, alpha):
    w_in_p, w_out, w_uk, w_uv, w_uv_t, w_gu, w_down = big
    conv_b_w, kv_norm, ln1_g, ln1_b, ln2_g, ln2_b, conv_f_w = small
    b, t = grp.b, grp.t
    m, d = x.shape
    carried = past is None
    tm = _row_tile(t, 256) if carried else m
    if carried:
        mods = [a.reshape(b, 1, d) for a in jnp.split(mod, N_MOD, axis=-1)]
    else:
        mods = [jnp.repeat(a, t, axis=0).reshape(1, m, d) for a in jnp.split(mod, N_MOD, axis=-1)]
    sh1, sc1, g1, sh2, sc2, g2 = mods

    outs = _in_projection(x, sc1, sh1, w_in_p, layer, tables, kv_norm, stacks, dm, tm)
    stacks = tuple(outs[o] for o in STATE_OUTPUTS)
    qa, _, kab, qi, qcr, _, kib, _, krb, _, vab, bg, u, qcn, _, latb, wi, vat = outs

    three = lambda a: a.reshape(b, t, a.shape[-1])
    if carried:
        tq = _row_tile(t, 256)
        kn, vct = _kv_up(latb, w_uk, w_uv_t, layer)
        oa = _dsa_attention_t(three(qi), three(wi), three(kib), three(qa), three(kab), vat, grp, tq)
        oc = _mla_attention_t(three(qcn), three(qcr), three(kn), vct, three(krb), grp, _row_tile(t, 512))
        prev_b = prev_f = None
    else:
        c_ak, c_av, c_ik, c_lat, c_kr, prev_b, prev_f = past
        new_rows = lambda a: jnp.pad(three(a), ((0, 0), (0, -t % LANE), (0, 0)))
        oa = _dsa_attention(three(qi), three(wi), three(qa), c_ik, c_ak, c_av,
                            new_rows(kib), new_rows(kab), new_rows(vab), layer, grp)
        oc = _mla_attention_latent(three(qcn), three(qcr), c_lat, c_kr, new_rows(latb), new_rows(krb),
                                   w_uk, w_uv, layer, grp)

    def expand(state):
        return jnp.repeat(state[:, 0], t, axis=0), jnp.repeat(state[:, 1], t, axis=0)

    x1, h2 = _out_projection(x, oa.reshape(m, -1), bg, u, oc.reshape(m, -1), w_out, layer, conv_b_w, g1,
                             ln1_g, ln1_b, sc2, sh2, None if carried else expand(prev_b), grp, alpha,
                             _row_tile(t, 512) if carried else m)
    tm_f = _row_tile(t, 512) if carried else m
    tf = _row_tile(dm.dff, 512)
    x2, gate_rows = _channel_mixer(h2, x1, w_gu, w_down, layer, conv_f_w, g2, ln2_g, ln2_b,
                                   None if carried else expand(prev_f), grp, alpha, tm_f, tf)
    if carried:
        new_f = gate_rows.reshape(b, t // tm_f, SUBLANE, dm.dff)[:, -1, SUBLANE - (CONV_W - 1):, :]
    else:
        new_f = gate_rows.reshape(b, t, dm.dff)[:, t - (CONV_W - 1):, :]
    new_b = u.reshape(b, t, dm.bw)[:, t - (CONV_W - 1):, :]
    return x2, stacks, (new_b, new_f)


def _state_outputs(stacks, conv_rows, grp, dm):
    ka, ki, kr, va, lat = stacks
    b, t, heads = grp.b, grp.t, dm.aw // HEAD
    new_b, new_f = [jnp.stack(r) for r in zip(*conv_rows)]
    return (ka.reshape(dm.depth, b, t, heads, HEAD), va.reshape(dm.depth, b, t, heads, HEAD),
            ki.reshape(dm.depth, b, t, HEAD), lat.reshape(dm.depth, b, t, dm.r), kr.reshape(dm.depth, b, t, HEAD),
            new_b, new_f)


def kernel(x_prompt, x_sample, c_prompt, c_sample, cache_a_k, cache_a_v, cache_idx_k, cache_mla_latent,
           cache_mla_krope, state_conv_b, state_conv_ffn, w_in, w_out, conv_b_w, mla_kv_norm, mla_w_uk,
           mla_w_uv, w_mod, b_mod, ln1_g, ln1_b, ln2_g, ln2_b, ffn_w_gu, ffn_conv_w, ffn_w_down):
    depth, d, _ = w_in.shape
    a_heads = cache_a_k.shape[3]
    dm = Dims(d=d, aw=a_heads * HEAD, bw=conv_b_w.shape[2], ch=mla_w_uk.shape[2] // C_NOPE,
              r=mla_w_uk.shape[1], dff=ffn_w_down.shape[1], depth=depth)
    alpha = (2 * depth) ** 0.25
    grp_p = Group(b=x_prompt.shape[0], t=x_prompt.shape[1], past=0)
    grp_s = Group(b=x_sample.shape[0], t=x_sample.shape[1], past=cache_a_k.shape[2])

    n_c = grp_p.b + grp_s.b
    c_all = jnp.concatenate([c_prompt, c_sample, jnp.zeros((-n_c % SUBLANE, d), F32)], axis=0)
    mod = _modulation(c_all, w_mod, b_mod)

    tab_p = _rope_tables(jnp.arange(grp_p.t, dtype=I32))
    tab_s = tuple(jnp.tile(a, (grp_s.b, 1)) for a in _rope_tables(grp_s.past + jnp.arange(grp_s.t, dtype=I32)))

    xp = x_prompt.reshape(grp_p.b * grp_p.t, d)
    xs = x_sample.reshape(grp_s.b * grp_s.t, d)
    conv_p, conv_s = [], []
    stacks_p = stacks_s = None
    w_uv_b = mla_w_uv.astype(BF16)
    big = (_pack_w_in(w_in, dm), w_out.astype(BF16), mla_w_uk.astype(BF16), w_uv_b, jnp.swapaxes(w_uv_b, 1, 2),
           ffn_w_gu.astype(BF16), ffn_w_down.astype(BF16))
    idx_k = cache_idx_k.astype(BF16)
    caches = (cache_a_k.reshape(cache_a_k.shape[:3] + (-1,)).astype(BF16),
              cache_a_v.reshape(cache_a_v.shape[:3] + (-1,)).astype(BF16),
              jnp.concatenate([idx_k, idx_k], axis=-1))
    for l in range(depth):
        small = (conv_b_w[l], mla_kv_norm[l], ln1_g[l], ln1_b[l], ln2_g[l], ln2_b[l], ffn_conv_w[l])
        xp, stacks_p, rp = _layer(xp, mod[l, :grp_p.b], tab_p, grp_p, None, l, big, small, stacks_p, dm, alpha)
        past_l = (*caches, cache_mla_latent, cache_mla_krope, state_conv_b[l], state_conv_ffn[l])
        xs, stacks_s, rs = _layer(xs, mod[l, grp_p.b:n_c], tab_s, grp_s, past_l, l, big, small, stacks_s, dm,
                                  alpha)
        conv_p.append(rp)
        conv_s.append(rs)
    return (xp.reshape(x_prompt.shape), xs.reshape(x_sample.shape),
            *_state_outputs(stacks_p, conv_p, grp_p, dm), *_state_outputs(stacks_s, conv_s, grp_s, dm))
```

```python
import functools
from typing import NamedTuple

import numpy as np
import jax
import jax.numpy as jnp
from jax import lax
from jax.experimental import pallas as pl
from jax.experimental.pallas import tpu as pltpu

F32, BF16, I32, I16 = jnp.float32, jnp.bfloat16, jnp.int32, jnp.int16

CHUNK = 64
CONV_W = 3
ROPE_THETA = 10000.0
HEAD = 64
IDX_HEADS = 16
TOPK_MAX = 256
C_NOPE = 128
C_V = 128
N_MOD = 6
LN_EPS = 1e-5
RMS_EPS = 1e-6

LANE = 128
SUBLANE = 8
VMEM_LIMIT = 50 * 1024 * 1024

MASKED = -1e30
LOG2E = 1.4426950408889634
INT_MIN = -2 ** 31
I16_MIN = -2 ** 15
KEY_NEG_INF = int(np.array(-np.inf, np.float32).view(np.int32)) ^ 0x7FFFFFFF


class Dims(NamedTuple):
    d: int
    aw: int
    bw: int
    ch: int
    r: int
    dff: int
    depth: int


class Group(NamedTuple):
    b: int
    t: int
    past: int


def _row_tile(m, pref):
    if m <= pref:
        return m
    t = pref - pref % SUBLANE
    while m % t:
        t -= SUBLANE
    return t


def _key_tile(l):
    lp = -(-l // LANE) * LANE
    for tk in (512, 384, 256, 128):
        if lp % tk == 0:
            return lp, tk
    raise AssertionError(lp)


def _params(sem):
    return pltpu.CompilerParams(dimension_semantics=sem, vmem_limit_bytes=VMEM_LIMIT)


def _resident(shape, index_map):
    return pl.BlockSpec(shape, index_map, pipeline_mode=pl.Buffered(1))


def _mod_kernel(c_ref, w_ref, b_ref, o_ref):
    a = jax.nn.silu(c_ref[...]).astype(BF16)
    o_ref[...] = jnp.dot(a, w_ref[...].astype(BF16), preferred_element_type=F32) + b_ref[...]


def _modulation(c, w_mod, b_mod):
    depth, d, n = w_mod.shape
    rows = c.shape[0]
    tn = _row_tile(n, 1024)
    return pl.pallas_call(
        _mod_kernel,
        grid=(depth, n // tn),
        in_specs=[pl.BlockSpec((rows, d), lambda l, j: (0, 0)),
                  pl.BlockSpec((None, d, tn), lambda l, j: (l, 0, j)),
                  pl.BlockSpec((None, 1, tn), lambda l, j: (l, 0, j))],
        out_specs=pl.BlockSpec((None, rows, tn), lambda l, j: (l, 0, j)),
        out_shape=jax.ShapeDtypeStruct((depth, rows, n), F32),
        compiler_params=_params(("arbitrary", "arbitrary")),
        name="modulation",
    )(c, w_mod, b_mod.reshape(depth, 1, n))


def _pack_w_in(w, dm):
    d, aw, bw, ch, r = dm.d, dm.aw, dm.bw, dm.ch, dm.r
    lead = w.shape[:-1]
    o = np.cumsum([0, aw, aw, aw, IDX_HEADS * HEAD, HEAD, IDX_HEADS, bw, bw, bw, ch * (C_NOPE + HEAD), r, HEAD])
    qa, ka, va, qi, ki, wi, bg, cg, xb, qc, lat, kr = [w[..., o[i]:o[i + 1]] for i in range(12)]
    qc = qc.reshape(lead + (ch, C_NOPE + HEAD))
    qcn = qc[..., :C_NOPE].reshape(lead + (ch * C_NOPE,))
    qcr = qc[..., C_NOPE:].reshape(lead + (ch * HEAD,))
    pad = jnp.zeros(lead + (LANE - IDX_HEADS,), w.dtype)
    return jnp.concatenate([qa, ka, qi, qcr, ki, ki, kr, kr, va, bg, cg, xb, qcn, lat, wi, pad],
                           axis=-1).astype(BF16)


def _rope(acc, cos, s1, s2):
    outs = []
    for s in range(acc.shape[1] // LANE):
        xs = acc[:, s * LANE:(s + 1) * LANE]
        outs.append(xs * cos + pltpu.roll(xs, HEAD // 2, 1) * s1 + pltpu.roll(xs, LANE - HEAD // 2, 1) * s2)
    return outs[0] if len(outs) == 1 else jnp.concatenate(outs, axis=1)


def _inproj_kernel(x_ref, sc_ref, sh_ref, w_ref, cos_ref, s1_ref, s2_ref, nrm_ref, *rest, dm, n_alias):
    (qa_ref, ka_ref, kab_ref, qi_ref, qcr_ref, ki_ref, kib_ref, kr_ref, krb_ref,
     va_ref, vab_ref, bg_ref, u_ref, qcn_ref, lat_ref, latb_ref, wi_ref, vat_ref) = rest[n_alias:]
    aw, bw, ch, r = dm.aw, dm.bw, dm.ch, dm.r
    h = (x_ref[...] * (1.0 + sc_ref[...]) + sh_ref[...]).astype(BF16)
    cos, s1, s2 = cos_ref[...], s1_ref[...], s2_ref[...]
    col = [0]

    def proj(width):
        c0 = col[0]
        col[0] = c0 + width
        return jnp.dot(h, w_ref[:, c0:c0 + width], preferred_element_type=F32)

    def pieces(width, step=512):
        return [(o, min(step, width - o)) for o in range(0, width, step)]

    for o, wd in pieces(aw):
        qa_ref[:, o:o + wd] = _rope(proj(wd), cos, s1, s2).astype(BF16)
    for o, wd in pieces(aw):
        y = _rope(proj(wd), cos, s1, s2)
        ka_ref[:, o // HEAD:(o + wd) // HEAD, :] = y.reshape(y.shape[0], wd // HEAD, HEAD)
        kab_ref[:, o:o + wd] = y.astype(BF16)
    for o, wd in pieces(IDX_HEADS * HEAD):
        qi_ref[:, o:o + wd] = _rope(proj(wd), cos, s1, s2).astype(BF16)
    for o, wd in pieces(ch * HEAD):
        qcr_ref[:, o:o + wd] = _rope(proj(wd), cos, s1, s2).astype(BF16)
    y = _rope(proj(2 * LANE), cos, s1, s2)
    for c, (f32_ref, b16_ref) in enumerate(((ki_ref, kib_ref), (kr_ref, krb_ref))):
        f32_ref[...] = y[:, c * LANE:c * LANE + HEAD]
        b16_ref[...] = y[:, c * LANE:(c + 1) * LANE].astype(BF16)
    for o, wd in pieces(aw):
        y = proj(wd)
        va_ref[:, o // HEAD:(o + wd) // HEAD, :] = y.reshape(y.shape[0], wd // HEAD, HEAD)
        vab_ref[:, o:o + wd] = y.astype(BF16)
        vat_ref[o:o + wd, :] = y.T.astype(BF16)
    for o, wd in pieces(bw):
        bg_ref[:, o:o + wd] = proj(wd)
    c_cg = col[0]
    for o, wd in pieces(bw):
        cg = jnp.dot(h, w_ref[:, c_cg + o:c_cg + o + wd], preferred_element_type=F32)
        xb = jnp.dot(h, w_ref[:, c_cg + bw + o:c_cg + bw + o + wd], preferred_element_type=F32)
        u_ref[:, o:o + wd] = cg * xb
    col[0] = c_cg + 2 * bw
    for o, wd in pieces(ch * C_NOPE):
        qcn_ref[:, o:o + wd] = proj(wd).astype(BF16)
    lat = proj(r)
    lat = lat * lax.rsqrt(jnp.mean(lat * lat, axis=-1, keepdims=True) + RMS_EPS) * nrm_ref[...]
    lat_ref[...] = lat
    latb_ref[...] = lat.astype(BF16)
    wi_ref[...] = proj(LANE) * (IDX_HEADS ** -0.5)


STATE_OUTPUTS = (1, 5, 7, 9, 14)
HEAD_OUTPUTS = (1, 9)


def _in_projection(x, sc, sh, w_packed, layer, tables, kv_norm, stacks, dm, tm):
    m, d = x.shape
    aw, bw, ch, r = dm.aw, dm.bw, dm.ch, dm.r
    npk = w_packed.shape[2]
    n_i = m // tm
    tiles_per_mod = n_i // sc.shape[0]
    mod_rows = sc.shape[1]
    tab_tiles = tables[0].shape[0] // tm
    widths = [(aw, BF16), (aw, F32), (aw, BF16), (IDX_HEADS * HEAD, BF16), (ch * HEAD, BF16),
              (HEAD, F32), (LANE, BF16), (HEAD, F32), (LANE, BF16),
              (aw, F32), (aw, BF16), (bw, F32), (bw, F32), (ch * C_NOPE, BF16), (r, F32), (r, BF16),
              (LANE, F32)]
    row = lambda i: (i, 0)
    mod_spec = pl.BlockSpec((None, mod_rows, d), lambda i: (i // tiles_per_mod, 0, 0))
    tab_spec = pl.BlockSpec((tm, LANE), lambda i: (i % tab_tiles, 0))
    in_specs = [pl.BlockSpec((tm, d), row), mod_spec, mod_spec,
                _resident((None, d, npk), lambda i: (layer, 0, 0)),
                tab_spec, tab_spec, tab_spec,
                pl.BlockSpec((1, r), lambda i: (0, 0))]
    args = [x, sc, sh, w_packed, *tables, kv_norm.reshape(1, r)]
    out_specs = [pl.BlockSpec((tm, w), row) for w, _ in widths] + [pl.BlockSpec((aw, tm), lambda i: (0, i))]
    out_shape = [jax.ShapeDtypeStruct((m, w), dt) for w, dt in widths] + [jax.ShapeDtypeStruct((aw, m), BF16)]
    for o in STATE_OUTPUTS:
        w, dt = widths[o]
        tail = (w // HEAD, HEAD) if o in HEAD_OUTPUTS else (w,)
        out_specs[o] = pl.BlockSpec((None, tm) + tail, lambda i, n=len(tail): (layer, i) + (0,) * n)
        out_shape[o] = jax.ShapeDtypeStruct((dm.depth, m) + tail, dt)
    aliases = {}
    if stacks is not None:
        aliases = {len(args) + k: o for k, o in enumerate(STATE_OUTPUTS)}
        in_specs += [pl.BlockSpec(memory_space=pl.ANY)] * len(stacks)
        args += list(stacks)
    return pl.pallas_call(
        functools.partial(_inproj_kernel, dm=dm, n_alias=len(aliases)),
        grid=(n_i,),
        in_specs=in_specs,
        out_specs=out_specs,
        out_shape=out_shape,
        input_output_aliases=aliases,
        compiler_params=_params(("arbitrary",)),
        name="in_projection",
    )(*args)


def _kvup_kernel(l_ref, wk_ref, wvt_ref, k_ref, vt_ref):
    lat = l_ref[...]
    k_ref[...] = jnp.dot(lat, wk_ref[...], preferred_element_type=F32).astype(BF16)
    vt_ref[...] = lax.dot_general(wvt_ref[...], lat, _NT, preferred_element_type=F32).astype(BF16)


def _kv_up(lat, w_uk, w_uv_t, layer):
    m, r = lat.shape
    n = w_uk.shape[2]
    tm = _row_tile(m, 512)
    return pl.pallas_call(
        _kvup_kernel,
        grid=(m // tm,),
        in_specs=[pl.BlockSpec((tm, r), lambda i: (i, 0)),
                  pl.BlockSpec((None, r, n), lambda i: (layer, 0, 0)),
                  pl.BlockSpec((None, n, r), lambda i: (layer, 0, 0))],
        out_specs=[pl.BlockSpec((tm, n), lambda i: (i, 0)), pl.BlockSpec((n, tm), lambda i: (0, i))],
        out_shape=[jax.ShapeDtypeStruct((m, n), BF16), jax.ShapeDtypeStruct((n, m), BF16)],
        compiler_params=_params(("arbitrary",)),
        name="latent_up_projection",
    )(lat, w_uk, w_uv_t)


_NT = (((1,), (1,)), ((), ()))


def _visible_tiles(qpos0, tq, l, tk):
    nvis = jnp.minimum(l, ((qpos0 + tq - 1) // CHUNK + 1) * CHUNK)
    return (nvis + tk - 1) // tk


def _flash_step(carry, s, v):
    m, l, acc = carry
    m_new = jnp.maximum(m, jnp.max(s, axis=1, keepdims=True))
    alpha = jnp.exp(m - m_new)
    p = jnp.exp(s - m_new)
    l = alpha * l + jnp.sum(p, axis=1, keepdims=True)
    acc = alpha * acc + jnp.dot(p.astype(BF16), v, preferred_element_type=F32)
    return m_new, l, acc


def _flash_init(rows, width):
    return (jnp.full((rows, 1), MASKED, F32), jnp.zeros((rows, 1), F32), jnp.zeros((rows, width), F32))


def _ordered_key(x):
    b = pltpu.bitcast(x, I32)
    return jnp.where(b < 0, b ^ 0x7FFFFFFF, b)


def _dsa_kernel(qi_ref, wi_ref, qa_ref, kip_ref, kp_ref, vp_ref, kin_ref, kn_ref, vn_ref, o_ref,
                key_scr, bias_scr, thr2_scr, *, tq, tk, past, topk, aw):
    l = past + tq
    tn = kn_ref.shape[0]
    span = past + tn
    tiles = [(kip_ref, kp_ref, vp_ref, r0, tk, r0) for r0 in range(0, past, tk)]
    tiles.append((kin_ref, kn_ref, vn_ref, 0, tn, past))
    qchunk = (past + lax.broadcasted_iota(I32, (tq, 1), 0)) // CHUNK
    lane = lax.broadcasted_iota(I32, (1, LANE), 1)
    lo_half = lane < HEAD
    kf = jnp.float32(topk)

    def split_heads(qs):
        zero = jnp.zeros_like(qs)
        return jnp.concatenate([jnp.where(lo_half, qs, zero), jnp.where(lo_half, zero, qs)], axis=0)

    wi = wi_ref[...] * (HEAD ** -0.5)
    q_idx = jnp.concatenate([split_heads(qi_ref[:, s * LANE:(s + 1) * LANE]) for s in range(IDX_HEADS // 2)],
                            axis=0)
    w_idx = [wi[:, h:h + 1] for h in range(IDX_HEADS)]

    for kt, (ki_ref, _, _, r0, width, pos0) in enumerate(tiles):
        s_all = lax.dot_general(q_idx, ki_ref[r0:r0 + width, :], _NT, preferred_element_type=F32)
        score = jnp.zeros((tq, width), F32)
        for h in range(IDX_HEADS):
            score = score + jnp.maximum(s_all[h * tq:(h + 1) * tq], 0.0) * w_idx[h]
        kpos = pos0 + lax.broadcasted_iota(I32, (1, width), 1)
        vis = (kpos // CHUNK <= qchunk) & (kpos < l)
        key_scr[kt, :, :width] = _ordered_key(jnp.where(vis, score, -jnp.inf))

    def slabs():
        return [(kt, j, pos0 + j) for kt, (_, _, _, _, width, pos0) in enumerate(tiles)
                for j in range(0, width, LANE)]

    def count(pred):
        acc = jnp.zeros((tq, LANE), F32)
        for kt, j, pos in slabs():
            acc = acc + jnp.where(pred(key_scr[kt, :, j:j + LANE], pos), 1.0, 0.0)
        return jnp.sum(acc, axis=1, keepdims=True)

    def wide(col):
        return jnp.broadcast_to(col, (tq, LANE))

    def bits_body(b, lo):
        step = jnp.left_shift(jnp.int32(1), 30 - 2 * b)
        for mult in (1, 2, 3):
            cand = lo + mult * step
            cand_w = wide(cand)
            lo_next = jnp.where(count(lambda ks, _, cand_w=cand_w: ks >= cand_w) >= kf, cand,
                                lo if mult == 1 else lo_next)
        return lo_next

    thr = lax.fori_loop(0, 16, bits_body, jnp.full((tq, 1), INT_MIN, I32))
    thr_w = wide(thr)

    n_ge = count(lambda ks, _: ks >= thr_w)
    n_gt = count(lambda ks, _: ks > thr_w)
    excess = (n_ge > kf) & (thr > KEY_NEG_INF)
    thr2_scr[...] = jnp.zeros((tq, 1), I32)

    @pl.when(jnp.max(jnp.where(excess, 1.0, 0.0)) > 0.0)
    def _():
        need = kf - n_gt
        nbits = span.bit_length()

        def bit2_body(b, lo):
            cand = lo + jnp.left_shift(jnp.int32(1), nbits - 1 - b)
            cand_w = wide(cand)
            c = count(lambda ks, pos: jnp.where(ks == thr_w, span - (pos + lane), 0) >= cand_w)
            return jnp.where(c >= need, cand, lo)

        thr2_scr[...] = lax.fori_loop(0, nbits, bit2_body, jnp.zeros((tq, 1), I32))

    thr2_w = wide(thr2_scr[...])

    for kt, j, pos in slabs():
        ks = key_scr[kt, :, j:j + LANE]
        tie = jnp.where(span - (pos + lane) >= thr2_w, 0.0, MASKED)
        bias = jnp.where(ks > thr_w, 0.0, jnp.where(ks == thr_w, tie, MASKED))
        bias_scr[kt, :, j:j + LANE] = jnp.where(ks > KEY_NEG_INF, bias, MASKED)

    for pr in range(aw // LANE):
        cols = slice(pr * LANE, (pr + 1) * LANE)
        q2 = split_heads(qa_ref[:, cols])
        carry = _flash_init(2 * tq, LANE)
        for kt, (_, k_ref, v_ref, r0, width, _) in enumerate(tiles):
            s = lax.dot_general(q2, k_ref[r0:r0 + width, cols], _NT, preferred_element_type=F32)
            bias = bias_scr[kt, :, :width]
            s = s * (HEAD ** -0.5) + jnp.concatenate([bias, bias], axis=0)
            carry = _flash_step(carry, s, v_ref[r0:r0 + width, cols])
        _, den, acc = carry
        o2 = acc / den
        o_ref[:, cols] = jnp.where(lo_half, o2[:tq], o2[tq:]).astype(BF16)


def _dsa_attention(qi, wi, qa, ki_cache, k_cache, v_cache, ki_new, k_new, v_new, layer, grp):
    b, t, aw = qa.shape
    past, tn = k_cache.shape[2], k_new.shape[1]
    assert past % LANE == 0 and tn % LANE == 0
    tk = next(c for c in (512, 384, 256, 128) if past % c == 0)
    width = max(tk, tn)
    topk = min(TOPK_MAX, (past + t) // 4)
    qblk = lambda w: pl.BlockSpec((None, t, w), lambda i: (i, 0, 0))
    cblk = lambda w: pl.BlockSpec((None, None, past, w), lambda i: (layer, i, 0, 0))
    nblk = lambda w: pl.BlockSpec((None, tn, w), lambda i: (i, 0, 0))
    return pl.pallas_call(
        functools.partial(_dsa_kernel, tq=t, tk=tk, past=past, topk=topk, aw=aw),
        grid=(b,),
        in_specs=[qblk(IDX_HEADS * HEAD), qblk(LANE), qblk(aw), cblk(LANE), cblk(aw), cblk(aw),
                  nblk(LANE), nblk(aw), nblk(aw)],
        out_specs=qblk(aw),
        out_shape=jax.ShapeDtypeStruct((b, t, aw), BF16),
        scratch_shapes=[pltpu.VMEM((past // tk + 1, t, width), I32),
                        pltpu.VMEM((past // tk + 1, t, width), F32),
                        pltpu.VMEM((t, 1), I32)],
        compiler_params=_params(("arbitrary",)),
        name="dsa_attention",
    )(qi, wi, qa, ki_cache, k_cache, v_cache, ki_new, k_new, v_new)


def _mla_latent_kernel(qn_ref, qr_ref, latp_ref, krp_ref, latn_ref, krn_ref, wuk_ref, wuv_ref, o_ref,
                       *, t, l, past, ch):
    lo_half = lax.broadcasted_iota(I32, (1, LANE), 1) < HEAD
    scale = (C_NOPE + HEAD) ** -0.5
    q_lat, q_rope = [], []
    for h in range(ch):
        cols = slice(h * C_NOPE, (h + 1) * C_NOPE)
        q_lat.append(lax.dot_general(qn_ref[:, cols], wuk_ref[:, cols], _NT, preferred_element_type=F32))
        qs = qr_ref[:, (h // 2) * LANE:(h // 2 + 1) * LANE]
        zero = jnp.zeros_like(qs)
        q_rope.append(jnp.where(lo_half, qs, zero) if h % 2 == 0 else jnp.where(lo_half, zero, qs))
    q_lat = jnp.concatenate(q_lat, axis=0).astype(BF16)
    q_rope = jnp.concatenate(q_rope, axis=0)
    qchunk = jnp.concatenate([(past + lax.broadcasted_iota(I32, (t, 1), 0)) // CHUNK] * ch, axis=0)
    kr_past = krp_ref[...].astype(BF16)
    segments = ((latp_ref[...].astype(BF16), jnp.concatenate([kr_past, kr_past], axis=1), 0, past),
                (latn_ref[...], krn_ref[...], past, l))
    scores = []
    for lat, kr, pos0, pos_end in segments:
        s = (lax.dot_general(q_lat, lat, _NT, preferred_element_type=F32)
             + lax.dot_general(q_rope, kr, _NT, preferred_element_type=F32)) * scale
        kpos = pos0 + lax.broadcasted_iota(I32, (1, lat.shape[0]), 1)
        scores.append(jnp.where((kpos // CHUNK <= qchunk) & (kpos < pos_end), s, MASKED))
    m = functools.reduce(jnp.maximum, [jnp.max(s, axis=1, keepdims=True) for s in scores])
    ps = [jnp.exp(s - m) for s in scores]
    den = sum(jnp.sum(p, axis=1, keepdims=True) for p in ps)
    acc = sum(jnp.dot(p.astype(BF16), seg[0], preferred_element_type=F32) for p, seg in zip(ps, segments))
    o_lat = (acc / den).astype(BF16)
    for h in range(ch):
        cols = slice(h * C_V, (h + 1) * C_V)
        o_ref[:, cols] = jnp.dot(o_lat[h * t:(h + 1) * t, :], wuv_ref[:, cols],
                                 preferred_element_type=F32).astype(BF16)


def _mla_attention_latent(qn, qr, lat_cache, kr_cache, lat_new, kr_new, w_uk, w_uv, layer, grp):
    b, t, wn = qn.shape
    ch = wn // C_NOPE
    past, r = lat_cache.shape[2:]
    tn = lat_new.shape[1]
    qblk = lambda w: pl.BlockSpec((None, t, w), lambda i: (i, 0, 0))
    cblk = lambda w: pl.BlockSpec((None, None, past, w), lambda i: (layer, i, 0, 0))
    nblk = lambda w: pl.BlockSpec((None, tn, w), lambda i: (i, 0, 0))
    wblk = pl.BlockSpec((None, r, wn), lambda i: (layer, 0, 0))
    return pl.pallas_call(
        functools.partial(_mla_latent_kernel, t=t, l=past + t, past=past, ch=ch),
        grid=(b,),
        in_specs=[qblk(wn), qblk(ch * HEAD), cblk(r), cblk(HEAD), nblk(r), nblk(LANE), wblk, wblk],
        out_specs=qblk(wn),
        out_shape=jax.ShapeDtypeStruct((b, t, wn), BF16),
        compiler_params=_params(("arbitrary",)),
        name="mla_attention_latent",
    )(qn, qr, lat_cache, kr_cache, lat_new, kr_new, w_uk, w_uv)


ONES_ROWS = 16


def _flash_t_init(m_scr, acc_scr):
    m_scr[...] = jnp.full(m_scr.shape, MASKED, F32)
    acc_scr[...] = jnp.zeros(acc_scr.shape, F32)


def _flash_t_stage(slot, h, s, s_scr):
    s_scr[slot, h] = s
    return jnp.max(s, axis=0, keepdims=True)


def _flash_t_step(slot, h, s_max, vt, c, s_scr, m_scr, acc_scr):
    m_old = m_scr[h]
    m_new = jnp.maximum(m_old, s_max)
    alpha = jnp.exp2((m_old - m_new) * c)
    p = jnp.exp2((s_scr[slot, h] - m_new) * c).astype(BF16)
    vt_ones = jnp.concatenate([vt, jnp.ones((ONES_ROWS, vt.shape[1]), BF16)], axis=0)
    acc_scr[h] = alpha * acc_scr[h] + jnp.dot(vt_ones, p, preferred_element_type=F32)
    m_scr[h] = m_new


def _flash_t_out(o_ref, acc_scr):
    heads, width = acc_scr.shape[0], acc_scr.shape[1] - ONES_ROWS
    ot = jnp.concatenate([acc_scr[h, :width, :] / acc_scr[h, width:width + 1, :] for h in range(heads)], axis=0)
    o_ref[...] = ot.T.astype(BF16)


def _split_heads_t(slab_t):
    row_lo = lax.broadcasted_iota(I32, (LANE, 1), 0) < HEAD
    zero = jnp.zeros_like(slab_t)
    return jnp.where(row_lo, slab_t, zero), jnp.where(row_lo, zero, slab_t)


def _dsa_t_kernel(qi_ref, wi_ref, ki_ref, qa_ref, k_ref, vt_ref, o_ref,
                  key_scr, hi_scr, lo_scr, thr2_scr, qit_scr, qat_scr, s_scr, m_scr, acc_scr,
                  *, tq, tk, l, past, topk):
    heads = acc_scr.shape[0]
    qpos0 = past + pl.program_id(1) * tq
    ntiles = _visible_tiles(qpos0, tq, l, tk)
    qchunk = (qpos0 + lax.broadcasted_iota(I32, (1, tq), 1)) // CHUNK
    krow = lax.broadcasted_iota(I32, (tk, 1), 0)

    for s in range(IDX_HEADS // 2):
        qit_scr[2 * s], qit_scr[2 * s + 1] = _split_heads_t(qi_ref[:, s * LANE:(s + 1) * LANE].T)
    for s in range(heads // 2):
        slab_t = (qa_ref[:, s * LANE:(s + 1) * LANE].astype(F32) * (HEAD ** -0.5)).astype(BF16).T
        qat_scr[2 * s], qat_scr[2 * s + 1] = _split_heads_t(slab_t)
    w_t = wi_ref[...].T * (HEAD ** -0.5)
    w_rows = [w_t[h:h + 1, :] for h in range(IDX_HEADS)]

    nfull = jnp.minimum(l, (qpos0 // CHUNK + 1) * CHUNK) // tk

    def score_body(kt, masked):
        k0 = pl.multiple_of(kt * tk, tk)
        ki_tile = ki_ref[pl.ds(k0, tk), :]
        score = jnp.zeros((tk, tq), F32)
        for h in range(IDX_HEADS):
            s = jnp.dot(ki_tile, qit_scr[h], preferred_element_type=F32)
            score = score + jnp.maximum(s, 0.0) * w_rows[h]
        if masked:
            kpos = k0 + krow
            score = jnp.where((kpos // CHUNK <= qchunk) & (kpos < l), score, -jnp.inf)
        key = _ordered_key(score)
        key_scr[kt] = key
        hi_scr[kt] = (key >> 16).astype(I16)
        lo_scr[kt] = ((key & 0xFFFF) + I16_MIN).astype(I16)
        return 0

    lax.fori_loop(0, nfull, lambda kt, _: score_body(kt, False), 0)
    lax.fori_loop(nfull, ntiles, lambda kt, _: score_body(kt, True), 0)

    def count(pred):
        def body(kt, acc):
            hit = jnp.where(pred(key_scr[kt], kt * tk), 1.0, 0.0)
            return acc + hit.reshape(tk // SUBLANE, SUBLANE, tq).sum(axis=0)
        acc = lax.fori_loop(0, ntiles, body, jnp.zeros((SUBLANE, tq), F32))
        return jnp.sum(acc, axis=0, keepdims=True)

    def count16(half_scr, cand, strict):
        rows = 2 * SUBLANE
        cand16 = cand.astype(I16)

        def body(kt, acc):
            half = half_scr[kt]
            hit = jnp.where(half > cand16 if strict else half >= cand16, jnp.int16(1), jnp.int16(0))
            parts = hit.reshape(tk // (4 * rows), 4, rows, tq)
            for g in range(parts.shape[0]):
                acc = acc + parts[g]
            return acc

        acc = lax.fori_loop(0, ntiles, body, jnp.zeros((4, rows, tq), I16))
        return acc.astype(I32).sum(axis=0).sum(axis=0, keepdims=True)

    def kth_largest16(half_scr, k_need):
        def bit_body(b, lo):
            cand = lo + jnp.left_shift(jnp.int32(1), 15 - b)
            return jnp.where(count16(half_scr, cand, False) >= k_need, cand, lo)
        return lax.fori_loop(0, 16, bit_body, jnp.full((1, tq), I16_MIN, I32))

    thr_hi = kth_largest16(hi_scr, jnp.full((1, tq), topk, I32))
    n_gt_hi = count16(hi_scr, thr_hi, True)
    thr_hi16 = thr_hi.astype(I16)

    def mark_body(kt, _):
        lo_scr[kt] = jnp.where(hi_scr[kt] == thr_hi16, lo_scr[kt], jnp.int16(I16_MIN))
        return 0

    lax.fori_loop(0, ntiles, mark_body, 0)
    thr_lo = kth_largest16(lo_scr, topk - n_gt_hi)
    thr = thr_hi * 65536 + (thr_lo - I16_MIN)

    lp = key_scr.shape[0] * tk
    n_gt = n_gt_hi + count16(lo_scr, thr_lo, True)
    n_ge = jnp.where(thr_lo > I16_MIN, n_gt_hi + count16(lo_scr, thr_lo, False), count16(hi_scr, thr_hi, False))
    excess = (n_ge > topk) & (thr > KEY_NEG_INF)
    thr2_scr[...] = jnp.zeros((1, tq), I32)

    @pl.when(jnp.max(jnp.where(excess, 1.0, 0.0)) > 0.0)
    def _():
        need = (topk - n_gt).astype(F32)
        nbits = lp.bit_length()

        def bit2_body(b, lo):
            cand = lo + jnp.left_shift(jnp.int32(1), nbits - 1 - b)
            c = count(lambda ks, base: jnp.where(ks == thr, lp - (base + krow), 0) >= cand)
            return jnp.where(c >= need, cand, lo)

        thr2_scr[...] = lax.fori_loop(0, nbits, bit2_body, jnp.zeros((1, tq), I32))

    thr2 = thr2_scr[...]

    def bias_body(kt, _):
        ks = key_scr[kt]
        tie = jnp.where(lp - (kt * tk + krow) >= thr2, 0.0, MASKED)
        bias = jnp.where(ks > thr, 0.0, jnp.where(ks == thr, tie, MASKED))
        key_scr[kt] = pltpu.bitcast(jnp.where(ks > KEY_NEG_INF, bias, MASKED), I32)
        return 0

    lax.fori_loop(0, ntiles, bias_body, 0)

    _flash_t_init(m_scr, acc_scr)

    def att_body(kt, _):
        k0 = pl.multiple_of(kt * tk, tk)
        slot = kt % 2
        bias = pltpu.bitcast(key_scr[kt], F32)
        s_max = []
        for h in range(heads):
            cols = slice((h // 2) * LANE, (h // 2 + 1) * LANE)
            s = jnp.dot(k_ref[pl.ds(k0, tk), cols], qat_scr[h], preferred_element_type=F32) + bias
            s_max.append(_flash_t_stage(slot, h, s, s_scr))
        for h in range(heads):
            _flash_t_step(slot, h, s_max[h], vt_ref[h * HEAD:(h + 1) * HEAD, pl.ds(k0, tk)], LOG2E,
                          s_scr, m_scr, acc_scr)
        return 0

    lax.fori_loop(0, ntiles, att_body, 0)
    _flash_t_out(o_ref, acc_scr)


def _dsa_attention_t(qi, wi, ki, qa, k, vt, grp, tq, tk):
    b, t, aw = qa.shape
    assert t % tk == 0
    heads = aw // HEAD
    topk = min(TOPK_MAX, t // 4)
    qblk = lambda w: pl.BlockSpec((None, tq, w), lambda i, j: (i, j, 0))
    kblk = lambda w: _resident((None, t, w), lambda i, j: (i, 0, 0))
    return pl.pallas_call(
        functools.partial(_dsa_t_kernel, tq=tq, tk=tk, l=t, past=grp.past, topk=topk),
        grid=(b, t // tq),
        in_specs=[qblk(IDX_HEADS * HEAD), qblk(LANE), kblk(LANE), qblk(aw), kblk(aw),
                  _resident((aw, t), lambda i, j: (0, i))],
        out_specs=qblk(aw),
        out_shape=jax.ShapeDtypeStruct((b, t, aw), BF16),
        scratch_shapes=[pltpu.VMEM((t // tk, tk, tq), I32),
                        pltpu.VMEM((t // tk, tk, tq), I16),
                        pltpu.VMEM((t // tk, tk, tq), I16),
                        pltpu.VMEM((1, tq), I32),
                        pltpu.VMEM((IDX_HEADS, LANE, tq), BF16),
                        pltpu.VMEM((heads, LANE, tq), BF16),
                        pltpu.VMEM((2, heads, tk, tq), F32),
                        pltpu.VMEM((heads, 1, tq), F32),
                        pltpu.VMEM((heads, HEAD + ONES_ROWS, tq), F32)],
        compiler_params=_params(("arbitrary", "arbitrary")),
        name="dsa_attention_t",
    )(qi, wi, ki, qa, k, vt)


def _mla_t_kernel(qn_ref, qr_ref, kn_ref, vt_ref, kr_ref, o_ref, qt_scr, s_scr, m_scr, acc_scr,
                  *, tq, tk, l, past):
    ch = acc_scr.shape[0]
    qpos0 = past + pl.program_id(1) * tq
    ntiles = _visible_tiles(qpos0, tq, l, tk)
    nfull = jnp.minimum(l, (qpos0 // CHUNK + 1) * CHUNK) // tk
    qchunk = (qpos0 + lax.broadcasted_iota(I32, (1, tq), 1)) // CHUNK
    krow = lax.broadcasted_iota(I32, (tk, 1), 0)
    c = (C_NOPE + HEAD) ** -0.5 * LOG2E
    for s in range(ch // 2):
        pair = _split_heads_t(qr_ref[:, s * LANE:(s + 1) * LANE].T)
        for half in range(2):
            h = 2 * s + half
            qt_scr[h] = jnp.concatenate([qn_ref[:, h * LANE:(h + 1) * LANE].T, pair[half]], axis=0)
    _flash_t_init(m_scr, acc_scr)

    def tile(kt, masked):
        k0 = pl.multiple_of(kt * tk, tk)
        k_rope = kr_ref[pl.ds(k0, tk), :]
        slot = kt % 2
        if masked:
            kpos = k0 + krow
            bias = jnp.where((kpos // CHUNK <= qchunk) & (kpos < l), 0.0, MASKED)
        s_max = []
        for h in range(ch):
            kcat = jnp.concatenate([kn_ref[pl.ds(k0, tk), h * LANE:(h + 1) * LANE], k_rope], axis=1)
            s = jnp.dot(kcat, qt_scr[h], preferred_element_type=F32)
            s_max.append(_flash_t_stage(slot, h, s + bias if masked else s, s_scr))
        for h in range(ch):
            _flash_t_step(slot, h, s_max[h], vt_ref[h * C_V:(h + 1) * C_V, pl.ds(k0, tk)], c,
                          s_scr, m_scr, acc_scr)
        return 0

    lax.fori_loop(0, nfull, lambda kt, _: tile(kt, False), 0)
    lax.fori_loop(nfull, ntiles, lambda kt, _: tile(kt, True), 0)
    _flash_t_out(o_ref, acc_scr)


def _mla_attention_t(qn, qr, kn, vt, kr, grp, tq):
    b, t, wn = qn.shape
    ch = wn // C_NOPE
    lp, tk = _key_tile(t)
    assert lp == t and ch % 2 == 0
    qblk = lambda w: pl.BlockSpec((None, tq, w), lambda i, j: (i, j, 0))
    kblk = lambda w: _resident((None, t, w), lambda i, j: (i, 0, 0))
    return pl.pallas_call(
        functools.partial(_mla_t_kernel, tq=tq, tk=tk, l=t, past=grp.past),
        grid=(b, t // tq),
        in_specs=[qblk(wn), qblk(ch * HEAD), kblk(wn), _resident((wn, t), lambda i, j: (0, i)), kblk(LANE)],
        out_specs=qblk(wn),
        out_shape=jax.ShapeDtypeStruct((b, t, wn), BF16),
        scratch_shapes=[pltpu.VMEM((ch, 2 * LANE, tq), BF16),
                        pltpu.VMEM((2, ch, tk, tq), F32),
                        pltpu.VMEM((ch, 1, tq), F32),
                        pltpu.VMEM((ch, C_V + ONES_ROWS, tq), F32)],
        compiler_params=_params(("arbitrary", "arbitrary")),
        name="mla_attention_t",
    )(qn, qr, kn, vt, kr)


def _causal_conv(u, e0, e1, w, seg):
    rmod = lax.broadcasted_iota(I32, (u.shape[0], 1), 0) % seg
    u1 = jnp.where(rmod == 0, e1, pltpu.roll(u, 1, 0))
    u2 = jnp.where(rmod == 0, e0, jnp.where(rmod == 1, e1, pltpu.roll(u, 2, 0)))
    return u2 * w[0:1] + u1 * w[1:2] + u * w[2:3]


def _layer_norm(z, g, b):
    mu = jnp.mean(z, axis=-1, keepdims=True)
    zc = z - mu
    var = jnp.mean(zc * zc, axis=-1, keepdims=True)
    return zc * lax.rsqrt(var + LN_EPS) * g + b


def _outproj_kernel(*refs, alpha, seq_tiles, seg, carried):
    if carried:
        (x_ref, oa_ref, bg_ref, u_ref, oc_ref, w_ref, cw_ref, g1_ref, lng_ref, lnb_ref, sc2_ref, sh2_ref,
         x1_ref, h2_ref, prev_scr) = refs

        @pl.when(pl.program_id(0) % seq_tiles == 0)
        def _():
            prev_scr[...] = jnp.zeros_like(prev_scr)

        e0, e1 = prev_scr[SUBLANE - 2:SUBLANE - 1, :], prev_scr[SUBLANE - 1:SUBLANE, :]
    else:
        (x_ref, oa_ref, bg_ref, u_ref, oc_ref, w_ref, cw_ref, g1_ref, lng_ref, lnb_ref, sc2_ref, sh2_ref,
         e0_ref, e1_ref, x1_ref, h2_ref) = refs
        e0, e1 = e0_ref[...], e1_ref[...]
    u = u_ref[...]
    yb = bg_ref[...] * _causal_conv(u, e0, e1, cw_ref[...], seg)
    if carried:
        prev_scr[...] = u[u.shape[0] - SUBLANE:, :]
    mixed = jnp.concatenate([oa_ref[...], yb.astype(BF16), oc_ref[...]], axis=1)
    tm = mixed.shape[0]
    n_split = 4 if tm % (4 * 2 * SUBLANE) == 0 else 1

    def rows_of(ref, rows):
        return ref[...] if ref.shape[0] == 1 else ref[rows, :]

    halves = [slice(c * tm // n_split, (c + 1) * tm // n_split) for c in range(n_split)]
    mixes = [jnp.dot(mixed[rows, :], w_ref[...], preferred_element_type=F32) for rows in halves]
    for rows, mix in zip(halves, mixes):
        x1 = _layer_norm(alpha * x_ref[rows, :] + (1.0 + rows_of(g1_ref, rows)) * mix, lng_ref[...], lnb_ref[...])
        x1_ref[rows, :] = x1
        h2_ref[rows, :] = (x1 * (1.0 + rows_of(sc2_ref, rows)) + rows_of(sh2_ref, rows)).astype(BF16)


def _out_projection(x, oa, bg, u, oc, w_out, layer, conv_w, g1, ln_g, ln_b, sc2, sh2, prev, grp, alpha, tm):
    m, d = x.shape
    aw, bw, cw = oa.shape[1], bg.shape[1], oc.shape[1]
    n_i = m // tm
    tiles_per_mod = n_i // g1.shape[0]
    mod_rows = g1.shape[1]
    row = lambda i: (i, 0)
    fix = lambda i: (0, 0)
    mod_spec = pl.BlockSpec((None, mod_rows, d), lambda i: (i // tiles_per_mod, 0, 0))
    in_specs = [pl.BlockSpec((tm, d), row), pl.BlockSpec((tm, aw), row), pl.BlockSpec((tm, bw), row),
                pl.BlockSpec((tm, bw), row), pl.BlockSpec((tm, cw), row),
                _resident((None, aw + bw + cw, d), lambda i: (layer, 0, 0)), pl.BlockSpec((CONV_W, bw), fix),
                mod_spec, pl.BlockSpec((1, d), fix), pl.BlockSpec((1, d), fix), mod_spec, mod_spec]
    args = [x, oa, bg, u, oc, w_out, conv_w, g1, ln_g.reshape(1, d), ln_b.reshape(1, d), sc2, sh2]
    scratch = []
    if prev is None:
        assert grp.t % tm == 0
        scratch = [pltpu.VMEM((SUBLANE, bw), F32)]
    else:
        assert tm % grp.t == 0
        in_specs += [pl.BlockSpec((tm, bw), row), pl.BlockSpec((tm, bw), row)]
        args += [prev[0], prev[1]]
    return pl.pallas_call(
        functools.partial(_outproj_kernel, alpha=alpha, seq_tiles=max(grp.t // tm, 1), seg=min(grp.t, tm),
                          carried=prev is None),
        grid=(n_i,),
        in_specs=in_specs,
        out_specs=[pl.BlockSpec((tm, d), row), pl.BlockSpec((tm, d), row)],
        out_shape=[jax.ShapeDtypeStruct((m, d), F32), jax.ShapeDtypeStruct((m, d), BF16)],
        scratch_shapes=scratch,
        compiler_params=_params(("arbitrary",)),
        name="out_projection",
    )(*args)


def _ffn_kernel(*refs, alpha, seq_tiles, seg, carried):
    if carried:
        (h_ref, x_ref, wg_ref, wu_ref, wd_ref, cw_ref, g2_ref, lng_ref, lnb_ref,
         o_ref, gt_ref, acc_scr, prev_scr) = refs
    else:
        (h_ref, x_ref, wg_ref, wu_ref, wd_ref, cw_ref, g2_ref, lng_ref, lnb_ref, e0_ref, e1_ref,
         o_ref, gt_ref, acc_scr) = refs
    f = pl.program_id(1)
    if carried:
        @pl.when(pl.program_id(0) % seq_tiles == 0)
        def _():
            prev_scr[f] = jnp.zeros(prev_scr.shape[1:], F32)

    @pl.when(f == 0)
    def _():
        acc_scr[...] = jnp.zeros_like(acc_scr)

    h = h_ref[...]
    tm, tf = h.shape[0], wg_ref.shape[1]
    n_split = 2 if tf % (2 * LANE) == 0 else 1
    halves = [slice(c * tf // n_split, (c + 1) * tf // n_split) for c in range(n_split)]
    gates = [jnp.dot(h, wg_ref[:, cols], preferred_element_type=F32) for cols in halves]
    ups = [jnp.dot(h, wu_ref[:, cols], preferred_element_type=F32) for cols in halves]
    for cols, gate, up in zip(halves, gates, ups):
        if carried:
            e0, e1 = prev_scr[f, SUBLANE - 2:SUBLANE - 1, cols], prev_scr[f, SUBLANE - 1:SUBLANE, cols]
        else:
            e0, e1 = e0_ref[:, cols], e1_ref[:, cols]
        conv = _causal_conv(gate, e0, e1, cw_ref[:, cols], seg)
        if carried:
            prev_scr[f, :, cols] = gate[tm - SUBLANE:, :]
        gt_ref[:, cols] = gate[tm - gt_ref.shape[0]:, :]
        act = (jax.nn.silu(conv) * up).astype(BF16)
        acc_scr[...] = jnp.dot(act, wd_ref[cols, :], preferred_element_type=F32) + acc_scr[...]

    @pl.when(f == pl.num_programs(1) - 1)
    def _():
        z = alpha * x_ref[...] + (1.0 + g2_ref[...]) * acc_scr[...]
        o_ref[...] = _layer_norm(z, lng_ref[...], lnb_ref[...])


def _channel_mixer(h2, x1, w_gu, w_down, layer, conv_w, g2, ln_g, ln_b, prev, grp, alpha, tm, tf):
    m, d = x1.shape
    dff = w_down.shape[1]
    n_i, n_f = m // tm, dff // tf
    tiles_per_mod = n_i // g2.shape[0]
    mod_rows = g2.shape[1]
    fix = lambda i, f: (0, 0)
    in_specs = [pl.BlockSpec((tm, d), lambda i, f: (i, 0)), pl.BlockSpec((tm, d), lambda i, f: (i, 0)),
                pl.BlockSpec((None, d, tf), lambda i, f: (layer, 0, f)),
                pl.BlockSpec((None, d, tf), lambda i, f: (layer, 0, n_f + f)),
                pl.BlockSpec((None, tf, d), lambda i, f: (layer, f, 0)),
                pl.BlockSpec((CONV_W, tf), lambda i, f: (0, f)),
                pl.BlockSpec((None, mod_rows, d), lambda i, f: (i // tiles_per_mod, 0, 0)),
                pl.BlockSpec((1, d), fix), pl.BlockSpec((1, d), fix)]
    args = [h2, x1, w_gu, w_gu, w_down, conv_w, g2, ln_g.reshape(1, d), ln_b.reshape(1, d)]
    scratch = [pltpu.VMEM((tm, d), F32)]
    if prev is None:
        assert grp.t % tm == 0
        scratch.append(pltpu.VMEM((n_f, SUBLANE, tf), F32))
        gt_spec = pl.BlockSpec((None, SUBLANE, tf), lambda i, f: (i, 0, f))
        gt_shape = jax.ShapeDtypeStruct((n_i, SUBLANE, dff), F32)
    else:
        assert tm % grp.t == 0
        in_specs += [pl.BlockSpec((tm, tf), lambda i, f: (i, f))] * 2
        args += [prev[0], prev[1]]
        gt_spec = pl.BlockSpec((tm, tf), lambda i, f: (i, f))
        gt_shape = jax.ShapeDtypeStruct((m, dff), F32)
    return pl.pallas_call(
        functools.partial(_ffn_kernel, alpha=alpha, seq_tiles=max(grp.t // tm, 1), seg=min(grp.t, tm),
                          carried=prev is None),
        grid=(n_i, n_f),
        in_specs=in_specs,
        out_specs=[pl.BlockSpec((tm, d), lambda i, f: (i, 0)), gt_spec],
        out_shape=[jax.ShapeDtypeStruct((m, d), F32), gt_shape],
        scratch_shapes=scratch,
        compiler_params=_params(("arbitrary", "arbitrary")),
        name="channel_mixer",
    )(*args)


def _rope_tables(pos):
    half = HEAD // 2
    inv = jnp.power(jnp.float32(ROPE_THETA), -jnp.arange(half, dtype=F32) / half)
    ang = pos.astype(F32)[:, None] * inv[None, :]
    cos, sin = jnp.cos(ang), jnp.sin(ang)
    zero = jnp.zeros_like(sin)
    reps = LANE // HEAD
    return (jnp.tile(jnp.concatenate([cos, cos], axis=1), (1, reps)),
            jnp.tile(jnp.concatenate([zero, sin], axis=1), (1, reps)),
            jnp.tile(jnp.concatenate([-sin, zero], axis=1), (1, reps)))


def _layer(x, mod, tables, grp, past, layer, big, small, stacks, dm, alpha):
    w_in_p, w_out, w_uk, w_uv, w_uv_t, w_gu, w_down = big
    conv_b_w, kv_norm, ln1_g, ln1_b, ln2_g, ln2_b, conv_f_w = small
    b, t = grp.b, grp.t
    m, d = x.shape
    carried = past is None
    tm = _row_tile(t, 256) if carried else m
    if carried:
        mods = [a.reshape(b, 1, d) for a in jnp.split(mod, N_MOD, axis=-1)]
    else:
        mods = [jnp.repeat(a, t, axis=0).reshape(1, m, d) for a in jnp.split(mod, N_MOD, axis=-1)]
    sh1, sc1, g1, sh2, sc2, g2 = mods

    outs = _in_projection(x, sc1, sh1, w_in_p, layer, tables, kv_norm, stacks, dm, tm)
    stacks = tuple(outs[o] for o in STATE_OUTPUTS)
    qa, _, kab, qi, qcr, _, kib, _, krb, _, vab, bg, u, qcn, _, latb, wi, vat = outs

    three = lambda a: a.reshape(b, t, a.shape[-1])
    if carried:
        kn, vct = _kv_up(latb, w_uk, w_uv_t, layer)
        oa = _dsa_attention_t(three(qi), three(wi), three(kib), three(qa), three(kab), vat, grp,
                              _row_tile(t, 256), _row_tile(t, 512))
        oc = _mla_attention_t(three(qcn), three(qcr), three(kn), vct, three(krb), grp, _row_tile(t, 512))
        prev_b = prev_f = None
    else:
        c_ak, c_av, c_ik, c_lat, c_kr, prev_b, prev_f = past
        new_rows = lambda a: jnp.pad(three(a), ((0, 0), (0, -t % LANE), (0, 0)))
        oa = _dsa_attention(three(qi), three(wi), three(qa), c_ik, c_ak, c_av,
                            new_rows(kib), new_rows(kab), new_rows(vab), layer, grp)
        oc = _mla_attention_latent(three(qcn), three(qcr), c_lat, c_kr, new_rows(latb), new_rows(krb),
                                   w_uk, w_uv, layer, grp)

    def expand(state):
        return jnp.repeat(state[:, 0], t, axis=0), jnp.repeat(state[:, 1], t, axis=0)

    x1, h2 = _out_projection(x, oa.reshape(m, -1), bg, u, oc.reshape(m, -1), w_out, layer, conv_b_w, g1,
                             ln1_g, ln1_b, sc2, sh2, None if carried else expand(prev_b), grp, alpha,
                             _row_tile(t, 512) if carried else m)
    tm_f = _row_tile(t, 512) if carried else m
    tf = _row_tile(dm.dff, 512)
    x2, gate_rows = _channel_mixer(h2, x1, w_gu, w_down, layer, conv_f_w, g2, ln2_g, ln2_b,
                                   None if carried else expand(prev_f), grp, alpha, tm_f, tf)
    if carried:
        new_f = gate_rows.reshape(b, t // tm_f, SUBLANE, dm.dff)[:, -1, SUBLANE - (CONV_W - 1):, :]
    else:
        new_f = gate_rows.reshape(b, t, dm.dff)[:, t - (CONV_W - 1):, :]
    new_b = u.reshape(b, t, dm.bw)[:, t - (CONV_W - 1):, :]
    return x2, stacks, (new_b, new_f)


def _state_outputs(stacks, conv_rows, grp, dm):
    ka, ki, kr, va, lat = stacks
    b, t, heads = grp.b, grp.t, dm.aw // HEAD
    new_b, new_f = [jnp.stack(r) for r in zip(*conv_rows)]
    return (ka.reshape(dm.depth, b, t, heads, HEAD), va.reshape(dm.depth, b, t, heads, HEAD),
            ki.reshape(dm.depth, b, t, HEAD), lat.reshape(dm.depth, b, t, dm.r), kr.reshape(dm.depth, b, t, HEAD),
            new_b, new_f)


def kernel(x_prompt, x_sample, c_prompt, c_sample, cache_a_k, cache_a_v, cache_idx_k, cache_mla_latent,
           cache_mla_krope, state_conv_b, state_conv_ffn, w_in, w_out, conv_b_w, mla_kv_norm, mla_w_uk,
           mla_w_uv, w_mod, b_mod, ln1_g, ln1_b, ln2_g, ln2_b, ffn_w_gu, ffn_conv_w, ffn_w_down):
    depth, d, _ = w_in.shape
    a_heads = cache_a_k.shape[3]
    dm = Dims(d=d, aw=a_heads * HEAD, bw=conv_b_w.shape[2], ch=mla_w_uk.shape[2] // C_NOPE,
              r=mla_w_uk.shape[1], dff=ffn_w_down.shape[1], depth=depth)
    alpha = (2 * depth) ** 0.25
    grp_p = Group(b=x_prompt.shape[0], t=x_prompt.shape[1], past=0)
    grp_s = Group(b=x_sample.shape[0], t=x_sample.shape[1], past=cache_a_k.shape[2])

    n_c = grp_p.b + grp_s.b
    c_all = jnp.concatenate([c_prompt, c_sample, jnp.zeros((-n_c % SUBLANE, d), F32)], axis=0)
    mod = _modulation(c_all, w_mod, b_mod)

    tab_p = _rope_tables(jnp.arange(grp_p.t, dtype=I32))
    tab_s = tuple(jnp.tile(a, (grp_s.b, 1)) for a in _rope_tables(grp_s.past + jnp.arange(grp_s.t, dtype=I32)))

    xp = x_prompt.reshape(grp_p.b * grp_p.t, d)
    xs = x_sample.reshape(grp_s.b * grp_s.t, d)
    conv_p, conv_s = [], []
    stacks_p = stacks_s = None
    w_uv_b = mla_w_uv.astype(BF16)
    big = (_pack_w_in(w_in, dm), w_out.astype(BF16), mla_w_uk.astype(BF16), w_uv_b, jnp.swapaxes(w_uv_b, 1, 2),
           ffn_w_gu.astype(BF16), ffn_w_down.astype(BF16))
    idx_k = cache_idx_k.astype(BF16)
    caches = (cache_a_k.reshape(cache_a_k.shape[:3] + (-1,)).astype(BF16),
              cache_a_v.reshape(cache_a_v.shape[:3] + (-1,)).astype(BF16),
              jnp.concatenate([idx_k, idx_k], axis=-1))
    for l in range(depth):
        small = (conv_b_w[l], mla_kv_norm[l], ln1_g[l], ln1_b[l], ln2_g[l], ln2_b[l], ffn_conv_w[l])
        xp, stacks_p, rp = _layer(xp, mod[l, :grp_p.b], tab_p, grp_p, None, l, big, small, stacks_p, dm, alpha)
        past_l = (*caches, cache_mla_latent, cache_mla_krope, state_conv_b[l], state_conv_ffn[l])
        xs, stacks_s, rs = _layer(xs, mod[l, grp_p.b:n_c], tab_s, grp_s, past_l, l, big, small, stacks_s, dm,
                                  alpha)
        conv_p.append(rp)
        conv_s.append(rs)
    return (xp.reshape(x_prompt.shape), xs.reshape(x_sample.shape),
            *_state_outputs(stacks_p, conv_p, grp_p, dm), *_state_outputs(stacks_s, conv_s, grp_s, dm))
```

```python
import functools
from typing import NamedTuple

import numpy as np
import jax
import jax.numpy as jnp
from jax import lax
from jax.experimental import pallas as pl
from jax.experimental.pallas import tpu as pltpu

F32, BF16, I32, I16 = jnp.float32, jnp.bfloat16, jnp.int32, jnp.int16

CHUNK = 64
CONV_W = 3
ROPE_THETA = 10000.0
HEAD = 64
IDX_HEADS = 16
TOPK_MAX = 256
C_NOPE = 128
C_V = 128
N_MOD = 6
LN_EPS = 1e-5
RMS_EPS = 1e-6

LANE = 128
SUBLANE = 8
VMEM_LIMIT = 50 * 1024 * 1024

MASKED = -1e30
LOG2E = 1.4426950408889634
INT_MIN = -2 ** 31
I16_MIN = -2 ** 15
KEY_NEG_INF = int(np.array(-np.inf, np.float32).view(np.int32)) ^ 0x7FFFFFFF


class Dims(NamedTuple):
    d: int
    aw: int
    bw: int
    ch: int
    r: int
    dff: int
    depth: int


class Group(NamedTuple):
    b: int
    t: int
    past: int


def _row_tile(m, pref):
    if m <= pref:
        return m
    t = pref - pref % SUBLANE
    while m % t:
        t -= SUBLANE
    return t


def _key_tile(l):
    lp = -(-l // LANE) * LANE
    for tk in (512, 384, 256, 128):
        if lp % tk == 0:
            return lp, tk
    raise AssertionError(lp)


def _params(sem):
    return pltpu.CompilerParams(dimension_semantics=sem, vmem_limit_bytes=VMEM_LIMIT)


def _resident(shape, index_map):
    return pl.BlockSpec(shape, index_map, pipeline_mode=pl.Buffered(1))


def _mod_kernel(c_ref, w_ref, b_ref, o_ref):
    a = jax.nn.silu(c_ref[...]).astype(BF16)
    o_ref[...] = jnp.dot(a, w_ref[...].astype(BF16), preferred_element_type=F32) + b_ref[...]


def _modulation(c, w_mod, b_mod):
    depth, d, n = w_mod.shape
    rows = c.shape[0]
    tn = _row_tile(n, 1024)
    return pl.pallas_call(
        _mod_kernel,
        grid=(depth, n // tn),
        in_specs=[pl.BlockSpec((rows, d), lambda l, j: (0, 0)),
                  pl.BlockSpec((None, d, tn), lambda l, j: (l, 0, j)),
                  pl.BlockSpec((None, 1, tn), lambda l, j: (l, 0, j))],
        out_specs=pl.BlockSpec((None, rows, tn), lambda l, j: (l, 0, j)),
        out_shape=jax.ShapeDtypeStruct((depth, rows, n), F32),
        compiler_params=_params(("arbitrary", "arbitrary")),
        name="modulation",
    )(c, w_mod, b_mod.reshape(depth, 1, n))


def _pack_w_in(w, dm):
    d, aw, bw, ch, r = dm.d, dm.aw, dm.bw, dm.ch, dm.r
    lead = w.shape[:-1]
    o = np.cumsum([0, aw, aw, aw, IDX_HEADS * HEAD, HEAD, IDX_HEADS, bw, bw, bw, ch * (C_NOPE + HEAD), r, HEAD])
    qa, ka, va, qi, ki, wi, bg, cg, xb, qc, lat, kr = [w[..., o[i]:o[i + 1]] for i in range(12)]
    qc = qc.reshape(lead + (ch, C_NOPE + HEAD))
    qcn = qc[..., :C_NOPE].reshape(lead + (ch * C_NOPE,))
    qcr = qc[..., C_NOPE:].reshape(lead + (ch * HEAD,))
    pad = jnp.zeros(lead + (LANE - IDX_HEADS,), w.dtype)
    return jnp.concatenate([qa, ka, qi, qcr, ki, ki, kr, kr, va, bg, cg, xb, qcn, lat, wi, pad],
                           axis=-1).astype(BF16)


def _rope(acc, cos, s1, s2):
    outs = []
    for s in range(acc.shape[1] // LANE):
        xs = acc[:, s * LANE:(s + 1) * LANE]
        outs.append(xs * cos + pltpu.roll(xs, HEAD // 2, 1) * s1 + pltpu.roll(xs, LANE - HEAD // 2, 1) * s2)
    return outs[0] if len(outs) == 1 else jnp.concatenate(outs, axis=1)


def _inproj_kernel(x_ref, sc_ref, sh_ref, w_ref, cos_ref, s1_ref, s2_ref, nrm_ref, *rest, dm, n_alias):
    (qa_ref, ka_ref, kab_ref, qi_ref, qcr_ref, ki_ref, kib_ref, kr_ref, krb_ref,
     va_ref, vab_ref, bg_ref, u_ref, qcn_ref, lat_ref, latb_ref, wi_ref, vat_ref) = rest[n_alias:]
    aw, bw, ch, r = dm.aw, dm.bw, dm.ch, dm.r
    h = (x_ref[...] * (1.0 + sc_ref[...]) + sh_ref[...]).astype(BF16)
    cos, s1, s2 = cos_ref[...], s1_ref[...], s2_ref[...]
    col = [0]

    def proj(width):
        c0 = col[0]
        col[0] = c0 + width
        return jnp.dot(h, w_ref[:, c0:c0 + width], preferred_element_type=F32)

    def pieces(width, step=512):
        return [(o, min(step, width - o)) for o in range(0, width, step)]

    for o, wd in pieces(aw):
        qa_ref[:, o:o + wd] = _rope(proj(wd), cos, s1, s2).astype(BF16)
    for o, wd in pieces(aw):
        y = _rope(proj(wd), cos, s1, s2)
        ka_ref[:, o // HEAD:(o + wd) // HEAD, :] = y.reshape(y.shape[0], wd // HEAD, HEAD)
        kab_ref[:, o:o + wd] = y.astype(BF16)
    for o, wd in pieces(IDX_HEADS * HEAD):
        qi_ref[:, o:o + wd] = _rope(proj(wd), cos, s1, s2).astype(BF16)
    for o, wd in pieces(ch * HEAD):
        qcr_ref[:, o:o + wd] = _rope(proj(wd), cos, s1, s2).astype(BF16)
    y = _rope(proj(2 * LANE), cos, s1, s2)
    for c, (f32_ref, b16_ref) in enumerate(((ki_ref, kib_ref), (kr_ref, krb_ref))):
        f32_ref[...] = y[:, c * LANE:c * LANE + HEAD]
        b16_ref[...] = y[:, c * LANE:(c + 1) * LANE].astype(BF16)
    for o, wd in pieces(aw):
        y = proj(wd)
        va_ref[:, o // HEAD:(o + wd) // HEAD, :] = y.reshape(y.shape[0], wd // HEAD, HEAD)
        vab_ref[:, o:o + wd] = y.astype(BF16)
        vat_ref[o:o + wd, :] = y.T.astype(BF16)
    for o, wd in pieces(bw):
        bg_ref[:, o:o + wd] = proj(wd)
    c_cg = col[0]
    for o, wd in pieces(bw):
        cg = jnp.dot(h, w_ref[:, c_cg + o:c_cg + o + wd], preferred_element_type=F32)
        xb = jnp.dot(h, w_ref[:, c_cg + bw + o:c_cg + bw + o + wd], preferred_element_type=F32)
        u_ref[:, o:o + wd] = cg * xb
    col[0] = c_cg + 2 * bw
    for o, wd in pieces(ch * C_NOPE):
        qcn_ref[:, o:o + wd] = proj(wd).astype(BF16)
    lat = proj(r)
    lat = lat * lax.rsqrt(jnp.mean(lat * lat, axis=-1, keepdims=True) + RMS_EPS) * nrm_ref[...]
    lat_ref[...] = lat
    latb_ref[...] = lat.astype(BF16)
    wi_ref[...] = proj(LANE) * (IDX_HEADS ** -0.5)


STATE_OUTPUTS = (1, 5, 7, 9, 14)
HEAD_OUTPUTS = (1, 9)


MOD_SH1, MOD_SC1, MOD_G1, MOD_SH2, MOD_SC2, MOD_G2 = range(N_MOD)


def _mod_spec(mod, which, layer, n_i, grid_rank):
    groups, rows, d = mod.shape[1], mod.shape[2], mod.shape[3] // N_MOD
    tiles_per_group = n_i // groups
    if grid_rank == 1:
        return pl.BlockSpec((None, None, rows, d), lambda i: (layer, i // tiles_per_group, 0, which))
    return pl.BlockSpec((None, None, rows, d), lambda i, f: (layer, i // tiles_per_group, 0, which))


def _in_projection(x, mod, w_packed, layer, tables, kv_norm, stacks, dm, tm):
    m, d = x.shape
    aw, bw, ch, r = dm.aw, dm.bw, dm.ch, dm.r
    npk = w_packed.shape[2]
    n_i = m // tm
    tab_tiles = tables[0].shape[0] // tm
    widths = [(aw, BF16), (aw, F32), (aw, BF16), (IDX_HEADS * HEAD, BF16), (ch * HEAD, BF16),
              (HEAD, F32), (LANE, BF16), (HEAD, F32), (LANE, BF16),
              (aw, F32), (aw, BF16), (bw, F32), (bw, F32), (ch * C_NOPE, BF16), (r, F32), (r, BF16),
              (LANE, F32)]
    row = lambda i: (i, 0)
    tab_spec = pl.BlockSpec((tm, LANE), lambda i: (i % tab_tiles, 0))
    in_specs = [pl.BlockSpec((tm, d), row),
                _mod_spec(mod, MOD_SC1, layer, n_i, 1), _mod_spec(mod, MOD_SH1, layer, n_i, 1),
                _resident((None, d, npk), lambda i: (layer, 0, 0)),
                tab_spec, tab_spec, tab_spec,
                pl.BlockSpec((1, r), lambda i: (0, 0))]
    args = [x, mod, mod, w_packed, *tables, kv_norm.reshape(1, r)]
    out_specs = [pl.BlockSpec((tm, w), row) for w, _ in widths] + [pl.BlockSpec((aw, tm), lambda i: (0, i))]
    out_shape = [jax.ShapeDtypeStruct((m, w), dt) for w, dt in widths] + [jax.ShapeDtypeStruct((aw, m), BF16)]
    for o in STATE_OUTPUTS:
        w, dt = widths[o]
        tail = (w // HEAD, HEAD) if o in HEAD_OUTPUTS else (w,)
        out_specs[o] = pl.BlockSpec((None, tm) + tail, lambda i, n=len(tail): (layer, i) + (0,) * n)
        out_shape[o] = jax.ShapeDtypeStruct((dm.depth, m) + tail, dt)
    aliases = {}
    if stacks is not None:
        aliases = {len(args) + k: o for k, o in enumerate(STATE_OUTPUTS)}
        in_specs += [pl.BlockSpec(memory_space=pl.ANY)] * len(stacks)
        args += list(stacks)
    return pl.pallas_call(
        functools.partial(_inproj_kernel, dm=dm, n_alias=len(aliases)),
        grid=(n_i,),
        in_specs=in_specs,
        out_specs=out_specs,
        out_shape=out_shape,
        input_output_aliases=aliases,
        compiler_params=_params(("arbitrary",)),
        name="in_projection",
    )(*args)


def _kvup_kernel(l_ref, wk_ref, wvt_ref, k_ref, vt_ref):
    lat = l_ref[...]
    k_ref[...] = jnp.dot(lat, wk_ref[...], preferred_element_type=F32).astype(BF16)
    vt_ref[...] = lax.dot_general(wvt_ref[...], lat, _NT, preferred_element_type=F32).astype(BF16)


def _kv_up(lat, w_uk, w_uv_t, layer):
    m, r = lat.shape
    n = w_uk.shape[2]
    tm = _row_tile(m, 512)
    return pl.pallas_call(
        _kvup_kernel,
        grid=(m // tm,),
        in_specs=[pl.BlockSpec((tm, r), lambda i: (i, 0)),
                  pl.BlockSpec((None, r, n), lambda i: (layer, 0, 0)),
                  pl.BlockSpec((None, n, r), lambda i: (layer, 0, 0))],
        out_specs=[pl.BlockSpec((tm, n), lambda i: (i, 0)), pl.BlockSpec((n, tm), lambda i: (0, i))],
        out_shape=[jax.ShapeDtypeStruct((m, n), BF16), jax.ShapeDtypeStruct((n, m), BF16)],
        compiler_params=_params(("arbitrary",)),
        name="latent_up_projection",
    )(lat, w_uk, w_uv_t)


_NT = (((1,), (1,)), ((), ()))


def _visible_tiles(qpos0, tq, l, tk):
    nvis = jnp.minimum(l, ((qpos0 + tq - 1) // CHUNK + 1) * CHUNK)
    return (nvis + tk - 1) // tk


def _flash_step(carry, s, v):
    m, l, acc = carry
    m_new = jnp.maximum(m, jnp.max(s, axis=1, keepdims=True))
    alpha = jnp.exp(m - m_new)
    p = jnp.exp(s - m_new)
    l = alpha * l + jnp.sum(p, axis=1, keepdims=True)
    acc = alpha * acc + jnp.dot(p.astype(BF16), v, preferred_element_type=F32)
    return m_new, l, acc


def _flash_init(rows, width):
    return (jnp.full((rows, 1), MASKED, F32), jnp.zeros((rows, 1), F32), jnp.zeros((rows, width), F32))


def _ordered_key(x):
    b = pltpu.bitcast(x, I32)
    return jnp.where(b < 0, b ^ 0x7FFFFFFF, b)


def _dsa_kernel(qi_ref, wi_ref, qa_ref, kip_ref, kp_ref, vp_ref, kin_ref, kn_ref, vn_ref, o_ref,
                key_scr, bias_scr, thr2_scr, *, tq, tk, past, topk, aw):
    l = past + tq
    tn = kn_ref.shape[0]
    span = past + tn
    tiles = [(kip_ref, kp_ref, vp_ref, r0, tk, r0) for r0 in range(0, past, tk)]
    tiles.append((kin_ref, kn_ref, vn_ref, 0, tn, past))
    qchunk = (past + lax.broadcasted_iota(I32, (tq, 1), 0)) // CHUNK
    lane = lax.broadcasted_iota(I32, (1, LANE), 1)
    lo_half = lane < HEAD
    kf = jnp.float32(topk)

    def split_heads(qs):
        zero = jnp.zeros_like(qs)
        return jnp.concatenate([jnp.where(lo_half, qs, zero), jnp.where(lo_half, zero, qs)], axis=0)

    wi = wi_ref[...] * (HEAD ** -0.5)
    q_idx = jnp.concatenate([split_heads(qi_ref[:, s * LANE:(s + 1) * LANE]) for s in range(IDX_HEADS // 2)],
                            axis=0)
    w_idx = [wi[:, h:h + 1] for h in range(IDX_HEADS)]

    for kt, (ki_ref, _, _, r0, width, pos0) in enumerate(tiles):
        s_all = lax.dot_general(q_idx, ki_ref[r0:r0 + width, :], _NT, preferred_element_type=F32)
        score = jnp.zeros((tq, width), F32)
        for h in range(IDX_HEADS):
            score = score + jnp.maximum(s_all[h * tq:(h + 1) * tq], 0.0) * w_idx[h]
        kpos = pos0 + lax.broadcasted_iota(I32, (1, width), 1)
        vis = (kpos // CHUNK <= qchunk) & (kpos < l)
        key_scr[kt, :, :width] = _ordered_key(jnp.where(vis, score, -jnp.inf))

    def slabs():
        return [(kt, j, pos0 + j) for kt, (_, _, _, _, width, pos0) in enumerate(tiles)
                for j in range(0, width, LANE)]

    def count(pred):
        acc = jnp.zeros((tq, LANE), F32)
        for kt, j, pos in slabs():
            acc = acc + jnp.where(pred(key_scr[kt, :, j:j + LANE], pos), 1.0, 0.0)
        return jnp.sum(acc, axis=1, keepdims=True)

    def wide(col):
        return jnp.broadcast_to(col, (tq, LANE))

    def bits_body(b, lo):
        step = jnp.left_shift(jnp.int32(1), 30 - 2 * b)
        for mult in (1, 2, 3):
            cand = lo + mult * step
            cand_w = wide(cand)
            lo_next = jnp.where(count(lambda ks, _, cand_w=cand_w: ks >= cand_w) >= kf, cand,
                                lo if mult == 1 else lo_next)
        return lo_next

    thr = lax.fori_loop(0, 16, bits_body, jnp.full((tq, 1), INT_MIN, I32))
    thr_w = wide(thr)

    n_ge = count(lambda ks, _: ks >= thr_w)
    n_gt = count(lambda ks, _: ks > thr_w)
    excess = (n_ge > kf) & (thr > KEY_NEG_INF)
    thr2_scr[...] = jnp.zeros((tq, 1), I32)

    @pl.when(jnp.max(jnp.where(excess, 1.0, 0.0)) > 0.0)
    def _():
        need = kf - n_gt
        nbits = span.bit_length()

        def bit2_body(b, lo):
            cand = lo + jnp.left_shift(jnp.int32(1), nbits - 1 - b)
            cand_w = wide(cand)
            c = count(lambda ks, pos: jnp.where(ks == thr_w, span - (pos + lane), 0) >= cand_w)
            return jnp.where(c >= need, cand, lo)

        thr2_scr[...] = lax.fori_loop(0, nbits, bit2_body, jnp.zeros((tq, 1), I32))

    thr2_w = wide(thr2_scr[...])

    for kt, j, pos in slabs():
        ks = key_scr[kt, :, j:j + LANE]
        tie = jnp.where(span - (pos + lane) >= thr2_w, 0.0, MASKED)
        bias = jnp.where(ks > thr_w, 0.0, jnp.where(ks == thr_w, tie, MASKED))
        bias_scr[kt, :, j:j + LANE] = jnp.where(ks > KEY_NEG_INF, bias, MASKED)

    for pr in range(aw // LANE):
        cols = slice(pr * LANE, (pr + 1) * LANE)
        q2 = split_heads(qa_ref[:, cols])
        carry = _flash_init(2 * tq, LANE)
        for kt, (_, k_ref, v_ref, r0, width, _) in enumerate(tiles):
            s = lax.dot_general(q2, k_ref[r0:r0 + width, cols], _NT, preferred_element_type=F32)
            bias = bias_scr[kt, :, :width]
            s = s * (HEAD ** -0.5) + jnp.concatenate([bias, bias], axis=0)
            carry = _flash_step(carry, s, v_ref[r0:r0 + width, cols])
        _, den, acc = carry
        o2 = acc / den
        o_ref[:, cols] = jnp.where(lo_half, o2[:tq], o2[tq:]).astype(BF16)


def _dsa_attention(qi, wi, qa, ki_cache, k_cache, v_cache, ki_new, k_new, v_new, layer, grp):
    b, t, aw = qa.shape
    past, tn = k_cache.shape[2], k_new.shape[1]
    assert past % LANE == 0 and tn % LANE == 0
    tk = next(c for c in (512, 384, 256, 128) if past % c == 0)
    width = max(tk, tn)
    topk = min(TOPK_MAX, (past + t) // 4)
    qblk = lambda w: pl.BlockSpec((None, t, w), lambda i: (i, 0, 0))
    cblk = lambda w: pl.BlockSpec((None, None, past, w), lambda i: (layer, i, 0, 0))
    nblk = lambda w: pl.BlockSpec((None, tn, w), lambda i: (i, 0, 0))
    return pl.pallas_call(
        functools.partial(_dsa_kernel, tq=t, tk=tk, past=past, topk=topk, aw=aw),
        grid=(b,),
        in_specs=[qblk(IDX_HEADS * HEAD), qblk(LANE), qblk(aw), cblk(LANE), cblk(aw), cblk(aw),
                  nblk(LANE), nblk(aw), nblk(aw)],
        out_specs=qblk(aw),
        out_shape=jax.ShapeDtypeStruct((b, t, aw), BF16),
        scratch_shapes=[pltpu.VMEM((past // tk + 1, t, width), I32),
                        pltpu.VMEM((past // tk + 1, t, width), F32),
                        pltpu.VMEM((t, 1), I32)],
        compiler_params=_params(("arbitrary",)),
        name="dsa_attention",
    )(qi, wi, qa, ki_cache, k_cache, v_cache, ki_new, k_new, v_new)


def _mla_latent_kernel(qn_ref, qr_ref, latp_ref, krp_ref, latn_ref, krn_ref, wuk_ref, wuv_ref, o_ref,
                       *, t, l, past, ch):
    lo_half = lax.broadcasted_iota(I32, (1, LANE), 1) < HEAD
    scale = (C_NOPE + HEAD) ** -0.5
    q_lat, q_rope = [], []
    for h in range(ch):
        cols = slice(h * C_NOPE, (h + 1) * C_NOPE)
        q_lat.append(lax.dot_general(qn_ref[:, cols], wuk_ref[:, cols], _NT, preferred_element_type=F32))
        qs = qr_ref[:, (h // 2) * LANE:(h // 2 + 1) * LANE]
        zero = jnp.zeros_like(qs)
        q_rope.append(jnp.where(lo_half, qs, zero) if h % 2 == 0 else jnp.where(lo_half, zero, qs))
    q_lat = jnp.concatenate(q_lat, axis=0).astype(BF16)
    q_rope = jnp.concatenate(q_rope, axis=0)
    qchunk = jnp.concatenate([(past + lax.broadcasted_iota(I32, (t, 1), 0)) // CHUNK] * ch, axis=0)
    kr_past = krp_ref[...].astype(BF16)
    segments = ((latp_ref[...].astype(BF16), jnp.concatenate([kr_past, kr_past], axis=1), 0, past),
                (latn_ref[...], krn_ref[...], past, l))
    scores = []
    for lat, kr, pos0, pos_end in segments:
        s = (lax.dot_general(q_lat, lat, _NT, preferred_element_type=F32)
             + lax.dot_general(q_rope, kr, _NT, preferred_element_type=F32)) * scale
        kpos = pos0 + lax.broadcasted_iota(I32, (1, lat.shape[0]), 1)
        scores.append(jnp.where((kpos // CHUNK <= qchunk) & (kpos < pos_end), s, MASKED))
    m = functools.reduce(jnp.maximum, [jnp.max(s, axis=1, keepdims=True) for s in scores])
    ps = [jnp.exp(s - m) for s in scores]
    den = sum(jnp.sum(p, axis=1, keepdims=True) for p in ps)
    acc = sum(jnp.dot(p.astype(BF16), seg[0], preferred_element_type=F32) for p, seg in zip(ps, segments))
    o_lat = (acc / den).astype(BF16)
    for h in range(ch):
        cols = slice(h * C_V, (h + 1) * C_V)
        o_ref[:, cols] = jnp.dot(o_lat[h * t:(h + 1) * t, :], wuv_ref[:, cols],
                                 preferred_element_type=F32).astype(BF16)


def _mla_attention_latent(qn, qr, lat_cache, kr_cache, lat_new, kr_new, w_uk, w_uv, layer, grp):
    b, t, wn = qn.shape
    ch = wn // C_NOPE
    past, r = lat_cache.shape[2:]
    tn = lat_new.shape[1]
    qblk = lambda w: pl.BlockSpec((None, t, w), lambda i: (i, 0, 0))
    cblk = lambda w: pl.BlockSpec((None, None, past, w), lambda i: (layer, i, 0, 0))
    nblk = lambda w: pl.BlockSpec((None, tn, w), lambda i: (i, 0, 0))
    wblk = pl.BlockSpec((None, r, wn), lambda i: (layer, 0, 0))
    return pl.pallas_call(
        functools.partial(_mla_latent_kernel, t=t, l=past + t, past=past, ch=ch),
        grid=(b,),
        in_specs=[qblk(wn), qblk(ch * HEAD), cblk(r), cblk(HEAD), nblk(r), nblk(LANE), wblk, wblk],
        out_specs=qblk(wn),
        out_shape=jax.ShapeDtypeStruct((b, t, wn), BF16),
        compiler_params=_params(("arbitrary",)),
        name="mla_attention_latent",
    )(qn, qr, lat_cache, kr_cache, lat_new, kr_new, w_uk, w_uv)


ONES_ROWS = 16


def _flash_t_init(m_scr, acc_scr):
    m_scr[...] = jnp.full(m_scr.shape, MASKED, F32)
    acc_scr[...] = jnp.zeros(acc_scr.shape, F32)


def _flash_t_stage(slot, h, s, s_scr):
    s_scr[slot, h] = s
    return jnp.max(s, axis=0, keepdims=True)


def _flash_t_step(slot, h, s_max, vt, c, s_scr, m_scr, acc_scr):
    m_old = m_scr[h]
    m_new = jnp.maximum(m_old, s_max)
    alpha = jnp.exp2((m_old - m_new) * c)
    p = jnp.exp2((s_scr[slot, h] - m_new) * c).astype(BF16)
    vt_ones = jnp.concatenate([vt, jnp.ones((ONES_ROWS, vt.shape[1]), BF16)], axis=0)
    acc_scr[h] = alpha * acc_scr[h] + jnp.dot(vt_ones, p, preferred_element_type=F32)
    m_scr[h] = m_new


def _flash_t_out(o_ref, acc_scr):
    heads, width = acc_scr.shape[0], acc_scr.shape[1] - ONES_ROWS
    ot = jnp.concatenate([acc_scr[h, :width, :] / acc_scr[h, width:width + 1, :] for h in range(heads)], axis=0)
    o_ref[...] = ot.T.astype(BF16)


def _split_heads_t(slab_t):
    row_lo = lax.broadcasted_iota(I32, (LANE, 1), 0) < HEAD
    zero = jnp.zeros_like(slab_t)
    return jnp.where(row_lo, slab_t, zero), jnp.where(row_lo, zero, slab_t)


def _dsa_t_kernel(qi_ref, wi_ref, ki_ref, qa_ref, k_ref, vt_ref, o_ref,
                  key_scr, hi_scr, lo_scr, thr2_scr, qit_scr, qat_scr, s_scr, m_scr, acc_scr,
                  *, tq, tk, l, past, topk):
    heads = acc_scr.shape[0]
    qpos0 = past + pl.program_id(1) * tq
    ntiles = _visible_tiles(qpos0, tq, l, tk)
    qchunk = (qpos0 + lax.broadcasted_iota(I32, (1, tq), 1)) // CHUNK
    krow = lax.broadcasted_iota(I32, (tk, 1), 0)

    for s in range(IDX_HEADS // 2):
        qit_scr[2 * s], qit_scr[2 * s + 1] = _split_heads_t(qi_ref[:, s * LANE:(s + 1) * LANE].T)
    for s in range(heads // 2):
        slab_t = (qa_ref[:, s * LANE:(s + 1) * LANE].astype(F32) * (HEAD ** -0.5)).astype(BF16).T
        qat_scr[2 * s], qat_scr[2 * s + 1] = _split_heads_t(slab_t)
    w_t = wi_ref[...].T * (HEAD ** -0.5)
    w_rows = [w_t[h:h + 1, :] for h in range(IDX_HEADS)]

    nfull = jnp.minimum(l, (qpos0 // CHUNK + 1) * CHUNK) // tk

    def score_body(kt, masked):
        k0 = pl.multiple_of(kt * tk, tk)
        ki_tile = ki_ref[pl.ds(k0, tk), :]
        score = jnp.zeros((tk, tq), F32)
        for h in range(IDX_HEADS):
            s = jnp.dot(ki_tile, qit_scr[h], preferred_element_type=F32)
            score = score + jnp.maximum(s, 0.0) * w_rows[h]
        if masked:
            kpos = k0 + krow
            score = jnp.where((kpos // CHUNK <= qchunk) & (kpos < l), score, -jnp.inf)
        key = _ordered_key(score)
        key_scr[kt] = key
        hi_scr[kt] = (key >> 16).astype(I16)
        lo_scr[kt] = ((key & 0xFFFF) + I16_MIN).astype(I16)
        return 0

    lax.fori_loop(0, nfull, lambda kt, _: score_body(kt, False), 0)
    lax.fori_loop(nfull, ntiles, lambda kt, _: score_body(kt, True), 0)

    def count(pred):
        def body(kt, acc):
            hit = jnp.where(pred(key_scr[kt], kt * tk), 1.0, 0.0)
            return acc + hit.reshape(tk // SUBLANE, SUBLANE, tq).sum(axis=0)
        acc = lax.fori_loop(0, ntiles, body, jnp.zeros((SUBLANE, tq), F32))
        return jnp.sum(acc, axis=0, keepdims=True)

    def count16(half_scr, cand, strict):
        rows = 2 * SUBLANE
        cand16 = cand.astype(I16)

        def body(kt, acc):
            half = half_scr[kt]
            hit = jnp.where(half > cand16 if strict else half >= cand16, jnp.int16(1), jnp.int16(0))
            parts = hit.reshape(tk // (4 * rows), 4, rows, tq)
            for g in range(parts.shape[0]):
                acc = acc + parts[g]
            return acc

        acc = lax.fori_loop(0, ntiles, body, jnp.zeros((4, rows, tq), I16))
        return acc.astype(I32).sum(axis=0).sum(axis=0, keepdims=True)

    def kth_largest16(half_scr, k_need):
        def bit_body(b, lo):
            cand = lo + jnp.left_shift(jnp.int32(1), 15 - b)
            return jnp.where(count16(half_scr, cand, False) >= k_need, cand, lo)
        return lax.fori_loop(0, 16, bit_body, jnp.full((1, tq), I16_MIN, I32))

    thr_hi = kth_largest16(hi_scr, jnp.full((1, tq), topk, I32))
    n_gt_hi = count16(hi_scr, thr_hi, True)
    thr_hi16 = thr_hi.astype(I16)

    def mark_body(kt, _):
        lo_scr[kt] = jnp.where(hi_scr[kt] == thr_hi16, lo_scr[kt], jnp.int16(I16_MIN))
        return 0

    lax.fori_loop(0, ntiles, mark_body, 0)
    thr_lo = kth_largest16(lo_scr, topk - n_gt_hi)
    thr = thr_hi * 65536 + (thr_lo - I16_MIN)

    lp = key_scr.shape[0] * tk
    n_gt = n_gt_hi + count16(lo_scr, thr_lo, True)
    n_ge = jnp.where(thr_lo > I16_MIN, n_gt_hi + count16(lo_scr, thr_lo, False), count16(hi_scr, thr_hi, False))
    excess = (n_ge > topk) & (thr > KEY_NEG_INF)
    thr2_scr[...] = jnp.zeros((1, tq), I32)

    @pl.when(jnp.max(jnp.where(excess, 1.0, 0.0)) > 0.0)
    def _():
        need = (topk - n_gt).astype(F32)
        nbits = lp.bit_length()

        def bit2_body(b, lo):
            cand = lo + jnp.left_shift(jnp.int32(1), nbits - 1 - b)
            c = count(lambda ks, base: jnp.where(ks == thr, lp - (base + krow), 0) >= cand)
            return jnp.where(c >= need, cand, lo)

        thr2_scr[...] = lax.fori_loop(0, nbits, bit2_body, jnp.zeros((1, tq), I32))

    thr2 = thr2_scr[...]

    def bias_body(kt, _):
        ks = key_scr[kt]
        tie = jnp.where(lp - (kt * tk + krow) >= thr2, 0.0, MASKED)
        bias = jnp.where(ks > thr, 0.0, jnp.where(ks == thr, tie, MASKED))
        key_scr[kt] = pltpu.bitcast(jnp.where(ks > KEY_NEG_INF, bias, MASKED), I32)
        return 0

    lax.fori_loop(0, ntiles, bias_body, 0)

    _flash_t_init(m_scr, acc_scr)

    def att_body(kt, _):
        k0 = pl.multiple_of(kt * tk, tk)
        slot = kt % 2
        bias = pltpu.bitcast(key_scr[kt], F32)
        s_max = []
        for h in range(heads):
            cols = slice((h // 2) * LANE, (h // 2 + 1) * LANE)
            s = jnp.dot(k_ref[pl.ds(k0, tk), cols], qat_scr[h], preferred_element_type=F32) + bias
            s_max.append(_flash_t_stage(slot, h, s, s_scr))
        for h in range(heads):
            _flash_t_step(slot, h, s_max[h], vt_ref[h * HEAD:(h + 1) * HEAD, pl.ds(k0, tk)], LOG2E,
                          s_scr, m_scr, acc_scr)
        return 0

    lax.fori_loop(0, ntiles, att_body, 0)
    _flash_t_out(o_ref, acc_scr)


def _dsa_attention_t(qi, wi, ki, qa, k, vt, grp, tq, tk):
    b, t, aw = qa.shape
    assert t % tk == 0
    heads = aw // HEAD
    topk = min(TOPK_MAX, t // 4)
    qblk = lambda w: pl.BlockSpec((None, tq, w), lambda i, j: (i, j, 0))
    kblk = lambda w: _resident((None, t, w), lambda i, j: (i, 0, 0))
    return pl.pallas_call(
        functools.partial(_dsa_t_kernel, tq=tq, tk=tk, l=t, past=grp.past, topk=topk),
        grid=(b, t // tq),
        in_specs=[qblk(IDX_HEADS * HEAD), qblk(LANE), kblk(LANE), qblk(aw), kblk(aw),
                  _resident((aw, t), lambda i, j: (0, i))],
        out_specs=qblk(aw),
        out_shape=jax.ShapeDtypeStruct((b, t, aw), BF16),
        scratch_shapes=[pltpu.VMEM((t // tk, tk, tq), I32),
                        pltpu.VMEM((t // tk, tk, tq), I16),
                        pltpu.VMEM((t // tk, tk, tq), I16),
                        pltpu.VMEM((1, tq), I32),
                        pltpu.VMEM((IDX_HEADS, LANE, tq), BF16),
                        pltpu.VMEM((heads, LANE, tq), BF16),
                        pltpu.VMEM((2, heads, tk, tq), F32),
                        pltpu.VMEM((heads, 1, tq), F32),
                        pltpu.VMEM((heads, HEAD + ONES_ROWS, tq), F32)],
        compiler_params=_params(("arbitrary", "arbitrary")),
        name="dsa_attention_t",
    )(qi, wi, ki, qa, k, vt)


def _mla_t_kernel(qn_ref, qr_ref, kn_ref, vt_ref, kr_ref, o_ref, qt_scr, s_scr, m_scr, acc_scr,
                  *, tq, tk, l, past):
    ch = acc_scr.shape[0]
    qpos0 = past + pl.program_id(1) * tq
    ntiles = _visible_tiles(qpos0, tq, l, tk)
    nfull = jnp.minimum(l, (qpos0 // CHUNK + 1) * CHUNK) // tk
    qchunk = (qpos0 + lax.broadcasted_iota(I32, (1, tq), 1)) // CHUNK
    krow = lax.broadcasted_iota(I32, (tk, 1), 0)
    c = (C_NOPE + HEAD) ** -0.5 * LOG2E
    for s in range(ch // 2):
        pair = _split_heads_t(qr_ref[:, s * LANE:(s + 1) * LANE].T)
        for half in range(2):
            h = 2 * s + half
            qt_scr[h] = jnp.concatenate([qn_ref[:, h * LANE:(h + 1) * LANE].T, pair[half]], axis=0)
    _flash_t_init(m_scr, acc_scr)

    def tile(kt, masked):
        k0 = pl.multiple_of(kt * tk, tk)
        k_rope = kr_ref[pl.ds(k0, tk), :]
        slot = kt % 2
        if masked:
            kpos = k0 + krow
            bias = jnp.where((kpos // CHUNK <= qchunk) & (kpos < l), 0.0, MASKED)
        s_max = []
        for h in range(ch):
            kcat = jnp.concatenate([kn_ref[pl.ds(k0, tk), h * LANE:(h + 1) * LANE], k_rope], axis=1)
            s = jnp.dot(kcat, qt_scr[h], preferred_element_type=F32)
            s_max.append(_flash_t_stage(slot, h, s + bias if masked else s, s_scr))
        for h in range(ch):
            _flash_t_step(slot, h, s_max[h], vt_ref[h * C_V:(h + 1) * C_V, pl.ds(k0, tk)], c,
                          s_scr, m_scr, acc_scr)
        return 0

    lax.fori_loop(0, nfull, lambda kt, _: tile(kt, False), 0)
    lax.fori_loop(nfull, ntiles, lambda kt, _: tile(kt, True), 0)
    _flash_t_out(o_ref, acc_scr)


def _mla_attention_t(qn, qr, kn, vt, kr, grp, tq):
    b, t, wn = qn.shape
    ch = wn // C_NOPE
    lp, tk = _key_tile(t)
    assert lp == t and ch % 2 == 0
    qblk = lambda w: pl.BlockSpec((None, tq, w), lambda i, j: (i, j, 0))
    kblk = lambda w: _resident((None, t, w), lambda i, j: (i, 0, 0))
    return pl.pallas_call(
        functools.partial(_mla_t_kernel, tq=tq, tk=tk, l=t, past=grp.past),
        grid=(b, t // tq),
        in_specs=[qblk(wn), qblk(ch * HEAD), kblk(wn), _resident((wn, t), lambda i, j: (0, i)), kblk(LANE)],
        out_specs=qblk(wn),
        out_shape=jax.ShapeDtypeStruct((b, t, wn), BF16),
        scratch_shapes=[pltpu.VMEM((ch, 2 * LANE, tq), BF16),
                        pltpu.VMEM((2, ch, tk, tq), F32),
                        pltpu.VMEM((ch, 1, tq), F32),
                        pltpu.VMEM((ch, C_V + ONES_ROWS, tq), F32)],
        compiler_params=_params(("arbitrary", "arbitrary")),
        name="mla_attention_t",
    )(qn, qr, kn, vt, kr)


def _causal_conv(u, e0, e1, w, seg):
    rmod = lax.broadcasted_iota(I32, (u.shape[0], 1), 0) % seg
    u1 = jnp.where(rmod == 0, e1, pltpu.roll(u, 1, 0))
    u2 = jnp.where(rmod == 0, e0, jnp.where(rmod == 1, e1, pltpu.roll(u, 2, 0)))
    return u2 * w[0:1] + u1 * w[1:2] + u * w[2:3]


def _layer_norm(z, g, b):
    mu = jnp.mean(z, axis=-1, keepdims=True)
    zc = z - mu
    var = jnp.mean(zc * zc, axis=-1, keepdims=True)
    return zc * lax.rsqrt(var + LN_EPS) * g + b


def _outproj_kernel(*refs, alpha, seq_tiles, seg, carried):
    if carried:
        (x_ref, oa_ref, bg_ref, u_ref, oc_ref, w_ref, cw_ref, g1_ref, lng_ref, lnb_ref, sc2_ref, sh2_ref,
         x1_ref, h2_ref, prev_scr) = refs

        @pl.when(pl.program_id(0) % seq_tiles == 0)
        def _():
            prev_scr[...] = jnp.zeros_like(prev_scr)

        e0, e1 = prev_scr[SUBLANE - 2:SUBLANE - 1, :], prev_scr[SUBLANE - 1:SUBLANE, :]
    else:
        (x_ref, oa_ref, bg_ref, u_ref, oc_ref, w_ref, cw_ref, g1_ref, lng_ref, lnb_ref, sc2_ref, sh2_ref,
         e0_ref, e1_ref, x1_ref, h2_ref) = refs
        e0, e1 = e0_ref[...], e1_ref[...]
    u = u_ref[...]
    yb = bg_ref[...] * _causal_conv(u, e0, e1, cw_ref[...], seg)
    if carried:
        prev_scr[...] = u[u.shape[0] - SUBLANE:, :]
    mixed = jnp.concatenate([oa_ref[...], yb.astype(BF16), oc_ref[...]], axis=1)
    tm = mixed.shape[0]
    n_split = 4 if tm % (4 * 2 * SUBLANE) == 0 else 1

    def rows_of(ref, rows):
        return ref[...] if ref.shape[0] == 1 else ref[rows, :]

    halves = [slice(c * tm // n_split, (c + 1) * tm // n_split) for c in range(n_split)]
    mixes = [jnp.dot(mixed[rows, :], w_ref[...], preferred_element_type=F32) for rows in halves]
    for rows, mix in zip(halves, mixes):
        x1 = _layer_norm(alpha * x_ref[rows, :] + (1.0 + rows_of(g1_ref, rows)) * mix, lng_ref[...], lnb_ref[...])
        x1_ref[rows, :] = x1
        h2_ref[rows, :] = (x1 * (1.0 + rows_of(sc2_ref, rows)) + rows_of(sh2_ref, rows)).astype(BF16)


def _out_projection(x, oa, bg, u, oc, w_out, layer, conv_w, mod, ln_g, ln_b, prev, grp, alpha, tm):
    m, d = x.shape
    aw, bw, cw = oa.shape[1], bg.shape[1], oc.shape[1]
    n_i = m // tm
    row = lambda i: (i, 0)
    fix = lambda i: (0, 0)
    in_specs = [pl.BlockSpec((tm, d), row), pl.BlockSpec((tm, aw), row), pl.BlockSpec((tm, bw), row),
                pl.BlockSpec((tm, bw), row), pl.BlockSpec((tm, cw), row),
                _resident((None, aw + bw + cw, d), lambda i: (layer, 0, 0)), pl.BlockSpec((CONV_W, bw), fix),
                _mod_spec(mod, MOD_G1, layer, n_i, 1), pl.BlockSpec((1, d), fix), pl.BlockSpec((1, d), fix),
                _mod_spec(mod, MOD_SC2, layer, n_i, 1), _mod_spec(mod, MOD_SH2, layer, n_i, 1)]
    args = [x, oa, bg, u, oc, w_out, conv_w, mod, ln_g.reshape(1, d), ln_b.reshape(1, d), mod, mod]
    scratch = []
    if prev is None:
        assert grp.t % tm == 0
        scratch = [pltpu.VMEM((SUBLANE, bw), F32)]
    else:
        assert tm % grp.t == 0
        in_specs += [pl.BlockSpec((tm, bw), row), pl.BlockSpec((tm, bw), row)]
        args += [prev[0], prev[1]]
    return pl.pallas_call(
        functools.partial(_outproj_kernel, alpha=alpha, seq_tiles=max(grp.t // tm, 1), seg=min(grp.t, tm),
                          carried=prev is None),
        grid=(n_i,),
        in_specs=in_specs,
        out_specs=[pl.BlockSpec((tm, d), row), pl.BlockSpec((tm, d), row)],
        out_shape=[jax.ShapeDtypeStruct((m, d), F32), jax.ShapeDtypeStruct((m, d), BF16)],
        scratch_shapes=scratch,
        compiler_params=_params(("arbitrary",)),
        name="out_projection",
    )(*args)


def _ffn_kernel(*refs, alpha, seq_tiles, seg, carried):
    if carried:
        (h_ref, x_ref, wg_ref, wu_ref, wd_ref, cw_ref, g2_ref, lng_ref, lnb_ref,
         o_ref, gt_ref, acc_scr, prev_scr) = refs
    else:
        (h_ref, x_ref, wg_ref, wu_ref, wd_ref, cw_ref, g2_ref, lng_ref, lnb_ref, e0_ref, e1_ref,
         o_ref, gt_ref, acc_scr) = refs
    f = pl.program_id(1)
    if carried:
        @pl.when(pl.program_id(0) % seq_tiles == 0)
        def _():
            prev_scr[f] = jnp.zeros(prev_scr.shape[1:], F32)

    @pl.when(f == 0)
    def _():
        acc_scr[...] = jnp.zeros_like(acc_scr)

    h = h_ref[...]
    tm, tf = h.shape[0], wg_ref.shape[1]
    n_split = 2 if tf % (2 * LANE) == 0 else 1
    halves = [slice(c * tf // n_split, (c + 1) * tf // n_split) for c in range(n_split)]
    gates = [jnp.dot(h, wg_ref[:, cols], preferred_element_type=F32) for cols in halves]
    ups = [jnp.dot(h, wu_ref[:, cols], preferred_element_type=F32) for cols in halves]
    for cols, gate, up in zip(halves, gates, ups):
        if carried:
            e0, e1 = prev_scr[f, SUBLANE - 2:SUBLANE - 1, cols], prev_scr[f, SUBLANE - 1:SUBLANE, cols]
        else:
            e0, e1 = e0_ref[:, cols], e1_ref[:, cols]
        conv = _causal_conv(gate, e0, e1, cw_ref[:, cols], seg)
        if carried:
            prev_scr[f, :, cols] = gate[tm - SUBLANE:, :]
        gt_ref[:, cols] = gate[tm - gt_ref.shape[0]:, :]
        act = (jax.nn.silu(conv) * up).astype(BF16)
        acc_scr[...] = jnp.dot(act, wd_ref[cols, :], preferred_element_type=F32) + acc_scr[...]

    @pl.when(f == pl.num_programs(1) - 1)
    def _():
        z = alpha * x_ref[...] + (1.0 + g2_ref[...]) * acc_scr[...]
        o_ref[...] = _layer_norm(z, lng_ref[...], lnb_ref[...])


def _channel_mixer(h2, x1, w_gu, w_down, layer, conv_w, mod, ln_g, ln_b, prev, grp, alpha, tm, tf):
    m, d = x1.shape
    dff = w_down.shape[1]
    n_i, n_f = m // tm, dff // tf
    fix = lambda i, f: (0, 0)
    in_specs = [pl.BlockSpec((tm, d), lambda i, f: (i, 0)), pl.BlockSpec((tm, d), lambda i, f: (i, 0)),
                pl.BlockSpec((None, d, tf), lambda i, f: (layer, 0, f)),
                pl.BlockSpec((None, d, tf), lambda i, f: (layer, 0, n_f + f)),
                pl.BlockSpec((None, tf, d), lambda i, f: (layer, f, 0)),
                pl.BlockSpec((CONV_W, tf), lambda i, f: (0, f)),
                _mod_spec(mod, MOD_G2, layer, n_i, 2),
                pl.BlockSpec((1, d), fix), pl.BlockSpec((1, d), fix)]
    args = [h2, x1, w_gu, w_gu, w_down, conv_w, mod, ln_g.reshape(1, d), ln_b.reshape(1, d)]
    scratch = [pltpu.VMEM((tm, d), F32)]
    if prev is None:
        assert grp.t % tm == 0
        scratch.append(pltpu.VMEM((n_f, SUBLANE, tf), F32))
        gt_spec = pl.BlockSpec((None, SUBLANE, tf), lambda i, f: (i, 0, f))
        gt_shape = jax.ShapeDtypeStruct((n_i, SUBLANE, dff), F32)
    else:
        assert tm % grp.t == 0
        in_specs += [pl.BlockSpec((tm, tf), lambda i, f: (i, f))] * 2
        args += [prev[0], prev[1]]
        gt_spec = pl.BlockSpec((tm, tf), lambda i, f: (i, f))
        gt_shape = jax.ShapeDtypeStruct((m, dff), F32)
    return pl.pallas_call(
        functools.partial(_ffn_kernel, alpha=alpha, seq_tiles=max(grp.t // tm, 1), seg=min(grp.t, tm),
                          carried=prev is None),
        grid=(n_i, n_f),
        in_specs=in_specs,
        out_specs=[pl.BlockSpec((tm, d), lambda i, f: (i, 0)), gt_spec],
        out_shape=[jax.ShapeDtypeStruct((m, d), F32), gt_shape],
        scratch_shapes=scratch,
        compiler_params=_params(("arbitrary", "arbitrary")),
        name="channel_mixer",
    )(*args)


def _rope_tables(pos):
    half = HEAD // 2
    inv = jnp.power(jnp.float32(ROPE_THETA), -jnp.arange(half, dtype=F32) / half)
    ang = pos.astype(F32)[:, None] * inv[None, :]
    cos, sin = jnp.cos(ang), jnp.sin(ang)
    zero = jnp.zeros_like(sin)
    reps = LANE // HEAD
    return (jnp.tile(jnp.concatenate([cos, cos], axis=1), (1, reps)),
            jnp.tile(jnp.concatenate([zero, sin], axis=1), (1, reps)),
            jnp.tile(jnp.concatenate([-sin, zero], axis=1), (1, reps)))


def _layer(x, mod, tables, grp, past, layer, big, small, stacks, dm, alpha):
    w_in_p, w_out, w_uk, w_uv, w_uv_t, w_gu, w_down = big
    conv_b_w, kv_norm, ln1_g, ln1_b, ln2_g, ln2_b, conv_f_w = small
    b, t = grp.b, grp.t
    m, d = x.shape
    carried = past is None
    tm = _row_tile(t, 256) if carried else m

    outs = _in_projection(x, mod, w_in_p, layer, tables, kv_norm, stacks, dm, tm)
    stacks = tuple(outs[o] for o in STATE_OUTPUTS)
    qa, _, kab, qi, qcr, _, kib, _, krb, _, vab, bg, u, qcn, _, latb, wi, vat = outs

    three = lambda a: a.reshape(b, t, a.shape[-1])
    if carried:
        kn, vct = _kv_up(latb, w_uk, w_uv_t, layer)
        oa = _dsa_attention_t(three(qi), three(wi), three(kib), three(qa), three(kab), vat, grp,
                              _row_tile(t, 256), _row_tile(t, 512))
        oc = _mla_attention_t(three(qcn), three(qcr), three(kn), vct, three(krb), grp, _row_tile(t, 512))
        prev_b = prev_f = None
    else:
        c_ak, c_av, c_ik, c_lat, c_kr, prev_b, prev_f = past
        new_rows = lambda a: jnp.pad(three(a), ((0, 0), (0, -t % LANE), (0, 0)))
        oa = _dsa_attention(three(qi), three(wi), three(qa), c_ik, c_ak, c_av,
                            new_rows(kib), new_rows(kab), new_rows(vab), layer, grp)
        oc = _mla_attention_latent(three(qcn), three(qcr), c_lat, c_kr, new_rows(latb), new_rows(krb),
                                   w_uk, w_uv, layer, grp)

    def expand(state):
        return jnp.repeat(state[:, 0], t, axis=0), jnp.repeat(state[:, 1], t, axis=0)

    x1, h2 = _out_projection(x, oa.reshape(m, -1), bg, u, oc.reshape(m, -1), w_out, layer, conv_b_w, mod,
                             ln1_g, ln1_b, None if carried else expand(prev_b), grp, alpha,
                             _row_tile(t, 512) if carried else m)
    tm_f = _row_tile(t, 512) if carried else m
    tf = _row_tile(dm.dff, 512)
    x2, gate_rows = _channel_mixer(h2, x1, w_gu, w_down, layer, conv_f_w, mod, ln2_g, ln2_b,
                                   None if carried else expand(prev_f), grp, alpha, tm_f, tf)
    if carried:
        new_f = gate_rows.reshape(b, t // tm_f, SUBLANE, dm.dff)[:, -1, SUBLANE - (CONV_W - 1):, :]
    else:
        new_f = gate_rows.reshape(b, t, dm.dff)[:, t - (CONV_W - 1):, :]
    new_b = u.reshape(b, t, dm.bw)[:, t - (CONV_W - 1):, :]
    return x2, stacks, (new_b, new_f)


def _state_outputs(stacks, conv_rows, grp, dm):
    ka, ki, kr, va, lat = stacks
    b, t, heads = grp.b, grp.t, dm.aw // HEAD
    new_b, new_f = [jnp.stack(r) for r in zip(*conv_rows)]
    return (ka.reshape(dm.depth, b, t, heads, HEAD), va.reshape(dm.depth, b, t, heads, HEAD),
            ki.reshape(dm.depth, b, t, HEAD), lat.reshape(dm.depth, b, t, dm.r), kr.reshape(dm.depth, b, t, HEAD),
            new_b, new_f)


def kernel(x_prompt, x_sample, c_prompt, c_sample, cache_a_k, cache_a_v, cache_idx_k, cache_mla_latent,
           cache_mla_krope, state_conv_b, state_conv_ffn, w_in, w_out, conv_b_w, mla_kv_norm, mla_w_uk,
           mla_w_uv, w_mod, b_mod, ln1_g, ln1_b, ln2_g, ln2_b, ffn_w_gu, ffn_conv_w, ffn_w_down):
    depth, d, _ = w_in.shape
    a_heads = cache_a_k.shape[3]
    dm = Dims(d=d, aw=a_heads * HEAD, bw=conv_b_w.shape[2], ch=mla_w_uk.shape[2] // C_NOPE,
              r=mla_w_uk.shape[1], dff=ffn_w_down.shape[1], depth=depth)
    alpha = (2 * depth) ** 0.25
    grp_p = Group(b=x_prompt.shape[0], t=x_prompt.shape[1], past=0)
    grp_s = Group(b=x_sample.shape[0], t=x_sample.shape[1], past=cache_a_k.shape[2])

    n_c = grp_p.b + grp_s.b
    c_all = jnp.concatenate([c_prompt, c_sample, jnp.zeros((-n_c % SUBLANE, d), F32)], axis=0)
    mod = _modulation(c_all, w_mod, b_mod)
    mod_p = mod[:, :grp_p.b].reshape(depth, grp_p.b, 1, N_MOD * d)
    mod_s = jnp.repeat(mod[:, grp_p.b:n_c], grp_s.t, axis=1).reshape(depth, 1, grp_s.b * grp_s.t, N_MOD * d)

    tab_p = _rope_tables(jnp.arange(grp_p.t, dtype=I32))
    tab_s = tuple(jnp.tile(a, (grp_s.b, 1)) for a in _rope_tables(grp_s.past + jnp.arange(grp_s.t, dtype=I32)))

    xp = x_prompt.reshape(grp_p.b * grp_p.t, d)
    xs = x_sample.reshape(grp_s.b * grp_s.t, d)
    conv_p, conv_s = [], []
    stacks_p = stacks_s = None
    w_uv_b = mla_w_uv.astype(BF16)
    big = (_pack_w_in(w_in, dm), w_out.astype(BF16), mla_w_uk.astype(BF16), w_uv_b, jnp.swapaxes(w_uv_b, 1, 2),
           ffn_w_gu.astype(BF16), ffn_w_down.astype(BF16))
    idx_k = cache_idx_k.astype(BF16)
    caches = (cache_a_k.reshape(cache_a_k.shape[:3] + (-1,)).astype(BF16),
              cache_a_v.reshape(cache_a_v.shape[:3] + (-1,)).astype(BF16),
              jnp.concatenate([idx_k, idx_k], axis=-1))
    for l in range(depth):
        small = (conv_b_w[l], mla_kv_norm[l], ln1_g[l], ln1_b[l], ln2_g[l], ln2_b[l], ffn_conv_w[l])
        xp, stacks_p, rp = _layer(xp, mod_p, tab_p, grp_p, None, l, big, small, stacks_p, dm, alpha)
        past_l = (*caches, cache_mla_latent, cache_mla_krope, state_conv_b[l], state_conv_ffn[l])
        xs, stacks_s, rs = _layer(xs, mod_s, tab_s, grp_s, past_l, l, big, small, stacks_s, dm, alpha)
        conv_p.append(rp)
        conv_s.append(rs)
    return (xp.reshape(x_prompt.shape), xs.reshape(x_sample.shape),
            *_state_outputs(stacks_p, conv_p, grp_p, dm), *_state_outputs(stacks_s, conv_s, grp_s, dm))
```

```python
import functools
from typing import NamedTuple

import numpy as np
import jax
import jax.numpy as jnp
from jax import lax
from jax.experimental import pallas as pl
from jax.experimental.pallas import tpu as pltpu

F32, BF16, I32, I16 = jnp.float32, jnp.bfloat16, jnp.int32, jnp.int16

CHUNK = 64
CONV_W = 3
ROPE_THETA = 10000.0
HEAD = 64
IDX_HEADS = 16
TOPK_MAX = 256
C_NOPE = 128
C_V = 128
N_MOD = 6
LN_EPS = 1e-5
RMS_EPS = 1e-6

LANE = 128
SUBLANE = 8
VMEM_LIMIT = 50 * 1024 * 1024

MASKED = -1e30
LOG2E = 1.4426950408889634
INT_MIN = -2 ** 31
I16_MIN = -2 ** 15
KEY_NEG_INF = int(np.array(-np.inf, np.float32).view(np.int32)) ^ 0x7FFFFFFF


class Dims(NamedTuple):
    d: int
    aw: int
    bw: int
    ch: int
    r: int
    dff: int
    depth: int


class Group(NamedTuple):
    b: int
    t: int
    past: int


def _row_tile(m, pref):
    if m <= pref:
        return m
    t = pref - pref % SUBLANE
    while m % t:
        t -= SUBLANE
    return t


def _key_tile(l):
    lp = -(-l // LANE) * LANE
    for tk in (512, 384, 256, 128):
        if lp % tk == 0:
            return lp, tk
    raise AssertionError(lp)


def _params(sem):
    return pltpu.CompilerParams(dimension_semantics=sem, vmem_limit_bytes=VMEM_LIMIT)


def _resident(shape, index_map):
    return pl.BlockSpec(shape, index_map, pipeline_mode=pl.Buffered(1))


def _mod_kernel(c_ref, w_ref, b_ref, o_ref):
    a = jax.nn.silu(c_ref[...]).astype(BF16)
    o_ref[...] = jnp.dot(a, w_ref[...].astype(BF16), preferred_element_type=F32) + b_ref[...]


def _modulation(c, w_mod, b_mod):
    depth, d, n = w_mod.shape
    rows = c.shape[0]
    tn = _row_tile(n, 1024)
    return pl.pallas_call(
        _mod_kernel,
        grid=(depth, n // tn),
        in_specs=[pl.BlockSpec((rows, d), lambda l, j: (0, 0)),
                  pl.BlockSpec((None, d, tn), lambda l, j: (l, 0, j)),
                  pl.BlockSpec((None, 1, tn), lambda l, j: (l, 0, j))],
        out_specs=pl.BlockSpec((None, rows, tn), lambda l, j: (l, 0, j)),
        out_shape=jax.ShapeDtypeStruct((depth, rows, n), F32),
        compiler_params=_params(("arbitrary", "arbitrary")),
        name="modulation",
    )(c, w_mod, b_mod.reshape(depth, 1, n))


def _pack_w_in(w, dm):
    d, aw, bw, ch, r = dm.d, dm.aw, dm.bw, dm.ch, dm.r
    lead = w.shape[:-1]
    o = np.cumsum([0, aw, aw, aw, IDX_HEADS * HEAD, HEAD, IDX_HEADS, bw, bw, bw, ch * (C_NOPE + HEAD), r, HEAD])
    qa, ka, va, qi, ki, wi, bg, cg, xb, qc, lat, kr = [w[..., o[i]:o[i + 1]] for i in range(12)]
    qc = qc.reshape(lead + (ch, C_NOPE + HEAD))
    qcn = qc[..., :C_NOPE].reshape(lead + (ch * C_NOPE,))
    qcr = qc[..., C_NOPE:].reshape(lead + (ch * HEAD,))
    pad = jnp.zeros(lead + (LANE - IDX_HEADS,), w.dtype)
    return jnp.concatenate([qa, ka, qi, qcr, ki, ki, kr, kr, va, bg, cg, xb, qcn, lat, wi, pad],
                           axis=-1).astype(BF16)


def _rope(acc, cos, s1, s2):
    outs = []
    for s in range(acc.shape[1] // LANE):
        xs = acc[:, s * LANE:(s + 1) * LANE]
        outs.append(xs * cos + pltpu.roll(xs, HEAD // 2, 1) * s1 + pltpu.roll(xs, LANE - HEAD // 2, 1) * s2)
    return outs[0] if len(outs) == 1 else jnp.concatenate(outs, axis=1)


def _inproj_kernel(x_ref, sc_ref, sh_ref, w_ref, cos_ref, s1_ref, s2_ref, nrm_ref, *rest, dm, n_alias):
    (qa_ref, ka_ref, kab_ref, qi_ref, qcr_ref, ki_ref, kib_ref, kr_ref, krb_ref,
     va_ref, vab_ref, bg_ref, u_ref, qcn_ref, lat_ref, latb_ref, wi_ref, vat_ref,
     qat_ref, qit_ref, qcrt_ref, qcnt_ref, wit_ref) = rest[n_alias:]
    aw, bw, ch, r = dm.aw, dm.bw, dm.ch, dm.r
    h = (x_ref[...] * (1.0 + sc_ref[...]) + sh_ref[...]).astype(BF16)
    cos, s1, s2 = cos_ref[...], s1_ref[...], s2_ref[...]
    col = [0]

    def proj(width):
        c0 = col[0]
        col[0] = c0 + width
        return jnp.dot(h, w_ref[:, c0:c0 + width], preferred_element_type=F32)

    def pieces(width, step=512):
        return [(o, min(step, width - o)) for o in range(0, width, step)]

    for o, wd in pieces(aw):
        y = _rope(proj(wd), cos, s1, s2)
        qa_ref[:, o:o + wd] = y.astype(BF16)
        qat_ref[o:o + wd, :] = y.T.astype(BF16)
    for o, wd in pieces(aw):
        y = _rope(proj(wd), cos, s1, s2)
        ka_ref[:, o // HEAD:(o + wd) // HEAD, :] = y.reshape(y.shape[0], wd // HEAD, HEAD)
        kab_ref[:, o:o + wd] = y.astype(BF16)
    for o, wd in pieces(IDX_HEADS * HEAD):
        y = _rope(proj(wd), cos, s1, s2)
        qi_ref[:, o:o + wd] = y.astype(BF16)
        qit_ref[o:o + wd, :] = y.T.astype(BF16)
    for o, wd in pieces(ch * HEAD):
        y = _rope(proj(wd), cos, s1, s2)
        qcr_ref[:, o:o + wd] = y.astype(BF16)
        qcrt_ref[o:o + wd, :] = y.T.astype(BF16)
    y = _rope(proj(2 * LANE), cos, s1, s2)
    for c, (f32_ref, b16_ref) in enumerate(((ki_ref, kib_ref), (kr_ref, krb_ref))):
        f32_ref[...] = y[:, c * LANE:c * LANE + HEAD]
        b16_ref[...] = y[:, c * LANE:(c + 1) * LANE].astype(BF16)
    for o, wd in pieces(aw):
        y = proj(wd)
        va_ref[:, o // HEAD:(o + wd) // HEAD, :] = y.reshape(y.shape[0], wd // HEAD, HEAD)
        vab_ref[:, o:o + wd] = y.astype(BF16)
        vat_ref[o:o + wd, :] = y.T.astype(BF16)
    for o, wd in pieces(bw):
        bg_ref[:, o:o + wd] = proj(wd)
    c_cg = col[0]
    for o, wd in pieces(bw):
        cg = jnp.dot(h, w_ref[:, c_cg + o:c_cg + o + wd], preferred_element_type=F32)
        xb = jnp.dot(h, w_ref[:, c_cg + bw + o:c_cg + bw + o + wd], preferred_element_type=F32)
        u_ref[:, o:o + wd] = cg * xb
    col[0] = c_cg + 2 * bw
    for o, wd in pieces(ch * C_NOPE):
        y = proj(wd)
        qcn_ref[:, o:o + wd] = y.astype(BF16)
        qcnt_ref[o:o + wd, :] = y.T.astype(BF16)
    lat = proj(r)
    lat = lat * lax.rsqrt(jnp.mean(lat * lat, axis=-1, keepdims=True) + RMS_EPS) * nrm_ref[...]
    lat_ref[...] = lat
    latb_ref[...] = lat.astype(BF16)
    wi = proj(LANE) * (IDX_HEADS ** -0.5)
    wi_ref[...] = wi
    wit_ref[...] = wi.T


STATE_OUTPUTS = (1, 5, 7, 9, 14)
HEAD_OUTPUTS = (1, 9)


MOD_SH1, MOD_SC1, MOD_G1, MOD_SH2, MOD_SC2, MOD_G2 = range(N_MOD)


def _mod_spec(mod, which, layer, n_i, grid_rank):
    groups, rows, d = mod.shape[1], mod.shape[2], mod.shape[3] // N_MOD
    tiles_per_group = n_i // groups
    if grid_rank == 1:
        return pl.BlockSpec((None, None, rows, d), lambda i: (layer, i // tiles_per_group, 0, which))
    return pl.BlockSpec((None, None, rows, d), lambda i, f: (layer, i // tiles_per_group, 0, which))


def _in_projection(x, mod, w_packed, layer, tables, kv_norm, stacks, dm, tm):
    m, d = x.shape
    aw, bw, ch, r = dm.aw, dm.bw, dm.ch, dm.r
    npk = w_packed.shape[2]
    n_i = m // tm
    tab_tiles = tables[0].shape[0] // tm
    widths = [(aw, BF16), (aw, F32), (aw, BF16), (IDX_HEADS * HEAD, BF16), (ch * HEAD, BF16),
              (HEAD, F32), (LANE, BF16), (HEAD, F32), (LANE, BF16),
              (aw, F32), (aw, BF16), (bw, F32), (bw, F32), (ch * C_NOPE, BF16), (r, F32), (r, BF16),
              (LANE, F32)]
    row = lambda i: (i, 0)
    tab_spec = pl.BlockSpec((tm, LANE), lambda i: (i % tab_tiles, 0))
    in_specs = [pl.BlockSpec((tm, d), row),
                _mod_spec(mod, MOD_SC1, layer, n_i, 1), _mod_spec(mod, MOD_SH1, layer, n_i, 1),
                _resident((None, d, npk), lambda i: (layer, 0, 0)),
                tab_spec, tab_spec, tab_spec,
                pl.BlockSpec((1, r), lambda i: (0, 0))]
    args = [x, mod, mod, w_packed, *tables, kv_norm.reshape(1, r)]
    widths_t = [(aw, BF16), (aw, BF16), (IDX_HEADS * HEAD, BF16), (ch * HEAD, BF16), (ch * C_NOPE, BF16),
                (LANE, F32)]
    out_specs = ([pl.BlockSpec((tm, w), row) for w, _ in widths]
                 + [pl.BlockSpec((w, tm), lambda i: (0, i)) for w, _ in widths_t])
    out_shape = ([jax.ShapeDtypeStruct((m, w), dt) for w, dt in widths]
                 + [jax.ShapeDtypeStruct((w, m), dt) for w, dt in widths_t])
    for o in STATE_OUTPUTS:
        w, dt = widths[o]
        tail = (w // HEAD, HEAD) if o in HEAD_OUTPUTS else (w,)
        out_specs[o] = pl.BlockSpec((None, tm) + tail, lambda i, n=len(tail): (layer, i) + (0,) * n)
        out_shape[o] = jax.ShapeDtypeStruct((dm.depth, m) + tail, dt)
    aliases = {}
    if stacks is not None:
        aliases = {len(args) + k: o for k, o in enumerate(STATE_OUTPUTS)}
        in_specs += [pl.BlockSpec(memory_space=pl.ANY)] * len(stacks)
        args += list(stacks)
    return pl.pallas_call(
        functools.partial(_inproj_kernel, dm=dm, n_alias=len(aliases)),
        grid=(n_i,),
        in_specs=in_specs,
        out_specs=out_specs,
        out_shape=out_shape,
        input_output_aliases=aliases,
        compiler_params=_params(("arbitrary",)),
        name="in_projection",
    )(*args)


def _kvup_kernel(l_ref, wk_ref, wvt_ref, k_ref, vt_ref):
    lat = l_ref[...]
    k_ref[...] = jnp.dot(lat, wk_ref[...], preferred_element_type=F32).astype(BF16)
    vt_ref[...] = lax.dot_general(wvt_ref[...], lat, _NT, preferred_element_type=F32).astype(BF16)


def _kv_up(lat, w_uk, w_uv_t, layer):
    m, r = lat.shape
    n = w_uk.shape[2]
    tm = _row_tile(m, 512)
    return pl.pallas_call(
        _kvup_kernel,
        grid=(m // tm,),
        in_specs=[pl.BlockSpec((tm, r), lambda i: (i, 0)),
                  pl.BlockSpec((None, r, n), lambda i: (layer, 0, 0)),
                  pl.BlockSpec((None, n, r), lambda i: (layer, 0, 0))],
        out_specs=[pl.BlockSpec((tm, n), lambda i: (i, 0)), pl.BlockSpec((n, tm), lambda i: (0, i))],
        out_shape=[jax.ShapeDtypeStruct((m, n), BF16), jax.ShapeDtypeStruct((n, m), BF16)],
        compiler_params=_params(("arbitrary",)),
        name="latent_up_projection",
    )(lat, w_uk, w_uv_t)


_NT = (((1,), (1,)), ((), ()))


def _visible_tiles(qpos0, tq, l, tk):
    nvis = jnp.minimum(l, ((qpos0 + tq - 1) // CHUNK + 1) * CHUNK)
    return (nvis + tk - 1) // tk


def _flash_step(carry, s, v):
    m, l, acc = carry
    m_new = jnp.maximum(m, jnp.max(s, axis=1, keepdims=True))
    alpha = jnp.exp(m - m_new)
    p = jnp.exp(s - m_new)
    l = alpha * l + jnp.sum(p, axis=1, keepdims=True)
    acc = alpha * acc + jnp.dot(p.astype(BF16), v, preferred_element_type=F32)
    return m_new, l, acc


def _flash_init(rows, width):
    return (jnp.full((rows, 1), MASKED, F32), jnp.zeros((rows, 1), F32), jnp.zeros((rows, width), F32))


def _ordered_key(x):
    b = pltpu.bitcast(x, I32)
    return jnp.where(b < 0, b ^ 0x7FFFFFFF, b)


def _dsa_kernel(qi_ref, wi_ref, qa_ref, kip_ref, kp_ref, vp_ref, kin_ref, kn_ref, vn_ref, o_ref,
                key_scr, bias_scr, thr2_scr, *, tq, tk, past, topk, aw):
    l = past + tq
    tn = kn_ref.shape[0]
    span = past + tn
    tiles = [(kip_ref, kp_ref, vp_ref, r0, tk, r0) for r0 in range(0, past, tk)]
    tiles.append((kin_ref, kn_ref, vn_ref, 0, tn, past))
    qchunk = (past + lax.broadcasted_iota(I32, (tq, 1), 0)) // CHUNK
    lane = lax.broadcasted_iota(I32, (1, LANE), 1)
    lo_half = lane < HEAD
    kf = jnp.float32(topk)

    def split_heads(qs):
        zero = jnp.zeros_like(qs)
        return jnp.concatenate([jnp.where(lo_half, qs, zero), jnp.where(lo_half, zero, qs)], axis=0)

    wi = wi_ref[...] * (HEAD ** -0.5)
    q_idx = jnp.concatenate([split_heads(qi_ref[:, s * LANE:(s + 1) * LANE]) for s in range(IDX_HEADS // 2)],
                            axis=0)
    w_idx = [wi[:, h:h + 1] for h in range(IDX_HEADS)]

    for kt, (ki_ref, _, _, r0, width, pos0) in enumerate(tiles):
        s_all = lax.dot_general(q_idx, ki_ref[r0:r0 + width, :], _NT, preferred_element_type=F32)
        score = jnp.zeros((tq, width), F32)
        for h in range(IDX_HEADS):
            score = score + jnp.maximum(s_all[h * tq:(h + 1) * tq], 0.0) * w_idx[h]
        kpos = pos0 + lax.broadcasted_iota(I32, (1, width), 1)
        vis = (kpos // CHUNK <= qchunk) & (kpos < l)
        key_scr[kt, :, :width] = _ordered_key(jnp.where(vis, score, -jnp.inf))

    def slabs():
        return [(kt, j, pos0 + j) for kt, (_, _, _, _, width, pos0) in enumerate(tiles)
                for j in range(0, width, LANE)]

    def count(pred):
        acc = jnp.zeros((tq, LANE), F32)
        for kt, j, pos in slabs():
            acc = acc + jnp.where(pred(key_scr[kt, :, j:j + LANE], pos), 1.0, 0.0)
        return jnp.sum(acc, axis=1, keepdims=True)

    def wide(col):
        return jnp.broadcast_to(col, (tq, LANE))

    def bits_body(b, lo):
        step = jnp.left_shift(jnp.int32(1), 30 - 2 * b)
        for mult in (1, 2, 3):
            cand = lo + mult * step
            cand_w = wide(cand)
            lo_next = jnp.where(count(lambda ks, _, cand_w=cand_w: ks >= cand_w) >= kf, cand,
                                lo if mult == 1 else lo_next)
        return lo_next

    thr = lax.fori_loop(0, 16, bits_body, jnp.full((tq, 1), INT_MIN, I32))
    thr_w = wide(thr)

    n_ge = count(lambda ks, _: ks >= thr_w)
    n_gt = count(lambda ks, _: ks > thr_w)
    excess = (n_ge > kf) & (thr > KEY_NEG_INF)
    thr2_scr[...] = jnp.zeros((tq, 1), I32)

    @pl.when(jnp.max(jnp.where(excess, 1.0, 0.0)) > 0.0)
    def _():
        need = kf - n_gt
        nbits = span.bit_length()

        def bit2_body(b, lo):
            cand = lo + jnp.left_shift(jnp.int32(1), nbits - 1 - b)
            cand_w = wide(cand)
            c = count(lambda ks, pos: jnp.where(ks == thr_w, span - (pos + lane), 0) >= cand_w)
            return jnp.where(c >= need, cand, lo)

        thr2_scr[...] = lax.fori_loop(0, nbits, bit2_body, jnp.zeros((tq, 1), I32))

    thr2_w = wide(thr2_scr[...])

    for kt, j, pos in slabs():
        ks = key_scr[kt, :, j:j + LANE]
        tie = jnp.where(span - (pos + lane) >= thr2_w, 0.0, MASKED)
        bias = jnp.where(ks > thr_w, 0.0, jnp.where(ks == thr_w, tie, MASKED))
        bias_scr[kt, :, j:j + LANE] = jnp.where(ks > KEY_NEG_INF, bias, MASKED)

    for pr in range(aw // LANE):
        cols = slice(pr * LANE, (pr + 1) * LANE)
        q2 = split_heads(qa_ref[:, cols])
        carry = _flash_init(2 * tq, LANE)
        for kt, (_, k_ref, v_ref, r0, width, _) in enumerate(tiles):
            s = lax.dot_general(q2, k_ref[r0:r0 + width, cols], _NT, preferred_element_type=F32)
            bias = bias_scr[kt, :, :width]
            s = s * (HEAD ** -0.5) + jnp.concatenate([bias, bias], axis=0)
            carry = _flash_step(carry, s, v_ref[r0:r0 + width, cols])
        _, den, acc = carry
        o2 = acc / den
        o_ref[:, cols] = jnp.where(lo_half, o2[:tq], o2[tq:]).astype(BF16)


def _dsa_attention(qi, wi, qa, ki_cache, k_cache, v_cache, ki_new, k_new, v_new, layer, grp):
    b, t, aw = qa.shape
    past, tn = k_cache.shape[2], k_new.shape[1]
    assert past % LANE == 0 and tn % LANE == 0
    tk = next(c for c in (512, 384, 256, 128) if past % c == 0)
    width = max(tk, tn)
    topk = min(TOPK_MAX, (past + t) // 4)
    qblk = lambda w: pl.BlockSpec((None, t, w), lambda i: (i, 0, 0))
    cblk = lambda w: pl.BlockSpec((None, None, past, w), lambda i: (layer, i, 0, 0))
    nblk = lambda w: pl.BlockSpec((None, tn, w), lambda i: (i, 0, 0))
    return pl.pallas_call(
        functools.partial(_dsa_kernel, tq=t, tk=tk, past=past, topk=topk, aw=aw),
        grid=(b,),
        in_specs=[qblk(IDX_HEADS * HEAD), qblk(LANE), qblk(aw), cblk(LANE), cblk(aw), cblk(aw),
                  nblk(LANE), nblk(aw), nblk(aw)],
        out_specs=qblk(aw),
        out_shape=jax.ShapeDtypeStruct((b, t, aw), BF16),
        scratch_shapes=[pltpu.VMEM((past // tk + 1, t, width), I32),
                        pltpu.VMEM((past // tk + 1, t, width), F32),
                        pltpu.VMEM((t, 1), I32)],
        compiler_params=_params(("arbitrary",)),
        name="dsa_attention",
    )(qi, wi, qa, ki_cache, k_cache, v_cache, ki_new, k_new, v_new)


def _mla_latent_kernel(qn_ref, qr_ref, latp_ref, krp_ref, latn_ref, krn_ref, wuk_ref, wuv_ref, o_ref,
                       *, t, l, past, ch):
    lo_half = lax.broadcasted_iota(I32, (1, LANE), 1) < HEAD
    scale = (C_NOPE + HEAD) ** -0.5
    q_lat, q_rope = [], []
    for h in range(ch):
        cols = slice(h * C_NOPE, (h + 1) * C_NOPE)
        q_lat.append(lax.dot_general(qn_ref[:, cols], wuk_ref[:, cols], _NT, preferred_element_type=F32))
        qs = qr_ref[:, (h // 2) * LANE:(h // 2 + 1) * LANE]
        zero = jnp.zeros_like(qs)
        q_rope.append(jnp.where(lo_half, qs, zero) if h % 2 == 0 else jnp.where(lo_half, zero, qs))
    q_lat = jnp.concatenate(q_lat, axis=0).astype(BF16)
    q_rope = jnp.concatenate(q_rope, axis=0)
    qchunk = jnp.concatenate([(past + lax.broadcasted_iota(I32, (t, 1), 0)) // CHUNK] * ch, axis=0)
    kr_past = krp_ref[...].astype(BF16)
    segments = ((latp_ref[...].astype(BF16), jnp.concatenate([kr_past, kr_past], axis=1), 0, past),
                (latn_ref[...], krn_ref[...], past, l))
    scores = []
    for lat, kr, pos0, pos_end in segments:
        s = (lax.dot_general(q_lat, lat, _NT, preferred_element_type=F32)
             + lax.dot_general(q_rope, kr, _NT, preferred_element_type=F32)) * scale
        kpos = pos0 + lax.broadcasted_iota(I32, (1, lat.shape[0]), 1)
        scores.append(jnp.where((kpos // CHUNK <= qchunk) & (kpos < pos_end), s, MASKED))
    m = functools.reduce(jnp.maximum, [jnp.max(s, axis=1, keepdims=True) for s in scores])
    ps = [jnp.exp(s - m) for s in scores]
    den = sum(jnp.sum(p, axis=1, keepdims=True) for p in ps)
    acc = sum(jnp.dot(p.astype(BF16), seg[0], preferred_element_type=F32) for p, seg in zip(ps, segments))
    o_lat = (acc / den).astype(BF16)
    for h in range(ch):
        cols = slice(h * C_V, (h + 1) * C_V)
        o_ref[:, cols] = jnp.dot(o_lat[h * t:(h + 1) * t, :], wuv_ref[:, cols],
                                 preferred_element_type=F32).astype(BF16)


def _mla_attention_latent(qn, qr, lat_cache, kr_cache, lat_new, kr_new, w_uk, w_uv, layer, grp):
    b, t, wn = qn.shape
    ch = wn // C_NOPE
    past, r = lat_cache.shape[2:]
    tn = lat_new.shape[1]
    qblk = lambda w: pl.BlockSpec((None, t, w), lambda i: (i, 0, 0))
    cblk = lambda w: pl.BlockSpec((None, None, past, w), lambda i: (layer, i, 0, 0))
    nblk = lambda w: pl.BlockSpec((None, tn, w), lambda i: (i, 0, 0))
    wblk = pl.BlockSpec((None, r, wn), lambda i: (layer, 0, 0))
    return pl.pallas_call(
        functools.partial(_mla_latent_kernel, t=t, l=past + t, past=past, ch=ch),
        grid=(b,),
        in_specs=[qblk(wn), qblk(ch * HEAD), cblk(r), cblk(HEAD), nblk(r), nblk(LANE), wblk, wblk],
        out_specs=qblk(wn),
        out_shape=jax.ShapeDtypeStruct((b, t, wn), BF16),
        compiler_params=_params(("arbitrary",)),
        name="mla_attention_latent",
    )(qn, qr, lat_cache, kr_cache, lat_new, kr_new, w_uk, w_uv)


ONES_ROWS = 16


def _flash_t_init(m_scr, acc_scr):
    m_scr[...] = jnp.full(m_scr.shape, MASKED, F32)
    acc_scr[...] = jnp.zeros(acc_scr.shape, F32)


def _flash_t_stage(slot, h, s, s_scr):
    s_scr[slot, h] = s
    return jnp.max(s, axis=0, keepdims=True)


def _flash_t_step(slot, h, s_max, vt, c, s_scr, m_scr, acc_scr):
    m_old = m_scr[h]
    m_new = jnp.maximum(m_old, s_max)
    alpha = jnp.exp2((m_old - m_new) * c)
    p = jnp.exp2((s_scr[slot, h] - m_new) * c).astype(BF16)
    vt_ones = jnp.concatenate([vt, jnp.ones((ONES_ROWS, vt.shape[1]), BF16)], axis=0)
    acc_scr[h] = alpha * acc_scr[h] + jnp.dot(vt_ones, p, preferred_element_type=F32)
    m_scr[h] = m_new


def _flash_t_out(o_ref, acc_scr):
    heads, width = acc_scr.shape[0], acc_scr.shape[1] - ONES_ROWS
    ot = jnp.concatenate([acc_scr[h, :width, :] / acc_scr[h, width:width + 1, :] for h in range(heads)], axis=0)
    o_ref[...] = ot.T.astype(BF16)


def _split_heads_t(slab_t):
    row_lo = lax.broadcasted_iota(I32, (LANE, 1), 0) < HEAD
    zero = jnp.zeros_like(slab_t)
    return jnp.where(row_lo, slab_t, zero), jnp.where(row_lo, zero, slab_t)


def _dsa_t_kernel(qi_ref, wi_ref, ki_ref, qa_ref, k_ref, vt_ref, o_ref,
                  key_scr, hi_scr, lo_scr, thr2_scr, qit_scr, qat_scr, s_scr, m_scr, acc_scr,
                  *, tq, tk, l, past, topk):
    heads = acc_scr.shape[0]
    qpos0 = past + pl.program_id(1) * tq
    ntiles = _visible_tiles(qpos0, tq, l, tk)
    qchunk = (qpos0 + lax.broadcasted_iota(I32, (1, tq), 1)) // CHUNK
    krow = lax.broadcasted_iota(I32, (tk, 1), 0)

    for s in range(IDX_HEADS // 2):
        qit_scr[2 * s], qit_scr[2 * s + 1] = _split_heads_t(qi_ref[s * LANE:(s + 1) * LANE, :])
    for s in range(heads // 2):
        slab_t = (qa_ref[s * LANE:(s + 1) * LANE, :].astype(F32) * (HEAD ** -0.5)).astype(BF16)
        qat_scr[2 * s], qat_scr[2 * s + 1] = _split_heads_t(slab_t)
    w_t = wi_ref[...] * (HEAD ** -0.5)
    w_rows = [w_t[h:h + 1, :] for h in range(IDX_HEADS)]

    nfull = jnp.minimum(l, (qpos0 // CHUNK + 1) * CHUNK) // tk

    def score_body(kt, masked):
        k0 = pl.multiple_of(kt * tk, tk)
        ki_tile = ki_ref[pl.ds(k0, tk), :]
        score = jnp.zeros((tk, tq), F32)
        for h in range(IDX_HEADS):
            s = jnp.dot(ki_tile, qit_scr[h], preferred_element_type=F32)
            score = score + jnp.maximum(s, 0.0) * w_rows[h]
        if masked:
            kpos = k0 + krow
            score = jnp.where((kpos // CHUNK <= qchunk) & (kpos < l), score, -jnp.inf)
        key = _ordered_key(score)
        key_scr[kt] = key
        hi_scr[kt] = (key >> 16).astype(I16)
        lo_scr[kt] = ((key & 0xFFFF) + I16_MIN).astype(I16)
        return 0

    lax.fori_loop(0, nfull, lambda kt, _: score_body(kt, False), 0)
    lax.fori_loop(nfull, ntiles, lambda kt, _: score_body(kt, True), 0)

    def count(pred):
        def body(kt, acc):
            hit = jnp.where(pred(key_scr[kt], kt * tk), 1.0, 0.0)
            return acc + hit.reshape(tk // SUBLANE, SUBLANE, tq).sum(axis=0)
        acc = lax.fori_loop(0, ntiles, body, jnp.zeros((SUBLANE, tq), F32))
        return jnp.sum(acc, axis=0, keepdims=True)

    def count16(half_scr, cand, strict):
        rows = 2 * SUBLANE
        cand16 = cand.astype(I16)

        def body(kt, acc):
            half = half_scr[kt]
            hit = jnp.where(half > cand16 if strict else half >= cand16, jnp.int16(1), jnp.int16(0))
            parts = hit.reshape(tk // (4 * rows), 4, rows, tq)
            for g in range(parts.shape[0]):
                acc = acc + parts[g]
            return acc

        acc = lax.fori_loop(0, ntiles, body, jnp.zeros((4, rows, tq), I16))
        return acc.astype(I32).sum(axis=0).sum(axis=0, keepdims=True)

    def kth_largest16(half_scr, k_need):
        def bit_body(b, lo):
            cand = lo + jnp.left_shift(jnp.int32(1), 15 - b)
            return jnp.where(count16(half_scr, cand, False) >= k_need, cand, lo)
        return lax.fori_loop(0, 16, bit_body, jnp.full((1, tq), I16_MIN, I32))

    thr_hi = kth_largest16(hi_scr, jnp.full((1, tq), topk, I32))
    n_gt_hi = count16(hi_scr, thr_hi, True)
    thr_hi16 = thr_hi.astype(I16)

    def mark_body(kt, _):
        lo_scr[kt] = jnp.where(hi_scr[kt] == thr_hi16, lo_scr[kt], jnp.int16(I16_MIN))
        return 0

    lax.fori_loop(0, ntiles, mark_body, 0)
    thr_lo = kth_largest16(lo_scr, topk - n_gt_hi)
    thr = thr_hi * 65536 + (thr_lo - I16_MIN)

    lp = key_scr.shape[0] * tk
    n_gt = n_gt_hi + count16(lo_scr, thr_lo, True)
    n_ge = jnp.where(thr_lo > I16_MIN, n_gt_hi + count16(lo_scr, thr_lo, False), count16(hi_scr, thr_hi, False))
    excess = (n_ge > topk) & (thr > KEY_NEG_INF)
    thr2_scr[...] = jnp.zeros((1, tq), I32)

    @pl.when(jnp.max(jnp.where(excess, 1.0, 0.0)) > 0.0)
    def _():
        need = (topk - n_gt).astype(F32)
        nbits = lp.bit_length()

        def bit2_body(b, lo):
            cand = lo + jnp.left_shift(jnp.int32(1), nbits - 1 - b)
            c = count(lambda ks, base: jnp.where(ks == thr, lp - (base + krow), 0) >= cand)
            return jnp.where(c >= need, cand, lo)

        thr2_scr[...] = lax.fori_loop(0, nbits, bit2_body, jnp.zeros((1, tq), I32))

    thr2 = thr2_scr[...]

    def bias_body(kt, _):
        ks = key_scr[kt]
        tie = jnp.where(lp - (kt * tk + krow) >= thr2, 0.0, MASKED)
        bias = jnp.where(ks > thr, 0.0, jnp.where(ks == thr, tie, MASKED))
        key_scr[kt] = pltpu.bitcast(jnp.where(ks > KEY_NEG_INF, bias, MASKED), I32)
        return 0

    lax.fori_loop(0, ntiles, bias_body, 0)

    _flash_t_init(m_scr, acc_scr)

    def att_body(kt, _):
        k0 = pl.multiple_of(kt * tk, tk)
        slot = kt % 2
        bias = pltpu.bitcast(key_scr[kt], F32)
        s_max = []
        for h in range(heads):
            cols = slice((h // 2) * LANE, (h // 2 + 1) * LANE)
            s = jnp.dot(k_ref[pl.ds(k0, tk), cols], qat_scr[h], preferred_element_type=F32) + bias
            s_max.append(_flash_t_stage(slot, h, s, s_scr))
        for h in range(heads):
            _flash_t_step(slot, h, s_max[h], vt_ref[h * HEAD:(h + 1) * HEAD, pl.ds(k0, tk)], LOG2E,
                          s_scr, m_scr, acc_scr)
        return 0

    lax.fori_loop(0, ntiles, att_body, 0)
    _flash_t_out(o_ref, acc_scr)


def _dsa_attention_t(qit, wit, ki, qat, k, vt, grp, tq, tk):
    b, t, aw = k.shape
    assert t % tk == 0
    heads = aw // HEAD
    n_q = t // tq
    topk = min(TOPK_MAX, t // 4)
    qblk = lambda w: pl.BlockSpec((w, tq), lambda i, j: (0, i * n_q + j))
    kblk = lambda w: _resident((None, t, w), lambda i, j: (i, 0, 0))
    return pl.pallas_call(
        functools.partial(_dsa_t_kernel, tq=tq, tk=tk, l=t, past=grp.past, topk=topk),
        grid=(b, n_q),
        in_specs=[qblk(IDX_HEADS * HEAD), qblk(LANE), kblk(LANE), qblk(aw), kblk(aw),
                  _resident((aw, t), lambda i, j: (0, i))],
        out_specs=pl.BlockSpec((None, tq, aw), lambda i, j: (i, j, 0)),
        out_shape=jax.ShapeDtypeStruct((b, t, aw), BF16),
        scratch_shapes=[pltpu.VMEM((t // tk, tk, tq), I32),
                        pltpu.VMEM((t // tk, tk, tq), I16),
                        pltpu.VMEM((t // tk, tk, tq), I16),
                        pltpu.VMEM((1, tq), I32),
                        pltpu.VMEM((IDX_HEADS, LANE, tq), BF16),
                        pltpu.VMEM((heads, LANE, tq), BF16),
                        pltpu.VMEM((2, heads, tk, tq), F32),
                        pltpu.VMEM((heads, 1, tq), F32),
                        pltpu.VMEM((heads, HEAD + ONES_ROWS, tq), F32)],
        compiler_params=_params(("arbitrary", "arbitrary")),
        name="dsa_attention_t",
    )(qit, wit, ki, qat, k, vt)


def _mla_t_kernel(qn_ref, qr_ref, kn_ref, vt_ref, kr_ref, o_ref, qt_scr, s_scr, m_scr, acc_scr,
                  *, tq, tk, l, past):
    ch = acc_scr.shape[0]
    qpos0 = past + pl.program_id(1) * tq
    ntiles = _visible_tiles(qpos0, tq, l, tk)
    nfull = jnp.minimum(l, (qpos0 // CHUNK + 1) * CHUNK) // tk
    qchunk = (qpos0 + lax.broadcasted_iota(I32, (1, tq), 1)) // CHUNK
    krow = lax.broadcasted_iota(I32, (tk, 1), 0)
    c = (C_NOPE + HEAD) ** -0.5 * LOG2E
    for s in range(ch // 2):
        pair = _split_heads_t(qr_ref[s * LANE:(s + 1) * LANE, :])
        for half in range(2):
            h = 2 * s + half
            qt_scr[h] = jnp.concatenate([qn_ref[h * LANE:(h + 1) * LANE, :], pair[half]], axis=0)
    _flash_t_init(m_scr, acc_scr)

    def tile(kt, masked):
        k0 = pl.multiple_of(kt * tk, tk)
        k_rope = kr_ref[pl.ds(k0, tk), :]
        slot = kt % 2
        if masked:
            kpos = k0 + krow
            bias = jnp.where((kpos // CHUNK <= qchunk) & (kpos < l), 0.0, MASKED)
        s_max = []
        for h in range(ch):
            kcat = jnp.concatenate([kn_ref[pl.ds(k0, tk), h * LANE:(h + 1) * LANE], k_rope], axis=1)
            s = jnp.dot(kcat, qt_scr[h], preferred_element_type=F32)
            s_max.append(_flash_t_stage(slot, h, s + bias if masked else s, s_scr))
        for h in range(ch):
            _flash_t_step(slot, h, s_max[h], vt_ref[h * C_V:(h + 1) * C_V, pl.ds(k0, tk)], c,
                          s_scr, m_scr, acc_scr)
        return 0

    lax.fori_loop(0, nfull, lambda kt, _: tile(kt, False), 0)
    lax.fori_loop(nfull, ntiles, lambda kt, _: tile(kt, True), 0)
    _flash_t_out(o_ref, acc_scr)


def _mla_attention_t(qnt, qrt, kn, vt, kr, grp, tq):
    b, t, wn = kn.shape
    ch = wn // C_NOPE
    lp, tk = _key_tile(t)
    assert lp == t and ch % 2 == 0
    n_q = t // tq
    qblk = lambda w: pl.BlockSpec((w, tq), lambda i, j: (0, i * n_q + j))
    kblk = lambda w: _resident((None, t, w), lambda i, j: (i, 0, 0))
    return pl.pallas_call(
        functools.partial(_mla_t_kernel, tq=tq, tk=tk, l=t, past=grp.past),
        grid=(b, n_q),
        in_specs=[qblk(wn), qblk(ch * HEAD), kblk(wn), _resident((wn, t), lambda i, j: (0, i)), kblk(LANE)],
        out_specs=pl.BlockSpec((None, tq, wn), lambda i, j: (i, j, 0)),
        out_shape=jax.ShapeDtypeStruct((b, t, wn), BF16),
        scratch_shapes=[pltpu.VMEM((ch, 2 * LANE, tq), BF16),
                        pltpu.VMEM((2, ch, tk, tq), F32),
                        pltpu.VMEM((ch, 1, tq), F32),
                        pltpu.VMEM((ch, C_V + ONES_ROWS, tq), F32)],
        compiler_params=_params(("arbitrary", "arbitrary")),
        name="mla_attention_t",
    )(qnt, qrt, kn, vt, kr)


def _causal_conv(u, e0, e1, w, seg):
    rmod = lax.broadcasted_iota(I32, (u.shape[0], 1), 0) % seg
    u1 = jnp.where(rmod == 0, e1, pltpu.roll(u, 1, 0))
    u2 = jnp.where(rmod == 0, e0, jnp.where(rmod == 1, e1, pltpu.roll(u, 2, 0)))
    return u2 * w[0:1] + u1 * w[1:2] + u * w[2:3]


def _layer_norm(z, g, b):
    mu = jnp.mean(z, axis=-1, keepdims=True)
    zc = z - mu
    var = jnp.mean(zc * zc, axis=-1, keepdims=True)
    return zc * lax.rsqrt(var + LN_EPS) * g + b


def _outproj_kernel(*refs, alpha, seq_tiles, seg, carried):
    if carried:
        (x_ref, oa_ref, bg_ref, u_ref, oc_ref, w_ref, cw_ref, g1_ref, lng_ref, lnb_ref, sc2_ref, sh2_ref,
         x1_ref, h2_ref, prev_scr) = refs

        @pl.when(pl.program_id(0) % seq_tiles == 0)
        def _():
            prev_scr[...] = jnp.zeros_like(prev_scr)

        e0, e1 = prev_scr[SUBLANE - 2:SUBLANE - 1, :], prev_scr[SUBLANE - 1:SUBLANE, :]
    else:
        (x_ref, oa_ref, bg_ref, u_ref, oc_ref, w_ref, cw_ref, g1_ref, lng_ref, lnb_ref, sc2_ref, sh2_ref,
         e0_ref, e1_ref, x1_ref, h2_ref) = refs
        e0, e1 = e0_ref[...], e1_ref[...]
    u = u_ref[...]
    yb = bg_ref[...] * _causal_conv(u, e0, e1, cw_ref[...], seg)
    if carried:
        prev_scr[...] = u[u.shape[0] - SUBLANE:, :]
    mixed = jnp.concatenate([oa_ref[...], yb.astype(BF16), oc_ref[...]], axis=1)
    tm = mixed.shape[0]
    n_split = 4 if tm % (4 * 2 * SUBLANE) == 0 else 1

    def rows_of(ref, rows):
        return ref[...] if ref.shape[0] == 1 else ref[rows, :]

    halves = [slice(c * tm // n_split, (c + 1) * tm // n_split) for c in range(n_split)]
    mixes = [jnp.dot(mixed[rows, :], w_ref[...], preferred_element_type=F32) for rows in halves]
    for rows, mix in zip(halves, mixes):
        x1 = _layer_norm(alpha * x_ref[rows, :] + (1.0 + rows_of(g1_ref, rows)) * mix, lng_ref[...], lnb_ref[...])
        x1_ref[rows, :] = x1
        h2_ref[rows, :] = (x1 * (1.0 + rows_of(sc2_ref, rows)) + rows_of(sh2_ref, rows)).astype(BF16)


def _out_projection(x, oa, bg, u, oc, w_out, layer, conv_w, mod, ln_g, ln_b, prev, grp, alpha, tm):
    m, d = x.shape
    aw, bw, cw = oa.shape[1], bg.shape[1], oc.shape[1]
    n_i = m // tm
    row = lambda i: (i, 0)
    fix = lambda i: (0, 0)
    in_specs = [pl.BlockSpec((tm, d), row), pl.BlockSpec((tm, aw), row), pl.BlockSpec((tm, bw), row),
                pl.BlockSpec((tm, bw), row), pl.BlockSpec((tm, cw), row),
                _resident((None, aw + bw + cw, d), lambda i: (layer, 0, 0)), pl.BlockSpec((CONV_W, bw), fix),
                _mod_spec(mod, MOD_G1, layer, n_i, 1), pl.BlockSpec((1, d), fix), pl.BlockSpec((1, d), fix),
                _mod_spec(mod, MOD_SC2, layer, n_i, 1), _mod_spec(mod, MOD_SH2, layer, n_i, 1)]
    args = [x, oa, bg, u, oc, w_out, conv_w, mod, ln_g.reshape(1, d), ln_b.reshape(1, d), mod, mod]
    scratch = []
    if prev is None:
        assert grp.t % tm == 0
        scratch = [pltpu.VMEM((SUBLANE, bw), F32)]
    else:
        assert tm % grp.t == 0
        in_specs += [pl.BlockSpec((tm, bw), row), pl.BlockSpec((tm, bw), row)]
        args += [prev[0], prev[1]]
    return pl.pallas_call(
        functools.partial(_outproj_kernel, alpha=alpha, seq_tiles=max(grp.t // tm, 1), seg=min(grp.t, tm),
                          carried=prev is None),
        grid=(n_i,),
        in_specs=in_specs,
        out_specs=[pl.BlockSpec((tm, d), row), pl.BlockSpec((tm, d), row)],
        out_shape=[jax.ShapeDtypeStruct((m, d), F32), jax.ShapeDtypeStruct((m, d), BF16)],
        scratch_shapes=scratch,
        compiler_params=_params(("arbitrary",)),
        name="out_projection",
    )(*args)


def _ffn_kernel(*refs, alpha, seq_tiles, seg, carried):
    if carried:
        (h_ref, x_ref, wg_ref, wu_ref, wd_ref, cw_ref, g2_ref, lng_ref, lnb_ref,
         o_ref, gt_ref, acc_scr, prev_scr) = refs
    else:
        (h_ref, x_ref, wg_ref, wu_ref, wd_ref, cw_ref, g2_ref, lng_ref, lnb_ref, e0_ref, e1_ref,
         o_ref, gt_ref, acc_scr) = refs
    f = pl.program_id(1)
    if carried:
        @pl.when(pl.program_id(0) % seq_tiles == 0)
        def _():
            prev_scr[f] = jnp.zeros(prev_scr.shape[1:], F32)

    @pl.when(f == 0)
    def _():
        acc_scr[...] = jnp.zeros_like(acc_scr)

    h = h_ref[...]
    tm, tf = h.shape[0], wg_ref.shape[1]
    n_split = 2 if tf % (2 * LANE) == 0 else 1
    halves = [slice(c * tf // n_split, (c + 1) * tf // n_split) for c in range(n_split)]
    gates = [jnp.dot(h, wg_ref[:, cols], preferred_element_type=F32) for cols in halves]
    ups = [jnp.dot(h, wu_ref[:, cols], preferred_element_type=F32) for cols in halves]
    for cols, gate, up in zip(halves, gates, ups):
        if carried:
            e0, e1 = prev_scr[f, SUBLANE - 2:SUBLANE - 1, cols], prev_scr[f, SUBLANE - 1:SUBLANE, cols]
        else:
            e0, e1 = e0_ref[:, cols], e1_ref[:, cols]
        conv = _causal_conv(gate, e0, e1, cw_ref[:, cols], seg)
        if carried:
            prev_scr[f, :, cols] = gate[tm - SUBLANE:, :]
        gt_ref[:, cols] = gate[tm - gt_ref.shape[0]:, :]
        act = (jax.nn.silu(conv) * up).astype(BF16)
        acc_scr[...] = jnp.dot(act, wd_ref[cols, :], preferred_element_type=F32) + acc_scr[...]

    @pl.when(f == pl.num_programs(1) - 1)
    def _():
        z = alpha * x_ref[...] + (1.0 + g2_ref[...]) * acc_scr[...]
        o_ref[...] = _layer_norm(z, lng_ref[...], lnb_ref[...])


def _channel_mixer(h2, x1, w_gu, w_down, layer, conv_w, mod, ln_g, ln_b, prev, grp, alpha, tm, tf):
    m, d = x1.shape
    dff = w_down.shape[1]
    n_i, n_f = m // tm, dff // tf
    fix = lambda i, f: (0, 0)
    in_specs = [pl.BlockSpec((tm, d), lambda i, f: (i, 0)), pl.BlockSpec((tm, d), lambda i, f: (i, 0)),
                pl.BlockSpec((None, d, tf), lambda i, f: (layer, 0, f)),
                pl.BlockSpec((None, d, tf), lambda i, f: (layer, 0, n_f + f)),
                pl.BlockSpec((None, tf, d), lambda i, f: (layer, f, 0)),
                pl.BlockSpec((CONV_W, tf), lambda i, f: (0, f)),
                _mod_spec(mod, MOD_G2, layer, n_i, 2),
                pl.BlockSpec((1, d), fix), pl.BlockSpec((1, d), fix)]
    args = [h2, x1, w_gu, w_gu, w_down, conv_w, mod, ln_g.reshape(1, d), ln_b.reshape(1, d)]
    scratch = [pltpu.VMEM((tm, d), F32)]
    if prev is None:
        assert grp.t % tm == 0
        scratch.append(pltpu.VMEM((n_f, SUBLANE, tf), F32))
        gt_spec = pl.BlockSpec((None, SUBLANE, tf), lambda i, f: (i, 0, f))
        gt_shape = jax.ShapeDtypeStruct((n_i, SUBLANE, dff), F32)
    else:
        assert tm % grp.t == 0
        in_specs += [pl.BlockSpec((tm, tf), lambda i, f: (i, f))] * 2
        args += [prev[0], prev[1]]
        gt_spec = pl.BlockSpec((tm, tf), lambda i, f: (i, f))
        gt_shape = jax.ShapeDtypeStruct((m, dff), F32)
    return pl.pallas_call(
        functools.partial(_ffn_kernel, alpha=alpha, seq_tiles=max(grp.t // tm, 1), seg=min(grp.t, tm),
                          carried=prev is None),
        grid=(n_i, n_f),
        in_specs=in_specs,
        out_specs=[pl.BlockSpec((tm, d), lambda i, f: (i, 0)), gt_spec],
        out_shape=[jax.ShapeDtypeStruct((m, d), F32), gt_shape],
        scratch_shapes=scratch,
        compiler_params=_params(("arbitrary", "arbitrary")),
        name="channel_mixer",
    )(*args)


def _rope_tables(pos):
    half = HEAD // 2
    inv = jnp.power(jnp.float32(ROPE_THETA), -jnp.arange(half, dtype=F32) / half)
    ang = pos.astype(F32)[:, None] * inv[None, :]
    cos, sin = jnp.cos(ang), jnp.sin(ang)
    zero = jnp.zeros_like(sin)
    reps = LANE // HEAD
    return (jnp.tile(jnp.concatenate([cos, cos], axis=1), (1, reps)),
            jnp.tile(jnp.concatenate([zero, sin], axis=1), (1, reps)),
            jnp.tile(jnp.concatenate([-sin, zero], axis=1), (1, reps)))


def _layer(x, mod, tables, grp, past, layer, big, small, stacks, dm, alpha):
    w_in_p, w_out, w_uk, w_uv, w_uv_t, w_gu, w_down = big
    conv_b_w, kv_norm, ln1_g, ln1_b, ln2_g, ln2_b, conv_f_w = small
    b, t = grp.b, grp.t
    m, d = x.shape
    carried = past is None
    tm = _row_tile(t, 256) if carried else m

    outs = _in_projection(x, mod, w_in_p, layer, tables, kv_norm, stacks, dm, tm)
    stacks = tuple(outs[o] for o in STATE_OUTPUTS)
    qa, _, kab, qi, qcr, _, kib, _, krb, _, vab, bg, u, qcn, _, latb, wi, vat, qat, qit, qcrt, qcnt, wit = outs

    three = lambda a: a.reshape(b, t, a.shape[-1])
    if carried:
        kn, vct = _kv_up(latb, w_uk, w_uv_t, layer)
        oa = _dsa_attention_t(qit, wit, three(kib), qat, three(kab), vat, grp,
                              _row_tile(t, 256), _row_tile(t, 512))
        oc = _mla_attention_t(qcnt, qcrt, three(kn), vct, three(krb), grp, _row_tile(t, 512))
        prev_b = prev_f = None
    else:
        c_ak, c_av, c_ik, c_lat, c_kr, prev_b, prev_f = past
        new_rows = lambda a: jnp.pad(three(a), ((0, 0), (0, -t % LANE), (0, 0)))
        oa = _dsa_attention(three(qi), three(wi), three(qa), c_ik, c_ak, c_av,
                            new_rows(kib), new_rows(kab), new_rows(vab), layer, grp)
        oc = _mla_attention_latent(three(qcn), three(qcr), c_lat, c_kr, new_rows(latb), new_rows(krb),
                                   w_uk, w_uv, layer, grp)

    def expand(state):
        return jnp.repeat(state[:, 0], t, axis=0), jnp.repeat(state[:, 1], t, axis=0)

    x1, h2 = _out_projection(x, oa.reshape(m, -1), bg, u, oc.reshape(m, -1), w_out, layer, conv_b_w, mod,
                             ln1_g, ln1_b, None if carried else expand(prev_b), grp, alpha,
                             _row_tile(t, 512) if carried else m)
    tm_f = _row_tile(t, 512) if carried else m
    tf = _row_tile(dm.dff, 512)
    x2, gate_rows = _channel_mixer(h2, x1, w_gu, w_down, layer, conv_f_w, mod, ln2_g, ln2_b,
                                   None if carried else expand(prev_f), grp, alpha, tm_f, tf)
    if carried:
        new_f = gate_rows.reshape(b, t // tm_f, SUBLANE, dm.dff)[:, -1, SUBLANE - (CONV_W - 1):, :]
    else:
        new_f = gate_rows.reshape(b, t, dm.dff)[:, t - (CONV_W - 1):, :]
    new_b = u.reshape(b, t, dm.bw)[:, t - (CONV_W - 1):, :]
    return x2, stacks, (new_b, new_f)


def _state_outputs(stacks, conv_rows, grp, dm):
    ka, ki, kr, va, lat = stacks
    b, t, heads = grp.b, grp.t, dm.aw // HEAD
    new_b, new_f = [jnp.stack(r) for r in zip(*conv_rows)]
    return (ka.reshape(dm.depth, b, t, heads, HEAD), va.reshape(dm.depth, b, t, heads, HEAD),
            ki.reshape(dm.depth, b, t, HEAD), lat.reshape(dm.depth, b, t, dm.r), kr.reshape(dm.depth, b, t, HEAD),
            new_b, new_f)


def kernel(x_prompt, x_sample, c_prompt, c_sample, cache_a_k, cache_a_v, cache_idx_k, cache_mla_latent,
           cache_mla_krope, state_conv_b, state_conv_ffn, w_in, w_out, conv_b_w, mla_kv_norm, mla_w_uk,
           mla_w_uv, w_mod, b_mod, ln1_g, ln1_b, ln2_g, ln2_b, ffn_w_gu, ffn_conv_w, ffn_w_down):
    depth, d, _ = w_in.shape
    a_heads = cache_a_k.shape[3]
    dm = Dims(d=d, aw=a_heads * HEAD, bw=conv_b_w.shape[2], ch=mla_w_uk.shape[2] // C_NOPE,
              r=mla_w_uk.shape[1], dff=ffn_w_down.shape[1], depth=depth)
    alpha = (2 * depth) ** 0.25
    grp_p = Group(b=x_prompt.shape[0], t=x_prompt.shape[1], past=0)
    grp_s = Group(b=x_sample.shape[0], t=x_sample.shape[1], past=cache_a_k.shape[2])

    n_c = grp_p.b + grp_s.b
    c_all = jnp.concatenate([c_prompt, c_sample, jnp.zeros((-n_c % SUBLANE, d), F32)], axis=0)
    mod = _modulation(c_all, w_mod, b_mod)
    mod_p = mod[:, :grp_p.b].reshape(depth, grp_p.b, 1, N_MOD * d)
    mod_s = jnp.repeat(mod[:, grp_p.b:n_c], grp_s.t, axis=1).reshape(depth, 1, grp_s.b * grp_s.t, N_MOD * d)

    tab_p = _rope_tables(jnp.arange(grp_p.t, dtype=I32))
    tab_s = tuple(jnp.tile(a, (grp_s.b, 1)) for a in _rope_tables(grp_s.past + jnp.arange(grp_s.t, dtype=I32)))

    xp = x_prompt.reshape(grp_p.b * grp_p.t, d)
    xs = x_sample.reshape(grp_s.b * grp_s.t, d)
    conv_p, conv_s = [], []
    stacks_p = stacks_s = None
    w_uv_b = mla_w_uv.astype(BF16)
    big = (_pack_w_in(w_in, dm), w_out.astype(BF16), mla_w_uk.astype(BF16), w_uv_b, jnp.swapaxes(w_uv_b, 1, 2),
           ffn_w_gu.astype(BF16), ffn_w_down.astype(BF16))
    idx_k = cache_idx_k.astype(BF16)
    caches = (cache_a_k.reshape(cache_a_k.shape[:3] + (-1,)).astype(BF16),
              cache_a_v.reshape(cache_a_v.shape[:3] + (-1,)).astype(BF16),
              jnp.concatenate([idx_k, idx_k], axis=-1))
    for l in range(depth):
        small = (conv_b_w[l], mla_kv_norm[l], ln1_g[l], ln1_b[l], ln2_g[l], ln2_b[l], ffn_conv_w[l])
        xp, stacks_p, rp = _layer(xp, mod_p, tab_p, grp_p, None, l, big, small, stacks_p, dm, alpha)
        past_l = (*caches, cache_mla_latent, cache_mla_krope, state_conv_b[l], state_conv_ffn[l])
        xs, stacks_s, rs = _layer(xs, mod_s, tab_s, grp_s, past_l, l, big, small, stacks_s, dm, alpha)
        conv_p.append(rp)
        conv_s.append(rs)
    return (xp.reshape(x_prompt.shape), xs.reshape(x_sample.shape),
            *_state_outputs(stacks_p, conv_p, grp_p, dm), *_state_outputs(stacks_s, conv_s, grp_s, dm))
```

```python
import functools
from typing import NamedTuple

import numpy as np
import jax
import jax.numpy as jnp
from jax import lax
from jax.experimental import pallas as pl
from jax.experimental.pallas import tpu as pltpu

F32, BF16, I32, I16 = jnp.float32, jnp.bfloat16, jnp.int32, jnp.int16

CHUNK = 64
CONV_W = 3
ROPE_THETA = 10000.0
HEAD = 64
IDX_HEADS = 16
TOPK_MAX = 256
C_NOPE = 128
C_V = 128
N_MOD = 6
LN_EPS = 1e-5
RMS_EPS = 1e-6

LANE = 128
SUBLANE = 8
VMEM_LIMIT = 50 * 1024 * 1024

MASKED = -1e30
LOG2E = 1.4426950408889634
INT_MIN = -2 ** 31
I16_MIN = -2 ** 15
KEY_NEG_INF = int(np.array(-np.inf, np.float32).view(np.int32)) ^ 0x7FFFFFFF


class Dims(NamedTuple):
    d: int
    aw: int
    bw: int
    ch: int
    r: int
    dff: int
    depth: int


class Group(NamedTuple):
    b: int
    t: int
    past: int


def _row_tile(m, pref):
    if m <= pref:
        return m
    t = pref - pref % SUBLANE
    while m % t:
        t -= SUBLANE
    return t


def _key_tile(l):
    lp = -(-l // LANE) * LANE
    for tk in (512, 384, 256, 128):
        if lp % tk == 0:
            return lp, tk
    raise AssertionError(lp)


def _params(sem):
    return pltpu.CompilerParams(dimension_semantics=sem, vmem_limit_bytes=VMEM_LIMIT)


def _resident(shape, index_map):
    return pl.BlockSpec(shape, index_map, pipeline_mode=pl.Buffered(1))


def _mod_kernel(c_ref, w_ref, b_ref, o_ref):
    a = jax.nn.silu(c_ref[...]).astype(BF16)
    o_ref[...] = jnp.dot(a, w_ref[...].astype(BF16), preferred_element_type=F32) + b_ref[...]


def _modulation(c, w_mod, b_mod):
    depth, d, n = w_mod.shape
    rows = c.shape[0]
    tn = _row_tile(n, 1024)
    return pl.pallas_call(
        _mod_kernel,
        grid=(depth, n // tn),
        in_specs=[pl.BlockSpec((rows, d), lambda l, j: (0, 0)),
                  pl.BlockSpec((None, d, tn), lambda l, j: (l, 0, j)),
                  pl.BlockSpec((None, 1, tn), lambda l, j: (l, 0, j))],
        out_specs=pl.BlockSpec((None, rows, tn), lambda l, j: (l, 0, j)),
        out_shape=jax.ShapeDtypeStruct((depth, rows, n), F32),
        compiler_params=_params(("arbitrary", "arbitrary")),
        name="modulation",
    )(c, w_mod, b_mod.reshape(depth, 1, n))


def _pack_w_in(w, dm):
    d, aw, bw, ch, r = dm.d, dm.aw, dm.bw, dm.ch, dm.r
    lead = w.shape[:-1]
    o = np.cumsum([0, aw, aw, aw, IDX_HEADS * HEAD, HEAD, IDX_HEADS, bw, bw, bw, ch * (C_NOPE + HEAD), r, HEAD])
    qa, ka, va, qi, ki, wi, bg, cg, xb, qc, lat, kr = [w[..., o[i]:o[i + 1]] for i in range(12)]
    qc = qc.reshape(lead + (ch, C_NOPE + HEAD))
    qcn = qc[..., :C_NOPE].reshape(lead + (ch * C_NOPE,))
    qcr = qc[..., C_NOPE:].reshape(lead + (ch * HEAD,))
    pad = jnp.zeros(lead + (LANE - IDX_HEADS,), w.dtype)
    return jnp.concatenate([qa, ka, qi, qcr, ki, ki, kr, kr, va, bg, cg, xb, qcn, lat, wi, pad],
                           axis=-1).astype(BF16)


def _rope(acc, cos, s1, s2):
    outs = []
    for s in range(acc.shape[1] // LANE):
        xs = acc[:, s * LANE:(s + 1) * LANE]
        outs.append(xs * cos + pltpu.roll(xs, HEAD // 2, 1) * s1 + pltpu.roll(xs, LANE - HEAD // 2, 1) * s2)
    return outs[0] if len(outs) == 1 else jnp.concatenate(outs, axis=1)


def _inproj_kernel(x_ref, sc_ref, sh_ref, w_ref, cos_ref, s1_ref, s2_ref, nrm_ref, *rest, dm, n_alias):
    (qa_ref, ka_ref, kab_ref, qi_ref, qcr_ref, ki_ref, kib_ref, kr_ref, krb_ref,
     va_ref, vab_ref, bg_ref, u_ref, qcn_ref, lat_ref, latb_ref, wi_ref, vat_ref,
     qat_ref, qit_ref, qcrt_ref, qcnt_ref, wit_ref) = rest[n_alias:]
    aw, bw, ch, r = dm.aw, dm.bw, dm.ch, dm.r
    h = (x_ref[...] * (1.0 + sc_ref[...]) + sh_ref[...]).astype(BF16)
    cos, s1, s2 = cos_ref[...], s1_ref[...], s2_ref[...]
    col = [0]

    def proj(width):
        c0 = col[0]
        col[0] = c0 + width
        return jnp.dot(h, w_ref[:, c0:c0 + width], preferred_element_type=F32)

    def pieces(width, step=512):
        return [(o, min(step, width - o)) for o in range(0, width, step)]

    for o, wd in pieces(aw):
        y = _rope(proj(wd), cos, s1, s2)
        qa_ref[:, o:o + wd] = y.astype(BF16)
        qat_ref[o:o + wd, :] = y.T.astype(BF16)
    for o, wd in pieces(aw):
        y = _rope(proj(wd), cos, s1, s2)
        ka_ref[:, o // HEAD:(o + wd) // HEAD, :] = y.reshape(y.shape[0], wd // HEAD, HEAD)
        kab_ref[:, o:o + wd] = y.astype(BF16)
    for o, wd in pieces(IDX_HEADS * HEAD):
        y = _rope(proj(wd), cos, s1, s2)
        qi_ref[:, o:o + wd] = y.astype(BF16)
        qit_ref[o:o + wd, :] = y.T.astype(BF16)
    for o, wd in pieces(ch * HEAD):
        y = _rope(proj(wd), cos, s1, s2)
        qcr_ref[:, o:o + wd] = y.astype(BF16)
        qcrt_ref[o:o + wd, :] = y.T.astype(BF16)
    y = _rope(proj(2 * LANE), cos, s1, s2)
    for c, (f32_ref, b16_ref) in enumerate(((ki_ref, kib_ref), (kr_ref, krb_ref))):
        f32_ref[...] = y[:, c * LANE:c * LANE + HEAD]
        b16_ref[...] = y[:, c * LANE:(c + 1) * LANE].astype(BF16)
    for o, wd in pieces(aw):
        y = proj(wd)
        va_ref[:, o // HEAD:(o + wd) // HEAD, :] = y.reshape(y.shape[0], wd // HEAD, HEAD)
        vab_ref[:, o:o + wd] = y.astype(BF16)
        vat_ref[o:o + wd, :] = y.T.astype(BF16)
    for o, wd in pieces(bw):
        bg_ref[:, o:o + wd] = proj(wd)
    c_cg = col[0]
    for o, wd in pieces(bw):
        cg = jnp.dot(h, w_ref[:, c_cg + o:c_cg + o + wd], preferred_element_type=F32)
        xb = jnp.dot(h, w_ref[:, c_cg + bw + o:c_cg + bw + o + wd], preferred_element_type=F32)
        u_ref[:, o:o + wd] = cg * xb
    col[0] = c_cg + 2 * bw
    for o, wd in pieces(ch * C_NOPE):
        y = proj(wd)
        qcn_ref[:, o:o + wd] = y.astype(BF16)
        qcnt_ref[o:o + wd, :] = y.T.astype(BF16)
    lat = proj(r)
    lat = lat * lax.rsqrt(jnp.mean(lat * lat, axis=-1, keepdims=True) + RMS_EPS) * nrm_ref[...]
    lat_ref[...] = lat
    latb_ref[...] = lat.astype(BF16)
    wi = proj(LANE) * (IDX_HEADS ** -0.5)
    wi_ref[...] = wi
    wit_ref[...] = wi.T


STATE_OUTPUTS = (1, 5, 7, 9, 14)
HEAD_OUTPUTS = (1, 9)


MOD_SH1, MOD_SC1, MOD_G1, MOD_SH2, MOD_SC2, MOD_G2 = range(N_MOD)


def _mod_spec(mod, which, layer, n_i, grid_rank):
    groups, rows, d = mod.shape[1], mod.shape[2], mod.shape[3] // N_MOD
    tiles_per_group = n_i // groups
    if grid_rank == 1:
        return pl.BlockSpec((None, None, rows, d), lambda i: (layer, i // tiles_per_group, 0, which))
    return pl.BlockSpec((None, None, rows, d), lambda i, f: (layer, i // tiles_per_group, 0, which))


def _in_projection(x, mod, w_packed, layer, tables, kv_norm, stacks, dm, tm):
    m, d = x.shape
    aw, bw, ch, r = dm.aw, dm.bw, dm.ch, dm.r
    npk = w_packed.shape[2]
    n_i = m // tm
    tab_tiles = tables[0].shape[0] // tm
    widths = [(aw, BF16), (aw, F32), (aw, BF16), (IDX_HEADS * HEAD, BF16), (ch * HEAD, BF16),
              (HEAD, F32), (LANE, BF16), (HEAD, F32), (LANE, BF16),
              (aw, F32), (aw, BF16), (bw, F32), (bw, F32), (ch * C_NOPE, BF16), (r, F32), (r, BF16),
              (LANE, F32)]
    row = lambda i: (i, 0)
    tab_spec = pl.BlockSpec((tm, LANE), lambda i: (i % tab_tiles, 0))
    in_specs = [pl.BlockSpec((tm, d), row),
                _mod_spec(mod, MOD_SC1, layer, n_i, 1), _mod_spec(mod, MOD_SH1, layer, n_i, 1),
                _resident((None, d, npk), lambda i: (layer, 0, 0)),
                tab_spec, tab_spec, tab_spec,
                pl.BlockSpec((1, r), lambda i: (0, 0))]
    args = [x, mod, mod, w_packed, *tables, kv_norm.reshape(1, r)]
    widths_t = [(aw, BF16), (aw, BF16), (IDX_HEADS * HEAD, BF16), (ch * HEAD, BF16), (ch * C_NOPE, BF16),
                (LANE, F32)]
    out_specs = ([pl.BlockSpec((tm, w), row) for w, _ in widths]
                 + [pl.BlockSpec((w, tm), lambda i: (0, i)) for w, _ in widths_t])
    out_shape = ([jax.ShapeDtypeStruct((m, w), dt) for w, dt in widths]
                 + [jax.ShapeDtypeStruct((w, m), dt) for w, dt in widths_t])
    for o in STATE_OUTPUTS:
        w, dt = widths[o]
        tail = (w // HEAD, HEAD) if o in HEAD_OUTPUTS else (w,)
        out_specs[o] = pl.BlockSpec((None, tm) + tail, lambda i, n=len(tail): (layer, i) + (0,) * n)
        out_shape[o] = jax.ShapeDtypeStruct((dm.depth, m) + tail, dt)
    aliases = {}
    if stacks is not None:
        aliases = {len(args) + k: o for k, o in enumerate(STATE_OUTPUTS)}
        in_specs += [pl.BlockSpec(memory_space=pl.ANY)] * len(stacks)
        args += list(stacks)
    return pl.pallas_call(
        functools.partial(_inproj_kernel, dm=dm, n_alias=len(aliases)),
        grid=(n_i,),
        in_specs=in_specs,
        out_specs=out_specs,
        out_shape=out_shape,
        input_output_aliases=aliases,
        compiler_params=_params(("arbitrary",)),
        name="in_projection",
    )(*args)


def _kvup_kernel(l_ref, wk_ref, wvt_ref, k_ref, vt_ref):
    lat = l_ref[...]
    k_ref[...] = jnp.dot(lat, wk_ref[...], preferred_element_type=F32).astype(BF16)
    vt_ref[...] = lax.dot_general(wvt_ref[...], lat, _NT, preferred_element_type=F32).astype(BF16)


def _kv_up(lat, w_uk, w_uv_t, layer):
    m, r = lat.shape
    n = w_uk.shape[2]
    tm = _row_tile(m, 512)
    return pl.pallas_call(
        _kvup_kernel,
        grid=(m // tm,),
        in_specs=[pl.BlockSpec((tm, r), lambda i: (i, 0)),
                  pl.BlockSpec((None, r, n), lambda i: (layer, 0, 0)),
                  pl.BlockSpec((None, n, r), lambda i: (layer, 0, 0))],
        out_specs=[pl.BlockSpec((tm, n), lambda i: (i, 0)), pl.BlockSpec((n, tm), lambda i: (0, i))],
        out_shape=[jax.ShapeDtypeStruct((m, n), BF16), jax.ShapeDtypeStruct((n, m), BF16)],
        compiler_params=_params(("arbitrary",)),
        name="latent_up_projection",
    )(lat, w_uk, w_uv_t)


_NT = (((1,), (1,)), ((), ()))


def _visible_tiles(qpos0, tq, l, tk):
    nvis = jnp.minimum(l, ((qpos0 + tq - 1) // CHUNK + 1) * CHUNK)
    return (nvis + tk - 1) // tk


def _flash_step(carry, s, v):
    m, l, acc = carry
    m_new = jnp.maximum(m, jnp.max(s, axis=1, keepdims=True))
    alpha = jnp.exp(m - m_new)
    p = jnp.exp(s - m_new)
    l = alpha * l + jnp.sum(p, axis=1, keepdims=True)
    acc = alpha * acc + jnp.dot(p.astype(BF16), v, preferred_element_type=F32)
    return m_new, l, acc


def _flash_init(rows, width):
    return (jnp.full((rows, 1), MASKED, F32), jnp.zeros((rows, 1), F32), jnp.zeros((rows, width), F32))


def _ordered_key(x):
    b = pltpu.bitcast(x, I32)
    return jnp.where(b < 0, b ^ 0x7FFFFFFF, b)


def _dsa_kernel(qi_ref, wi_ref, qa_ref, kip_ref, kp_ref, vp_ref, kin_ref, kn_ref, vn_ref, o_ref,
                key_scr, bias_scr, thr2_scr, *, tq, tk, past, topk, aw):
    l = past + tq
    tn = kn_ref.shape[0]
    span = past + tn
    tiles = [(kip_ref, kp_ref, vp_ref, r0, tk, r0) for r0 in range(0, past, tk)]
    tiles.append((kin_ref, kn_ref, vn_ref, 0, tn, past))
    qchunk = (past + lax.broadcasted_iota(I32, (tq, 1), 0)) // CHUNK
    lane = lax.broadcasted_iota(I32, (1, LANE), 1)
    lo_half = lane < HEAD
    kf = jnp.float32(topk)

    def split_heads(qs):
        zero = jnp.zeros_like(qs)
        return jnp.concatenate([jnp.where(lo_half, qs, zero), jnp.where(lo_half, zero, qs)], axis=0)

    wi = wi_ref[...] * (HEAD ** -0.5)
    q_idx = jnp.concatenate([split_heads(qi_ref[:, s * LANE:(s + 1) * LANE]) for s in range(IDX_HEADS // 2)],
                            axis=0)
    w_idx = [wi[:, h:h + 1] for h in range(IDX_HEADS)]

    for kt, (ki_ref, _, _, r0, width, pos0) in enumerate(tiles):
        s_all = lax.dot_general(q_idx, ki_ref[r0:r0 + width, :], _NT, preferred_element_type=F32)
        score = jnp.zeros((tq, width), F32)
        for h in range(IDX_HEADS):
            score = score + jnp.maximum(s_all[h * tq:(h + 1) * tq], 0.0) * w_idx[h]
        kpos = pos0 + lax.broadcasted_iota(I32, (1, width), 1)
        vis = (kpos // CHUNK <= qchunk) & (kpos < l)
        key_scr[kt, :, :width] = _ordered_key(jnp.where(vis, score, -jnp.inf))

    def slabs():
        return [(kt, j, pos0 + j) for kt, (_, _, _, _, width, pos0) in enumerate(tiles)
                for j in range(0, width, LANE)]

    def count(pred):
        acc = jnp.zeros((tq, LANE), F32)
        for kt, j, pos in slabs():
            acc = acc + jnp.where(pred(key_scr[kt, :, j:j + LANE], pos), 1.0, 0.0)
        return jnp.sum(acc, axis=1, keepdims=True)

    def wide(col):
        return jnp.broadcast_to(col, (tq, LANE))

    def bits_body(b, lo):
        step = jnp.left_shift(jnp.int32(1), 30 - 2 * b)
        for mult in (1, 2, 3):
            cand = lo + mult * step
            cand_w = wide(cand)
            lo_next = jnp.where(count(lambda ks, _, cand_w=cand_w: ks >= cand_w) >= kf, cand,
                                lo if mult == 1 else lo_next)
        return lo_next

    thr = lax.fori_loop(0, 16, bits_body, jnp.full((tq, 1), INT_MIN, I32))
    thr_w = wide(thr)

    n_ge = count(lambda ks, _: ks >= thr_w)
    n_gt = count(lambda ks, _: ks > thr_w)
    excess = (n_ge > kf) & (thr > KEY_NEG_INF)
    thr2_scr[...] = jnp.zeros((tq, 1), I32)

    @pl.when(jnp.max(jnp.where(excess, 1.0, 0.0)) > 0.0)
    def _():
        need = kf - n_gt
        nbits = span.bit_length()

        def bit2_body(b, lo):
            cand = lo + jnp.left_shift(jnp.int32(1), nbits - 1 - b)
            cand_w = wide(cand)
            c = count(lambda ks, pos: jnp.where(ks == thr_w, span - (pos + lane), 0) >= cand_w)
            return jnp.where(c >= need, cand, lo)

        thr2_scr[...] = lax.fori_loop(0, nbits, bit2_body, jnp.zeros((tq, 1), I32))

    thr2_w = wide(thr2_scr[...])

    for kt, j, pos in slabs():
        ks = key_scr[kt, :, j:j + LANE]
        tie = jnp.where(span - (pos + lane) >= thr2_w, 0.0, MASKED)
        bias = jnp.where(ks > thr_w, 0.0, jnp.where(ks == thr_w, tie, MASKED))
        bias_scr[kt, :, j:j + LANE] = jnp.where(ks > KEY_NEG_INF, bias, MASKED)

    for pr in range(aw // LANE):
        cols = slice(pr * LANE, (pr + 1) * LANE)
        q2 = split_heads(qa_ref[:, cols])
        carry = _flash_init(2 * tq, LANE)
        for kt, (_, k_ref, v_ref, r0, width, _) in enumerate(tiles):
            s = lax.dot_general(q2, k_ref[r0:r0 + width, cols], _NT, preferred_element_type=F32)
            bias = bias_scr[kt, :, :width]
            s = s * (HEAD ** -0.5) + jnp.concatenate([bias, bias], axis=0)
            carry = _flash_step(carry, s, v_ref[r0:r0 + width, cols])
        _, den, acc = carry
        o2 = acc / den
        o_ref[:, cols] = jnp.where(lo_half, o2[:tq], o2[tq:]).astype(BF16)


def _dsa_attention(qi, wi, qa, ki_cache, k_cache, v_cache, ki_new, k_new, v_new, layer, grp):
    b, t, aw = qa.shape
    past, tn = k_cache.shape[2], k_new.shape[1]
    assert past % LANE == 0 and tn % LANE == 0
    tk = next(c for c in (512, 384, 256, 128) if past % c == 0)
    width = max(tk, tn)
    topk = min(TOPK_MAX, (past + t) // 4)
    qblk = lambda w: pl.BlockSpec((None, t, w), lambda i: (i, 0, 0))
    cblk = lambda w: pl.BlockSpec((None, None, past, w), lambda i: (layer, i, 0, 0))
    nblk = lambda w: pl.BlockSpec((None, tn, w), lambda i: (i, 0, 0))
    return pl.pallas_call(
        functools.partial(_dsa_kernel, tq=t, tk=tk, past=past, topk=topk, aw=aw),
        grid=(b,),
        in_specs=[qblk(IDX_HEADS * HEAD), qblk(LANE), qblk(aw), cblk(LANE), cblk(aw), cblk(aw),
                  nblk(LANE), nblk(aw), nblk(aw)],
        out_specs=qblk(aw),
        out_shape=jax.ShapeDtypeStruct((b, t, aw), BF16),
        scratch_shapes=[pltpu.VMEM((past // tk + 1, t, width), I32),
                        pltpu.VMEM((past // tk + 1, t, width), F32),
                        pltpu.VMEM((t, 1), I32)],
        compiler_params=_params(("arbitrary",)),
        name="dsa_attention",
    )(qi, wi, qa, ki_cache, k_cache, v_cache, ki_new, k_new, v_new)


def _mla_latent_kernel(qn_ref, qr_ref, latp_ref, krp_ref, latn_ref, krn_ref, wuk_ref, wuv_ref, o_ref,
                       *, t, l, past, ch):
    lo_half = lax.broadcasted_iota(I32, (1, LANE), 1) < HEAD
    scale = (C_NOPE + HEAD) ** -0.5
    q_lat, q_rope = [], []
    for h in range(ch):
        cols = slice(h * C_NOPE, (h + 1) * C_NOPE)
        q_lat.append(lax.dot_general(qn_ref[:, cols], wuk_ref[:, cols], _NT, preferred_element_type=F32))
        qs = qr_ref[:, (h // 2) * LANE:(h // 2 + 1) * LANE]
        zero = jnp.zeros_like(qs)
        q_rope.append(jnp.where(lo_half, qs, zero) if h % 2 == 0 else jnp.where(lo_half, zero, qs))
    q_lat = jnp.concatenate(q_lat, axis=0).astype(BF16)
    q_rope = jnp.concatenate(q_rope, axis=0)
    qchunk = jnp.concatenate([(past + lax.broadcasted_iota(I32, (t, 1), 0)) // CHUNK] * ch, axis=0)
    kr_past = krp_ref[...].astype(BF16)
    segments = ((latp_ref[...].astype(BF16), jnp.concatenate([kr_past, kr_past], axis=1), 0, past),
                (latn_ref[...], krn_ref[...], past, l))
    scores = []
    for lat, kr, pos0, pos_end in segments:
        s = (lax.dot_general(q_lat, lat, _NT, preferred_element_type=F32)
             + lax.dot_general(q_rope, kr, _NT, preferred_element_type=F32)) * scale
        kpos = pos0 + lax.broadcasted_iota(I32, (1, lat.shape[0]), 1)
        scores.append(jnp.where((kpos // CHUNK <= qchunk) & (kpos < pos_end), s, MASKED))
    m = functools.reduce(jnp.maximum, [jnp.max(s, axis=1, keepdims=True) for s in scores])
    ps = [jnp.exp(s - m) for s in scores]
    den = sum(jnp.sum(p, axis=1, keepdims=True) for p in ps)
    acc = sum(jnp.dot(p.astype(BF16), seg[0], preferred_element_type=F32) for p, seg in zip(ps, segments))
    o_lat = (acc / den).astype(BF16)
    for h in range(ch):
        cols = slice(h * C_V, (h + 1) * C_V)
        o_ref[:, cols] = jnp.dot(o_lat[h * t:(h + 1) * t, :], wuv_ref[:, cols],
                                 preferred_element_type=F32).astype(BF16)


def _mla_attention_latent(qn, qr, lat_cache, kr_cache, lat_new, kr_new, w_uk, w_uv, layer, grp):
    b, t, wn = qn.shape
    ch = wn // C_NOPE
    past, r = lat_cache.shape[2:]
    tn = lat_new.shape[1]
    qblk = lambda w: pl.BlockSpec((None, t, w), lambda i: (i, 0, 0))
    cblk = lambda w: pl.BlockSpec((None, None, past, w), lambda i: (layer, i, 0, 0))
    nblk = lambda w: pl.BlockSpec((None, tn, w), lambda i: (i, 0, 0))
    wblk = pl.BlockSpec((None, r, wn), lambda i: (layer, 0, 0))
    return pl.pallas_call(
        functools.partial(_mla_latent_kernel, t=t, l=past + t, past=past, ch=ch),
        grid=(b,),
        in_specs=[qblk(wn), qblk(ch * HEAD), cblk(r), cblk(HEAD), nblk(r), nblk(LANE), wblk, wblk],
        out_specs=qblk(wn),
        out_shape=jax.ShapeDtypeStruct((b, t, wn), BF16),
        compiler_params=_params(("arbitrary",)),
        name="mla_attention_latent",
    )(qn, qr, lat_cache, kr_cache, lat_new, kr_new, w_uk, w_uv)


ONES_ROWS = 16


def _flash_t_init(m_scr, acc_scr):
    m_scr[...] = jnp.full(m_scr.shape, MASKED, F32)
    acc_scr[...] = jnp.zeros(acc_scr.shape, F32)


def _flash_t_stage(slot, h, s, s_scr):
    s_scr[slot, h] = s
    return jnp.max(s, axis=0, keepdims=True)


def _flash_t_step(slot, h, s_max, vt, c, s_scr, m_scr, acc_scr):
    m_old = m_scr[h]
    m_new = jnp.maximum(m_old, s_max)
    alpha = jnp.exp2((m_old - m_new) * c)
    p = jnp.exp2((s_scr[slot, h] - m_new) * c).astype(BF16)
    vt_ones = jnp.concatenate([vt, jnp.ones((ONES_ROWS, vt.shape[1]), BF16)], axis=0)
    acc_scr[h] = alpha * acc_scr[h] + jnp.dot(vt_ones, p, preferred_element_type=F32)
    m_scr[h] = m_new


def _flash_t_out(o_ref, acc_scr):
    heads, width = acc_scr.shape[0], acc_scr.shape[1] - ONES_ROWS
    ot = jnp.concatenate([acc_scr[h, :width, :] / acc_scr[h, width:width + 1, :] for h in range(heads)], axis=0)
    o_ref[...] = ot.astype(BF16)


def _split_heads_t(slab_t):
    row_lo = lax.broadcasted_iota(I32, (LANE, 1), 0) < HEAD
    zero = jnp.zeros_like(slab_t)
    return jnp.where(row_lo, slab_t, zero), jnp.where(row_lo, zero, slab_t)


def _dsa_t_kernel(qi_ref, wi_ref, ki_ref, qa_ref, k_ref, vt_ref, o_ref,
                  key_scr, hi_scr, lo_scr, thr2_scr, qit_scr, qat_scr, s_scr, m_scr, acc_scr,
                  *, tq, tk, l, past, topk):
    heads = acc_scr.shape[0]
    qpos0 = past + pl.program_id(1) * tq
    ntiles = _visible_tiles(qpos0, tq, l, tk)
    qchunk = (qpos0 + lax.broadcasted_iota(I32, (1, tq), 1)) // CHUNK
    krow = lax.broadcasted_iota(I32, (tk, 1), 0)

    for s in range(IDX_HEADS // 2):
        qit_scr[2 * s], qit_scr[2 * s + 1] = _split_heads_t(qi_ref[s * LANE:(s + 1) * LANE, :])
    for s in range(heads // 2):
        slab_t = (qa_ref[s * LANE:(s + 1) * LANE, :].astype(F32) * (HEAD ** -0.5)).astype(BF16)
        qat_scr[2 * s], qat_scr[2 * s + 1] = _split_heads_t(slab_t)
    w_t = wi_ref[...] * (HEAD ** -0.5)
    w_rows = [w_t[h:h + 1, :] for h in range(IDX_HEADS)]

    nfull = jnp.minimum(l, (qpos0 // CHUNK + 1) * CHUNK) // tk

    def score_body(kt, masked):
        k0 = pl.multiple_of(kt * tk, tk)
        ki_tile = ki_ref[pl.ds(k0, tk), :]
        score = jnp.zeros((tk, tq), F32)
        for h in range(IDX_HEADS):
            s = jnp.dot(ki_tile, qit_scr[h], preferred_element_type=F32)
            score = score + jnp.maximum(s, 0.0) * w_rows[h]
        if masked:
            kpos = k0 + krow
            score = jnp.where((kpos // CHUNK <= qchunk) & (kpos < l), score, -jnp.inf)
        key = _ordered_key(score)
        key_scr[kt] = key
        hi_scr[kt] = (key >> 16).astype(I16)
        lo_scr[kt] = ((key & 0xFFFF) + I16_MIN).astype(I16)
        return 0

    lax.fori_loop(0, nfull, lambda kt, _: score_body(kt, False), 0)
    lax.fori_loop(nfull, ntiles, lambda kt, _: score_body(kt, True), 0)

    def count(pred):
        def body(kt, acc):
            hit = jnp.where(pred(key_scr[kt], kt * tk), 1.0, 0.0)
            return acc + hit.reshape(tk // SUBLANE, SUBLANE, tq).sum(axis=0)
        acc = lax.fori_loop(0, ntiles, body, jnp.zeros((SUBLANE, tq), F32))
        return jnp.sum(acc, axis=0, keepdims=True)

    def count16(half_scr, cand, strict):
        rows = 2 * SUBLANE
        cand16 = cand.astype(I16)

        def body(kt, acc):
            half = half_scr[kt]
            hit = jnp.where(half > cand16 if strict else half >= cand16, jnp.int16(1), jnp.int16(0))
            parts = hit.reshape(tk // (4 * rows), 4, rows, tq)
            for g in range(parts.shape[0]):
                acc = acc + parts[g]
            return acc

        acc = lax.fori_loop(0, ntiles, body, jnp.zeros((4, rows, tq), I16))
        return acc.astype(I32).sum(axis=0).sum(axis=0, keepdims=True)

    def kth_largest16(half_scr, k_need):
        def bit_body(b, lo):
            cand = lo + jnp.left_shift(jnp.int32(1), 15 - b)
            return jnp.where(count16(half_scr, cand, False) >= k_need, cand, lo)
        return lax.fori_loop(0, 16, bit_body, jnp.full((1, tq), I16_MIN, I32))

    thr_hi = kth_largest16(hi_scr, jnp.full((1, tq), topk, I32))
    n_gt_hi = count16(hi_scr, thr_hi, True)
    thr_hi16 = thr_hi.astype(I16)

    def mark_body(kt, _):
        lo_scr[kt] = jnp.where(hi_scr[kt] == thr_hi16, lo_scr[kt], jnp.int16(I16_MIN))
        return 0

    lax.fori_loop(0, ntiles, mark_body, 0)
    thr_lo = kth_largest16(lo_scr, topk - n_gt_hi)
    thr = thr_hi * 65536 + (thr_lo - I16_MIN)

    lp = key_scr.shape[0] * tk
    n_gt = n_gt_hi + count16(lo_scr, thr_lo, True)
    n_ge = jnp.where(thr_lo > I16_MIN, n_gt_hi + count16(lo_scr, thr_lo, False), count16(hi_scr, thr_hi, False))
    excess = (n_ge > topk) & (thr > KEY_NEG_INF)
    thr2_scr[...] = jnp.zeros((1, tq), I32)

    @pl.when(jnp.max(jnp.where(excess, 1.0, 0.0)) > 0.0)
    def _():
        need = (topk - n_gt).astype(F32)
        nbits = lp.bit_length()

        def bit2_body(b, lo):
            cand = lo + jnp.left_shift(jnp.int32(1), nbits - 1 - b)
            c = count(lambda ks, base: jnp.where(ks == thr, lp - (base + krow), 0) >= cand)
            return jnp.where(c >= need, cand, lo)

        thr2_scr[...] = lax.fori_loop(0, nbits, bit2_body, jnp.zeros((1, tq), I32))

    thr2 = thr2_scr[...]

    def bias_body(kt, _):
        ks = key_scr[kt]
        tie = jnp.where(lp - (kt * tk + krow) >= thr2, 0.0, MASKED)
        bias = jnp.where(ks > thr, 0.0, jnp.where(ks == thr, tie, MASKED))
        key_scr[kt] = pltpu.bitcast(jnp.where(ks > KEY_NEG_INF, bias, MASKED), I32)
        return 0

    lax.fori_loop(0, ntiles, bias_body, 0)

    _flash_t_init(m_scr, acc_scr)

    def att_body(kt, _):
        k0 = pl.multiple_of(kt * tk, tk)
        slot = kt % 2
        bias = pltpu.bitcast(key_scr[kt], F32)
        s_max = []
        for h in range(heads):
            cols = slice((h // 2) * LANE, (h // 2 + 1) * LANE)
            s = jnp.dot(k_ref[pl.ds(k0, tk), cols], qat_scr[h], preferred_element_type=F32) + bias
            s_max.append(_flash_t_stage(slot, h, s, s_scr))
        for h in range(heads):
            _flash_t_step(slot, h, s_max[h], vt_ref[h * HEAD:(h + 1) * HEAD, pl.ds(k0, tk)], LOG2E,
                          s_scr, m_scr, acc_scr)
        return 0

    lax.fori_loop(0, ntiles, att_body, 0)
    _flash_t_out(o_ref, acc_scr)


def _dsa_attention_t(qit, wit, ki, qat, k, vt, grp, tq, tk):
    b, t, aw = k.shape
    assert t % tk == 0
    heads = aw // HEAD
    n_q = t // tq
    topk = min(TOPK_MAX, t // 4)
    qblk = lambda w: pl.BlockSpec((w, tq), lambda i, j: (0, i * n_q + j))
    kblk = lambda w: _resident((None, t, w), lambda i, j: (i, 0, 0))
    return pl.pallas_call(
        functools.partial(_dsa_t_kernel, tq=tq, tk=tk, l=t, past=grp.past, topk=topk),
        grid=(b, n_q),
        in_specs=[qblk(IDX_HEADS * HEAD), qblk(LANE), kblk(LANE), qblk(aw), kblk(aw),
                  _resident((aw, t), lambda i, j: (0, i))],
        out_specs=qblk(aw),
        out_shape=jax.ShapeDtypeStruct((aw, b * t), BF16),
        scratch_shapes=[pltpu.VMEM((t // tk, tk, tq), I32),
                        pltpu.VMEM((t // tk, tk, tq), I16),
                        pltpu.VMEM((t // tk, tk, tq), I16),
                        pltpu.VMEM((1, tq), I32),
                        pltpu.VMEM((IDX_HEADS, LANE, tq), BF16),
                        pltpu.VMEM((heads, LANE, tq), BF16),
                        pltpu.VMEM((2, heads, tk, tq), F32),
                        pltpu.VMEM((heads, 1, tq), F32),
                        pltpu.VMEM((heads, HEAD + ONES_ROWS, tq), F32)],
        compiler_params=_params(("arbitrary", "arbitrary")),
        name="dsa_attention_t",
    )(qit, wit, ki, qat, k, vt)


def _mla_t_kernel(qn_ref, qr_ref, kn_ref, vt_ref, kr_ref, o_ref, qt_scr, s_scr, m_scr, acc_scr,
                  *, tq, tk, l, past):
    ch = acc_scr.shape[0]
    qpos0 = past + pl.program_id(1) * tq
    ntiles = _visible_tiles(qpos0, tq, l, tk)
    nfull = jnp.minimum(l, (qpos0 // CHUNK + 1) * CHUNK) // tk
    qchunk = (qpos0 + lax.broadcasted_iota(I32, (1, tq), 1)) // CHUNK
    krow = lax.broadcasted_iota(I32, (tk, 1), 0)
    c = (C_NOPE + HEAD) ** -0.5 * LOG2E
    for s in range(ch // 2):
        pair = _split_heads_t(qr_ref[s * LANE:(s + 1) * LANE, :])
        for half in range(2):
            h = 2 * s + half
            qt_scr[h] = jnp.concatenate([qn_ref[h * LANE:(h + 1) * LANE, :], pair[half]], axis=0)
    _flash_t_init(m_scr, acc_scr)

    def tile(kt, masked):
        k0 = pl.multiple_of(kt * tk, tk)
        k_rope = kr_ref[pl.ds(k0, tk), :]
        slot = kt % 2
        if masked:
            kpos = k0 + krow
            bias = jnp.where((kpos // CHUNK <= qchunk) & (kpos < l), 0.0, MASKED)
        s_max = []
        for h in range(ch):
            kcat = jnp.concatenate([kn_ref[pl.ds(k0, tk), h * LANE:(h + 1) * LANE], k_rope], axis=1)
            s = jnp.dot(kcat, qt_scr[h], preferred_element_type=F32)
            s_max.append(_flash_t_stage(slot, h, s + bias if masked else s, s_scr))
        for h in range(ch):
            _flash_t_step(slot, h, s_max[h], vt_ref[h * C_V:(h + 1) * C_V, pl.ds(k0, tk)], c,
                          s_scr, m_scr, acc_scr)
        return 0

    lax.fori_loop(0, nfull, lambda kt, _: tile(kt, False), 0)
    lax.fori_loop(nfull, ntiles, lambda kt, _: tile(kt, True), 0)
    _flash_t_out(o_ref, acc_scr)


def _mla_attention_t(qnt, qrt, kn, vt, kr, grp, tq):
    b, t, wn = kn.shape
    ch = wn // C_NOPE
    lp, tk = _key_tile(t)
    assert lp == t and ch % 2 == 0
    n_q = t // tq
    qblk = lambda w: pl.BlockSpec((w, tq), lambda i, j: (0, i * n_q + j))
    kblk = lambda w: _resident((None, t, w), lambda i, j: (i, 0, 0))
    return pl.pallas_call(
        functools.partial(_mla_t_kernel, tq=tq, tk=tk, l=t, past=grp.past),
        grid=(b, n_q),
        in_specs=[qblk(wn), qblk(ch * HEAD), kblk(wn), _resident((wn, t), lambda i, j: (0, i)), kblk(LANE)],
        out_specs=qblk(wn),
        out_shape=jax.ShapeDtypeStruct((wn, b * t), BF16),
        scratch_shapes=[pltpu.VMEM((ch, 2 * LANE, tq), BF16),
                        pltpu.VMEM((2, ch, tk, tq), F32),
                        pltpu.VMEM((ch, 1, tq), F32),
                        pltpu.VMEM((ch, C_V + ONES_ROWS, tq), F32)],
        compiler_params=_params(("arbitrary", "arbitrary")),
        name="mla_attention_t",
    )(qnt, qrt, kn, vt, kr)


def _causal_conv(u, e0, e1, w, seg):
    rmod = lax.broadcasted_iota(I32, (u.shape[0], 1), 0) % seg
    u1 = jnp.where(rmod == 0, e1, pltpu.roll(u, 1, 0))
    u2 = jnp.where(rmod == 0, e0, jnp.where(rmod == 1, e1, pltpu.roll(u, 2, 0)))
    return u2 * w[0:1] + u1 * w[1:2] + u * w[2:3]


def _layer_norm(z, g, b):
    mu = jnp.mean(z, axis=-1, keepdims=True)
    zc = z - mu
    var = jnp.mean(zc * zc, axis=-1, keepdims=True)
    return zc * lax.rsqrt(var + LN_EPS) * g + b


def _outproj_kernel(*refs, alpha, seq_tiles, seg, carried):
    if carried:
        (x_ref, oa_ref, bg_ref, u_ref, oc_ref, w_ref, cw_ref, g1_ref, lng_ref, lnb_ref, sc2_ref, sh2_ref,
         x1_ref, h2_ref, prev_scr) = refs

        @pl.when(pl.program_id(0) % seq_tiles == 0)
        def _():
            prev_scr[...] = jnp.zeros_like(prev_scr)

        e0, e1 = prev_scr[SUBLANE - 2:SUBLANE - 1, :], prev_scr[SUBLANE - 1:SUBLANE, :]
    else:
        (x_ref, oa_ref, bg_ref, u_ref, oc_ref, w_ref, cw_ref, g1_ref, lng_ref, lnb_ref, sc2_ref, sh2_ref,
         e0_ref, e1_ref, x1_ref, h2_ref) = refs
        e0, e1 = e0_ref[...], e1_ref[...]
    u = u_ref[...]
    yb = bg_ref[...] * _causal_conv(u, e0, e1, cw_ref[...], seg)
    if carried:
        prev_scr[...] = u[u.shape[0] - SUBLANE:, :]
    oa, oc = (oa_ref[...].T, oc_ref[...].T) if carried else (oa_ref[...], oc_ref[...])
    mixed = jnp.concatenate([oa, yb.astype(BF16), oc], axis=1)
    tm = mixed.shape[0]
    n_split = 4 if tm % (4 * 2 * SUBLANE) == 0 else 1

    def rows_of(ref, rows):
        return ref[...] if ref.shape[0] == 1 else ref[rows, :]

    halves = [slice(c * tm // n_split, (c + 1) * tm // n_split) for c in range(n_split)]
    mixes = [jnp.dot(mixed[rows, :], w_ref[...], preferred_element_type=F32) for rows in halves]
    for rows, mix in zip(halves, mixes):
        x1 = _layer_norm(alpha * x_ref[rows, :] + (1.0 + rows_of(g1_ref, rows)) * mix, lng_ref[...], lnb_ref[...])
        x1_ref[rows, :] = x1
        h2_ref[rows, :] = (x1 * (1.0 + rows_of(sc2_ref, rows)) + rows_of(sh2_ref, rows)).astype(BF16)


def _out_projection(x, oa, bg, u, oc, w_out, layer, conv_w, mod, ln_g, ln_b, prev, grp, alpha, tm):
    m, d = x.shape
    t_axis = 0 if prev is None else 1
    aw, bw, cw = oa.shape[t_axis], bg.shape[1], oc.shape[t_axis]
    n_i = m // tm
    row = lambda i: (i, 0)
    fix = lambda i: (0, 0)
    attn = (lambda w: pl.BlockSpec((w, tm), lambda i: (0, i))) if prev is None else (
        lambda w: pl.BlockSpec((tm, w), row))
    in_specs = [pl.BlockSpec((tm, d), row), attn(aw), pl.BlockSpec((tm, bw), row),
                pl.BlockSpec((tm, bw), row), attn(cw),
                _resident((None, aw + bw + cw, d), lambda i: (layer, 0, 0)), pl.BlockSpec((CONV_W, bw), fix),
                _mod_spec(mod, MOD_G1, layer, n_i, 1), pl.BlockSpec((1, d), fix), pl.BlockSpec((1, d), fix),
                _mod_spec(mod, MOD_SC2, layer, n_i, 1), _mod_spec(mod, MOD_SH2, layer, n_i, 1)]
    args = [x, oa, bg, u, oc, w_out, conv_w, mod, ln_g.reshape(1, d), ln_b.reshape(1, d), mod, mod]
    scratch = []
    if prev is None:
        assert grp.t % tm == 0
        scratch = [pltpu.VMEM((SUBLANE, bw), F32)]
    else:
        assert tm % grp.t == 0
        in_specs += [pl.BlockSpec((tm, bw), row), pl.BlockSpec((tm, bw), row)]
        args += [prev[0], prev[1]]
    return pl.pallas_call(
        functools.partial(_outproj_kernel, alpha=alpha, seq_tiles=max(grp.t // tm, 1), seg=min(grp.t, tm),
                          carried=prev is None),
        grid=(n_i,),
        in_specs=in_specs,
        out_specs=[pl.BlockSpec((tm, d), row), pl.BlockSpec((tm, d), row)],
        out_shape=[jax.ShapeDtypeStruct((m, d), F32), jax.ShapeDtypeStruct((m, d), BF16)],
        scratch_shapes=scratch,
        compiler_params=_params(("arbitrary",)),
        name="out_projection",
    )(*args)


def _ffn_kernel(*refs, alpha, seq_tiles, seg, carried):
    if carried:
        (h_ref, x_ref, wg_ref, wu_ref, wd_ref, cw_ref, g2_ref, lng_ref, lnb_ref,
         o_ref, gt_ref, acc_scr, prev_scr) = refs
    else:
        (h_ref, x_ref, wg_ref, wu_ref, wd_ref, cw_ref, g2_ref, lng_ref, lnb_ref, e0_ref, e1_ref,
         o_ref, gt_ref, acc_scr) = refs
    f = pl.program_id(1)
    if carried:
        @pl.when(pl.program_id(0) % seq_tiles == 0)
        def _():
            prev_scr[f] = jnp.zeros(prev_scr.shape[1:], F32)

    @pl.when(f == 0)
    def _():
        acc_scr[...] = jnp.zeros_like(acc_scr)

    h = h_ref[...]
    tm, tf = h.shape[0], wg_ref.shape[1]
    n_split = 2 if tf % (2 * LANE) == 0 else 1
    halves = [slice(c * tf // n_split, (c + 1) * tf // n_split) for c in range(n_split)]
    gates = [jnp.dot(h, wg_ref[:, cols], preferred_element_type=F32) for cols in halves]
    ups = [jnp.dot(h, wu_ref[:, cols], preferred_element_type=F32) for cols in halves]
    for cols, gate, up in zip(halves, gates, ups):
        if carried:
            e0, e1 = prev_scr[f, SUBLANE - 2:SUBLANE - 1, cols], prev_scr[f, SUBLANE - 1:SUBLANE, cols]
        else:
            e0, e1 = e0_ref[:, cols], e1_ref[:, cols]
        conv = _causal_conv(gate, e0, e1, cw_ref[:, cols], seg)
        if carried:
            prev_scr[f, :, cols] = gate[tm - SUBLANE:, :]
        gt_ref[:, cols] = gate[tm - gt_ref.shape[0]:, :]
        act = (jax.nn.silu(conv) * up).astype(BF16)
        acc_scr[...] = jnp.dot(act, wd_ref[cols, :], preferred_element_type=F32) + acc_scr[...]

    @pl.when(f == pl.num_programs(1) - 1)
    def _():
        z = alpha * x_ref[...] + (1.0 + g2_ref[...]) * acc_scr[...]
        o_ref[...] = _layer_norm(z, lng_ref[...], lnb_ref[...])


def _channel_mixer(h2, x1, w_gu, w_down, layer, conv_w, mod, ln_g, ln_b, prev, grp, alpha, tm, tf):
    m, d = x1.shape
    dff = w_down.shape[1]
    n_i, n_f = m // tm, dff // tf
    fix = lambda i, f: (0, 0)
    in_specs = [pl.BlockSpec((tm, d), lambda i, f: (i, 0)), pl.BlockSpec((tm, d), lambda i, f: (i, 0)),
                pl.BlockSpec((None, d, tf), lambda i, f: (layer, 0, f)),
                pl.BlockSpec((None, d, tf), lambda i, f: (layer, 0, n_f + f)),
                pl.BlockSpec((None, tf, d), lambda i, f: (layer, f, 0)),
                pl.BlockSpec((CONV_W, tf), lambda i, f: (0, f)),
                _mod_spec(mod, MOD_G2, layer, n_i, 2),
                pl.BlockSpec((1, d), fix), pl.BlockSpec((1, d), fix)]
    args = [h2, x1, w_gu, w_gu, w_down, conv_w, mod, ln_g.reshape(1, d), ln_b.reshape(1, d)]
    scratch = [pltpu.VMEM((tm, d), F32)]
    if prev is None:
        assert grp.t % tm == 0
        scratch.append(pltpu.VMEM((n_f, SUBLANE, tf), F32))
        gt_spec = pl.BlockSpec((None, SUBLANE, tf), lambda i, f: (i, 0, f))
        gt_shape = jax.ShapeDtypeStruct((n_i, SUBLANE, dff), F32)
    else:
        assert tm % grp.t == 0
        in_specs += [pl.BlockSpec((tm, tf), lambda i, f: (i, f))] * 2
        args += [prev[0], prev[1]]
        gt_spec = pl.BlockSpec((tm, tf), lambda i, f: (i, f))
        gt_shape = jax.ShapeDtypeStruct((m, dff), F32)
    return pl.pallas_call(
        functools.partial(_ffn_kernel, alpha=alpha, seq_tiles=max(grp.t // tm, 1), seg=min(grp.t, tm),
                          carried=prev is None),
        grid=(n_i, n_f),
        in_specs=in_specs,
        out_specs=[pl.BlockSpec((tm, d), lambda i, f: (i, 0)), gt_spec],
        out_shape=[jax.ShapeDtypeStruct((m, d), F32), gt_shape],
        scratch_shapes=scratch,
        compiler_params=_params(("arbitrary", "arbitrary")),
        name="channel_mixer",
    )(*args)


def _rope_tables(pos):
    half = HEAD // 2
    inv = jnp.power(jnp.float32(ROPE_THETA), -jnp.arange(half, dtype=F32) / half)
    ang = pos.astype(F32)[:, None] * inv[None, :]
    cos, sin = jnp.cos(ang), jnp.sin(ang)
    zero = jnp.zeros_like(sin)
    reps = LANE // HEAD
    return (jnp.tile(jnp.concatenate([cos, cos], axis=1), (1, reps)),
            jnp.tile(jnp.concatenate([zero, sin], axis=1), (1, reps)),
            jnp.tile(jnp.concatenate([-sin, zero], axis=1), (1, reps)))


def _layer(x, mod, tables, grp, past, layer, big, small, stacks, dm, alpha):
    w_in_p, w_out, w_uk, w_uv, w_uv_t, w_gu, w_down = big
    conv_b_w, kv_norm, ln1_g, ln1_b, ln2_g, ln2_b, conv_f_w = small
    b, t = grp.b, grp.t
    m, d = x.shape
    carried = past is None
    tm = _row_tile(t, 256) if carried else m

    outs = _in_projection(x, mod, w_in_p, layer, tables, kv_norm, stacks, dm, tm)
    stacks = tuple(outs[o] for o in STATE_OUTPUTS)
    qa, _, kab, qi, qcr, _, kib, _, krb, _, vab, bg, u, qcn, _, latb, wi, vat, qat, qit, qcrt, qcnt, wit = outs

    three = lambda a: a.reshape(b, t, a.shape[-1])
    if carried:
        kn, vct = _kv_up(latb, w_uk, w_uv_t, layer)
        oa = _dsa_attention_t(qit, wit, three(kib), qat, three(kab), vat, grp,
                              _row_tile(t, 256), _row_tile(t, 512))
        oc = _mla_attention_t(qcnt, qcrt, three(kn), vct, three(krb), grp, _row_tile(t, 512))
        prev_b = prev_f = None
    else:
        c_ak, c_av, c_ik, c_lat, c_kr, prev_b, prev_f = past
        new_rows = lambda a: jnp.pad(three(a), ((0, 0), (0, -t % LANE), (0, 0)))
        oa = _dsa_attention(three(qi), three(wi), three(qa), c_ik, c_ak, c_av,
                            new_rows(kib), new_rows(kab), new_rows(vab), layer, grp)
        oc = _mla_attention_latent(three(qcn), three(qcr), c_lat, c_kr, new_rows(latb), new_rows(krb),
                                   w_uk, w_uv, layer, grp)

    def expand(state):
        return jnp.repeat(state[:, 0], t, axis=0), jnp.repeat(state[:, 1], t, axis=0)

    if not carried:
        oa, oc = oa.reshape(m, -1), oc.reshape(m, -1)
    x1, h2 = _out_projection(x, oa, bg, u, oc, w_out, layer, conv_b_w, mod,
                             ln1_g, ln1_b, None if carried else expand(prev_b), grp, alpha,
                             _row_tile(t, 512) if carried else m)
    tm_f = _row_tile(t, 512) if carried else m
    tf = _row_tile(dm.dff, 512)
    x2, gate_rows = _channel_mixer(h2, x1, w_gu, w_down, layer, conv_f_w, mod, ln2_g, ln2_b,
                                   None if carried else expand(prev_f), grp, alpha, tm_f, tf)
    if carried:
        new_f = gate_rows.reshape(b, t // tm_f, SUBLANE, dm.dff)[:, -1, SUBLANE - (CONV_W - 1):, :]
    else:
        new_f = gate_rows.reshape(b, t, dm.dff)[:, t - (CONV_W - 1):, :]
    new_b = u.reshape(b, t, dm.bw)[:, t - (CONV_W - 1):, :]
    return x2, stacks, (new_b, new_f)


def _state_outputs(stacks, conv_rows, grp, dm):
    ka, ki, kr, va, lat = stacks
    b, t, heads = grp.b, grp.t, dm.aw // HEAD
    new_b, new_f = [jnp.stack(r) for r in zip(*conv_rows)]
    return (ka.reshape(dm.depth, b, t, heads, HEAD), va.reshape(dm.depth, b, t, heads, HEAD),
            ki.reshape(dm.depth, b, t, HEAD), lat.reshape(dm.depth, b, t, dm.r), kr.reshape(dm.depth, b, t, HEAD),
            new_b, new_f)


def kernel(x_prompt, x_sample, c_prompt, c_sample, cache_a_k, cache_a_v, cache_idx_k, cache_mla_latent,
           cache_mla_krope, state_conv_b, state_conv_ffn, w_in, w_out, conv_b_w, mla_kv_norm, mla_w_uk,
           mla_w_uv, w_mod, b_mod, ln1_g, ln1_b, ln2_g, ln2_b, ffn_w_gu, ffn_conv_w, ffn_w_down):
    depth, d, _ = w_in.shape
    a_heads = cache_a_k.shape[3]
    dm = Dims(d=d, aw=a_heads * HEAD, bw=conv_b_w.shape[2], ch=mla_w_uk.shape[2] // C_NOPE,
              r=mla_w_uk.shape[1], dff=ffn_w_down.shape[1], depth=depth)
    alpha = (2 * depth) ** 0.25
    grp_p = Group(b=x_prompt.shape[0], t=x_prompt.shape[1], past=0)
    grp_s = Group(b=x_sample.shape[0], t=x_sample.shape[1], past=cache_a_k.shape[2])

    n_c = grp_p.b + grp_s.b
    c_all = jnp.concatenate([c_prompt, c_sample, jnp.zeros((-n_c % SUBLANE, d), F32)], axis=0)
    mod = _modulation(c_all, w_mod, b_mod)
    mod_p = mod[:, :grp_p.b].reshape(depth, grp_p.b, 1, N_MOD * d)
    mod_s = jnp.repeat(mod[:, grp_p.b:n_c], grp_s.t, axis=1).reshape(depth, 1, grp_s.b * grp_s.t, N_MOD * d)

    tab_p = _rope_tables(jnp.arange(grp_p.t, dtype=I32))
    tab_s = tuple(jnp.tile(a, (grp_s.b, 1)) for a in _rope_tables(grp_s.past + jnp.arange(grp_s.t, dtype=I32)))

    xp = x_prompt.reshape(grp_p.b * grp_p.t, d)
    xs = x_sample.reshape(grp_s.b * grp_s.t, d)
    conv_p, conv_s = [], []
    stacks_p = stacks_s = None
    w_uv_b = mla_w_uv.astype(BF16)
    big = (_pack_w_in(w_in, dm), w_out.astype(BF16), mla_w_uk.astype(BF16), w_uv_b, jnp.swapaxes(w_uv_b, 1, 2),
           ffn_w_gu.astype(BF16), ffn_w_down.astype(BF16))
    idx_k = cache_idx_k.astype(BF16)
    caches = (cache_a_k.reshape(cache_a_k.shape[:3] + (-1,)).astype(BF16),
              cache_a_v.reshape(cache_a_v.shape[:3] + (-1,)).astype(BF16),
              jnp.concatenate([idx_k, idx_k], axis=-1))
    for l in range(depth):
        small = (conv_b_w[l], mla_kv_norm[l], ln1_g[l], ln1_b[l], ln2_g[l], ln2_b[l], ffn_conv_w[l])
        xp, stacks_p, rp = _layer(xp, mod_p, tab_p, grp_p, None, l, big, small, stacks_p, dm, alpha)
        past_l = (*caches, cache_mla_latent, cache_mla_krope, state_conv_b[l], state_conv_ffn[l])
        xs, stacks_s, rs = _layer(xs, mod_s, tab_s, grp_s, past_l, l, big, small, stacks_s, dm, alpha)
        conv_p.append(rp)
        conv_s.append(rs)
    return (xp.reshape(x_prompt.shape), xs.reshape(x_sample.shape),
            *_state_outputs(stacks_p, conv_p, grp_p, dm), *_state_outputs(stacks_s, conv_s, grp_s, dm))
```
